```python
import math
import jax, jax.numpy as jnp
from jax import lax
import numpy as np

D_MODEL = 2048
BATCH = 8
SEQ = 4096
DEPTH = 4

D_MIX = D_MODEL
C_CONV = D_MIX // 2
HEAD_DIM = 64
N_HEADS = (D_MIX - C_CONV) // HEAD_DIM
N_KV = 2
GROUP = N_HEADS // N_KV
ATTN_W = N_HEADS * HEAD_DIM
KV_W = N_KV * HEAD_DIM
CONV_WIDTH = 31
WINDOW = 128
BLOCK = 128
LN_EPS = 1e-5
NEG_INF = -1e30
DEEPNORM_ALPHA = (2 * DEPTH) ** 0.25
DEEPNORM_BETA = (8 * DEPTH) ** -0.25

SPLIT_SIZES = (C_CONV, C_CONV, C_CONV, ATTN_W, KV_W, KV_W, ATTN_W)
D_IN = sum(SPLIT_SIZES)
SPLIT_POINTS = tuple(int(p) for p in np.cumsum(SPLIT_SIZES)[:-1])

kernel_name = "hybrid_conformer_conv_swa_sink_deepnorm"


def layer_norm(x, g, b):
    xf = x.astype(jnp.float32)
    mu = jnp.mean(xf, axis=-1, keepdims=True)
    var = jnp.mean(jnp.square(xf - mu), axis=-1, keepdims=True)
    y = (xf - mu) * lax.rsqrt(var + LN_EPS)
    return (y * g.astype(jnp.float32) + b.astype(jnp.float32)).astype(x.dtype)


def conformer_conv_branch(u_val, u_glu, conv_w, conv_b, ln_g, ln_b):
    h = u_val * jax.nn.sigmoid(u_glu)
    h = lax.conv_general_dilated(
        h, conv_w[:, None, :].astype(h.dtype), window_strides=(1,),
        padding=[(CONV_WIDTH - 1, 0)],
        dimension_numbers=("NWC", "WIO", "NWC"),
        feature_group_count=C_CONV) + conv_b
    h = layer_norm(h, ln_g, ln_b)
    return jax.nn.silu(h)


def swa_sink_attention(q, k, v, sinks):
    B, S, _ = q.shape
    nb = S // BLOCK
    q = q.reshape(B, nb, BLOCK, N_KV, GROUP, HEAD_DIM)
    k = k.reshape(B, nb, BLOCK, N_KV, HEAD_DIM)
    v = v.reshape(B, nb, BLOCK, N_KV, HEAD_DIM)
    pad = ((0, 0), (1, 0), (0, 0), (0, 0), (0, 0))
    kk = jnp.concatenate([jnp.pad(k, pad)[:, :-1], k], axis=2)
    vv = jnp.concatenate([jnp.pad(v, pad)[:, :-1], v], axis=2)
    scale = HEAD_DIM ** -0.5
    scores = jnp.einsum("bnqkgd,bnskd->bnkgqs", q, kk).astype(jnp.float32) * scale
    qi = jnp.arange(BLOCK)[:, None]
    si = jnp.arange(2 * BLOCK)[None, :]
    diff = qi + BLOCK - si
    band = (diff >= 0) & (diff < WINDOW)
    blk = jnp.arange(nb)[:, None, None]
    valid = band[None] & ((blk > 0) | (si >= BLOCK)[None])
    scores = jnp.where(valid[None, :, None, None], scores, NEG_INF)
    sink = sinks.astype(jnp.float32).reshape(1, 1, N_KV, GROUP, 1, 1)
    m = jnp.maximum(jnp.max(scores, axis=-1, keepdims=True), sink)
    p = jnp.exp(scores - m)
    denom = jnp.sum(p, axis=-1, keepdims=True) + jnp.exp(sink - m)
    probs = (p / denom).astype(vv.dtype)
    out = jnp.einsum("bnkgqs,bnskd->bnqkgd", probs, vv)
    return out.reshape(B, S, ATTN_W)


def hybrid_layer(x, w_in, b_in, conv_w, conv_b, conv_ln_g, conv_ln_b, sinks,
                 w_out, b_out, ln_g, ln_b):
    proj = jnp.einsum("bsd,de->bse", x, w_in) + b_in
    c_val, c_glu, c_gate, q, k, v, a_gate = jnp.split(proj, SPLIT_POINTS, axis=-1)
    y_conv = conformer_conv_branch(c_val, c_glu, conv_w, conv_b, conv_ln_g, conv_ln_b)
    y_conv = y_conv * jax.nn.silu(c_gate)
    y_attn = swa_sink_attention(q, k, v, sinks) * jax.nn.silu(a_gate)
    y = jnp.concatenate([y_conv, y_attn], axis=-1)
    y = jnp.einsum("bse,ed->bsd", y, w_out) + b_out
    return layer_norm(DEEPNORM_ALPHA * x + y, ln_g, ln_b)


def _fwd_setup_inputs(seed: int = 0) -> dict:
    key = jax.random.key(seed)
    ks = jax.random.split(key, 12)
    f32 = jnp.float32
    x = jax.random.normal(ks[0], (BATCH, SEQ, D_MODEL), f32)
    w_in = jax.random.normal(ks[1], (DEPTH, D_MODEL, D_IN), f32) * D_MODEL ** -0.5
    b_in = jax.random.normal(ks[2], (DEPTH, D_IN), f32) * 0.02
    conv_w = jax.random.normal(ks[3], (DEPTH, CONV_WIDTH, C_CONV), f32) * CONV_WIDTH ** -0.5
    conv_b = jax.random.normal(ks[4], (DEPTH, C_CONV), f32) * 0.02
    conv_ln_g = 1.0 + 0.1 * jax.random.normal(ks[5], (DEPTH, C_CONV), f32)
    conv_ln_b = 0.02 * jax.random.normal(ks[6], (DEPTH, C_CONV), f32)
    sinks = 0.5 * jax.random.normal(ks[7], (DEPTH, N_HEADS), f32)
    w_out = (jax.random.normal(ks[8], (DEPTH, D_MIX, D_MODEL), f32)
             * D_MIX ** -0.5 * DEEPNORM_BETA)
    b_out = jax.random.normal(ks[9], (DEPTH, D_MODEL), f32) * 0.02
    ln_g = 1.0 + 0.1 * jax.random.normal(ks[10], (DEPTH, D_MODEL), f32)
    ln_b = 0.02 * jax.random.normal(ks[11], (DEPTH, D_MODEL), f32)
    return {"x": x, "w_in": w_in, "b_in": b_in, "conv_w": conv_w, "conv_b": conv_b,
            "conv_ln_g": conv_ln_g, "conv_ln_b": conv_ln_b, "sinks": sinks,
            "w_out": w_out, "b_out": b_out, "ln_g": ln_g, "ln_b": ln_b}


def _fwd_reference(x, w_in, b_in, conv_w, conv_b, conv_ln_g, conv_ln_b, sinks,
              w_out, b_out, ln_g, ln_b):
    h = x
    for l in range(DEPTH):
        h = hybrid_layer(h, w_in[l], b_in[l], conv_w[l], conv_b[l], conv_ln_g[l],
                         conv_ln_b[l], sinks[l], w_out[l], b_out[l], ln_g[l], ln_b[l])
    return h


import jax as _jax
import jax.numpy as _jnp

TWIN_FORMAT = 'train_step'
FWD_PARAMS = ['x', 'w_in', 'b_in', 'conv_w', 'conv_b', 'conv_ln_g', 'conv_ln_b', 'sinks', 'w_out', 'b_out', 'ln_g', 'ln_b']
TWIN_WEIGHTS = ['w_in', 'b_in', 'conv_w', 'conv_b', 'conv_ln_g', 'conv_ln_b', 'sinks', 'w_out', 'b_out', 'ln_g', 'ln_b']
TWIN_DIFF_INPUT = 'x'
TWIN_INPUTS = ['x', 'w_in', 'b_in', 'conv_w', 'conv_b', 'conv_ln_g', 'conv_ln_b', 'sinks', 'w_out', 'b_out', 'ln_g', 'ln_b', 'loss_target', 'm_w_in', 'm_b_in', 'm_conv_w', 'm_conv_b', 'm_conv_ln_g', 'm_conv_ln_b', 'm_sinks', 'm_w_out', 'm_b_out', 'm_ln_g', 'm_ln_b', 'v_w_in', 'v_b_in', 'v_conv_w', 'v_conv_b', 'v_conv_ln_g', 'v_conv_ln_b', 'v_sinks', 'v_w_out', 'v_b_out', 'v_ln_g', 'v_ln_b']
TWIN_OUTPUTS = ['loss', 'grad_x', 'grad_w_in', 'grad_b_in', 'grad_conv_w', 'grad_conv_b', 'grad_conv_ln_g', 'grad_conv_ln_b', 'grad_sinks', 'grad_w_out', 'grad_b_out', 'grad_ln_g', 'grad_ln_b', 'delta_w_in', 'delta_b_in', 'delta_conv_w', 'delta_conv_b', 'delta_conv_ln_g', 'delta_conv_ln_b', 'delta_sinks', 'delta_w_out', 'delta_b_out', 'delta_ln_g', 'delta_ln_b', 'new_m_w_in', 'new_m_b_in', 'new_m_conv_w', 'new_m_conv_b', 'new_m_conv_ln_g', 'new_m_conv_ln_b', 'new_m_sinks', 'new_m_w_out', 'new_m_b_out', 'new_m_ln_g', 'new_m_ln_b', 'new_v_w_in', 'new_v_b_in', 'new_v_conv_w', 'new_v_conv_b', 'new_v_conv_ln_g', 'new_v_conv_ln_b', 'new_v_sinks', 'new_v_w_out', 'new_v_b_out', 'new_v_ln_g', 'new_v_ln_b']
TWIN_LEAF_KINDS = {'loss': 'loss', 'grad_x': 'grad_x', 'grad_w_in': 'grad_w', 'grad_b_in': 'grad_w', 'grad_conv_w': 'grad_w', 'grad_conv_b': 'grad_w', 'grad_conv_ln_g': 'grad_w', 'grad_conv_ln_b': 'grad_w', 'grad_sinks': 'grad_w', 'grad_w_out': 'grad_w', 'grad_b_out': 'grad_w', 'grad_ln_g': 'grad_w', 'grad_ln_b': 'grad_w', 'delta_w_in': 'delta_w', 'delta_b_in': 'delta_w', 'delta_conv_w': 'delta_w', 'delta_conv_b': 'delta_w', 'delta_conv_ln_g': 'delta_w', 'delta_conv_ln_b': 'delta_w', 'delta_sinks': 'delta_w', 'delta_w_out': 'delta_w', 'delta_b_out': 'delta_w', 'delta_ln_g': 'delta_w', 'delta_ln_b': 'delta_w', 'new_m_w_in': 'new_m', 'new_m_b_in': 'new_m', 'new_m_conv_w': 'new_m', 'new_m_conv_b': 'new_m', 'new_m_conv_ln_g': 'new_m', 'new_m_conv_ln_b': 'new_m', 'new_m_sinks': 'new_m', 'new_m_w_out': 'new_m', 'new_m_b_out': 'new_m', 'new_m_ln_g': 'new_m', 'new_m_ln_b': 'new_m', 'new_v_w_in': 'new_v', 'new_v_b_in': 'new_v', 'new_v_conv_w': 'new_v', 'new_v_conv_b': 'new_v', 'new_v_conv_ln_g': 'new_v', 'new_v_conv_ln_b': 'new_v', 'new_v_sinks': 'new_v', 'new_v_w_out': 'new_v', 'new_v_b_out': 'new_v', 'new_v_ln_g': 'new_v', 'new_v_ln_b': 'new_v'}


def _forward(args):
    return _fwd_reference(*[args[k] for k in FWD_PARAMS])


def _output_shape():
    def fwd():
        inp = _fwd_setup_inputs(0)
        return _fwd_reference(*[inp[k] for k in FWD_PARAMS])
    out = _jax.eval_shape(fwd)
    return out.shape, out.dtype

N_MICROBATCH = 1
ADAM_LR = 0.001
ADAM_B1 = 0.9
ADAM_B2 = 0.999
ADAM_EPS = 1e-08
ADAM_WD = 0.01
ADAM_STEP = 10
PER_EXAMPLE_BATCH_AXIS = {'x': 0, 'loss_target': 0}
SHARED_INPUTS = []
_WEIGHT_DTYPES = {'w_in': _jnp.float32, 'b_in': _jnp.float32, 'conv_w': _jnp.float32, 'conv_b': _jnp.float32, 'conv_ln_g': _jnp.float32, 'conv_ln_b': _jnp.float32, 'sinks': _jnp.float32, 'w_out': _jnp.float32, 'b_out': _jnp.float32, 'ln_g': _jnp.float32, 'ln_b': _jnp.float32}
MOMENT_SCALE = {'w_in': 5.950205e-03, 'b_in': 1.089275e-02, 'conv_w': 8.545279e-03, 'conv_b': 1.911995e-02, 'conv_ln_g': 1.077018e-02, 'conv_ln_b': 1.114152e-02, 'sinks': 2.031296e-03, 'w_out': 1.493724e-02, 'b_out': 1.909412e-01, 'ln_g': 8.977408e+00, 'ln_b': 4.489618e-01}


def _to_microbatches(a, axis):
    t = _jnp.moveaxis(a, axis, 0)
    t = t.reshape((N_MICROBATCH, t.shape[0] // N_MICROBATCH) + t.shape[1:])
    return _jnp.moveaxis(t, 1, axis + 1)


def setup_inputs(seed: int = 0) -> dict:
    inp = _fwd_setup_inputs(seed)
    key = _jax.random.fold_in(_jax.random.key(seed), 7919)
    shape, _ = _output_shape()
    out = dict(inp)
    out["loss_target"] = _jax.random.normal(_jax.random.fold_in(key, 0), shape, _jnp.float32)
    for i, name in enumerate(TWIN_WEIGHTS):
        w = inp[name].astype(_jnp.float32)
        if MOMENT_SCALE is None:
            s = _jnp.sqrt(_jnp.mean(_jnp.square(w)) + 1e-30)
        else:
            s = MOMENT_SCALE[name]
        km, kv = _jax.random.split(_jax.random.fold_in(key, i + 1))
        out[name] = w
        out["m_" + name] = s * _jax.random.normal(km, w.shape, _jnp.float32)
        out["v_" + name] = (s * s) * _jax.random.uniform(kv, w.shape, _jnp.float32, 0.5, 1.5)
    if N_MICROBATCH > 1:
        for name, axis in PER_EXAMPLE_BATCH_AXIS.items():
            out[name] = _to_microbatches(out[name], axis)
    return {'x': out['x'], 'w_in': out['w_in'], 'b_in': out['b_in'], 'conv_w': out['conv_w'], 'conv_b': out['conv_b'], 'conv_ln_g': out['conv_ln_g'], 'conv_ln_b': out['conv_ln_b'], 'sinks': out['sinks'], 'w_out': out['w_out'], 'b_out': out['b_out'], 'ln_g': out['ln_g'], 'ln_b': out['ln_b'], 'loss_target': out['loss_target'], 'm_w_in': out['m_w_in'], 'm_b_in': out['m_b_in'], 'm_conv_w': out['m_conv_w'], 'm_conv_b': out['m_conv_b'], 'm_conv_ln_g': out['m_conv_ln_g'], 'm_conv_ln_b': out['m_conv_ln_b'], 'm_sinks': out['m_sinks'], 'm_w_out': out['m_w_out'], 'm_b_out': out['m_b_out'], 'm_ln_g': out['m_ln_g'], 'm_ln_b': out['m_ln_b'], 'v_w_in': out['v_w_in'], 'v_b_in': out['v_b_in'], 'v_conv_w': out['v_conv_w'], 'v_conv_b': out['v_conv_b'], 'v_conv_ln_g': out['v_conv_ln_g'], 'v_conv_ln_b': out['v_conv_ln_b'], 'v_sinks': out['v_sinks'], 'v_w_out': out['v_w_out'], 'v_b_out': out['v_b_out'], 'v_ln_g': out['v_ln_g'], 'v_ln_b': out['v_ln_b']}


def _loss(weights, diff, rest, loss_target):
    with _jax.named_scope("forward"):
        args = {**rest, TWIN_DIFF_INPUT: diff, **{k: w.astype(_WEIGHT_DTYPES[k]) for k, w in weights.items()}}
        y = _forward(args)
    with _jax.named_scope("loss_head"):
        err = _jnp.square(y.astype(_jnp.float32) - loss_target)
        return 0.5 * _jnp.sum(_jnp.mean(err, axis=-1)) if err.ndim else 0.5 * err


def _adamw(w, g, m, v):
    m = ADAM_B1 * m + (1.0 - ADAM_B1) * g
    v = ADAM_B2 * v + (1.0 - ADAM_B2) * _jnp.square(g)
    m_hat = m / (1.0 - ADAM_B1 ** ADAM_STEP)
    v_hat = v / (1.0 - ADAM_B2 ** ADAM_STEP)
    delta = -ADAM_LR * (m_hat / (_jnp.sqrt(v_hat) + ADAM_EPS) + ADAM_WD * w)
    return delta, m, v


def reference(x, w_in, b_in, conv_w, conv_b, conv_ln_g, conv_ln_b, sinks, w_out, b_out, ln_g, ln_b, loss_target, m_w_in, m_b_in, m_conv_w, m_conv_b, m_conv_ln_g, m_conv_ln_b, m_sinks, m_w_out, m_b_out, m_ln_g, m_ln_b, v_w_in, v_b_in, v_conv_w, v_conv_b, v_conv_ln_g, v_conv_ln_b, v_sinks, v_w_out, v_b_out, v_ln_g, v_ln_b):
    given = dict(x=x, w_in=w_in, b_in=b_in, conv_w=conv_w, conv_b=conv_b, conv_ln_g=conv_ln_g, conv_ln_b=conv_ln_b, sinks=sinks, w_out=w_out, b_out=b_out, ln_g=ln_g, ln_b=ln_b, loss_target=loss_target, m_w_in=m_w_in, m_b_in=m_b_in, m_conv_w=m_conv_w, m_conv_b=m_conv_b, m_conv_ln_g=m_conv_ln_g, m_conv_ln_b=m_conv_ln_b, m_sinks=m_sinks, m_w_out=m_w_out, m_b_out=m_b_out, m_ln_g=m_ln_g, m_ln_b=m_ln_b, v_w_in=v_w_in, v_b_in=v_b_in, v_conv_w=v_conv_w, v_conv_b=v_conv_b, v_conv_ln_g=v_conv_ln_g, v_conv_ln_b=v_conv_ln_b, v_sinks=v_sinks, v_w_out=v_w_out, v_b_out=v_b_out, v_ln_g=v_ln_g, v_ln_b=v_ln_b)
    weights = {n: given[n] for n in TWIN_WEIGHTS}
    shared = {n: given[n] for n in SHARED_INPUTS}
    per_example = {n: given[n] for n in ['x']}
    grad_fn = _jax.value_and_grad(_loss, argnums=(0, 1))

    def one_microbatch(ex, loss_target):
        ex = dict(ex)
        diff = ex.pop(TWIN_DIFF_INPUT)
        return grad_fn(weights, diff, {**shared, **ex}, loss_target)

    if N_MICROBATCH == 1:
        loss, (grad_w, grad_x) = one_microbatch(per_example, given["loss_target"])
    else:
        def body(carry, xs):
            loss_sum, grad_sum = carry
            l_k, (gw_k, gx_k) = one_microbatch(xs[0], xs[1])
            with _jax.named_scope("update"):
                return (loss_sum + l_k, _jax.tree.map(_jnp.add, grad_sum, gw_k)), gx_k

        init = (_jnp.zeros((), _jnp.float32), _jax.tree.map(_jnp.zeros_like, weights))
        (loss, grad_w), grad_x = _jax.lax.scan(body, init, (per_example, given["loss_target"]))
    with _jax.named_scope("update"):
        delta_w, new_m, new_v = {}, {}, {}
        for n in TWIN_WEIGHTS:
            delta_w[n], new_m[n], new_v[n] = _adamw(weights[n], grad_w[n], given["m_" + n], given["v_" + n])
    return (loss, grad_x, *[grad_w[n] for n in TWIN_WEIGHTS], *[delta_w[n] for n in TWIN_WEIGHTS],
            *[new_m[n] for n in TWIN_WEIGHTS], *[new_v[n] for n in TWIN_WEIGHTS])
```

```python
import functools

import jax
import jax.numpy as jnp
from jax import lax
from jax.experimental import pallas as pl
from jax.experimental.pallas import tpu as pltpu

F32 = jnp.float32
BF16 = jnp.bfloat16
MESH = pl.DeviceIdType.MESH

N_DEV = 8
N_CHIPS = 4
HEAD_DIM = 64
N_KV = 2
KV_W = N_KV * HEAD_DIM
CONV_WIDTH = 31
CONV_ROWS = 32
HALO = 32
WINDOW = 128
LN_EPS = 1e-5
NEG_INF = -1e30
LANES = 128

ADAM_LR = 0.001
ADAM_B1 = 0.9
ADAM_B2 = 0.999
ADAM_EPS = 1e-08
ADAM_WD = 0.01
ADAM_STEP = 10

VMEM_LIMIT = 56 * 1024 * 1024


def _tile(n, target, align=LANES):
    best = None
    for t in range(align, min(n, target) + 1, align):
        if n % t == 0:
            best = t
    return n if best is None else best


def _sigmoid(x):
    return jax.nn.sigmoid(x)


def _dsilu(x, s):
    return s * (1.0 + x * (1.0 - s))


def _cparams(sem, vmem=VMEM_LIMIT):
    return pltpu.CompilerParams(dimension_semantics=sem, vmem_limit_bytes=vmem)


def _matmul(a, b, *, ta=False, tb=False, tm, tn, tk, out_dtype, name, bias=None, resid=None,
            resid_scale=1.0):
    m, k = (a.shape[1], a.shape[0]) if ta else a.shape
    n = b.shape[0] if tb else b.shape[1]
    assert (b.shape[1] if tb else b.shape[0]) == k
    assert m % tm == 0 and n % tn == 0 and k % tk == 0
    nk = k // tk
    dn = (((0 if ta else 1,), (1 if tb else 0,)), ((), ()))

    def body(*refs):
        a_ref, b_ref = refs[0], refs[1]
        pos = 2
        bias_ref = resid_ref = None
        if bias is not None:
            bias_ref = refs[pos]
            pos += 1
        if resid is not None:
            resid_ref = refs[pos]
            pos += 1
        o_ref = refs[pos]
        acc_ref = refs[pos + 1] if nk > 1 else None

        def finish(acc):
            if bias_ref is not None:
                acc = acc + bias_ref[...]
            if resid_ref is not None:
                acc = acc + resid_scale * resid_ref[...]
            o_ref[...] = acc.astype(out_dtype)

        p = lax.dot_general(a_ref[...], b_ref[...], dn, preferred_element_type=F32)
        if nk == 1:
            finish(p)
        else:
            kk = pl.program_id(2)

            @pl.when(kk == 0)
            def _():
                acc_ref[...] = p

            @pl.when(kk > 0)
            def _():
                acc_ref[...] += p

            @pl.when(kk == nk - 1)
            def _():
                finish(acc_ref[...])

    a_spec = (pl.BlockSpec((tk, tm), lambda i, j, kk: (kk, i)) if ta
              else pl.BlockSpec((tm, tk), lambda i, j, kk: (i, kk)))
    b_spec = (pl.BlockSpec((tn, tk), lambda i, j, kk: (j, kk)) if tb
              else pl.BlockSpec((tk, tn), lambda i, j, kk: (kk, j)))
    in_specs = [a_spec, b_spec]
    args = [a, b]
    if bias is not None:
        in_specs.append(pl.BlockSpec((1, tn), lambda i, j, kk: (0, j)))
        args.append(bias)
    if resid is not None:
        in_specs.append(pl.BlockSpec((tm, tn), lambda i, j, kk: (i, j)))
        args.append(resid)
    return pl.pallas_call(
        body, name=name,
        out_shape=jax.ShapeDtypeStruct((m, n), out_dtype),
        grid=(m // tm, n // tn, nk),
        in_specs=in_specs,
        out_specs=pl.BlockSpec((tm, tn), lambda i, j, kk: (i, j)),
        scratch_shapes=[pltpu.VMEM((tm, tn), F32)] if nk > 1 else [],
        compiler_params=_cparams(("parallel", "parallel", "arbitrary")),
    )(*args)


class _Dims:
    def __init__(self, s, d, c, depth):
        self.s, self.d, self.c, self.depth = s, d, c, depth
        self.a = d - c
        self.nh = self.a // HEAD_DIM
        self.group = self.nh // N_KV
        self.din = 3 * c + 2 * self.a + 2 * KV_W
        self.o_q = 3 * c
        self.o_k = 3 * c + self.a
        self.o_v = self.o_k + KV_W
        self.o_ag = self.o_k + 2 * KV_W
        self.alpha = (2 * depth) ** 0.25
        assert self.nh % 2 == 0 and self.group % 2 == 0 and self.o_k % (2 * KV_W) == 0
        assert c % LANES == 0 and self.a % LANES == 0


def _conv_taps(w_ref, x_ref, first_row, rows, reverse):
    acc = None
    for j in range(CONV_WIDTH):
        off = first_row - j if reverse else first_row + j
        term = w_ref[j:j + 1, :] * x_ref[pl.ds(off, rows), :]
        acc = term if acc is None else acc + term
    return acc


def _kv_operands(kv_ref, lo):
    kext = kv_ref[:, 0:KV_W]
    vext = kv_ref[:, KV_W:2 * KV_W]
    ksw = pltpu.roll(kext, HEAD_DIM, 1)
    vsw = pltpu.roll(vext, HEAD_DIM, 1)
    zero = jnp.zeros_like(kext)
    k2 = [jnp.where(lo, kext, ksw).astype(BF16), jnp.where(lo, ksw, kext).astype(BF16)]
    khalf = [[jnp.where(lo, kext, zero).astype(BF16), jnp.where(lo, zero, ksw).astype(BF16)],
             [jnp.where(lo, ksw, zero).astype(BF16), jnp.where(lo, zero, kext).astype(BF16)]]
    vhalf = [[jnp.where(lo, vext, zero).astype(BF16), jnp.where(lo, zero, vsw).astype(BF16)],
             [jnp.where(lo, vsw, zero).astype(BF16), jnp.where(lo, zero, vext).astype(BF16)]]
    return k2, khalf, vhalf


def _valid_mask(global_block):
    qi = lax.broadcasted_iota(jnp.int32, (WINDOW, 2 * WINDOW), 0)
    si = lax.broadcasted_iota(jnp.int32, (WINDOW, 2 * WINDOW), 1)
    diff = qi + WINDOW - si
    first_key = jnp.where(global_block == 0, WINDOW, 0)
    return (diff >= 0) & (diff < WINDOW) & (si >= first_key)


def _softmax_with_sink(qm, k2rows, valid, sink):
    s = lax.dot_general(qm, k2rows, (((1,), (1,)), ((), ())), preferred_element_type=F32)
    s = s * (HEAD_DIM ** -0.5)
    s = jnp.where(valid, s, NEG_INF)
    m = jnp.maximum(jnp.max(s, axis=1, keepdims=True), sink)
    e = jnp.exp(s - m)
    den = jnp.sum(e, axis=1, keepdims=True) + jnp.exp(sink - m)
    inv = 1.0 / den
    return e * inv, m, inv


def _mixer_specs(dm, t, idx):
    return [
        pl.BlockSpec((t, dm.din), lambda g: (idx(g), 0)),
        pl.BlockSpec((HALO, 2 * dm.c), lambda g: (jnp.maximum(idx(g) * (t // HALO) - 1, 0), 0)),
        pl.BlockSpec((WINDOW, 2 * KV_W),
                     lambda g: (jnp.maximum(idx(g) * (t // WINDOW) - 1, 0), dm.o_k // (2 * KV_W))),
    ]


def _mixer_fwd(dm, proj, conv_w, conv_b, cln_g, cln_b, sinks, t):
    c, nq = dm.c, t // WINDOW

    def body(sinks_ref, pr_ref, ch_ref, kvh_ref, cw_ref, cb_ref, cg_ref, cbb_ref,
             y_ref, hext_ref, kv_ref):
        i = pl.program_id(0)
        first = i == 0
        h = pr_ref[:, 0:c] * _sigmoid(pr_ref[:, c:2 * c])
        hh = ch_ref[:, 0:c] * _sigmoid(ch_ref[:, c:2 * c])
        hext_ref[0:HALO, :] = jnp.where(first, 0.0, hh)
        hext_ref[HALO:HALO + t, :] = h
        conv = _conv_taps(cw_ref, hext_ref, HALO - (CONV_WIDTH - 1), t, False) + cb_ref[...]
        mu = jnp.mean(conv, axis=1, keepdims=True)
        dlt = conv - mu
        var = jnp.mean(dlt * dlt, axis=1, keepdims=True)
        u = dlt * lax.rsqrt(var + LN_EPS) * cg_ref[...] + cbb_ref[...]
        gate = pr_ref[:, 2 * c:3 * c]
        y_ref[:, 0:c] = (u * _sigmoid(u) * (gate * _sigmoid(gate))).astype(BF16)

        kv_ref[0:WINDOW, :] = jnp.where(first, 0.0, kvh_ref[...])
        kv_ref[WINDOW:WINDOW + t, :] = pr_ref[:, dm.o_k:dm.o_k + 2 * KV_W]
        lo = lax.broadcasted_iota(jnp.int32, (1, LANES), 1) < HEAD_DIM
        k2, _, vhalf = _kv_operands(kv_ref, lo)
        for qb in range(nq):
            r0 = qb * WINDOW
            valid = _valid_mask(i * nq + qb)
            for p in range(dm.nh // 2):
                kvh = (2 * p) // dm.group
                qp = pr_ref[r0:r0 + WINDOW, dm.o_q + LANES * p:dm.o_q + LANES * (p + 1)]
                opair = None
                for half in range(2):
                    qm = jnp.where(lo if half == 0 else jnp.logical_not(lo), qp, 0.0).astype(BF16)
                    prob, _, _ = _softmax_with_sink(qm, k2[kvh][r0:r0 + 2 * WINDOW], valid,
                                                    sinks_ref[2 * p + half])
                    o_h = jnp.dot(prob.astype(BF16), vhalf[kvh][half][r0:r0 + 2 * WINDOW],
                                  preferred_element_type=F32)
                    opair = o_h if opair is None else opair + o_h
                ag = pr_ref[r0:r0 + WINDOW, dm.o_ag + LANES * p:dm.o_ag + LANES * (p + 1)]
                y_ref[r0:r0 + WINDOW, c + LANES * p:c + LANES * (p + 1)] = (
                    opair * (ag * _sigmoid(ag))).astype(BF16)

    vec = pl.BlockSpec((1, c), lambda g: (0, 0))
    return pl.pallas_call(
        body, name="mixer_fwd",
        out_shape=jax.ShapeDtypeStruct((dm.s, dm.d), BF16),
        grid=(dm.s // t,),
        in_specs=[pl.BlockSpec(memory_space=pltpu.SMEM)] + _mixer_specs(dm, t, lambda g: g)
        + [pl.BlockSpec((CONV_ROWS, c), lambda g: (0, 0)), vec, vec, vec],
        out_specs=pl.BlockSpec((t, dm.d), lambda g: (g, 0)),
        scratch_shapes=[pltpu.VMEM((t + HALO, c), F32), pltpu.VMEM((t + WINDOW, 2 * KV_W), F32)],
        compiler_params=_cparams(("arbitrary",)),
    )(sinks, proj, proj, proj, conv_w, conv_b, cln_g, cln_b)


def _mixer_bwd(dm, proj, dymix, conv_w, conv_b, cln_g, cln_b, sinks, t):
    c, nq, nt = dm.c, t // WINDOW, dm.s // t

    def body(sinks_ref, pr_ref, ch_ref, kvh_ref, dy_ref, cw_ref, cb_ref, cg_ref, cbb_ref,
             dpr_ref, dcw_ref, dsm_ref, dbin_ref, dsk_ref,
             hext_ref, kv_ref, dcx_ref, dkv_ref, carry_dc_ref, carry_kv_ref):
        g = pl.program_id(0)
        i = nt - 1 - g
        first = i == 0

        @pl.when(g == 0)
        def _():
            dcw_ref[...] = jnp.zeros_like(dcw_ref)
            dsm_ref[...] = jnp.zeros_like(dsm_ref)
            dbin_ref[...] = jnp.zeros_like(dbin_ref)
            dsk_ref[...] = jnp.zeros_like(dsk_ref)
            carry_dc_ref[...] = jnp.zeros_like(carry_dc_ref)
            carry_kv_ref[...] = jnp.zeros_like(carry_kv_ref)

        def emit(col, width, val):
            dpr_ref[:, col:col + width] = val.astype(BF16)
            dbin_ref[0:1, col:col + width] += jnp.sum(val, axis=0, keepdims=True)

        val = pr_ref[:, 0:c]
        sg = _sigmoid(pr_ref[:, c:2 * c])
        h = val * sg
        hh = ch_ref[:, 0:c] * _sigmoid(ch_ref[:, c:2 * c])
        hext_ref[0:HALO, :] = jnp.where(first, 0.0, hh)
        hext_ref[HALO:HALO + t, :] = h
        conv = _conv_taps(cw_ref, hext_ref, HALO - (CONV_WIDTH - 1), t, False) + cb_ref[...]
        mu = jnp.mean(conv, axis=1, keepdims=True)
        dlt = conv - mu
        var = jnp.mean(dlt * dlt, axis=1, keepdims=True)
        rstd = lax.rsqrt(var + LN_EPS)
        xhat = dlt * rstd
        u = xhat * cg_ref[...] + cbb_ref[...]
        su = _sigmoid(u)
        gate = pr_ref[:, 2 * c:3 * c]
        sgate = _sigmoid(gate)
        dyc = dy_ref[:, 0:c]
        emit(2 * c, c, dyc * (u * su) * _dsilu(gate, sgate))
        du = dyc * (gate * sgate) * _dsilu(u, su)
        dsm_ref[1:2, :] += jnp.sum(du * xhat, axis=0, keepdims=True)
        dsm_ref[2:3, :] += jnp.sum(du, axis=0, keepdims=True)
        dxh = du * cg_ref[...]
        dconv = rstd * (dxh - jnp.mean(dxh, axis=1, keepdims=True)
                        - xhat * jnp.mean(dxh * xhat, axis=1, keepdims=True))
        dsm_ref[0:1, :] += jnp.sum(dconv, axis=0, keepdims=True)
        dcx_ref[0:t, :] = dconv
        dcx_ref[t:t + HALO, :] = carry_dc_ref[...]
        carry_dc_ref[...] = dconv[0:HALO, :]
        dh = _conv_taps(cw_ref, dcx_ref, CONV_WIDTH - 1, t, True)
        for j in range(CONV_WIDTH):
            shifted = hext_ref[pl.ds(HALO - (CONV_WIDTH - 1) + j, t), :]
            dcw_ref[j:j + 1, :] += jnp.sum(shifted * dconv, axis=0, keepdims=True)
        emit(0, c, dh * sg)
        emit(c, c, dh * val * sg * (1.0 - sg))

        kv_ref[0:WINDOW, :] = jnp.where(first, 0.0, kvh_ref[...])
        kv_ref[WINDOW:WINDOW + t, :] = pr_ref[:, dm.o_k:dm.o_k + 2 * KV_W]
        dkv_ref[0:t, :] = jnp.zeros((t, 2 * KV_W), F32)
        dkv_ref[t:t + WINDOW, :] = carry_kv_ref[...]
        lane = lax.broadcasted_iota(jnp.int32, (1, LANES), 1)
        lo = lane < HEAD_DIM
        k2, khalf, vhalf = _kv_operands(kv_ref, lo)
        tn_dims = (((0,), (0,)), ((), ()))
        nt_dims = (((1,), (1,)), ((), ()))
        dsk = jnp.zeros((1, LANES), F32)
        for qb in range(nq):
            r0 = qb * WINDOW
            rows = slice(r0, r0 + 2 * WINDOW)
            valid = _valid_mask(i * nq + qb)
            dka = [None, None]
            dva = [None, None]
            for p in range(dm.nh // 2):
                kvh = (2 * p) // dm.group
                cols = slice(LANES * p, LANES * (p + 1))
                qp = pr_ref[r0:r0 + WINDOW, dm.o_q + cols.start:dm.o_q + cols.stop]
                ag = pr_ref[r0:r0 + WINDOW, dm.o_ag + cols.start:dm.o_ag + cols.stop]
                dya = dy_ref[r0:r0 + WINDOW, c + cols.start:c + cols.stop]
                sag = _sigmoid(ag)
                d_o = dya * (ag * sag)
                d_o_b = d_o.astype(BF16)
                opair = None
                dqpair = None
                for half in range(2):
                    hmask = lo if half == 0 else jnp.logical_not(lo)
                    sink = sinks_ref[2 * p + half]
                    qm = jnp.where(hmask, qp, 0.0).astype(BF16)
                    prob, m, inv = _softmax_with_sink(qm, k2[kvh][rows], valid, sink)
                    pb = prob.astype(BF16)
                    vh = vhalf[kvh][half][rows]
                    o_h = jnp.dot(pb, vh, preferred_element_type=F32)
                    opair = o_h if opair is None else opair + o_h
                    delta = jnp.sum(d_o * o_h, axis=1, keepdims=True)
                    dp = lax.dot_general(d_o_b, vh, nt_dims, preferred_element_type=F32)
                    ds = prob * (dp - delta)
                    dsink = -jnp.sum(jnp.exp(sink - m) * inv * delta)
                    dsk = dsk + jnp.where(lane == 2 * p + half, dsink, 0.0)
                    dsb = (ds * (HEAD_DIM ** -0.5)).astype(BF16)
                    dq_h = jnp.dot(dsb, khalf[kvh][half][rows], preferred_element_type=F32)
                    dqpair = dq_h if dqpair is None else dqpair + dq_h
                    dk_h = lax.dot_general(dsb, qm, tn_dims, preferred_element_type=F32)
                    dv_h = lax.dot_general(pb, jnp.where(hmask, d_o, 0.0).astype(BF16), tn_dims,
                                           preferred_element_type=F32)
                    dka[kvh] = dk_h if dka[kvh] is None else dka[kvh] + dk_h
                    dva[kvh] = dv_h if dva[kvh] is None else dva[kvh] + dv_h
                d_ag = dya * opair * _dsilu(ag, sag)
                dpr_ref[r0:r0 + WINDOW, dm.o_q + cols.start:dm.o_q + cols.stop] = dqpair.astype(BF16)
                dbin_ref[0:1, dm.o_q + cols.start:dm.o_q + cols.stop] += jnp.sum(
                    dqpair, axis=0, keepdims=True)
                dpr_ref[r0:r0 + WINDOW, dm.o_ag + cols.start:dm.o_ag + cols.stop] = d_ag.astype(BF16)
                dbin_ref[0:1, dm.o_ag + cols.start:dm.o_ag + cols.stop] += jnp.sum(
                    d_ag, axis=0, keepdims=True)
            fold = [x + pltpu.roll(x, HEAD_DIM, 1) for x in (dka[0], dka[1], dva[0], dva[1])]
            dkv_ref[r0:r0 + 2 * WINDOW, 0:KV_W] += jnp.where(lo, fold[0], fold[1])
            dkv_ref[r0:r0 + 2 * WINDOW, KV_W:2 * KV_W] += jnp.where(lo, fold[2], fold[3])
        dsk_ref[0:1, :] += dsk
        carry_kv_ref[...] = dkv_ref[0:WINDOW, :]
        emit(dm.o_k, 2 * KV_W, dkv_ref[WINDOW:WINDOW + t, :])

    rev = lambda g: nt - 1 - g
    vec = pl.BlockSpec((1, c), lambda g: (0, 0))
    const = lambda shape: pl.BlockSpec(shape, lambda g: (0, 0))
    return pl.pallas_call(
        body, name="mixer_bwd",
        out_shape=(jax.ShapeDtypeStruct((dm.s, dm.din), BF16),
                   jax.ShapeDtypeStruct((CONV_ROWS, c), F32),
                   jax.ShapeDtypeStruct((8, c), F32),
                   jax.ShapeDtypeStruct((8, dm.din), F32),
                   jax.ShapeDtypeStruct((8, LANES), F32)),
        grid=(nt,),
        in_specs=[pl.BlockSpec(memory_space=pltpu.SMEM)] + _mixer_specs(dm, t, rev)
        + [pl.BlockSpec((t, dm.d), lambda g: (rev(g), 0)),
           pl.BlockSpec((CONV_ROWS, c), lambda g: (0, 0)), vec, vec, vec],
        out_specs=(pl.BlockSpec((t, dm.din), lambda g: (rev(g), 0)),
                   const((CONV_ROWS, c)), const((8, c)), const((8, dm.din)), const((8, LANES))),
        scratch_shapes=[pltpu.VMEM((t + HALO, c), F32), pltpu.VMEM((t + WINDOW, 2 * KV_W), F32),
                        pltpu.VMEM((t + HALO, c), F32), pltpu.VMEM((t + WINDOW, 2 * KV_W), F32),
                        pltpu.VMEM((HALO, c), F32), pltpu.VMEM((WINDOW, 2 * KV_W), F32)],
        compiler_params=_cparams(("arbitrary",)),
    )(sinks, proj, proj, proj, dymix, conv_w, conv_b, cln_g, cln_b)


def _outproj_ln(dm, ymix, w_out, b_out, x, ln_g, ln_b, target, tm):
    last = target is not None
    d = dm.d

    def body(*refs):
        y_ref, w_ref, bo_ref, x_ref, g_ref, b_ref = refs[:6]
        z = dm.alpha * x_ref[...] + (
            jnp.dot(y_ref[...], w_ref[...], preferred_element_type=F32) + bo_ref[...])
        mu = jnp.mean(z, axis=1, keepdims=True)
        dlt = z - mu
        var = jnp.mean(dlt * dlt, axis=1, keepdims=True)
        out = dlt * lax.rsqrt(var + LN_EPS) * g_ref[...] + b_ref[...]
        if last:
            t_ref, z_ref, dout_ref, loss_ref = refs[6:]
            z_ref[...] = z
            err = out - t_ref[...]
            dout_ref[...] = err * (1.0 / d)

            @pl.when(pl.program_id(0) == 0)
            def _():
                loss_ref[...] = jnp.zeros_like(loss_ref)

            loss_ref[...] += 0.5 * jnp.sum(jnp.mean(err * err, axis=1, keepdims=True))
        else:
            z_ref, o_ref, ob_ref = refs[6:]
            z_ref[...] = z
            o_ref[...] = out
            ob_ref[...] = out.astype(BF16)

    row = pl.BlockSpec((tm, d), lambda i: (i, 0))
    vec = pl.BlockSpec((1, d), lambda i: (0, 0))
    in_specs = [row, pl.BlockSpec((d, d), lambda i: (0, 0)), vec, row, vec, vec]
    args = [ymix, w_out, b_out, x, ln_g, ln_b]
    act = jax.ShapeDtypeStruct((dm.s, d), F32)
    if last:
        in_specs.append(row)
        args.append(target)
        out_shape = (act, act, jax.ShapeDtypeStruct((8, LANES), F32))
        out_specs = (row, row, pl.BlockSpec((8, LANES), lambda i: (0, 0)))
    else:
        out_shape = (act, act, jax.ShapeDtypeStruct((dm.s, d), BF16))
        out_specs = (row, row, row)
    return pl.pallas_call(
        body, name="outproj_ln_loss" if last else "outproj_ln",
        out_shape=out_shape, grid=(dm.s // tm,), in_specs=in_specs, out_specs=out_specs,
        compiler_params=_cparams(("arbitrary",)),
    )(*args)


def _ln_bwd(dm, dout, z, ln_g, tm):
    d = dm.d

    def body(do_ref, z_ref, g_ref, dz_ref, dzb_ref, sm_ref):
        @pl.when(pl.program_id(0) == 0)
        def _():
            sm_ref[...] = jnp.zeros_like(sm_ref)

        z = z_ref[...]
        mu = jnp.mean(z, axis=1, keepdims=True)
        dlt = z - mu
        var = jnp.mean(dlt * dlt, axis=1, keepdims=True)
        rstd = lax.rsqrt(var + LN_EPS)
        zhat = dlt * rstd
        do = do_ref[...]
        dzh = do * g_ref[...]
        dz = rstd * (dzh - jnp.mean(dzh, axis=1, keepdims=True)
                     - zhat * jnp.mean(dzh * zhat, axis=1, keepdims=True))
        dz_ref[...] = dz
        dzb_ref[...] = dz.astype(BF16)
        sm_ref[0:1, :] += jnp.sum(do * zhat, axis=0, keepdims=True)
        sm_ref[1:2, :] += jnp.sum(do, axis=0, keepdims=True)
        sm_ref[2:3, :] += jnp.sum(dz, axis=0, keepdims=True)

    row = pl.BlockSpec((tm, d), lambda i: (i, 0))
    return pl.pallas_call(
        body, name="ln_bwd",
        out_shape=(jax.ShapeDtypeStruct((dm.s, d), F32), jax.ShapeDtypeStruct((dm.s, d), BF16),
                   jax.ShapeDtypeStruct((8, d), F32)),
        grid=(dm.s // tm,),
        in_specs=[row, row, pl.BlockSpec((1, d), lambda i: (0, 0))],
        out_specs=(row, row, pl.BlockSpec((8, d), lambda i: (0, 0))),
        compiler_params=_cparams(("arbitrary",)),
    )(dout, z, ln_g)


_ANY = pl.BlockSpec(memory_space=pl.ANY)


def _position():
    return lax.axis_index("x"), lax.axis_index("y"), lax.axis_index("c")


def _allgather(shards):
    n = len(shards)

    def body(*refs):
        ins, outs = refs[:n], refs[n:2 * n]
        send_sems, recv_sems, local_sems = refs[2 * n:]
        x, y, c = _position()
        me, sibling = (x, y, c), (x, y, 1 - c)
        chips = [(1 - x, y), (x, 1 - y), (1 - x, 1 - y)]

        def copy(t, k, block, to, src=None):
            px, py, pc = block
            dst = outs[t].at[4 * px + 2 * py + pc]
            return pltpu.make_async_remote_copy(
                src_ref=dst if src is None else src, dst_ref=dst,
                send_sem=send_sems.at[t, k], recv_sem=recv_sems.at[t, k],
                device_id=to, device_id_type=MESH)

        mine = [pltpu.make_async_copy(ins[t], outs[t].at[4 * x + 2 * y + c], local_sems.at[t])
                for t in range(n)]
        for cp in mine:
            cp.start()
        first = []
        for t in range(n):
            first.append(copy(t, 0, me, sibling, src=ins[t]))
            first += [copy(t, 1 + j, me, (*chip, c), src=ins[t]) for j, chip in enumerate(chips)]
        for cp in first:
            cp.start()
        passed = []
        for j, chip in enumerate(chips):
            for t in range(n):
                copy(t, 1 + j, (*chip, c), me).wait_recv()
                cp = copy(t, 4 + j, (*chip, c), sibling)
                cp.start()
                passed.append(cp)
        for t in range(n):
            copy(t, 0, sibling, me).wait_recv()
            for j, chip in enumerate(chips):
                copy(t, 4 + j, (*chip, 1 - c), me).wait_recv()
        for cp in first + passed:
            cp.wait_send()
        for cp in mine:
            cp.wait()

    return pl.pallas_call(
        body, name="weights_allgather",
        out_shape=[jax.ShapeDtypeStruct((N_DEV,) + s.shape, s.dtype) for s in shards],
        in_specs=[_ANY] * n, out_specs=[_ANY] * n,
        scratch_shapes=[pltpu.SemaphoreType.DMA((n, 7)), pltpu.SemaphoreType.DMA((n, 7)),
                        pltpu.SemaphoreType.DMA((n,))],
    )(*shards)


def _sibling_exchange(parts):
    n = len(parts)

    def body(*refs):
        ins, outs = refs[:n], refs[n:2 * n]
        send_sems, recv_sems = refs[2 * n:]
        x, y, c = _position()
        copies = [pltpu.make_async_remote_copy(
            src_ref=ins[t].at[:, 1 - c], dst_ref=outs[t],
            send_sem=send_sems.at[t], recv_sem=recv_sems.at[t],
            device_id=(x, y, 1 - c), device_id_type=MESH) for t in range(n)]
        for cp in copies:
            cp.start()
        for cp in copies:
            cp.wait()

    return pl.pallas_call(
        body, name="grad_sibling_exchange",
        out_shape=[jax.ShapeDtypeStruct((p.shape[0],) + p.shape[2:], p.dtype) for p in parts],
        in_specs=[_ANY] * n, out_specs=[_ANY] * n,
        scratch_shapes=[pltpu.SemaphoreType.DMA((n,)), pltpu.SemaphoreType.DMA((n,))],
    )(*parts)


def _chip_exchange(blocks):
    n = len(blocks)

    def body(*refs):
        ins, outs = refs[:n], refs[n:2 * n]
        send_sems, recv_sems = refs[2 * n:]
        x, y, c = _position()
        my_chip = 2 * x + y
        chips = [(1 - x, y), (x, 1 - y), (1 - x, 1 - y)]
        copies = []
        for t in range(n):
            for j, (px, py) in enumerate(chips):
                copies.append(pltpu.make_async_remote_copy(
                    src_ref=ins[t].at[2 * px + py], dst_ref=outs[t].at[my_chip],
                    send_sem=send_sems.at[t, j], recv_sem=recv_sems.at[t, j],
                    device_id=(px, py, c), device_id_type=MESH))
        for cp in copies:
            cp.start()
        for t in range(n):
            for j, (px, py) in enumerate(chips):
                pltpu.make_async_remote_copy(
                    src_ref=ins[t].at[my_chip], dst_ref=outs[t].at[2 * px + py],
                    send_sem=send_sems.at[t, j], recv_sem=recv_sems.at[t, j],
                    device_id=(px, py, c), device_id_type=MESH).wait_recv()
        for cp in copies:
            cp.wait_send()

    return pl.pallas_call(
        body, name="grad_chip_exchange",
        out_shape=[jax.ShapeDtypeStruct(b.shape, b.dtype) for b in blocks],
        in_specs=[_ANY] * n, out_specs=[_ANY] * n,
        scratch_shapes=[pltpu.SemaphoreType.DMA((n, 3)), pltpu.SemaphoreType.DMA((n, 3))],
    )(*blocks)


def _pair_sum(part, got, tr):
    _, _, r, w = part.shape

    def body(c_ref, p_ref, g_ref, o_ref):
        o_ref[...] = (p_ref[...] + g_ref[...]).astype(BF16)

    return pl.pallas_call(
        body, name="grad_pair_sum",
        out_shape=jax.ShapeDtypeStruct((N_CHIPS, r, w), BF16),
        grid_spec=pltpu.PrefetchScalarGridSpec(
            num_scalar_prefetch=1, grid=(N_CHIPS, r // tr),
            in_specs=[pl.BlockSpec((None, None, tr, w), lambda k, i, c_ref: (k, c_ref[0], i, 0)),
                      pl.BlockSpec((None, tr, w), lambda k, i, c_ref: (k, i, 0))],
            out_specs=pl.BlockSpec((None, tr, w), lambda k, i, c_ref: (k, i, 0))),
        compiler_params=_cparams(("parallel", "parallel")),
    )(lax.axis_index("c").reshape(1).astype(jnp.int32), part, got)


def _final_sum(part, got, recv, tr):
    _, _, r, w = part.shape

    def body(idx_ref, p_ref, g_ref, r0_ref, r1_ref, r2_ref, o_ref):
        acc = p_ref[...] + g_ref[...]
        for ref in (r0_ref, r1_ref, r2_ref):
            acc = acc + ref[...].astype(F32)
        o_ref[...] = acc

    x, y, c = _position()
    idx = jnp.stack([c, 2 * x + y, 2 * (1 - x) + y, 2 * x + (1 - y),
                     2 * (1 - x) + (1 - y)]).astype(jnp.int32)
    other = lambda j: pl.BlockSpec((None, tr, w), lambda i, s: (s[2 + j], i, 0))
    return pl.pallas_call(
        body, name="grad_final_sum",
        out_shape=jax.ShapeDtypeStruct((r, w), F32),
        grid_spec=pltpu.PrefetchScalarGridSpec(
            num_scalar_prefetch=1, grid=(r // tr,),
            in_specs=[pl.BlockSpec((None, None, tr, w), lambda i, s: (s[1], s[0], i, 0)),
                      pl.BlockSpec((None, tr, w), lambda i, s: (s[1], i, 0)),
                      other(0), other(1), other(2)],
            out_specs=pl.BlockSpec((tr, w), lambda i, s: (i, 0))),
        compiler_params=_cparams(("parallel",)),
    )(idx, part, got, recv, recv, recv)


def _reduce_scatter(parts, row_tiles):
    parts = [p.reshape((N_CHIPS, 2) + p.shape[1:]) for p in parts]
    got = _sibling_exchange(parts)
    sums = [_pair_sum(p, g, tr) for p, g, tr in zip(parts, got, row_tiles)]
    recv = _chip_exchange(sums)
    return [_final_sum(p, g, r, tr) for p, g, r, tr in zip(parts, got, recv, row_tiles)]


def _adamw_math(w, g, m, v):
    m = ADAM_B1 * m + (1.0 - ADAM_B1) * g
    v = ADAM_B2 * v + (1.0 - ADAM_B2) * (g * g)
    m_hat = m / (1.0 - ADAM_B1 ** ADAM_STEP)
    v_hat = v / (1.0 - ADAM_B2 ** ADAM_STEP)
    delta = -ADAM_LR * (m_hat / (jnp.sqrt(v_hat) + ADAM_EPS) + ADAM_WD * w)
    return delta, m, v


def _adamw(w, g, m, v, tr):
    depth, r, width = w.shape

    def body(w_ref, g_ref, m_ref, v_ref, d_ref, nm_ref, nv_ref):
        d_ref[...], nm_ref[...], nv_ref[...] = _adamw_math(
            w_ref[...], g_ref[...], m_ref[...], v_ref[...])

    spec = pl.BlockSpec((None, tr, width), lambda l, i: (l, i, 0))
    shape = jax.ShapeDtypeStruct(w.shape, F32)
    return pl.pallas_call(
        body, name="adamw", out_shape=(shape, shape, shape), grid=(depth, r // tr),
        in_specs=[spec] * 4, out_specs=(spec, spec, spec),
        compiler_params=_cparams(("parallel", "parallel")),
    )(w, g, m, v)


def _small_allreduce_adamw(partial, w, m, v):
    rows = partial.shape[0]
    prows = w.shape[0]

    def body(p_ref, w_ref, m_ref, v_ref, tot_ref, d_ref, nm_ref, nv_ref, gath_ref,
             send_sems, recv_sems):
        x, y, c = _position()
        me = 4 * x + 2 * y + c
        copies = []
        for k in range(1, N_DEV):
            px = 1 - x if k & 4 else x
            py = 1 - y if k & 2 else y
            pc = 1 - c if k & 1 else c
            copies.append((pltpu.make_async_remote_copy(
                src_ref=p_ref, dst_ref=gath_ref.at[me],
                send_sem=send_sems.at[k - 1], recv_sem=recv_sems.at[k - 1],
                device_id=(px, py, pc), device_id_type=MESH), 4 * px + 2 * py + pc))
        for cp, _ in copies:
            cp.start()
        gath_ref[me] = p_ref[...]
        for k, (cp, peer) in enumerate(copies):
            pltpu.make_async_remote_copy(
                src_ref=p_ref, dst_ref=gath_ref.at[peer],
                send_sem=send_sems.at[k], recv_sem=recv_sems.at[k],
                device_id=(x, y, c), device_id_type=MESH).wait_recv()
        for cp, _ in copies:
            cp.wait_send()
        tot = gath_ref[0]
        for d in range(1, N_DEV):
            tot = tot + gath_ref[d]
        tot_ref[...] = tot
        d_ref[...], nm_ref[...], nv_ref[...] = _adamw_math(
            w_ref[...], tot[0:prows, :], m_ref[...], v_ref[...])

    vm = pl.BlockSpec(memory_space=pltpu.VMEM)
    pshape = jax.ShapeDtypeStruct(w.shape, F32)
    return pl.pallas_call(
        body, name="small_allreduce_adamw",
        out_shape=(jax.ShapeDtypeStruct(partial.shape, F32), pshape, pshape, pshape),
        in_specs=[vm] * 4, out_specs=(vm, vm, vm, vm),
        scratch_shapes=[pltpu.VMEM((N_DEV, rows, LANES), F32),
                        pltpu.SemaphoreType.DMA((N_DEV - 1,)), pltpu.SemaphoreType.DMA((N_DEV - 1,))],
        compiler_params=pltpu.CompilerParams(vmem_limit_bytes=VMEM_LIMIT),
    )(partial, w, m, v)


def _pack_rows(vec):
    depth, n = vec.shape
    rows = -(-n // LANES)
    rows = -(-rows // 8) * 8
    return jnp.pad(vec, ((0, 0), (0, rows * LANES - n))).reshape(depth, rows, LANES)


def _pack_small(named):
    blocks = [_pack_rows(a) for a in named]
    extents = [(b.shape[1], a.shape[1]) for b, a in zip(blocks, named)]
    depth = named[0].shape[0]
    packed = jnp.concatenate(blocks, axis=1).reshape(depth * sum(r for r, _ in extents), LANES)
    return packed, extents


def _unpack_small(packed, extents, depth):
    per_layer = sum(r for r, _ in extents)
    packed = packed.reshape(depth, per_layer, LANES)
    out, r0 = [], 0
    for rows, n in extents:
        out.append(packed[:, r0:r0 + rows, :].reshape(depth, rows * LANES)[:, :n])
        r0 += rows
    return out


def kernel(x, w_in, b_in, conv_w, conv_b, conv_ln_g, conv_ln_b, sinks, w_out, b_out, ln_g, ln_b, loss_target, m_w_in, m_b_in, m_conv_w, m_conv_b, m_conv_ln_g, m_conv_ln_b, m_sinks, m_w_out, m_b_out, m_ln_g, m_ln_b, v_w_in, v_b_in, v_conv_w, v_conv_b, v_conv_ln_g, v_conv_ln_b, v_sinks, v_w_out, v_b_out, v_ln_g, v_ln_b):
    depth, d, din_shard = w_in.shape
    s = x.shape[1]
    c_shard = conv_w.shape[2]
    dm = _Dims(s, d, N_DEV * c_shard, depth)
    assert dm.din == N_DEV * din_shard and x.shape[0] == 1 and sinks.shape[1] == dm.nh
    d_shard = w_out.shape[1]
    c, din = dm.c, dm.din

    t_mix = _tile(s, 256, WINDOW)
    tm_row = _tile(s, 256, 8)
    tm_big = _tile(s, 1024, 8)

    conv_w_pad = jnp.pad(conv_w, ((0, 0), (0, CONV_ROWS - CONV_WIDTH), (0, 0)))
    g_in, g_out, g_conv = _allgather([w_in.astype(BF16), w_out.astype(BF16), conv_w_pad])
    w_in_full = g_in.transpose(1, 2, 0, 3).reshape(depth, d, din)
    w_out_full = g_out.transpose(1, 0, 2, 3).reshape(depth, d, d)
    conv_w_full = g_conv.transpose(1, 2, 0, 3).reshape(depth, CONV_ROWS, c)

    xs = x[0]
    xb = xs.astype(BF16)
    saved = []
    loss_part = dout = None
    for l in range(depth):
        proj = _matmul(xb, w_in_full[l], tm=tm_big, tn=_tile(din, 768), tk=d, out_dtype=F32,
                       name="in_proj", bias=b_in[l][None, :])
        ymix = _mixer_fwd(dm, proj, conv_w_full[l], conv_b[l][None, :], conv_ln_g[l][None, :],
                          conv_ln_b[l][None, :], sinks[l], t_mix)
        target = loss_target[0] if l == depth - 1 else None
        res = _outproj_ln(dm, ymix, w_out_full[l], b_out[l][None, :], xs, ln_g[l][None, :],
                          ln_b[l][None, :], target, tm_row)
        saved.append((xb, proj, ymix, res[0]))
        if l == depth - 1:
            dout, loss_part = res[1], res[2]
        else:
            xs, xb = res[1], res[2]

    g_w_in, g_w_out = [None] * depth, [None] * depth
    small_parts = [None] * depth
    dconv_w = [None] * depth
    for l in reversed(range(depth)):
        xb, proj, ymix, z = saved[l]
        dz, dzb, ln_small = _ln_bwd(dm, dout, z, ln_g[l][None, :], tm_row)
        dymix = _matmul(dzb, w_out_full[l], tb=True, tm=tm_big, tn=_tile(d, 1024), tk=d,
                        out_dtype=F32, name="dymix")
        dw_out = _matmul(ymix, dzb, ta=True, tm=_tile(d, 1024), tn=_tile(d, 1024),
                         tk=_tile(s, 2048), out_dtype=F32, name="dw_out")
        dproj, dcw, conv_small, dbin, dsk = _mixer_bwd(
            dm, proj, dymix, conv_w_full[l], conv_b[l][None, :], conv_ln_g[l][None, :],
            conv_ln_b[l][None, :], sinks[l], t_mix)
        dout = _matmul(dproj, w_in_full[l], tb=True, tm=_tile(s, 512), tn=_tile(d, 1024),
                       tk=_tile(din, 2688), out_dtype=F32, name="dx", resid=dz,
                       resid_scale=dm.alpha)
        dw_in = _matmul(xb, dproj, ta=True, tm=_tile(d, 1024), tn=_tile(din, 768),
                        tk=_tile(s, 2048), out_dtype=F32, name="dw_in")
        dw_in_by_dev = dw_in.reshape(d, N_DEV, din_shard).transpose(1, 0, 2)
        dw_out_by_dev = dw_out.reshape(N_DEV, d_shard, d)
        g_w_in[l], g_w_out[l] = _reduce_scatter(
            [dw_in_by_dev, dw_out_by_dev], [_tile(d, 512, 8), _tile(d_shard, 256, 8)])
        small_parts[l] = [dbin[0], conv_small[0], conv_small[1], conv_small[2], dsk[0, :dm.nh],
                          ln_small[2], ln_small[0], ln_small[1]]
        dconv_w[l] = dcw
    grad_x = dout[None]

    small_w = [b_in, conv_b, conv_ln_g, conv_ln_b, sinks, b_out, ln_g, ln_b]
    small_m = [m_b_in, m_conv_b, m_conv_ln_g, m_conv_ln_b, m_sinks, m_b_out, m_ln_g, m_ln_b]
    small_v = [v_b_in, v_conv_b, v_conv_ln_g, v_conv_ln_b, v_sinks, v_b_out, v_ln_g, v_ln_b]
    n_small = len(small_w)
    packed_g, extents = _pack_small([jnp.stack([small_parts[l][k] for l in range(depth)])
                                     for k in range(n_small)])
    packed_w, _ = _pack_small(small_w)
    packed_m, _ = _pack_small(small_m)
    packed_v, _ = _pack_small(small_v)
    prows = packed_g.shape[0]
    conv_rows = depth * CONV_ROWS * c // LANES
    partial = jnp.concatenate(
        [packed_g, jnp.stack(dconv_w).reshape(conv_rows, LANES), loss_part], axis=0)
    total, sm_delta, sm_m, sm_v = _small_allreduce_adamw(partial, packed_w, packed_m, packed_v)
    loss = total[prows + conv_rows, 0]
    dconv_w_full = total[prows:prows + conv_rows].reshape(depth, CONV_ROWS, c)
    me = 4 * lax.axis_index("x") + 2 * lax.axis_index("y") + lax.axis_index("c")
    grad_conv_w = lax.dynamic_slice_in_dim(dconv_w_full, me * c_shard, c_shard, axis=2)[:, :CONV_WIDTH]

    grads_small = _unpack_small(total[:prows], extents, depth)
    delta_small = _unpack_small(sm_delta, extents, depth)
    newm_small = _unpack_small(sm_m, extents, depth)
    newv_small = _unpack_small(sm_v, extents, depth)

    grad_w_in = jnp.stack(g_w_in)
    grad_w_out = jnp.stack(g_w_out)
    d_w_in, nm_w_in, nv_w_in = _adamw(w_in, grad_w_in, m_w_in, v_w_in, _tile(d, 512, 8))
    d_w_out, nm_w_out, nv_w_out = _adamw(w_out, grad_w_out, m_w_out, v_w_out, _tile(d_shard, 256, 8))
    d_cw, nm_cw, nv_cw = _adamw(conv_w, grad_conv_w, m_conv_w, v_conv_w, CONV_WIDTH)

    def assemble(w_in_leaf, conv_w_leaf, w_out_leaf, small):
        b_in_, conv_b_, cg_, cb_, sinks_, b_out_, ln_g_, ln_b_ = small
        return [w_in_leaf, b_in_, conv_w_leaf, conv_b_, cg_, cb_, sinks_, w_out_leaf, b_out_,
                ln_g_, ln_b_]

    return (loss, grad_x,
            *assemble(grad_w_in, grad_conv_w, grad_w_out, grads_small),
            *assemble(d_w_in, d_cw, d_w_out, delta_small),
            *assemble(nm_w_in, nm_cw, nm_w_out, newm_small),
            *assemble(nv_w_in, nv_cw, nv_w_out, newv_small))
```

```python
import functools

import jax
import jax.numpy as jnp
from jax import lax
from jax.experimental import pallas as pl
from jax.experimental.pallas import tpu as pltpu

F32 = jnp.float32
BF16 = jnp.bfloat16
MESH = pl.DeviceIdType.MESH

N_DEV = 8
N_CHIPS = 4
HEAD_DIM = 64
N_KV = 2
KV_W = N_KV * HEAD_DIM
CONV_WIDTH = 31
CONV_ROWS = 32
HALO = 32
WINDOW = 128
LN_EPS = 1e-5
NEG_INF = -1e30
LANES = 128

ADAM_LR = 0.001
ADAM_B1 = 0.9
ADAM_B2 = 0.999
ADAM_EPS = 1e-08
ADAM_WD = 0.01
ADAM_STEP = 10

VMEM_LIMIT = 56 * 1024 * 1024


def _tile(n, target, align=LANES):
    best = None
    for t in range(align, min(n, target) + 1, align):
        if n % t == 0:
            best = t
    return n if best is None else best


def _sigmoid(x):
    return jax.nn.sigmoid(x)


def _dsilu(x, s):
    return s * (1.0 + x * (1.0 - s))


def _cparams(sem, vmem=VMEM_LIMIT):
    return pltpu.CompilerParams(dimension_semantics=sem, vmem_limit_bytes=vmem)


_ANY = pl.BlockSpec(memory_space=pl.ANY)


def _position():
    return lax.axis_index("x"), lax.axis_index("y"), lax.axis_index("c")


def _other_chips(x, y):
    return [(1 - x, y), (x, 1 - y), (1 - x, 1 - y)]


class _Copies:
    def __init__(self, operands, landing, alias, sems, start, finish):
        self.operands, self.landing, self.alias, self.sems = operands, landing, alias, sems
        self.start, self.finish = start, finish


class _Hosting:
    def __init__(self, groups, n_in, n_out):
        self.groups = groups
        self.args = [a for g in groups for a in g.operands]
        self.out_shapes = [s for g in groups for s in g.landing]
        self.scratch = [pltpu.SemaphoreType.DMA(g.sems) for g in groups for _ in range(2)]
        self.aliases = {}
        i0, o0 = n_in, n_out
        for g in groups:
            for a, b in g.alias.items():
                self.aliases[i0 + a] = o0 + b
            i0 += len(g.operands)
            o0 += len(g.landing)

    def _each(self, in_refs, out_refs, sem_refs):
        i0 = o0 = 0
        for n, g in enumerate(self.groups):
            yield (g, in_refs[i0:i0 + len(g.operands)], out_refs[o0:o0 + len(g.landing)],
                   sem_refs[2 * n], sem_refs[2 * n + 1])
            i0 += len(g.operands)
            o0 += len(g.landing)

    def start(self, in_refs, out_refs, sem_refs):
        for g, ins, outs, send, recv in self._each(in_refs, out_refs, sem_refs):
            g.start(ins, outs, send, recv)

    def finish(self, in_refs, out_refs, sem_refs):
        for g, ins, outs, send, recv in self._each(in_refs, out_refs, sem_refs):
            g.finish(ins, outs, send, recv)


def _run_copies(groups, name):
    host = _Hosting(groups, 0, 0)
    n_in, n_out = len(host.args), len(host.out_shapes)

    def body(*refs):
        ins, outs, sems = refs[:n_in], refs[n_in:n_in + n_out], refs[n_in + n_out:]
        host.start(ins, outs, sems)
        host.finish(ins, outs, sems)

    return pl.pallas_call(
        body, name=name, out_shape=host.out_shapes, in_specs=[_ANY] * n_in,
        out_specs=[_ANY] * n_out, scratch_shapes=host.scratch,
        input_output_aliases=host.aliases,
    )(*host.args)


def _gather_own(shards):
    n = len(shards)

    def copies(ins, outs, send, recv):
        x, y, c = _position()
        peers = [(x, y, 1 - c)] + [(px, py, c) for px, py in _other_chips(x, y)]
        out = []
        for t in range(n):
            for k, peer in enumerate(peers):
                out.append((pltpu.make_async_remote_copy(
                    src_ref=ins[t], dst_ref=outs[t].at[4 * x + 2 * y + c],
                    send_sem=send.at[t, k], recv_sem=recv.at[t, k],
                    device_id=peer, device_id_type=MESH), t, k, peer))
        local = [pltpu.make_async_copy(ins[t], outs[t].at[4 * x + 2 * y + c], send.at[t, 4])
                 for t in range(n)]
        return out, local

    def start(ins, outs, send, recv):
        remote, local = copies(ins, outs, send, recv)
        for cp in local:
            cp.start()
        for cp, _, _, _ in remote:
            cp.start()

    def finish(ins, outs, send, recv):
        remote, local = copies(ins, outs, send, recv)
        x, y, c = _position()
        for _, t, k, (px, py, pc) in remote:
            pltpu.make_async_remote_copy(
                src_ref=ins[t], dst_ref=outs[t].at[4 * px + 2 * py + pc],
                send_sem=send.at[t, k], recv_sem=recv.at[t, k],
                device_id=(x, y, c), device_id_type=MESH).wait_recv()
        for cp, _, _, _ in remote:
            cp.wait_send()
        for cp in local:
            cp.wait()

    landing = [jax.ShapeDtypeStruct((N_DEV,) + s.shape, s.dtype) for s in shards]
    return _Copies(list(shards), landing, {}, (n, 5), start, finish)


def _gather_forward(buffers):
    n = len(buffers)

    def copies(ins, outs, send, recv):
        x, y, c = _position()
        out = []
        for t in range(n):
            for j, (px, py) in enumerate(_other_chips(x, y)):
                slot = 4 * px + 2 * py + c
                out.append((pltpu.make_async_remote_copy(
                    src_ref=ins[t].at[slot], dst_ref=outs[t].at[slot],
                    send_sem=send.at[t, j], recv_sem=recv.at[t, j],
                    device_id=(x, y, 1 - c), device_id_type=MESH), t, j, 4 * px + 2 * py + 1 - c))
        return out

    def start(ins, outs, send, recv):
        for cp, _, _, _ in copies(ins, outs, send, recv):
            cp.start()

    def finish(ins, outs, send, recv):
        x, y, c = _position()
        mine = copies(ins, outs, send, recv)
        for _, t, j, got in mine:
            pltpu.make_async_remote_copy(
                src_ref=ins[t].at[got], dst_ref=outs[t].at[got],
                send_sem=send.at[t, j], recv_sem=recv.at[t, j],
                device_id=(x, y, c), device_id_type=MESH).wait_recv()
        for cp, _, _, _ in mine:
            cp.wait_send()

    landing = [jax.ShapeDtypeStruct(b.shape, b.dtype) for b in buffers]
    return _Copies(list(buffers), landing, {t: t for t in range(n)}, (n, 3), start, finish)


def _scatter_sibling(parts):
    n = len(parts)

    def copies(ins, outs, send, recv):
        x, y, c = _position()
        return [pltpu.make_async_remote_copy(
            src_ref=ins[t].at[:, 1 - c], dst_ref=outs[t],
            send_sem=send.at[t, 0], recv_sem=recv.at[t, 0],
            device_id=(x, y, 1 - c), device_id_type=MESH) for t in range(n)]

    def start(ins, outs, send, recv):
        for cp in copies(ins, outs, send, recv):
            cp.start()

    def finish(ins, outs, send, recv):
        for cp in copies(ins, outs, send, recv):
            cp.wait()

    landing = [jax.ShapeDtypeStruct((p.shape[0],) + p.shape[2:], p.dtype) for p in parts]
    return _Copies(list(parts), landing, {}, (n, 1), start, finish)


def _scatter_chips(blocks):
    n = len(blocks)

    def start(ins, outs, send, recv):
        x, y, c = _position()
        for t in range(n):
            for j, (px, py) in enumerate(_other_chips(x, y)):
                pltpu.make_async_remote_copy(
                    src_ref=ins[t].at[2 * px + py], dst_ref=outs[t].at[2 * x + y],
                    send_sem=send.at[t, j], recv_sem=recv.at[t, j],
                    device_id=(px, py, c), device_id_type=MESH).start()

    def finish(ins, outs, send, recv):
        x, y, c = _position()
        for t in range(n):
            for j, (px, py) in enumerate(_other_chips(x, y)):
                cp = pltpu.make_async_remote_copy(
                    src_ref=ins[t].at[2 * px + py], dst_ref=outs[t].at[2 * px + py],
                    send_sem=send.at[t, j], recv_sem=recv.at[t, j],
                    device_id=(px, py, c), device_id_type=MESH)
                cp.wait_recv()
                cp.wait_send()

    landing = [jax.ShapeDtypeStruct(b.shape, b.dtype) for b in blocks]
    return _Copies(list(blocks), landing, {}, (n, 3), start, finish)


def _matmul(a, b, *, ta=False, tb=False, tm, tn, tk, out_dtype, name, bias=None, resid=None,
            resid_scale=1.0, copies=()):
    m, k = (a.shape[1], a.shape[0]) if ta else a.shape
    n = b.shape[0] if tb else b.shape[1]
    assert (b.shape[1] if tb else b.shape[0]) == k
    assert m % tm == 0 and n % tn == 0 and k % tk == 0
    ni, nj, nk = m // tm, n // tn, k // tk
    dn = (((0 if ta else 1,), (1 if tb else 0,)), ((), ()))
    n_in = 2 + (bias is not None) + (resid is not None)
    host = _Hosting(list(copies), n_in, 1)
    n_hin, n_hout = len(host.args), len(host.out_shapes)

    def body(*refs):
        a_ref, b_ref = refs[0], refs[1]
        pos = 2
        bias_ref = resid_ref = None
        if bias is not None:
            bias_ref = refs[pos]
            pos += 1
        if resid is not None:
            resid_ref = refs[pos]
            pos += 1
        h_in = refs[pos:pos + n_hin]
        pos += n_hin
        o_ref = refs[pos]
        h_out = refs[pos + 1:pos + 1 + n_hout]
        pos += 1 + n_hout
        acc_ref = refs[pos] if nk > 1 else None
        h_sems = refs[pos + (nk > 1):]
        step = (pl.program_id(0) * nj + pl.program_id(1)) * nk + pl.program_id(2)

        if copies:
            @pl.when(step == 0)
            def _():
                host.start(h_in, h_out, h_sems)

        def finish(acc):
            if bias_ref is not None:
                acc = acc + bias_ref[...]
            if resid_ref is not None:
                acc = acc + resid_scale * resid_ref[...]
            o_ref[...] = acc.astype(out_dtype)

        p = lax.dot_general(a_ref[...], b_ref[...], dn, preferred_element_type=F32)
        if nk == 1:
            finish(p)
        else:
            kk = pl.program_id(2)

            @pl.when(kk == 0)
            def _():
                acc_ref[...] = p

            @pl.when(kk > 0)
            def _():
                acc_ref[...] += p

            @pl.when(kk == nk - 1)
            def _():
                finish(acc_ref[...])

        if copies:
            @pl.when(step == ni * nj * nk - 1)
            def _():
                host.finish(h_in, h_out, h_sems)

    a_spec = (pl.BlockSpec((tk, tm), lambda i, j, kk: (kk, i)) if ta
              else pl.BlockSpec((tm, tk), lambda i, j, kk: (i, kk)))
    b_spec = (pl.BlockSpec((tn, tk), lambda i, j, kk: (j, kk)) if tb
              else pl.BlockSpec((tk, tn), lambda i, j, kk: (kk, j)))
    in_specs = [a_spec, b_spec]
    args = [a, b]
    if bias is not None:
        in_specs.append(pl.BlockSpec((1, tn), lambda i, j, kk: (0, j)))
        args.append(bias)
    if resid is not None:
        in_specs.append(pl.BlockSpec((tm, tn), lambda i, j, kk: (i, j)))
        args.append(resid)
    res = pl.pallas_call(
        body, name=name,
        out_shape=[jax.ShapeDtypeStruct((m, n), out_dtype)] + host.out_shapes,
        grid=(ni, nj, nk),
        in_specs=in_specs + [_ANY] * n_hin,
        out_specs=[pl.BlockSpec((tm, tn), lambda i, j, kk: (i, j))] + [_ANY] * n_hout,
        scratch_shapes=([pltpu.VMEM((tm, tn), F32)] if nk > 1 else []) + host.scratch,
        input_output_aliases=host.aliases,
        compiler_params=_cparams(("arbitrary",) * 3 if copies else
                                 ("parallel", "parallel", "arbitrary")),
    )(*args, *host.args)
    return (res[0], res[1:]) if copies else res[0]


class _Dims:
    def __init__(self, s, d, c, depth):
        self.s, self.d, self.c, self.depth = s, d, c, depth
        self.a = d - c
        self.nh = self.a // HEAD_DIM
        self.group = self.nh // N_KV
        self.din = 3 * c + 2 * self.a + 2 * KV_W
        self.o_q = 3 * c
        self.o_k = 3 * c + self.a
        self.o_v = self.o_k + KV_W
        self.o_ag = self.o_k + 2 * KV_W
        self.alpha = (2 * depth) ** 0.25
        assert self.nh % 2 == 0 and self.group % 2 == 0 and self.o_k % (2 * KV_W) == 0
        assert c % LANES == 0 and self.a % LANES == 0


SUBLANES = 8
TAP_ROWS = 64


def _shift_copies(src_ref, sh_ref, lanes, rows):
    for r in range(1, SUBLANES):
        sh_ref[r - 1, 0:rows, :] = src_ref[pl.ds(r, rows), lanes]


def _tap_rows(src_ref, sh_ref, lanes, off, start, rows):
    r = off % SUBLANES
    at = pl.multiple_of(start + (off - r), SUBLANES)
    if r == 0:
        return src_ref[pl.ds(at, rows), lanes]
    return sh_ref[r - 1, pl.ds(at, rows), :]


def _conv_chunk(w_ref, src_ref, sh_ref, out_ref, lanes, t, first_off, reverse, bias_ref=None):
    def block(it, carry):
        start = pl.multiple_of(it * TAP_ROWS, TAP_ROWS)
        acc = None
        for j in range(CONV_WIDTH):
            off = first_off - j if reverse else first_off + j
            term = w_ref[j:j + 1, lanes] * _tap_rows(src_ref, sh_ref, lanes, off, start, TAP_ROWS)
            acc = term if acc is None else acc + term
        if bias_ref is not None:
            acc = acc + bias_ref[:, lanes]
        out_ref[pl.ds(start, TAP_ROWS), lanes] = acc
        return carry

    lax.fori_loop(0, t // TAP_ROWS, block, 0)


def _conv_w_grad_chunk(src_ref, sh_ref, dconv_ref, acc_ref, lanes, t, first_off):
    def block(it, accs):
        start = pl.multiple_of(it * SUBLANES, SUBLANES)
        dv = dconv_ref[pl.ds(start, SUBLANES), lanes]
        return tuple(
            acc + _tap_rows(src_ref, sh_ref, lanes, first_off + j, start, SUBLANES) * dv
            for j, acc in enumerate(accs))

    zero = jnp.zeros((SUBLANES, LANES), F32)
    accs = lax.fori_loop(0, t // SUBLANES, block, (zero,) * CONV_WIDTH, unroll=2)
    for j in range(CONV_WIDTH):
        acc_ref[j, :, lanes] += accs[j]


def _kv_operands(kv_ref, lo):
    kext = kv_ref[:, 0:KV_W]
    vext = kv_ref[:, KV_W:2 * KV_W]
    ksw = pltpu.roll(kext, HEAD_DIM, 1)
    vsw = pltpu.roll(vext, HEAD_DIM, 1)
    zero = jnp.zeros_like(kext)
    k2 = [jnp.where(lo, kext, ksw).astype(BF16), jnp.where(lo, ksw, kext).astype(BF16)]
    khalf = [[jnp.where(lo, kext, zero).astype(BF16), jnp.where(lo, zero, ksw).astype(BF16)],
             [jnp.where(lo, ksw, zero).astype(BF16), jnp.where(lo, zero, kext).astype(BF16)]]
    vhalf = [[jnp.where(lo, vext, zero).astype(BF16), jnp.where(lo, zero, vsw).astype(BF16)],
             [jnp.where(lo, vsw, zero).astype(BF16), jnp.where(lo, zero, vext).astype(BF16)]]
    return k2, khalf, vhalf


def _valid_mask(global_block):
    qi = lax.broadcasted_iota(jnp.int32, (WINDOW, 2 * WINDOW), 0)
    si = lax.broadcasted_iota(jnp.int32, (WINDOW, 2 * WINDOW), 1)
    diff = qi + WINDOW - si
    first_key = jnp.where(global_block == 0, WINDOW, 0)
    return (diff >= 0) & (diff < WINDOW) & (si >= first_key)


def _softmax_with_sink(qm, k2rows, valid, sink):
    s = lax.dot_general(qm, k2rows, (((1,), (1,)), ((), ())), preferred_element_type=F32)
    s = s * (HEAD_DIM ** -0.5)
    s = jnp.where(valid, s, NEG_INF)
    m = jnp.maximum(jnp.max(s, axis=1, keepdims=True), sink)
    e = jnp.exp(s - m)
    den = jnp.sum(e, axis=1, keepdims=True) + jnp.exp(sink - m)
    inv = 1.0 / den
    return e * inv, m, inv


def _mixer_specs(dm, t, idx):
    return [
        pl.BlockSpec((t, dm.din), lambda g: (idx(g), 0)),
        pl.BlockSpec((HALO, 2 * dm.c), lambda g: (jnp.maximum(idx(g) * (t // HALO) - 1, 0), 0)),
        pl.BlockSpec((WINDOW, 2 * KV_W),
                     lambda g: (jnp.maximum(idx(g) * (t // WINDOW) - 1, 0), dm.o_k // (2 * KV_W))),
    ]


def _mixer_fwd(dm, proj, conv_w, conv_b, cln_g, cln_b, sinks, t):
    c, nq = dm.c, t // WINDOW

    def body(sinks_ref, pr_ref, ch_ref, kvh_ref, cw_ref, cb_ref, cg_ref, cbb_ref,
             y_ref, conv_ref, hext_ref, kv_ref, hs_ref):
        i = pl.program_id(0)
        first = i == 0
        h = pr_ref[:, 0:c] * _sigmoid(pr_ref[:, c:2 * c])
        hh = ch_ref[:, 0:c] * _sigmoid(ch_ref[:, c:2 * c])
        hext_ref[0:HALO, :] = jnp.where(first, 0.0, hh)
        hext_ref[HALO:HALO + t, :] = h
        for k in range(c // LANES):
            lanes = slice(LANES * k, LANES * (k + 1))
            _shift_copies(hext_ref, hs_ref, lanes, t + HALO - SUBLANES)
            _conv_chunk(cw_ref, hext_ref, hs_ref, conv_ref, lanes, t, HALO - (CONV_WIDTH - 1),
                        False, cb_ref)
        conv = conv_ref[...]
        mu = jnp.mean(conv, axis=1, keepdims=True)
        dlt = conv - mu
        var = jnp.mean(dlt * dlt, axis=1, keepdims=True)
        u = dlt * lax.rsqrt(var + LN_EPS) * cg_ref[...] + cbb_ref[...]
        gate = pr_ref[:, 2 * c:3 * c]
        y_ref[:, 0:c] = (u * _sigmoid(u) * (gate * _sigmoid(gate))).astype(BF16)

        kv_ref[0:WINDOW, :] = jnp.where(first, 0.0, kvh_ref[...])
        kv_ref[WINDOW:WINDOW + t, :] = pr_ref[:, dm.o_k:dm.o_k + 2 * KV_W]
        lo = lax.broadcasted_iota(jnp.int32, (1, LANES), 1) < HEAD_DIM
        k2, _, vhalf = _kv_operands(kv_ref, lo)
        for qb in range(nq):
            r0 = qb * WINDOW
            valid = _valid_mask(i * nq + qb)
            for p in range(dm.nh // 2):
                kvh = (2 * p) // dm.group
                qp = pr_ref[r0:r0 + WINDOW, dm.o_q + LANES * p:dm.o_q + LANES * (p + 1)]
                opair = None
                for half in range(2):
                    qm = jnp.where(lo if half == 0 else jnp.logical_not(lo), qp, 0.0).astype(BF16)
                    prob, _, _ = _softmax_with_sink(qm, k2[kvh][r0:r0 + 2 * WINDOW], valid,
                                                    sinks_ref[2 * p + half])
                    o_h = jnp.dot(prob.astype(BF16), vhalf[kvh][half][r0:r0 + 2 * WINDOW],
                                  preferred_element_type=F32)
                    opair = o_h if opair is None else opair + o_h
                ag = pr_ref[r0:r0 + WINDOW, dm.o_ag + LANES * p:dm.o_ag + LANES * (p + 1)]
                y_ref[r0:r0 + WINDOW, c + LANES * p:c + LANES * (p + 1)] = (
                    opair * (ag * _sigmoid(ag))).astype(BF16)

    vec = pl.BlockSpec((1, c), lambda g: (0, 0))
    return pl.pallas_call(
        body, name="mixer_fwd",
        out_shape=(jax.ShapeDtypeStruct((dm.s, dm.d), BF16),
                   jax.ShapeDtypeStruct((dm.s, c), F32)),
        grid=(dm.s // t,),
        in_specs=[pl.BlockSpec(memory_space=pltpu.SMEM)] + _mixer_specs(dm, t, lambda g: g)
        + [pl.BlockSpec((CONV_ROWS, c), lambda g: (0, 0)), vec, vec, vec],
        out_specs=(pl.BlockSpec((t, dm.d), lambda g: (g, 0)), pl.BlockSpec((t, c), lambda g: (g, 0))),
        scratch_shapes=[pltpu.VMEM((t + HALO, c), F32), pltpu.VMEM((t + WINDOW, 2 * KV_W), F32),
                        pltpu.VMEM((SUBLANES - 1, t + HALO, LANES), F32)],
        compiler_params=_cparams(("arbitrary",)),
    )(sinks, proj, proj, proj, conv_w, conv_b, cln_g, cln_b)


def _mixer_bwd(dm, proj, dymix, conv, conv_w, conv_b, cln_g, cln_b, sinks, t):
    c, nq, nt = dm.c, t // WINDOW, dm.s // t

    def body(sinks_ref, pr_ref, ch_ref, kvh_ref, dy_ref, cv_ref, cw_ref, cb_ref, cg_ref, cbb_ref,
             dpr_ref, dcw_ref, dsm_ref, dbin_ref, dsk_ref,
             hext_ref, kv_ref, dcx_ref, dkv_ref, carry_dc_ref, carry_kv_ref,
             hs_ref, ds_ref, dh_ref, dcw_acc_ref):
        g = pl.program_id(0)
        i = nt - 1 - g
        first = i == 0

        @pl.when(g == 0)
        def _():
            dcw_acc_ref[...] = jnp.zeros_like(dcw_acc_ref)
            dsm_ref[...] = jnp.zeros_like(dsm_ref)
            dbin_ref[...] = jnp.zeros_like(dbin_ref)
            dsk_ref[...] = jnp.zeros_like(dsk_ref)
            carry_dc_ref[...] = jnp.zeros_like(carry_dc_ref)
            carry_kv_ref[...] = jnp.zeros_like(carry_kv_ref)

        def emit(col, width, val):
            dpr_ref[:, col:col + width] = val.astype(BF16)
            dbin_ref[0:1, col:col + width] += jnp.sum(val, axis=0, keepdims=True)

        val = pr_ref[:, 0:c]
        sg = _sigmoid(pr_ref[:, c:2 * c])
        h = val * sg
        hh = ch_ref[:, 0:c] * _sigmoid(ch_ref[:, c:2 * c])
        hext_ref[0:HALO, :] = jnp.where(first, 0.0, hh)
        hext_ref[HALO:HALO + t, :] = h
        conv = cv_ref[...]
        mu = jnp.mean(conv, axis=1, keepdims=True)
        dlt = conv - mu
        var = jnp.mean(dlt * dlt, axis=1, keepdims=True)
        rstd = lax.rsqrt(var + LN_EPS)
        xhat = dlt * rstd
        u = xhat * cg_ref[...] + cbb_ref[...]
        su = _sigmoid(u)
        gate = pr_ref[:, 2 * c:3 * c]
        sgate = _sigmoid(gate)
        dyc = dy_ref[:, 0:c]
        emit(2 * c, c, dyc * (u * su) * _dsilu(gate, sgate))
        du = dyc * (gate * sgate) * _dsilu(u, su)
        dsm_ref[1:2, :] += jnp.sum(du * xhat, axis=0, keepdims=True)
        dsm_ref[2:3, :] += jnp.sum(du, axis=0, keepdims=True)
        dxh = du * cg_ref[...]
        dconv = rstd * (dxh - jnp.mean(dxh, axis=1, keepdims=True)
                        - xhat * jnp.mean(dxh * xhat, axis=1, keepdims=True))
        dsm_ref[0:1, :] += jnp.sum(dconv, axis=0, keepdims=True)
        dcx_ref[0:t, :] = dconv
        dcx_ref[t:t + HALO, :] = carry_dc_ref[...]
        carry_dc_ref[...] = dconv[0:HALO, :]
        for k in range(c // LANES):
            lanes = slice(LANES * k, LANES * (k + 1))
            _shift_copies(hext_ref, hs_ref, lanes, t + HALO - SUBLANES)
            _shift_copies(dcx_ref, ds_ref, lanes, t + HALO - SUBLANES)
            _conv_chunk(cw_ref, dcx_ref, ds_ref, dh_ref, lanes, t, CONV_WIDTH - 1, True)
            _conv_w_grad_chunk(hext_ref, hs_ref, dcx_ref, dcw_acc_ref, lanes, t,
                               HALO - (CONV_WIDTH - 1))
        dh = dh_ref[...]
        emit(0, c, dh * sg)
        emit(c, c, dh * val * sg * (1.0 - sg))

        kv_ref[0:WINDOW, :] = jnp.where(first, 0.0, kvh_ref[...])
        kv_ref[WINDOW:WINDOW + t, :] = pr_ref[:, dm.o_k:dm.o_k + 2 * KV_W]
        dkv_ref[0:t, :] = jnp.zeros((t, 2 * KV_W), F32)
        dkv_ref[t:t + WINDOW, :] = carry_kv_ref[...]
        lane = lax.broadcasted_iota(jnp.int32, (1, LANES), 1)
        lo = lane < HEAD_DIM
        k2, khalf, vhalf = _kv_operands(kv_ref, lo)
        tn_dims = (((0,), (0,)), ((), ()))
        nt_dims = (((1,), (1,)), ((), ()))
        dsk = jnp.zeros((1, LANES), F32)
        for qb in range(nq):
            r0 = qb * WINDOW
            rows = slice(r0, r0 + 2 * WINDOW)
            valid = _valid_mask(i * nq + qb)
            dka = [None, None]
            dva = [None, None]
            for p in range(dm.nh // 2):
                kvh = (2 * p) // dm.group
                cols = slice(LANES * p, LANES * (p + 1))
                qp = pr_ref[r0:r0 + WINDOW, dm.o_q + cols.start:dm.o_q + cols.stop]
                ag = pr_ref[r0:r0 + WINDOW, dm.o_ag + cols.start:dm.o_ag + cols.stop]
                dya = dy_ref[r0:r0 + WINDOW, c + cols.start:c + cols.stop]
                sag = _sigmoid(ag)
                d_o = dya * (ag * sag)
                d_o_b = d_o.astype(BF16)
                opair = None
                dqpair = None
                for half in range(2):
                    hmask = lo if half == 0 else jnp.logical_not(lo)
                    sink = sinks_ref[2 * p + half]
                    qm = jnp.where(hmask, qp, 0.0).astype(BF16)
                    prob, m, inv = _softmax_with_sink(qm, k2[kvh][rows], valid, sink)
                    pb = prob.astype(BF16)
                    vh = vhalf[kvh][half][rows]
                    o_h = jnp.dot(pb, vh, preferred_element_type=F32)
                    opair = o_h if opair is None else opair + o_h
                    delta = jnp.sum(d_o * o_h, axis=1, keepdims=True)
                    dp = lax.dot_general(d_o_b, vh, nt_dims, preferred_element_type=F32)
                    ds = prob * (dp - delta)
                    dsink = -jnp.sum(jnp.exp(sink - m) * inv * delta)
                    dsk = dsk + jnp.where(lane == 2 * p + half, dsink, 0.0)
                    dsb = (ds * (HEAD_DIM ** -0.5)).astype(BF16)
                    dq_h = jnp.dot(dsb, khalf[kvh][half][rows], preferred_element_type=F32)
                    dqpair = dq_h if dqpair is None else dqpair + dq_h
                    dk_h = lax.dot_general(dsb, qm, tn_dims, preferred_element_type=F32)
                    dv_h = lax.dot_general(pb, jnp.where(hmask, d_o, 0.0).astype(BF16), tn_dims,
                                           preferred_element_type=F32)
                    dka[kvh] = dk_h if dka[kvh] is None else dka[kvh] + dk_h
                    dva[kvh] = dv_h if dva[kvh] is None else dva[kvh] + dv_h
                d_ag = dya * opair * _dsilu(ag, sag)
                dpr_ref[r0:r0 + WINDOW, dm.o_q + cols.start:dm.o_q + cols.stop] = dqpair.astype(BF16)
                dbin_ref[0:1, dm.o_q + cols.start:dm.o_q + cols.stop] += jnp.sum(
                    dqpair, axis=0, keepdims=True)
                dpr_ref[r0:r0 + WINDOW, dm.o_ag + cols.start:dm.o_ag + cols.stop] = d_ag.astype(BF16)
                dbin_ref[0:1, dm.o_ag + cols.start:dm.o_ag + cols.stop] += jnp.sum(
                    d_ag, axis=0, keepdims=True)
            fold = [x + pltpu.roll(x, HEAD_DIM, 1) for x in (dka[0], dka[1], dva[0], dva[1])]
            dkv_ref[r0:r0 + 2 * WINDOW, 0:KV_W] += jnp.where(lo, fold[0], fold[1])
            dkv_ref[r0:r0 + 2 * WINDOW, KV_W:2 * KV_W] += jnp.where(lo, fold[2], fold[3])
        dsk_ref[0:1, :] += dsk
        carry_kv_ref[...] = dkv_ref[0:WINDOW, :]
        emit(dm.o_k, 2 * KV_W, dkv_ref[WINDOW:WINDOW + t, :])

        @pl.when(g == nt - 1)
        def _():
            for j in range(CONV_WIDTH):
                dcw_ref[j:j + 1, :] = jnp.sum(dcw_acc_ref[j], axis=0, keepdims=True)
            dcw_ref[CONV_WIDTH:CONV_ROWS, :] = jnp.zeros((CONV_ROWS - CONV_WIDTH, c), F32)

    rev = lambda g: nt - 1 - g
    vec = pl.BlockSpec((1, c), lambda g: (0, 0))
    const = lambda shape: pl.BlockSpec(shape, lambda g: (0, 0))
    return pl.pallas_call(
        body, name="mixer_bwd",
        out_shape=(jax.ShapeDtypeStruct((dm.s, dm.din), BF16),
                   jax.ShapeDtypeStruct((CONV_ROWS, c), F32),
                   jax.ShapeDtypeStruct((8, c), F32),
                   jax.ShapeDtypeStruct((8, dm.din), F32),
                   jax.ShapeDtypeStruct((8, LANES), F32)),
        grid=(nt,),
        in_specs=[pl.BlockSpec(memory_space=pltpu.SMEM)] + _mixer_specs(dm, t, rev)
        + [pl.BlockSpec((t, dm.d), lambda g: (rev(g), 0)), pl.BlockSpec((t, c), lambda g: (rev(g), 0)),
           pl.BlockSpec((CONV_ROWS, c), lambda g: (0, 0)), vec, vec, vec],
        out_specs=(pl.BlockSpec((t, dm.din), lambda g: (rev(g), 0)),
                   const((CONV_ROWS, c)), const((8, c)), const((8, dm.din)), const((8, LANES))),
        scratch_shapes=[pltpu.VMEM((t + HALO, c), F32), pltpu.VMEM((t + WINDOW, 2 * KV_W), F32),
                        pltpu.VMEM((t + HALO, c), F32), pltpu.VMEM((t + WINDOW, 2 * KV_W), F32),
                        pltpu.VMEM((HALO, c), F32), pltpu.VMEM((WINDOW, 2 * KV_W), F32),
                        pltpu.VMEM((SUBLANES - 1, t + HALO, LANES), F32),
                        pltpu.VMEM((SUBLANES - 1, t + HALO, LANES), F32),
                        pltpu.VMEM((t, c), F32), pltpu.VMEM((CONV_ROWS, SUBLANES, c), F32)],
        compiler_params=_cparams(("arbitrary",)),
    )(sinks, proj, proj, proj, dymix, conv, conv_w, conv_b, cln_g, cln_b)


def _outproj_ln(dm, ymix, w_out, b_out, x, ln_g, ln_b, target, tm, copies=()):
    last = target is not None
    d = dm.d
    n_in = 7 if last else 6
    host = _Hosting(list(copies), n_in, 3)
    n_hin, n_hout = len(host.args), len(host.out_shapes)
    steps = dm.s // tm

    def body(*refs):
        y_ref, w_ref, bo_ref, x_ref, g_ref, b_ref = refs[:6]
        h_in = refs[n_in:n_in + n_hin]
        h_out = refs[n_in + n_hin + 3:n_in + n_hin + 3 + n_hout]
        h_sems = refs[n_in + n_hin + 3 + n_hout:]
        if copies:
            @pl.when(pl.program_id(0) == 0)
            def _():
                host.start(h_in, h_out, h_sems)

        z = dm.alpha * x_ref[...] + (
            jnp.dot(y_ref[...], w_ref[...], preferred_element_type=F32) + bo_ref[...])
        mu = jnp.mean(z, axis=1, keepdims=True)
        dlt = z - mu
        var = jnp.mean(dlt * dlt, axis=1, keepdims=True)
        out = dlt * lax.rsqrt(var + LN_EPS) * g_ref[...] + b_ref[...]
        if last:
            t_ref = refs[6]
            z_ref, dout_ref, loss_ref = refs[n_in + n_hin:n_in + n_hin + 3]
            z_ref[...] = z
            err = out - t_ref[...]
            dout_ref[...] = err * (1.0 / d)

            @pl.when(pl.program_id(0) == 0)
            def _():
                loss_ref[...] = jnp.zeros_like(loss_ref)

            loss_ref[...] += 0.5 * jnp.sum(jnp.mean(err * err, axis=1, keepdims=True), axis=0,
                                           keepdims=True)
        else:
            z_ref, o_ref, ob_ref = refs[n_in + n_hin:n_in + n_hin + 3]
            z_ref[...] = z
            o_ref[...] = out
            ob_ref[...] = out.astype(BF16)

        if copies:
            @pl.when(pl.program_id(0) == steps - 1)
            def _():
                host.finish(h_in, h_out, h_sems)

    row = pl.BlockSpec((tm, d), lambda i: (i, 0))
    vec = pl.BlockSpec((1, d), lambda i: (0, 0))
    in_specs = [row, pl.BlockSpec((d, d), lambda i: (0, 0)), vec, row, vec, vec]
    args = [ymix, w_out, b_out, x, ln_g, ln_b]
    act = jax.ShapeDtypeStruct((dm.s, d), F32)
    if last:
        in_specs.append(row)
        args.append(target)
        out_shape = [act, act, jax.ShapeDtypeStruct((8, LANES), F32)]
        out_specs = [row, row, pl.BlockSpec((8, LANES), lambda i: (0, 0))]
    else:
        out_shape = [act, act, jax.ShapeDtypeStruct((dm.s, d), BF16)]
        out_specs = [row, row, row]
    res = pl.pallas_call(
        body, name="outproj_ln_loss" if last else "outproj_ln",
        out_shape=out_shape + host.out_shapes, grid=(steps,),
        in_specs=in_specs + [_ANY] * n_hin, out_specs=out_specs + [_ANY] * n_hout,
        scratch_shapes=host.scratch, input_output_aliases=host.aliases,
        compiler_params=_cparams(("arbitrary",)),
    )(*args, *host.args)
    return res[:3], res[3:]


def _ln_bwd(dm, dout, z, ln_g, tm):
    d = dm.d

    def body(do_ref, z_ref, g_ref, dz_ref, dzb_ref, sm_ref):
        @pl.when(pl.program_id(0) == 0)
        def _():
            sm_ref[...] = jnp.zeros_like(sm_ref)

        z = z_ref[...]
        mu = jnp.mean(z, axis=1, keepdims=True)
        dlt = z - mu
        var = jnp.mean(dlt * dlt, axis=1, keepdims=True)
        rstd = lax.rsqrt(var + LN_EPS)
        zhat = dlt * rstd
        do = do_ref[...]
        dzh = do * g_ref[...]
        dz = rstd * (dzh - jnp.mean(dzh, axis=1, keepdims=True)
                     - zhat * jnp.mean(dzh * zhat, axis=1, keepdims=True))
        dz_ref[...] = dz
        dzb_ref[...] = dz.astype(BF16)
        sm_ref[0:1, :] += jnp.sum(do * zhat, axis=0, keepdims=True)
        sm_ref[1:2, :] += jnp.sum(do, axis=0, keepdims=True)
        sm_ref[2:3, :] += jnp.sum(dz, axis=0, keepdims=True)

    row = pl.BlockSpec((tm, d), lambda i: (i, 0))
    return pl.pallas_call(
        body, name="ln_bwd",
        out_shape=(jax.ShapeDtypeStruct((dm.s, d), F32), jax.ShapeDtypeStruct((dm.s, d), BF16),
                   jax.ShapeDtypeStruct((8, d), F32)),
        grid=(dm.s // tm,),
        in_specs=[row, row, pl.BlockSpec((1, d), lambda i: (0, 0))],
        out_specs=(row, row, pl.BlockSpec((8, d), lambda i: (0, 0))),
        compiler_params=_cparams(("arbitrary",)),
    )(dout, z, ln_g)


def _pair_sum(part, got, tr):
    _, _, r, w = part.shape

    def body(c_ref, p_ref, g_ref, o_ref):
        o_ref[...] = (p_ref[...] + g_ref[...]).astype(BF16)

    return pl.pallas_call(
        body, name="grad_pair_sum",
        out_shape=jax.ShapeDtypeStruct((N_CHIPS, r, w), BF16),
        grid_spec=pltpu.PrefetchScalarGridSpec(
            num_scalar_prefetch=1, grid=(N_CHIPS, r // tr),
            in_specs=[pl.BlockSpec((None, None, tr, w), lambda k, i, c_ref: (k, c_ref[0], i, 0)),
                      pl.BlockSpec((None, tr, w), lambda k, i, c_ref: (k, i, 0))],
            out_specs=pl.BlockSpec((None, tr, w), lambda k, i, c_ref: (k, i, 0))),
        compiler_params=_cparams(("parallel", "parallel")),
    )(lax.axis_index("c").reshape(1).astype(jnp.int32), part, got)


def _final_sum(part, got, recv, tr):
    _, _, r, w = part.shape

    def body(idx_ref, p_ref, g_ref, r0_ref, r1_ref, r2_ref, o_ref):
        acc = p_ref[...] + g_ref[...]
        for ref in (r0_ref, r1_ref, r2_ref):
            acc = acc + ref[...].astype(F32)
        o_ref[...] = acc

    x, y, c = _position()
    idx = jnp.stack([c, 2 * x + y, 2 * (1 - x) + y, 2 * x + (1 - y),
                     2 * (1 - x) + (1 - y)]).astype(jnp.int32)
    other = lambda j: pl.BlockSpec((None, tr, w), lambda i, s: (s[2 + j], i, 0))
    return pl.pallas_call(
        body, name="grad_final_sum",
        out_shape=jax.ShapeDtypeStruct((r, w), F32),
        grid_spec=pltpu.PrefetchScalarGridSpec(
            num_scalar_prefetch=1, grid=(r // tr,),
            in_specs=[pl.BlockSpec((None, None, tr, w), lambda i, s: (s[1], s[0], i, 0)),
                      pl.BlockSpec((None, tr, w), lambda i, s: (s[1], i, 0)),
                      other(0), other(1), other(2)],
            out_specs=pl.BlockSpec((tr, w), lambda i, s: (i, 0))),
        compiler_params=_cparams(("parallel",)),
    )(idx, part, got, recv, recv, recv)


def _adamw_math(w, g, m, v):
    m = ADAM_B1 * m + (1.0 - ADAM_B1) * g
    v = ADAM_B2 * v + (1.0 - ADAM_B2) * (g * g)
    m_hat = m / (1.0 - ADAM_B1 ** ADAM_STEP)
    v_hat = v / (1.0 - ADAM_B2 ** ADAM_STEP)
    delta = -ADAM_LR * (m_hat / (jnp.sqrt(v_hat) + ADAM_EPS) + ADAM_WD * w)
    return delta, m, v


def _adamw(w, g, m, v, tr):
    depth, r, width = w.shape

    def body(w_ref, g_ref, m_ref, v_ref, d_ref, nm_ref, nv_ref):
        d_ref[...], nm_ref[...], nv_ref[...] = _adamw_math(
            w_ref[...], g_ref[...], m_ref[...], v_ref[...])

    spec = pl.BlockSpec((None, tr, width), lambda l, i: (l, i, 0))
    shape = jax.ShapeDtypeStruct(w.shape, F32)
    return pl.pallas_call(
        body, name="adamw", out_shape=(shape, shape, shape), grid=(depth, r // tr),
        in_specs=[spec] * 4, out_specs=(spec, spec, spec),
        compiler_params=_cparams(("parallel", "parallel")),
    )(w, g, m, v)


def _small_allreduce_adamw(partial, w, m, v):
    rows = partial.shape[0]
    prows = w.shape[0]

    def body(p_ref, w_ref, m_ref, v_ref, tot_ref, d_ref, nm_ref, nv_ref, gath_ref,
             send_sems, recv_sems):
        x, y, c = _position()
        me = 4 * x + 2 * y + c
        copies = []
        for k in range(1, N_DEV):
            px = 1 - x if k & 4 else x
            py = 1 - y if k & 2 else y
            pc = 1 - c if k & 1 else c
            copies.append((pltpu.make_async_remote_copy(
                src_ref=p_ref, dst_ref=gath_ref.at[me],
                send_sem=send_sems.at[k - 1], recv_sem=recv_sems.at[k - 1],
                device_id=(px, py, pc), device_id_type=MESH), 4 * px + 2 * py + pc))
        for cp, _ in copies:
            cp.start()
        gath_ref[me] = p_ref[...]
        for k, (cp, peer) in enumerate(copies):
            pltpu.make_async_remote_copy(
                src_ref=p_ref, dst_ref=gath_ref.at[peer],
                send_sem=send_sems.at[k], recv_sem=recv_sems.at[k],
                device_id=(x, y, c), device_id_type=MESH).wait_recv()
        for cp, _ in copies:
            cp.wait_send()
        tot = gath_ref[0]
        for d in range(1, N_DEV):
            tot = tot + gath_ref[d]
        tot_ref[...] = tot
        d_ref[...], nm_ref[...], nv_ref[...] = _adamw_math(
            w_ref[...], tot[0:prows, :], m_ref[...], v_ref[...])

    vm = pl.BlockSpec(memory_space=pltpu.VMEM)
    pshape = jax.ShapeDtypeStruct(w.shape, F32)
    return pl.pallas_call(
        body, name="small_allreduce_adamw",
        out_shape=(jax.ShapeDtypeStruct(partial.shape, F32), pshape, pshape, pshape),
        in_specs=[vm] * 4, out_specs=(vm, vm, vm, vm),
        scratch_shapes=[pltpu.VMEM((N_DEV, rows, LANES), F32),
                        pltpu.SemaphoreType.DMA((N_DEV - 1,)), pltpu.SemaphoreType.DMA((N_DEV - 1,))],
        compiler_params=pltpu.CompilerParams(vmem_limit_bytes=VMEM_LIMIT),
    )(partial, w, m, v)


def _pack_rows(vec):
    depth, n = vec.shape
    rows = -(-n // LANES)
    rows = -(-rows // 8) * 8
    return jnp.pad(vec, ((0, 0), (0, rows * LANES - n))).reshape(depth, rows, LANES)


def _pack_small(named):
    blocks = [_pack_rows(a) for a in named]
    extents = [(b.shape[1], a.shape[1]) for b, a in zip(blocks, named)]
    depth = named[0].shape[0]
    packed = jnp.concatenate(blocks, axis=1).reshape(depth * sum(r for r, _ in extents), LANES)
    return packed, extents


def _unpack_small(packed, extents, depth):
    per_layer = sum(r for r, _ in extents)
    packed = packed.reshape(depth, per_layer, LANES)
    out, r0 = [], 0
    for rows, n in extents:
        out.append(packed[:, r0:r0 + rows, :].reshape(depth, rows * LANES)[:, :n])
        r0 += rows
    return out


def kernel(x, w_in, b_in, conv_w, conv_b, conv_ln_g, conv_ln_b, sinks, w_out, b_out, ln_g, ln_b, loss_target, m_w_in, m_b_in, m_conv_w, m_conv_b, m_conv_ln_g, m_conv_ln_b, m_sinks, m_w_out, m_b_out, m_ln_g, m_ln_b, v_w_in, v_b_in, v_conv_w, v_conv_b, v_conv_ln_g, v_conv_ln_b, v_sinks, v_w_out, v_b_out, v_ln_g, v_ln_b):
    depth, d, din_shard = w_in.shape
    s = x.shape[1]
    c_shard = conv_w.shape[2]
    dm = _Dims(s, d, N_DEV * c_shard, depth)
    assert dm.din == N_DEV * din_shard and x.shape[0] == 1 and sinks.shape[1] == dm.nh
    d_shard = w_out.shape[1]
    c, din = dm.c, dm.din

    t_mix = _tile(s, 256, WINDOW)
    tm_row = _tile(s, 256, 8)
    tm_big = _tile(s, 1024, 8)

    w_in_b, w_out_b = w_in.astype(BF16), w_out.astype(BF16)
    conv_w_pad = jnp.pad(conv_w, ((0, 0), (0, CONV_ROWS - CONV_WIDTH), (0, 0)))
    first = _run_copies([_gather_own([w_in_b[0], w_out_b[0], conv_w_pad])], "weights_gather_own")
    g_in, g_out, g_conv = _run_copies([_gather_forward(first)], "weights_gather_forward")
    conv_w_full = g_conv.transpose(1, 2, 0, 3).reshape(depth, CONV_ROWS, c)
    by_cols = lambda g: g.transpose(1, 0, 2).reshape(d, din)
    by_rows = lambda g: g.reshape(d, d)

    xs = x[0]
    xb = xs.astype(BF16)
    saved = []
    loss_part = dout = None
    for l in range(depth):
        w_in_l = by_cols(g_in)
        riders = []
        if l + 1 < depth:
            riders.append(_gather_own([w_in_b[l + 1]]))
        if l > 0:
            riders.append(_gather_forward([g_out]))
        if riders:
            proj, landed = _matmul(xb, w_in_l, tm=tm_big, tn=_tile(din, 768), tk=d, out_dtype=F32,
                                   name="in_proj", bias=b_in[l][None, :], copies=riders)
            g_out = landed[-1] if l > 0 else g_out
            g_in_next = landed[0] if l + 1 < depth else None
        else:
            proj = _matmul(xb, w_in_l, tm=tm_big, tn=_tile(din, 768), tk=d, out_dtype=F32,
                           name="in_proj", bias=b_in[l][None, :])
        ymix, conv = _mixer_fwd(dm, proj, conv_w_full[l], conv_b[l][None, :],
                                conv_ln_g[l][None, :], conv_ln_b[l][None, :], sinks[l], t_mix)
        w_out_l = by_rows(g_out)
        target = loss_target[0] if l == depth - 1 else None
        riders = []
        if l + 1 < depth:
            riders = [_gather_forward([g_in_next]), _gather_own([w_out_b[l + 1]])]
        res, landed = _outproj_ln(dm, ymix, w_out_l, b_out[l][None, :], xs, ln_g[l][None, :],
                                  ln_b[l][None, :], target, tm_row, copies=riders)
        saved.append((xb, proj, ymix, conv, res[0], w_in_l, w_out_l))
        if l + 1 < depth:
            g_in, g_out = landed
            xs, xb = res[1], res[2]
        else:
            dout, loss_part = res[1], res[2]

    g_w_in, g_w_out = [None] * depth, [None] * depth
    small_parts = [None] * depth
    dconv_w = [None] * depth
    tr_in, tr_out = _tile(d, 512, 8), _tile(d_shard, 256, 8)
    parts_in = None
    for l in reversed(range(depth)):
        xb, proj, ymix, conv, z, w_in_l, w_out_l = saved[l]
        dz, dzb, ln_small = _ln_bwd(dm, dout, z, ln_g[l][None, :], tm_row)
        mm = dict(tb=True, tm=tm_big, tn=_tile(d, 1024), tk=d, out_dtype=F32, name="dymix")
        if parts_in is not None:
            dymix, (got_in,) = _matmul(dzb, w_out_l, copies=[_scatter_sibling([parts_in])], **mm)
            sums_in = _pair_sum(parts_in, got_in, tr_in)
        else:
            dymix = _matmul(dzb, w_out_l, **mm)
        dw_out = _matmul(ymix, dzb, ta=True, tm=_tile(d, 1024), tn=_tile(d, 1024),
                         tk=_tile(s, 2048), out_dtype=F32, name="dw_out")
        dproj, dcw, conv_small, dbin, dsk = _mixer_bwd(
            dm, proj, dymix, conv, conv_w_full[l], conv_b[l][None, :], conv_ln_g[l][None, :],
            conv_ln_b[l][None, :], sinks[l], t_mix)
        parts_out = dw_out.reshape(N_CHIPS, 2, d_shard, d)
        riders = [_scatter_sibling([parts_out])]
        if parts_in is not None:
            riders.append(_scatter_chips([sums_in]))
        dout, landed = _matmul(dproj, w_in_l, tb=True, tm=_tile(s, 512), tn=_tile(d, 1024),
                               tk=_tile(din, 2688), out_dtype=F32, name="dx", resid=dz,
                               resid_scale=dm.alpha, copies=riders)
        got_out = landed[0]
        if parts_in is not None:
            g_w_in[l + 1] = _final_sum(parts_in, got_in, landed[1], tr_in)
        sums_out = _pair_sum(parts_out, got_out, tr_out)
        dw_in, (recv_out,) = _matmul(xb, dproj, ta=True, tm=_tile(d, 1024), tn=_tile(din, 768),
                                     tk=_tile(s, 2048), out_dtype=F32, name="dw_in",
                                     copies=[_scatter_chips([sums_out])])
        g_w_out[l] = _final_sum(parts_out, got_out, recv_out, tr_out)
        parts_in = dw_in.reshape(d, N_DEV, din_shard).transpose(1, 0, 2).reshape(
            N_CHIPS, 2, d, din_shard)
        small_parts[l] = [dbin[0], conv_small[0], conv_small[1], conv_small[2], dsk[0, :dm.nh],
                          ln_small[2], ln_small[0], ln_small[1]]
        dconv_w[l] = dcw
    grad_x = dout[None]
    got_in, = _run_copies([_scatter_sibling([parts_in])], "grad_sibling_exchange")
    sums_in = _pair_sum(parts_in, got_in, tr_in)
    recv_in, = _run_copies([_scatter_chips([sums_in])], "grad_chip_exchange")
    g_w_in[0] = _final_sum(parts_in, got_in, recv_in, tr_in)

    small_w = [b_in, conv_b, conv_ln_g, conv_ln_b, sinks, b_out, ln_g, ln_b]
    small_m = [m_b_in, m_conv_b, m_conv_ln_g, m_conv_ln_b, m_sinks, m_b_out, m_ln_g, m_ln_b]
    small_v = [v_b_in, v_conv_b, v_conv_ln_g, v_conv_ln_b, v_sinks, v_b_out, v_ln_g, v_ln_b]
    n_small = len(small_w)
    packed_g, extents = _pack_small([jnp.stack([small_parts[l][k] for l in range(depth)])
                                     for k in range(n_small)])
    packed_w, _ = _pack_small(small_w)
    packed_m, _ = _pack_small(small_m)
    packed_v, _ = _pack_small(small_v)
    prows = packed_g.shape[0]
    conv_rows = depth * CONV_ROWS * c // LANES
    partial = jnp.concatenate(
        [packed_g, jnp.stack(dconv_w).reshape(conv_rows, LANES), loss_part], axis=0)
    total, sm_delta, sm_m, sm_v = _small_allreduce_adamw(partial, packed_w, packed_m, packed_v)
    loss = total[prows + conv_rows, 0]
    dconv_w_full = total[prows:prows + conv_rows].reshape(depth, CONV_ROWS, c)
    me = 4 * lax.axis_index("x") + 2 * lax.axis_index("y") + lax.axis_index("c")
    grad_conv_w = lax.dynamic_slice_in_dim(dconv_w_full, me * c_shard, c_shard, axis=2)[:, :CONV_WIDTH]

    grads_small = _unpack_small(total[:prows], extents, depth)
    delta_small = _unpack_small(sm_delta, extents, depth)
    newm_small = _unpack_small(sm_m, extents, depth)
    newv_small = _unpack_small(sm_v, extents, depth)

    grad_w_in = jnp.stack(g_w_in)
    grad_w_out = jnp.stack(g_w_out)
    d_w_in, nm_w_in, nv_w_in = _adamw(w_in, grad_w_in, m_w_in, v_w_in, _tile(d, 512, 8))
    d_w_out, nm_w_out, nv_w_out = _adamw(w_out, grad_w_out, m_w_out, v_w_out, _tile(d_shard, 256, 8))
    d_cw, nm_cw, nv_cw = _adamw(conv_w, grad_conv_w, m_conv_w, v_conv_w, CONV_WIDTH)

    def assemble(w_in_leaf, conv_w_leaf, w_out_leaf, small):
        b_in_, conv_b_, cg_, cb_, sinks_, b_out_, ln_g_, ln_b_ = small
        return [w_in_leaf, b_in_, conv_w_leaf, conv_b_, cg_, cb_, sinks_, w_out_leaf, b_out_,
                ln_g_, ln_b_]

    return (loss, grad_x,
            *assemble(grad_w_in, grad_conv_w, grad_w_out, grads_small),
            *assemble(d_w_in, d_cw, d_w_out, delta_small),
            *assemble(nm_w_in, nm_cw, nm_w_out, newm_small),
            *assemble(nv_w_in, nv_cw, nv_w_out, newv_small))
```

```python
import functools

import jax
import jax.numpy as jnp
from jax import lax
from jax.experimental import pallas as pl
from jax.experimental.pallas import tpu as pltpu

F32 = jnp.float32
BF16 = jnp.bfloat16
MESH = pl.DeviceIdType.MESH

N_DEV = 8
N_CHIPS = 4
HEAD_DIM = 64
N_KV = 2
KV_W = N_KV * HEAD_DIM
CONV_WIDTH = 31
CONV_ROWS = 32
HALO = 32
WINDOW = 128
LN_EPS = 1e-5
NEG_INF = -1e30
LANES = 128

ADAM_LR = 0.001
ADAM_B1 = 0.9
ADAM_B2 = 0.999
ADAM_EPS = 1e-08
ADAM_WD = 0.01
ADAM_STEP = 10

VMEM_LIMIT = 56 * 1024 * 1024


def _tile(n, target, align=LANES):
    best = None
    for t in range(align, min(n, target) + 1, align):
        if n % t == 0:
            best = t
    return n if best is None else best


def _sigmoid(x):
    return jax.nn.sigmoid(x)


def _dsilu(x, s):
    return s * (1.0 + x * (1.0 - s))


def _cparams(sem, vmem=VMEM_LIMIT):
    return pltpu.CompilerParams(dimension_semantics=sem, vmem_limit_bytes=vmem)


_ANY = pl.BlockSpec(memory_space=pl.ANY)


def _position():
    return lax.axis_index("x"), lax.axis_index("y"), lax.axis_index("c")


def _other_chips(x, y):
    return [(1 - x, y), (x, 1 - y), (1 - x, 1 - y)]


class _Copies:
    def __init__(self, operands, landing, alias, sems, start, finish):
        self.operands, self.landing, self.alias, self.sems = operands, landing, alias, sems
        self.start, self.finish = start, finish


class _Hosting:
    def __init__(self, groups, n_in, n_out):
        self.groups = groups
        self.args = [a for g in groups for a in g.operands]
        self.out_shapes = [s for g in groups for s in g.landing]
        self.scratch = [pltpu.SemaphoreType.DMA(g.sems) for g in groups for _ in range(2)]
        self.aliases = {}
        i0, o0 = n_in, n_out
        for g in groups:
            for a, b in g.alias.items():
                self.aliases[i0 + a] = o0 + b
            i0 += len(g.operands)
            o0 += len(g.landing)

    def _each(self, in_refs, out_refs, sem_refs):
        i0 = o0 = 0
        for n, g in enumerate(self.groups):
            yield (g, in_refs[i0:i0 + len(g.operands)], out_refs[o0:o0 + len(g.landing)],
                   sem_refs[2 * n], sem_refs[2 * n + 1])
            i0 += len(g.operands)
            o0 += len(g.landing)

    def start(self, in_refs, out_refs, sem_refs):
        for g, ins, outs, send, recv in self._each(in_refs, out_refs, sem_refs):
            g.start(ins, outs, send, recv)

    def finish(self, in_refs, out_refs, sem_refs):
        for g, ins, outs, send, recv in self._each(in_refs, out_refs, sem_refs):
            g.finish(ins, outs, send, recv)


def _run_copies(groups, name):
    host = _Hosting(groups, 0, 0)
    n_in, n_out = len(host.args), len(host.out_shapes)

    def body(*refs):
        ins, outs, sems = refs[:n_in], refs[n_in:n_in + n_out], refs[n_in + n_out:]
        host.start(ins, outs, sems)
        host.finish(ins, outs, sems)

    return pl.pallas_call(
        body, name=name, out_shape=host.out_shapes, in_specs=[_ANY] * n_in,
        out_specs=[_ANY] * n_out, scratch_shapes=host.scratch,
        input_output_aliases=host.aliases,
    )(*host.args)


def _gather_own(shards):
    n = len(shards)

    def copies(ins, outs, send, recv):
        x, y, c = _position()
        peers = [(x, y, 1 - c)] + [(px, py, c) for px, py in _other_chips(x, y)]
        out = []
        for t in range(n):
            for k, peer in enumerate(peers):
                out.append((pltpu.make_async_remote_copy(
                    src_ref=ins[t], dst_ref=outs[t].at[4 * x + 2 * y + c],
                    send_sem=send.at[t, k], recv_sem=recv.at[t, k],
                    device_id=peer, device_id_type=MESH), t, k, peer))
        local = [pltpu.make_async_copy(ins[t], outs[t].at[4 * x + 2 * y + c], send.at[t, 4])
                 for t in range(n)]
        return out, local

    def start(ins, outs, send, recv):
        remote, local = copies(ins, outs, send, recv)
        for cp in local:
            cp.start()
        for cp, _, _, _ in remote:
            cp.start()

    def finish(ins, outs, send, recv):
        remote, local = copies(ins, outs, send, recv)
        x, y, c = _position()
        for _, t, k, (px, py, pc) in remote:
            pltpu.make_async_remote_copy(
                src_ref=ins[t], dst_ref=outs[t].at[4 * px + 2 * py + pc],
                send_sem=send.at[t, k], recv_sem=recv.at[t, k],
                device_id=(x, y, c), device_id_type=MESH).wait_recv()
        for cp, _, _, _ in remote:
            cp.wait_send()
        for cp in local:
            cp.wait()

    landing = [jax.ShapeDtypeStruct((N_DEV,) + s.shape, s.dtype) for s in shards]
    return _Copies(list(shards), landing, {}, (n, 5), start, finish)


def _gather_forward(buffers):
    n = len(buffers)

    def copies(ins, outs, send, recv):
        x, y, c = _position()
        out = []
        for t in range(n):
            for j, (px, py) in enumerate(_other_chips(x, y)):
                slot = 4 * px + 2 * py + c
                out.append((pltpu.make_async_remote_copy(
                    src_ref=ins[t].at[slot], dst_ref=outs[t].at[slot],
                    send_sem=send.at[t, j], recv_sem=recv.at[t, j],
                    device_id=(x, y, 1 - c), device_id_type=MESH), t, j, 4 * px + 2 * py + 1 - c))
        return out

    def start(ins, outs, send, recv):
        for cp, _, _, _ in copies(ins, outs, send, recv):
            cp.start()

    def finish(ins, outs, send, recv):
        x, y, c = _position()
        mine = copies(ins, outs, send, recv)
        for _, t, j, got in mine:
            pltpu.make_async_remote_copy(
                src_ref=ins[t].at[got], dst_ref=outs[t].at[got],
                send_sem=send.at[t, j], recv_sem=recv.at[t, j],
                device_id=(x, y, c), device_id_type=MESH).wait_recv()
        for cp, _, _, _ in mine:
            cp.wait_send()

    landing = [jax.ShapeDtypeStruct(b.shape, b.dtype) for b in buffers]
    return _Copies(list(buffers), landing, {t: t for t in range(n)}, (n, 3), start, finish)


def _scatter_sibling(parts):
    n = len(parts)

    def copies(ins, outs, send, recv):
        x, y, c = _position()
        return [pltpu.make_async_remote_copy(
            src_ref=ins[t].at[:, 1 - c], dst_ref=outs[t],
            send_sem=send.at[t, 0], recv_sem=recv.at[t, 0],
            device_id=(x, y, 1 - c), device_id_type=MESH) for t in range(n)]

    def start(ins, outs, send, recv):
        for cp in copies(ins, outs, send, recv):
            cp.start()

    def finish(ins, outs, send, recv):
        for cp in copies(ins, outs, send, recv):
            cp.wait()

    landing = [jax.ShapeDtypeStruct((p.shape[0],) + p.shape[2:], p.dtype) for p in parts]
    return _Copies(list(parts), landing, {}, (n, 1), start, finish)


def _scatter_chips(blocks):
    n = len(blocks)

    def start(ins, outs, send, recv):
        x, y, c = _position()
        for t in range(n):
            for j, (px, py) in enumerate(_other_chips(x, y)):
                pltpu.make_async_remote_copy(
                    src_ref=ins[t].at[2 * px + py], dst_ref=outs[t].at[2 * x + y],
                    send_sem=send.at[t, j], recv_sem=recv.at[t, j],
                    device_id=(px, py, c), device_id_type=MESH).start()

    def finish(ins, outs, send, recv):
        x, y, c = _position()
        for t in range(n):
            for j, (px, py) in enumerate(_other_chips(x, y)):
                cp = pltpu.make_async_remote_copy(
                    src_ref=ins[t].at[2 * px + py], dst_ref=outs[t].at[2 * px + py],
                    send_sem=send.at[t, j], recv_sem=recv.at[t, j],
                    device_id=(px, py, c), device_id_type=MESH)
                cp.wait_recv()
                cp.wait_send()

    landing = [jax.ShapeDtypeStruct(b.shape, b.dtype) for b in blocks]
    return _Copies(list(blocks), landing, {}, (n, 3), start, finish)


def _matmul(a, b, *, ta=False, tb=False, tm, tn, tk, out_dtype, name, bias=None, resid=None,
            resid_scale=1.0, copies=()):
    m, k = (a.shape[1], a.shape[0]) if ta else a.shape
    n = b.shape[0] if tb else b.shape[1]
    assert (b.shape[1] if tb else b.shape[0]) == k
    assert m % tm == 0 and n % tn == 0 and k % tk == 0
    ni, nj, nk = m // tm, n // tn, k // tk
    dn = (((0 if ta else 1,), (1 if tb else 0,)), ((), ()))
    n_in = 2 + (bias is not None) + (resid is not None)
    host = _Hosting(list(copies), n_in, 1)
    n_hin, n_hout = len(host.args), len(host.out_shapes)

    def body(*refs):
        a_ref, b_ref = refs[0], refs[1]
        pos = 2
        bias_ref = resid_ref = None
        if bias is not None:
            bias_ref = refs[pos]
            pos += 1
        if resid is not None:
            resid_ref = refs[pos]
            pos += 1
        h_in = refs[pos:pos + n_hin]
        pos += n_hin
        o_ref = refs[pos]
        h_out = refs[pos + 1:pos + 1 + n_hout]
        pos += 1 + n_hout
        acc_ref = refs[pos] if nk > 1 else None
        h_sems = refs[pos + (nk > 1):]
        step = (pl.program_id(0) * nj + pl.program_id(1)) * nk + pl.program_id(2)

        if copies:
            @pl.when(step == 0)
            def _():
                host.start(h_in, h_out, h_sems)

        def finish(acc):
            if bias_ref is not None:
                acc = acc + bias_ref[...]
            if resid_ref is not None:
                acc = acc + resid_scale * resid_ref[...]
            o_ref[...] = acc.astype(out_dtype)

        p = lax.dot_general(a_ref[...], b_ref[...], dn, preferred_element_type=F32)
        if nk == 1:
            finish(p)
        else:
            kk = pl.program_id(2)

            @pl.when(kk == 0)
            def _():
                acc_ref[...] = p

            @pl.when(kk > 0)
            def _():
                acc_ref[...] += p

            @pl.when(kk == nk - 1)
            def _():
                finish(acc_ref[...])

        if copies:
            @pl.when(step == ni * nj * nk - 1)
            def _():
                host.finish(h_in, h_out, h_sems)

    a_spec = (pl.BlockSpec((tk, tm), lambda i, j, kk: (kk, i)) if ta
              else pl.BlockSpec((tm, tk), lambda i, j, kk: (i, kk)))
    b_spec = (pl.BlockSpec((tn, tk), lambda i, j, kk: (j, kk)) if tb
              else pl.BlockSpec((tk, tn), lambda i, j, kk: (kk, j)))
    in_specs = [a_spec, b_spec]
    args = [a, b]
    if bias is not None:
        in_specs.append(pl.BlockSpec((1, tn), lambda i, j, kk: (0, j)))
        args.append(bias)
    if resid is not None:
        in_specs.append(pl.BlockSpec((tm, tn), lambda i, j, kk: (i, j)))
        args.append(resid)
    res = pl.pallas_call(
        body, name=name,
        out_shape=[jax.ShapeDtypeStruct((m, n), out_dtype)] + host.out_shapes,
        grid=(ni, nj, nk),
        in_specs=in_specs + [_ANY] * n_hin,
        out_specs=[pl.BlockSpec((tm, tn), lambda i, j, kk: (i, j))] + [_ANY] * n_hout,
        scratch_shapes=([pltpu.VMEM((tm, tn), F32)] if nk > 1 else []) + host.scratch,
        input_output_aliases=host.aliases,
        compiler_params=_cparams(("arbitrary",) * 3 if copies else
                                 ("parallel", "parallel", "arbitrary")),
    )(*args, *host.args)
    return (res[0], res[1:]) if copies else res[0]


class _Dims:
    def __init__(self, s, d, c, depth):
        self.s, self.d, self.c, self.depth = s, d, c, depth
        self.a = d - c
        self.nh = self.a // HEAD_DIM
        self.group = self.nh // N_KV
        self.din = 3 * c + 2 * self.a + 2 * KV_W
        self.o_q = 3 * c
        self.o_k = 3 * c + self.a
        self.o_v = self.o_k + KV_W
        self.o_ag = self.o_k + 2 * KV_W
        self.alpha = (2 * depth) ** 0.25
        assert self.nh % 2 == 0 and self.group % 2 == 0 and self.o_k % (2 * KV_W) == 0
        assert c % LANES == 0 and self.a % LANES == 0


SUBLANES = 8
TAP_ROWS = 64


def _shift_copies(src_ref, sh_ref, lanes, rows):
    for r in range(1, SUBLANES):
        sh_ref[r - 1, 0:rows, :] = src_ref[pl.ds(r, rows), lanes]


def _tap_rows(src_ref, sh_ref, lanes, off, start, rows):
    r = off % SUBLANES
    at = pl.multiple_of(start + (off - r), SUBLANES)
    if r == 0:
        return src_ref[pl.ds(at, rows), lanes]
    return sh_ref[r - 1, pl.ds(at, rows), :]


def _conv_chunk(w_ref, src_ref, sh_ref, out_ref, lanes, t, first_off, reverse, bias_ref=None):
    def block(it, carry):
        start = pl.multiple_of(it * TAP_ROWS, TAP_ROWS)
        acc = None
        for j in range(CONV_WIDTH):
            off = first_off - j if reverse else first_off + j
            term = w_ref[j:j + 1, lanes] * _tap_rows(src_ref, sh_ref, lanes, off, start, TAP_ROWS)
            acc = term if acc is None else acc + term
        if bias_ref is not None:
            acc = acc + bias_ref[:, lanes]
        out_ref[pl.ds(start, TAP_ROWS), lanes] = acc
        return carry

    lax.fori_loop(0, t // TAP_ROWS, block, 0)


def _conv_w_grad_chunk(src_ref, sh_ref, dconv_ref, acc_ref, lanes, t, first_off):
    def block(it, accs):
        start = pl.multiple_of(it * SUBLANES, SUBLANES)
        dv = dconv_ref[pl.ds(start, SUBLANES), lanes]
        return tuple(
            acc + _tap_rows(src_ref, sh_ref, lanes, first_off + j, start, SUBLANES) * dv
            for j, acc in enumerate(accs))

    zero = jnp.zeros((SUBLANES, LANES), F32)
    accs = lax.fori_loop(0, t // SUBLANES, block, (zero,) * CONV_WIDTH, unroll=2)
    for j in range(CONV_WIDTH):
        acc_ref[j, :, lanes] += accs[j]


def _kv_operands(kv_ref, lo):
    kext = kv_ref[:, 0:KV_W]
    vext = kv_ref[:, KV_W:2 * KV_W]
    ksw = pltpu.roll(kext, HEAD_DIM, 1)
    vsw = pltpu.roll(vext, HEAD_DIM, 1)
    zero = jnp.zeros_like(kext)
    k2 = [jnp.where(lo, kext, ksw).astype(BF16), jnp.where(lo, ksw, kext).astype(BF16)]
    khalf = [[jnp.where(lo, kext, zero).astype(BF16), jnp.where(lo, zero, ksw).astype(BF16)],
             [jnp.where(lo, ksw, zero).astype(BF16), jnp.where(lo, zero, kext).astype(BF16)]]
    vhalf = [[jnp.where(lo, vext, zero).astype(BF16), jnp.where(lo, zero, vsw).astype(BF16)],
             [jnp.where(lo, vsw, zero).astype(BF16), jnp.where(lo, zero, vext).astype(BF16)]]
    return k2, khalf, vhalf


SCALE = HEAD_DIM ** -0.5


def _from_previous():
    row = lax.broadcasted_iota(jnp.int32, (WINDOW, WINDOW), 0)
    col = lax.broadcasted_iota(jnp.int32, (WINDOW, WINDOW), 1)
    return col > row


def _band_merge(both, prev):
    return jnp.where(prev, both[:, 0:WINDOW], both[:, WINDOW:2 * WINDOW])


def _band_split(tile, prev):
    zero = jnp.zeros_like(tile)
    return jnp.concatenate([jnp.where(prev, tile, zero), jnp.where(prev, zero, tile)],
                           axis=1).astype(BF16)


def _softmax_with_sink(qm, k2rows, prev, no_previous, sink):
    both = lax.dot_general(qm, k2rows, (((1,), (1,)), ((), ())), preferred_element_type=F32)
    s_prev = both[:, 0:WINDOW]
    if no_previous is not None:
        s_prev = jnp.where(no_previous, NEG_INF, s_prev)
    s = jnp.where(prev, s_prev, both[:, WINDOW:2 * WINDOW])
    m = jnp.maximum(jnp.max(s, axis=1, keepdims=True), sink)
    e = jnp.exp(s - m)
    den = jnp.sum(e, axis=1, keepdims=True) + jnp.exp(sink - m)
    inv = 1.0 / den
    return e * inv, m, inv


def _mixer_specs(dm, t, idx):
    return [
        pl.BlockSpec((t, dm.din), lambda g: (idx(g), 0)),
        pl.BlockSpec((HALO, 2 * dm.c), lambda g: (jnp.maximum(idx(g) * (t // HALO) - 1, 0), 0)),
        pl.BlockSpec((WINDOW, 2 * KV_W),
                     lambda g: (jnp.maximum(idx(g) * (t // WINDOW) - 1, 0), dm.o_k // (2 * KV_W))),
    ]


def _mixer_fwd(dm, proj, conv_w, conv_b, cln_g, cln_b, sinks, t):
    c, nq = dm.c, t // WINDOW

    def body(sinks_ref, pr_ref, ch_ref, kvh_ref, cw_ref, cb_ref, cg_ref, cbb_ref,
             y_ref, conv_ref, hext_ref, kv_ref, hs_ref):
        i = pl.program_id(0)
        first = i == 0
        h = pr_ref[:, 0:c] * _sigmoid(pr_ref[:, c:2 * c])
        hh = ch_ref[:, 0:c] * _sigmoid(ch_ref[:, c:2 * c])
        hext_ref[0:HALO, :] = jnp.where(first, 0.0, hh)
        hext_ref[HALO:HALO + t, :] = h
        for k in range(c // LANES):
            lanes = slice(LANES * k, LANES * (k + 1))
            _shift_copies(hext_ref, hs_ref, lanes, t + HALO - SUBLANES)
            _conv_chunk(cw_ref, hext_ref, hs_ref, conv_ref, lanes, t, HALO - (CONV_WIDTH - 1),
                        False, cb_ref)
        conv = conv_ref[...]
        mu = jnp.mean(conv, axis=1, keepdims=True)
        dlt = conv - mu
        var = jnp.mean(dlt * dlt, axis=1, keepdims=True)
        u = dlt * lax.rsqrt(var + LN_EPS) * cg_ref[...] + cbb_ref[...]
        gate = pr_ref[:, 2 * c:3 * c]
        y_ref[:, 0:c] = (u * _sigmoid(u) * (gate * _sigmoid(gate))).astype(BF16)

        kv_ref[0:WINDOW, :] = jnp.where(first, 0.0, kvh_ref[...])
        kv_ref[WINDOW:WINDOW + t, :] = pr_ref[:, dm.o_k:dm.o_k + 2 * KV_W]
        lo = lax.broadcasted_iota(jnp.int32, (1, LANES), 1) < HEAD_DIM
        k2, _, vhalf = _kv_operands(kv_ref, lo)
        prev = _from_previous()
        for qb in range(nq):
            r0 = qb * WINDOW
            no_previous = first if qb == 0 else None
            for p in range(dm.nh // 2):
                kvh = (2 * p) // dm.group
                qp = pr_ref[r0:r0 + WINDOW, dm.o_q + LANES * p:dm.o_q + LANES * (p + 1)] * SCALE
                opair = None
                for half in range(2):
                    qm = jnp.where(lo if half == 0 else jnp.logical_not(lo), qp, 0.0).astype(BF16)
                    prob, _, _ = _softmax_with_sink(qm, k2[kvh][r0:r0 + 2 * WINDOW], prev,
                                                    no_previous, sinks_ref[2 * p + half])
                    o_h = jnp.dot(_band_split(prob, prev), vhalf[kvh][half][r0:r0 + 2 * WINDOW],
                                  preferred_element_type=F32)
                    opair = o_h if opair is None else opair + o_h
                ag = pr_ref[r0:r0 + WINDOW, dm.o_ag + LANES * p:dm.o_ag + LANES * (p + 1)]
                y_ref[r0:r0 + WINDOW, c + LANES * p:c + LANES * (p + 1)] = (
                    opair * (ag * _sigmoid(ag))).astype(BF16)

    vec = pl.BlockSpec((1, c), lambda g: (0, 0))
    return pl.pallas_call(
        body, name="mixer_fwd",
        out_shape=(jax.ShapeDtypeStruct((dm.s, dm.d), BF16),
                   jax.ShapeDtypeStruct((dm.s, c), F32)),
        grid=(dm.s // t,),
        in_specs=[pl.BlockSpec(memory_space=pltpu.SMEM)] + _mixer_specs(dm, t, lambda g: g)
        + [pl.BlockSpec((CONV_ROWS, c), lambda g: (0, 0)), vec, vec, vec],
        out_specs=(pl.BlockSpec((t, dm.d), lambda g: (g, 0)), pl.BlockSpec((t, c), lambda g: (g, 0))),
        scratch_shapes=[pltpu.VMEM((t + HALO, c), F32), pltpu.VMEM((t + WINDOW, 2 * KV_W), F32),
                        pltpu.VMEM((SUBLANES - 1, t + HALO, LANES), F32)],
        compiler_params=_cparams(("arbitrary",)),
    )(sinks, proj, proj, proj, conv_w, conv_b, cln_g, cln_b)


def _mixer_bwd(dm, proj, dymix, conv, conv_w, conv_b, cln_g, cln_b, sinks, t):
    c, nq, nt = dm.c, t // WINDOW, dm.s // t

    def body(sinks_ref, pr_ref, ch_ref, kvh_ref, dy_ref, cv_ref, cw_ref, cb_ref, cg_ref, cbb_ref,
             dpr_ref, dcw_ref, dsm_ref, dbin_ref, dsk_ref,
             hext_ref, kv_ref, dcx_ref, dkv_ref, carry_dc_ref, carry_kv_ref,
             hs_ref, ds_ref, dh_ref, dcw_acc_ref):
        g = pl.program_id(0)
        i = nt - 1 - g
        first = i == 0

        @pl.when(g == 0)
        def _():
            dcw_acc_ref[...] = jnp.zeros_like(dcw_acc_ref)
            dsm_ref[...] = jnp.zeros_like(dsm_ref)
            dbin_ref[...] = jnp.zeros_like(dbin_ref)
            dsk_ref[...] = jnp.zeros_like(dsk_ref)
            carry_dc_ref[...] = jnp.zeros_like(carry_dc_ref)
            carry_kv_ref[...] = jnp.zeros_like(carry_kv_ref)

        def emit(col, width, val):
            dpr_ref[:, col:col + width] = val.astype(BF16)
            dbin_ref[0:1, col:col + width] += jnp.sum(val, axis=0, keepdims=True)

        val = pr_ref[:, 0:c]
        sg = _sigmoid(pr_ref[:, c:2 * c])
        h = val * sg
        hh = ch_ref[:, 0:c] * _sigmoid(ch_ref[:, c:2 * c])
        hext_ref[0:HALO, :] = jnp.where(first, 0.0, hh)
        hext_ref[HALO:HALO + t, :] = h
        conv = cv_ref[...]
        mu = jnp.mean(conv, axis=1, keepdims=True)
        dlt = conv - mu
        var = jnp.mean(dlt * dlt, axis=1, keepdims=True)
        rstd = lax.rsqrt(var + LN_EPS)
        xhat = dlt * rstd
        u = xhat * cg_ref[...] + cbb_ref[...]
        su = _sigmoid(u)
        gate = pr_ref[:, 2 * c:3 * c]
        sgate = _sigmoid(gate)
        dyc = dy_ref[:, 0:c]
        emit(2 * c, c, dyc * (u * su) * _dsilu(gate, sgate))
        du = dyc * (gate * sgate) * _dsilu(u, su)
        dsm_ref[1:2, :] += jnp.sum(du * xhat, axis=0, keepdims=True)
        dsm_ref[2:3, :] += jnp.sum(du, axis=0, keepdims=True)
        dxh = du * cg_ref[...]
        dconv = rstd * (dxh - jnp.mean(dxh, axis=1, keepdims=True)
                        - xhat * jnp.mean(dxh * xhat, axis=1, keepdims=True))
        dsm_ref[0:1, :] += jnp.sum(dconv, axis=0, keepdims=True)
        dcx_ref[0:t, :] = dconv
        dcx_ref[t:t + HALO, :] = carry_dc_ref[...]
        carry_dc_ref[...] = dconv[0:HALO, :]
        for k in range(c // LANES):
            lanes = slice(LANES * k, LANES * (k + 1))
            _shift_copies(hext_ref, hs_ref, lanes, t + HALO - SUBLANES)
            _shift_copies(dcx_ref, ds_ref, lanes, t + HALO - SUBLANES)
            _conv_chunk(cw_ref, dcx_ref, ds_ref, dh_ref, lanes, t, CONV_WIDTH - 1, True)
            _conv_w_grad_chunk(hext_ref, hs_ref, dcx_ref, dcw_acc_ref, lanes, t,
                               HALO - (CONV_WIDTH - 1))
        dh = dh_ref[...]
        emit(0, c, dh * sg)
        emit(c, c, dh * val * sg * (1.0 - sg))

        kv_ref[0:WINDOW, :] = jnp.where(first, 0.0, kvh_ref[...])
        kv_ref[WINDOW:WINDOW + t, :] = pr_ref[:, dm.o_k:dm.o_k + 2 * KV_W]
        dkv_ref[0:t, :] = jnp.zeros((t, 2 * KV_W), F32)
        dkv_ref[t:t + WINDOW, :] = carry_kv_ref[...]
        lane = lax.broadcasted_iota(jnp.int32, (1, LANES), 1)
        lo = lane < HEAD_DIM
        k2, khalf, vhalf = _kv_operands(kv_ref, lo)
        tn_dims = (((0,), (0,)), ((), ()))
        nt_dims = (((1,), (1,)), ((), ()))
        dsk = jnp.zeros((1, LANES), F32)
        prev = _from_previous()
        for qb in range(nq):
            r0 = qb * WINDOW
            rows = slice(r0, r0 + 2 * WINDOW)
            no_previous = first if qb == 0 else None
            dka = [None, None]
            dva = [None, None]
            for p in range(dm.nh // 2):
                kvh = (2 * p) // dm.group
                cols = slice(LANES * p, LANES * (p + 1))
                qp = pr_ref[r0:r0 + WINDOW, dm.o_q + cols.start:dm.o_q + cols.stop] * SCALE
                ag = pr_ref[r0:r0 + WINDOW, dm.o_ag + cols.start:dm.o_ag + cols.stop]
                dya = dy_ref[r0:r0 + WINDOW, c + cols.start:c + cols.stop]
                sag = _sigmoid(ag)
                d_o = dya * (ag * sag)
                d_o_b = d_o.astype(BF16)
                opair = None
                dqpair = None
                for half in range(2):
                    hmask = lo if half == 0 else jnp.logical_not(lo)
                    sink = sinks_ref[2 * p + half]
                    qm = jnp.where(hmask, qp, 0.0).astype(BF16)
                    prob, m, inv = _softmax_with_sink(qm, k2[kvh][rows], prev, no_previous, sink)
                    pb = _band_split(prob, prev)
                    vh = vhalf[kvh][half][rows]
                    o_h = jnp.dot(pb, vh, preferred_element_type=F32)
                    opair = o_h if opair is None else opair + o_h
                    delta = jnp.sum(d_o * o_h, axis=1, keepdims=True)
                    dp = _band_merge(
                        lax.dot_general(d_o_b, vh, nt_dims, preferred_element_type=F32), prev)
                    dsb = _band_split(prob * (dp - delta), prev)
                    dsink = -jnp.sum(jnp.exp(sink - m) * inv * delta)
                    dsk = dsk + jnp.where(lane == 2 * p + half, dsink, 0.0)
                    dq_h = jnp.dot(dsb, khalf[kvh][half][rows], preferred_element_type=F32)
                    dqpair = dq_h if dqpair is None else dqpair + dq_h
                    dk_h = lax.dot_general(dsb, qm, tn_dims, preferred_element_type=F32)
                    dv_h = lax.dot_general(pb, jnp.where(hmask, d_o, 0.0).astype(BF16), tn_dims,
                                           preferred_element_type=F32)
                    dka[kvh] = dk_h if dka[kvh] is None else dka[kvh] + dk_h
                    dva[kvh] = dv_h if dva[kvh] is None else dva[kvh] + dv_h
                dqpair = dqpair * SCALE
                d_ag = dya * opair * _dsilu(ag, sag)
                dpr_ref[r0:r0 + WINDOW, dm.o_q + cols.start:dm.o_q + cols.stop] = dqpair.astype(BF16)
                dbin_ref[0:1, dm.o_q + cols.start:dm.o_q + cols.stop] += jnp.sum(
                    dqpair, axis=0, keepdims=True)
                dpr_ref[r0:r0 + WINDOW, dm.o_ag + cols.start:dm.o_ag + cols.stop] = d_ag.astype(BF16)
                dbin_ref[0:1, dm.o_ag + cols.start:dm.o_ag + cols.stop] += jnp.sum(
                    d_ag, axis=0, keepdims=True)
            fold = [x + pltpu.roll(x, HEAD_DIM, 1) for x in (dka[0], dka[1], dva[0], dva[1])]
            dkv_ref[r0:r0 + 2 * WINDOW, 0:KV_W] += jnp.where(lo, fold[0], fold[1])
            dkv_ref[r0:r0 + 2 * WINDOW, KV_W:2 * KV_W] += jnp.where(lo, fold[2], fold[3])
        dsk_ref[0:1, :] += dsk
        carry_kv_ref[...] = dkv_ref[0:WINDOW, :]
        emit(dm.o_k, 2 * KV_W, dkv_ref[WINDOW:WINDOW + t, :])

        @pl.when(g == nt - 1)
        def _():
            for j in range(CONV_WIDTH):
                dcw_ref[j:j + 1, :] = jnp.sum(dcw_acc_ref[j], axis=0, keepdims=True)
            dcw_ref[CONV_WIDTH:CONV_ROWS, :] = jnp.zeros((CONV_ROWS - CONV_WIDTH, c), F32)

    rev = lambda g: nt - 1 - g
    vec = pl.BlockSpec((1, c), lambda g: (0, 0))
    const = lambda shape: pl.BlockSpec(shape, lambda g: (0, 0))
    return pl.pallas_call(
        body, name="mixer_bwd",
        out_shape=(jax.ShapeDtypeStruct((dm.s, dm.din), BF16),
                   jax.ShapeDtypeStruct((CONV_ROWS, c), F32),
                   jax.ShapeDtypeStruct((8, c), F32),
                   jax.ShapeDtypeStruct((8, dm.din), F32),
                   jax.ShapeDtypeStruct((8, LANES), F32)),
        grid=(nt,),
        in_specs=[pl.BlockSpec(memory_space=pltpu.SMEM)] + _mixer_specs(dm, t, rev)
        + [pl.BlockSpec((t, dm.d), lambda g: (rev(g), 0)), pl.BlockSpec((t, c), lambda g: (rev(g), 0)),
           pl.BlockSpec((CONV_ROWS, c), lambda g: (0, 0)), vec, vec, vec],
        out_specs=(pl.BlockSpec((t, dm.din), lambda g: (rev(g), 0)),
                   const((CONV_ROWS, c)), const((8, c)), const((8, dm.din)), const((8, LANES))),
        scratch_shapes=[pltpu.VMEM((t + HALO, c), F32), pltpu.VMEM((t + WINDOW, 2 * KV_W), F32),
                        pltpu.VMEM((t + HALO, c), F32), pltpu.VMEM((t + WINDOW, 2 * KV_W), F32),
                        pltpu.VMEM((HALO, c), F32), pltpu.VMEM((WINDOW, 2 * KV_W), F32),
                        pltpu.VMEM((SUBLANES - 1, t + HALO, LANES), F32),
                        pltpu.VMEM((SUBLANES - 1, t + HALO, LANES), F32),
                        pltpu.VMEM((t, c), F32), pltpu.VMEM((CONV_ROWS, SUBLANES, c), F32)],
        compiler_params=_cparams(("arbitrary",)),
    )(sinks, proj, proj, proj, dymix, conv, conv_w, conv_b, cln_g, cln_b)


def _outproj_ln(dm, ymix, w_out, b_out, x, ln_g, ln_b, target, tm, copies=()):
    last = target is not None
    d = dm.d
    n_in = 7 if last else 6
    host = _Hosting(list(copies), n_in, 3)
    n_hin, n_hout = len(host.args), len(host.out_shapes)
    steps = dm.s // tm

    def body(*refs):
        y_ref, w_ref, bo_ref, x_ref, g_ref, b_ref = refs[:6]
        h_in = refs[n_in:n_in + n_hin]
        h_out = refs[n_in + n_hin + 3:n_in + n_hin + 3 + n_hout]
        h_sems = refs[n_in + n_hin + 3 + n_hout:]
        if copies:
            @pl.when(pl.program_id(0) == 0)
            def _():
                host.start(h_in, h_out, h_sems)

        z = dm.alpha * x_ref[...] + (
            jnp.dot(y_ref[...], w_ref[...], preferred_element_type=F32) + bo_ref[...])
        mu = jnp.mean(z, axis=1, keepdims=True)
        dlt = z - mu
        var = jnp.mean(dlt * dlt, axis=1, keepdims=True)
        out = dlt * lax.rsqrt(var + LN_EPS) * g_ref[...] + b_ref[...]
        if last:
            t_ref = refs[6]
            z_ref, dout_ref, loss_ref = refs[n_in + n_hin:n_in + n_hin + 3]
            z_ref[...] = z
            err = out - t_ref[...]
            dout_ref[...] = err * (1.0 / d)

            @pl.when(pl.program_id(0) == 0)
            def _():
                loss_ref[...] = jnp.zeros_like(loss_ref)

            loss_ref[...] += 0.5 * jnp.sum(jnp.mean(err * err, axis=1, keepdims=True), axis=0,
                                           keepdims=True)
        else:
            z_ref, o_ref, ob_ref = refs[n_in + n_hin:n_in + n_hin + 3]
            z_ref[...] = z
            o_ref[...] = out
            ob_ref[...] = out.astype(BF16)

        if copies:
            @pl.when(pl.program_id(0) == steps - 1)
            def _():
                host.finish(h_in, h_out, h_sems)

    row = pl.BlockSpec((tm, d), lambda i: (i, 0))
    vec = pl.BlockSpec((1, d), lambda i: (0, 0))
    in_specs = [row, pl.BlockSpec((d, d), lambda i: (0, 0)), vec, row, vec, vec]
    args = [ymix, w_out, b_out, x, ln_g, ln_b]
    act = jax.ShapeDtypeStruct((dm.s, d), F32)
    if last:
        in_specs.append(row)
        args.append(target)
        out_shape = [act, act, jax.ShapeDtypeStruct((8, LANES), F32)]
        out_specs = [row, row, pl.BlockSpec((8, LANES), lambda i: (0, 0))]
    else:
        out_shape = [act, act, jax.ShapeDtypeStruct((dm.s, d), BF16)]
        out_specs = [row, row, row]
    res = pl.pallas_call(
        body, name="outproj_ln_loss" if last else "outproj_ln",
        out_shape=out_shape + host.out_shapes, grid=(steps,),
        in_specs=in_specs + [_ANY] * n_hin, out_specs=out_specs + [_ANY] * n_hout,
        scratch_shapes=host.scratch, input_output_aliases=host.aliases,
        compiler_params=_cparams(("arbitrary",)),
    )(*args, *host.args)
    return res[:3], res[3:]


def _ln_bwd(dm, dout, z, ln_g, tm):
    d = dm.d

    def body(do_ref, z_ref, g_ref, dz_ref, dzb_ref, sm_ref):
        @pl.when(pl.program_id(0) == 0)
        def _():
            sm_ref[...] = jnp.zeros_like(sm_ref)

        z = z_ref[...]
        mu = jnp.mean(z, axis=1, keepdims=True)
        dlt = z - mu
        var = jnp.mean(dlt * dlt, axis=1, keepdims=True)
        rstd = lax.rsqrt(var + LN_EPS)
        zhat = dlt * rstd
        do = do_ref[...]
        dzh = do * g_ref[...]
        dz = rstd * (dzh - jnp.mean(dzh, axis=1, keepdims=True)
                     - zhat * jnp.mean(dzh * zhat, axis=1, keepdims=True))
        dz_ref[...] = dz
        dzb_ref[...] = dz.astype(BF16)
        sm_ref[0:1, :] += jnp.sum(do * zhat, axis=0, keepdims=True)
        sm_ref[1:2, :] += jnp.sum(do, axis=0, keepdims=True)
        sm_ref[2:3, :] += jnp.sum(dz, axis=0, keepdims=True)

    row = pl.BlockSpec((tm, d), lambda i: (i, 0))
    return pl.pallas_call(
        body, name="ln_bwd",
        out_shape=(jax.ShapeDtypeStruct((dm.s, d), F32), jax.ShapeDtypeStruct((dm.s, d), BF16),
                   jax.ShapeDtypeStruct((8, d), F32)),
        grid=(dm.s // tm,),
        in_specs=[row, row, pl.BlockSpec((1, d), lambda i: (0, 0))],
        out_specs=(row, row, pl.BlockSpec((8, d), lambda i: (0, 0))),
        compiler_params=_cparams(("arbitrary",)),
    )(dout, z, ln_g)


def _pair_sum(part, got, tr):
    _, _, r, w = part.shape

    def body(c_ref, p_ref, g_ref, o_ref):
        o_ref[...] = (p_ref[...] + g_ref[...]).astype(BF16)

    return pl.pallas_call(
        body, name="grad_pair_sum",
        out_shape=jax.ShapeDtypeStruct((N_CHIPS, r, w), BF16),
        grid_spec=pltpu.PrefetchScalarGridSpec(
            num_scalar_prefetch=1, grid=(N_CHIPS, r // tr),
            in_specs=[pl.BlockSpec((None, None, tr, w), lambda k, i, c_ref: (k, c_ref[0], i, 0)),
                      pl.BlockSpec((None, tr, w), lambda k, i, c_ref: (k, i, 0))],
            out_specs=pl.BlockSpec((None, tr, w), lambda k, i, c_ref: (k, i, 0))),
        compiler_params=_cparams(("parallel", "parallel")),
    )(lax.axis_index("c").reshape(1).astype(jnp.int32), part, got)


def _final_sum(part, got, recv, tr):
    _, _, r, w = part.shape

    def body(idx_ref, p_ref, g_ref, r0_ref, r1_ref, r2_ref, o_ref):
        acc = p_ref[...] + g_ref[...]
        for ref in (r0_ref, r1_ref, r2_ref):
            acc = acc + ref[...].astype(F32)
        o_ref[...] = acc

    x, y, c = _position()
    idx = jnp.stack([c, 2 * x + y, 2 * (1 - x) + y, 2 * x + (1 - y),
                     2 * (1 - x) + (1 - y)]).astype(jnp.int32)
    other = lambda j: pl.BlockSpec((None, tr, w), lambda i, s: (s[2 + j], i, 0))
    return pl.pallas_call(
        body, name="grad_final_sum",
        out_shape=jax.ShapeDtypeStruct((r, w), F32),
        grid_spec=pltpu.PrefetchScalarGridSpec(
            num_scalar_prefetch=1, grid=(r // tr,),
            in_specs=[pl.BlockSpec((None, None, tr, w), lambda i, s: (s[1], s[0], i, 0)),
                      pl.BlockSpec((None, tr, w), lambda i, s: (s[1], i, 0)),
                      other(0), other(1), other(2)],
            out_specs=pl.BlockSpec((tr, w), lambda i, s: (i, 0))),
        compiler_params=_cparams(("parallel",)),
    )(idx, part, got, recv, recv, recv)


def _adamw_math(w, g, m, v):
    m = ADAM_B1 * m + (1.0 - ADAM_B1) * g
    v = ADAM_B2 * v + (1.0 - ADAM_B2) * (g * g)
    m_hat = m / (1.0 - ADAM_B1 ** ADAM_STEP)
    v_hat = v / (1.0 - ADAM_B2 ** ADAM_STEP)
    delta = -ADAM_LR * (m_hat / (jnp.sqrt(v_hat) + ADAM_EPS) + ADAM_WD * w)
    return delta, m, v


def _adamw(w, g, m, v, tr):
    depth, r, width = w.shape

    def body(w_ref, g_ref, m_ref, v_ref, d_ref, nm_ref, nv_ref):
        d_ref[...], nm_ref[...], nv_ref[...] = _adamw_math(
            w_ref[...], g_ref[...], m_ref[...], v_ref[...])

    spec = pl.BlockSpec((None, tr, width), lambda l, i: (l, i, 0))
    shape = jax.ShapeDtypeStruct(w.shape, F32)
    return pl.pallas_call(
        body, name="adamw", out_shape=(shape, shape, shape), grid=(depth, r // tr),
        in_specs=[spec] * 4, out_specs=(spec, spec, spec),
        compiler_params=_cparams(("parallel", "parallel")),
    )(w, g, m, v)


def _small_allreduce_adamw(partial, w, m, v):
    rows = partial.shape[0]
    prows = w.shape[0]

    def body(p_ref, w_ref, m_ref, v_ref, tot_ref, d_ref, nm_ref, nv_ref, gath_ref,
             send_sems, recv_sems):
        x, y, c = _position()
        me = 4 * x + 2 * y + c
        copies = []
        for k in range(1, N_DEV):
            px = 1 - x if k & 4 else x
            py = 1 - y if k & 2 else y
            pc = 1 - c if k & 1 else c
            copies.append((pltpu.make_async_remote_copy(
                src_ref=p_ref, dst_ref=gath_ref.at[me],
                send_sem=send_sems.at[k - 1], recv_sem=recv_sems.at[k - 1],
                device_id=(px, py, pc), device_id_type=MESH), 4 * px + 2 * py + pc))
        for cp, _ in copies:
            cp.start()
        gath_ref[me] = p_ref[...]
        for k, (cp, peer) in enumerate(copies):
            pltpu.make_async_remote_copy(
                src_ref=p_ref, dst_ref=gath_ref.at[peer],
                send_sem=send_sems.at[k], recv_sem=recv_sems.at[k],
                device_id=(x, y, c), device_id_type=MESH).wait_recv()
        for cp, _ in copies:
            cp.wait_send()
        tot = gath_ref[0]
        for d in range(1, N_DEV):
            tot = tot + gath_ref[d]
        tot_ref[...] = tot
        d_ref[...], nm_ref[...], nv_ref[...] = _adamw_math(
            w_ref[...], tot[0:prows, :], m_ref[...], v_ref[...])

    vm = pl.BlockSpec(memory_space=pltpu.VMEM)
    pshape = jax.ShapeDtypeStruct(w.shape, F32)
    return pl.pallas_call(
        body, name="small_allreduce_adamw",
        out_shape=(jax.ShapeDtypeStruct(partial.shape, F32), pshape, pshape, pshape),
        in_specs=[vm] * 4, out_specs=(vm, vm, vm, vm),
        scratch_shapes=[pltpu.VMEM((N_DEV, rows, LANES), F32),
                        pltpu.SemaphoreType.DMA((N_DEV - 1,)), pltpu.SemaphoreType.DMA((N_DEV - 1,))],
        compiler_params=pltpu.CompilerParams(vmem_limit_bytes=VMEM_LIMIT),
    )(partial, w, m, v)


def _pack_rows(vec):
    depth, n = vec.shape
    rows = -(-n // LANES)
    rows = -(-rows // 8) * 8
    return jnp.pad(vec, ((0, 0), (0, rows * LANES - n))).reshape(depth, rows, LANES)


def _pack_small(named):
    blocks = [_pack_rows(a) for a in named]
    extents = [(b.shape[1], a.shape[1]) for b, a in zip(blocks, named)]
    depth = named[0].shape[0]
    packed = jnp.concatenate(blocks, axis=1).reshape(depth * sum(r for r, _ in extents), LANES)
    return packed, extents


def _unpack_small(packed, extents, depth):
    per_layer = sum(r for r, _ in extents)
    packed = packed.reshape(depth, per_layer, LANES)
    out, r0 = [], 0
    for rows, n in extents:
        out.append(packed[:, r0:r0 + rows, :].reshape(depth, rows * LANES)[:, :n])
        r0 += rows
    return out


def kernel(x, w_in, b_in, conv_w, conv_b, conv_ln_g, conv_ln_b, sinks, w_out, b_out, ln_g, ln_b, loss_target, m_w_in, m_b_in, m_conv_w, m_conv_b, m_conv_ln_g, m_conv_ln_b, m_sinks, m_w_out, m_b_out, m_ln_g, m_ln_b, v_w_in, v_b_in, v_conv_w, v_conv_b, v_conv_ln_g, v_conv_ln_b, v_sinks, v_w_out, v_b_out, v_ln_g, v_ln_b):
    depth, d, din_shard = w_in.shape
    s = x.shape[1]
    c_shard = conv_w.shape[2]
    dm = _Dims(s, d, N_DEV * c_shard, depth)
    assert dm.din == N_DEV * din_shard and x.shape[0] == 1 and sinks.shape[1] == dm.nh
    d_shard = w_out.shape[1]
    c, din = dm.c, dm.din

    t_mix = _tile(s, 256, WINDOW)
    tm_row = _tile(s, 256, 8)
    tm_big = _tile(s, 1024, 8)

    w_in_t, m_w_in_t, v_w_in_t = (a.transpose(0, 2, 1) for a in (w_in, m_w_in, v_w_in))
    w_in_b, w_out_b = w_in_t.astype(BF16), w_out.astype(BF16)
    conv_w_pad = jnp.pad(conv_w, ((0, 0), (0, CONV_ROWS - CONV_WIDTH), (0, 0)))
    first = _run_copies([_gather_own([w_in_b[0], w_out_b[0], conv_w_pad])], "weights_gather_own")
    g_in, g_out, g_conv = _run_copies([_gather_forward(first)], "weights_gather_forward")
    conv_w_full = g_conv.transpose(1, 2, 0, 3).reshape(depth, CONV_ROWS, c)
    by_cols = lambda g: g.reshape(din, d)
    by_rows = lambda g: g.reshape(d, d)

    xs = x[0]
    xb = xs.astype(BF16)
    saved = []
    loss_part = dout = None
    for l in range(depth):
        w_in_l = by_cols(g_in)
        riders = []
        if l + 1 < depth:
            riders.append(_gather_own([w_in_b[l + 1]]))
        if l > 0:
            riders.append(_gather_forward([g_out]))
        mm = dict(tb=True, tm=tm_big, tn=_tile(din, 768), tk=d, out_dtype=F32, name="in_proj",
                  bias=b_in[l][None, :])
        if riders:
            proj, landed = _matmul(xb, w_in_l, copies=riders, **mm)
            g_out = landed[-1] if l > 0 else g_out
            g_in_next = landed[0] if l + 1 < depth else None
        else:
            proj = _matmul(xb, w_in_l, **mm)
        ymix, conv = _mixer_fwd(dm, proj, conv_w_full[l], conv_b[l][None, :],
                                conv_ln_g[l][None, :], conv_ln_b[l][None, :], sinks[l], t_mix)
        w_out_l = by_rows(g_out)
        target = loss_target[0] if l == depth - 1 else None
        riders = []
        if l + 1 < depth:
            riders = [_gather_forward([g_in_next]), _gather_own([w_out_b[l + 1]])]
        res, landed = _outproj_ln(dm, ymix, w_out_l, b_out[l][None, :], xs, ln_g[l][None, :],
                                  ln_b[l][None, :], target, tm_row, copies=riders)
        saved.append((xb, proj, ymix, conv, res[0], w_in_l, w_out_l))
        if l + 1 < depth:
            g_in, g_out = landed
            xs, xb = res[1], res[2]
        else:
            dout, loss_part = res[1], res[2]

    g_w_in, g_w_out = [None] * depth, [None] * depth
    small_parts = [None] * depth
    dconv_w = [None] * depth
    tr_in, tr_out = _tile(din_shard, 512, 8), _tile(d_shard, 256, 8)
    parts_in = None
    for l in reversed(range(depth)):
        xb, proj, ymix, conv, z, w_in_l, w_out_l = saved[l]
        dz, dzb, ln_small = _ln_bwd(dm, dout, z, ln_g[l][None, :], tm_row)
        mm = dict(tb=True, tm=tm_big, tn=_tile(d, 1024), tk=d, out_dtype=F32, name="dymix")
        if parts_in is not None:
            dymix, (got_in,) = _matmul(dzb, w_out_l, copies=[_scatter_sibling([parts_in])], **mm)
            sums_in = _pair_sum(parts_in, got_in, tr_in)
        else:
            dymix = _matmul(dzb, w_out_l, **mm)
        dw_out = _matmul(ymix, dzb, ta=True, tm=_tile(d, 1024), tn=_tile(d, 1024), tk=s,
                         out_dtype=F32, name="dw_out")
        dproj, dcw, conv_small, dbin, dsk = _mixer_bwd(
            dm, proj, dymix, conv, conv_w_full[l], conv_b[l][None, :], conv_ln_g[l][None, :],
            conv_ln_b[l][None, :], sinks[l], t_mix)
        parts_out = dw_out.reshape(N_CHIPS, 2, d_shard, d)
        riders = [_scatter_sibling([parts_out])]
        if parts_in is not None:
            riders.append(_scatter_chips([sums_in]))
        dout, landed = _matmul(dproj, w_in_l, tm=_tile(s, 512), tn=_tile(d, 1024), tk=din,
                               out_dtype=F32, name="dx", resid=dz, resid_scale=dm.alpha,
                               copies=riders)
        got_out = landed[0]
        if parts_in is not None:
            g_w_in[l + 1] = _final_sum(parts_in, got_in, landed[1], tr_in)
        sums_out = _pair_sum(parts_out, got_out, tr_out)
        dw_in_t, (recv_out,) = _matmul(dproj, xb, ta=True, tm=_tile(din, 768), tn=_tile(d, 1024),
                                       tk=s, out_dtype=F32, name="dw_in",
                                       copies=[_scatter_chips([sums_out])])
        g_w_out[l] = _final_sum(parts_out, got_out, recv_out, tr_out)
        parts_in = dw_in_t.reshape(N_CHIPS, 2, din_shard, d)
        small_parts[l] = [dbin[0], conv_small[0], conv_small[1], conv_small[2], dsk[0, :dm.nh],
                          ln_small[2], ln_small[0], ln_small[1]]
        dconv_w[l] = dcw
    grad_x = dout[None]
    got_in, = _run_copies([_scatter_sibling([parts_in])], "grad_sibling_exchange")
    sums_in = _pair_sum(parts_in, got_in, tr_in)
    recv_in, = _run_copies([_scatter_chips([sums_in])], "grad_chip_exchange")
    g_w_in[0] = _final_sum(parts_in, got_in, recv_in, tr_in)

    small_w = [b_in, conv_b, conv_ln_g, conv_ln_b, sinks, b_out, ln_g, ln_b]
    small_m = [m_b_in, m_conv_b, m_conv_ln_g, m_conv_ln_b, m_sinks, m_b_out, m_ln_g, m_ln_b]
    small_v = [v_b_in, v_conv_b, v_conv_ln_g, v_conv_ln_b, v_sinks, v_b_out, v_ln_g, v_ln_b]
    n_small = len(small_w)
    packed_g, extents = _pack_small([jnp.stack([small_parts[l][k] for l in range(depth)])
                                     for k in range(n_small)])
    packed_w, _ = _pack_small(small_w)
    packed_m, _ = _pack_small(small_m)
    packed_v, _ = _pack_small(small_v)
    prows = packed_g.shape[0]
    conv_rows = depth * CONV_ROWS * c // LANES
    partial = jnp.concatenate(
        [packed_g, jnp.stack(dconv_w).reshape(conv_rows, LANES), loss_part], axis=0)
    total, sm_delta, sm_m, sm_v = _small_allreduce_adamw(partial, packed_w, packed_m, packed_v)
    loss = total[prows + conv_rows, 0]
    dconv_w_full = total[prows:prows + conv_rows].reshape(depth, CONV_ROWS, c)
    me = 4 * lax.axis_index("x") + 2 * lax.axis_index("y") + lax.axis_index("c")
    grad_conv_w = lax.dynamic_slice_in_dim(dconv_w_full, me * c_shard, c_shard, axis=2)[:, :CONV_WIDTH]

    grads_small = _unpack_small(total[:prows], extents, depth)
    delta_small = _unpack_small(sm_delta, extents, depth)
    newm_small = _unpack_small(sm_m, extents, depth)
    newv_small = _unpack_small(sm_v, extents, depth)

    grad_w_in_t = jnp.stack(g_w_in)
    grad_w_out = jnp.stack(g_w_out)
    grad_w_in, d_w_in, nm_w_in, nv_w_in = (a.transpose(0, 2, 1) for a in (
        grad_w_in_t, *_adamw(w_in_t, grad_w_in_t, m_w_in_t, v_w_in_t, tr_in)))
    d_w_out, nm_w_out, nv_w_out = _adamw(w_out, grad_w_out, m_w_out, v_w_out, _tile(d_shard, 256, 8))
    d_cw, nm_cw, nv_cw = _adamw(conv_w, grad_conv_w, m_conv_w, v_conv_w, CONV_WIDTH)

    def assemble(w_in_leaf, conv_w_leaf, w_out_leaf, small):
        b_in_, conv_b_, cg_, cb_, sinks_, b_out_, ln_g_, ln_b_ = small
        return [w_in_leaf, b_in_, conv_w_leaf, conv_b_, cg_, cb_, sinks_, w_out_leaf, b_out_,
                ln_g_, ln_b_]

    return (loss, grad_x,
            *assemble(grad_w_in, grad_conv_w, grad_w_out, grads_small),
            *assemble(d_w_in, d_cw, d_w_out, delta_small),
            *assemble(nm_w_in, nm_cw, nm_w_out, newm_small),
            *assemble(nv_w_in, nv_cw, nv_w_out, newv_small))
```

```python
import functools

import jax
import jax.numpy as jnp
from jax import lax
from jax.experimental import pallas as pl
from jax.experimental.pallas import tpu as pltpu

F32 = jnp.float32
BF16 = jnp.bfloat16
MESH = pl.DeviceIdType.MESH

N_DEV = 8
N_CHIPS = 4
HEAD_DIM = 64
N_KV = 2
KV_W = N_KV * HEAD_DIM
CONV_WIDTH = 31
CONV_ROWS = 32
HALO = 32
WINDOW = 128
LN_EPS = 1e-5
NEG_INF = -1e30
LANES = 128

ADAM_LR = 0.001
ADAM_B1 = 0.9
ADAM_B2 = 0.999
ADAM_EPS = 1e-08
ADAM_WD = 0.01
ADAM_STEP = 10

VMEM_LIMIT = 56 * 1024 * 1024


def _tile(n, target, align=LANES):
    best = None
    for t in range(align, min(n, target) + 1, align):
        if n % t == 0:
            best = t
    return n if best is None else best


def _sigmoid(x):
    return jax.nn.sigmoid(x)


def _dsilu(x, s):
    return s * (1.0 + x * (1.0 - s))


def _cparams(sem, vmem=VMEM_LIMIT):
    return pltpu.CompilerParams(dimension_semantics=sem, vmem_limit_bytes=vmem)


_ANY = pl.BlockSpec(memory_space=pl.ANY)


def _position():
    return lax.axis_index("x"), lax.axis_index("y"), lax.axis_index("c")


def _other_chips(x, y):
    return [(1 - x, y), (x, 1 - y), (1 - x, 1 - y)]


class _Copies:
    def __init__(self, operands, landing, alias, sems, start, finish):
        self.operands, self.landing, self.alias, self.sems = operands, landing, alias, sems
        self.start, self.finish = start, finish


class _Hosting:
    def __init__(self, groups, n_in, n_out):
        self.groups = groups
        self.args = [a for g in groups for a in g.operands]
        self.out_shapes = [s for g in groups for s in g.landing]
        self.scratch = [pltpu.SemaphoreType.DMA(g.sems) for g in groups for _ in range(2)]
        self.aliases = {}
        i0, o0 = n_in, n_out
        for g in groups:
            for a, b in g.alias.items():
                self.aliases[i0 + a] = o0 + b
            i0 += len(g.operands)
            o0 += len(g.landing)

    def _each(self, in_refs, out_refs, sem_refs):
        i0 = o0 = 0
        for n, g in enumerate(self.groups):
            yield (g, in_refs[i0:i0 + len(g.operands)], out_refs[o0:o0 + len(g.landing)],
                   sem_refs[2 * n], sem_refs[2 * n + 1])
            i0 += len(g.operands)
            o0 += len(g.landing)

    def start(self, in_refs, out_refs, sem_refs):
        for g, ins, outs, send, recv in self._each(in_refs, out_refs, sem_refs):
            g.start(ins, outs, send, recv)

    def finish(self, in_refs, out_refs, sem_refs):
        for g, ins, outs, send, recv in self._each(in_refs, out_refs, sem_refs):
            g.finish(ins, outs, send, recv)


def _run_copies(groups, name):
    host = _Hosting(groups, 0, 0)
    n_in, n_out = len(host.args), len(host.out_shapes)

    def body(*refs):
        ins, outs, sems = refs[:n_in], refs[n_in:n_in + n_out], refs[n_in + n_out:]
        host.start(ins, outs, sems)
        host.finish(ins, outs, sems)

    return pl.pallas_call(
        body, name=name, out_shape=host.out_shapes, in_specs=[_ANY] * n_in,
        out_specs=[_ANY] * n_out, scratch_shapes=host.scratch,
        input_output_aliases=host.aliases,
    )(*host.args)


def _gather_own(shards):
    n = len(shards)

    def copies(ins, outs, send, recv):
        x, y, c = _position()
        peers = [(x, y, 1 - c)] + [(px, py, c) for px, py in _other_chips(x, y)]
        out = []
        for t in range(n):
            for k, peer in enumerate(peers):
                out.append((pltpu.make_async_remote_copy(
                    src_ref=ins[t], dst_ref=outs[t].at[4 * x + 2 * y + c],
                    send_sem=send.at[t, k], recv_sem=recv.at[t, k],
                    device_id=peer, device_id_type=MESH), t, k, peer))
        local = [pltpu.make_async_copy(ins[t], outs[t].at[4 * x + 2 * y + c], send.at[t, 4])
                 for t in range(n)]
        return out, local

    def start(ins, outs, send, recv):
        remote, local = copies(ins, outs, send, recv)
        for cp in local:
            cp.start()
        for cp, _, _, _ in remote:
            cp.start()

    def finish(ins, outs, send, recv):
        remote, local = copies(ins, outs, send, recv)
        x, y, c = _position()
        for _, t, k, (px, py, pc) in remote:
            pltpu.make_async_remote_copy(
                src_ref=ins[t], dst_ref=outs[t].at[4 * px + 2 * py + pc],
                send_sem=send.at[t, k], recv_sem=recv.at[t, k],
                device_id=(x, y, c), device_id_type=MESH).wait_recv()
        for cp, _, _, _ in remote:
            cp.wait_send()
        for cp in local:
            cp.wait()

    landing = [jax.ShapeDtypeStruct((N_DEV,) + s.shape, s.dtype) for s in shards]
    return _Copies(list(shards), landing, {}, (n, 5), start, finish)


def _gather_forward(buffers):
    n = len(buffers)

    def copies(ins, outs, send, recv):
        x, y, c = _position()
        out = []
        for t in range(n):
            for j, (px, py) in enumerate(_other_chips(x, y)):
                slot = 4 * px + 2 * py + c
                out.append((pltpu.make_async_remote_copy(
                    src_ref=ins[t].at[slot], dst_ref=outs[t].at[slot],
                    send_sem=send.at[t, j], recv_sem=recv.at[t, j],
                    device_id=(x, y, 1 - c), device_id_type=MESH), t, j, 4 * px + 2 * py + 1 - c))
        return out

    def start(ins, outs, send, recv):
        for cp, _, _, _ in copies(ins, outs, send, recv):
            cp.start()

    def finish(ins, outs, send, recv):
        x, y, c = _position()
        mine = copies(ins, outs, send, recv)
        for _, t, j, got in mine:
            pltpu.make_async_remote_copy(
                src_ref=ins[t].at[got], dst_ref=outs[t].at[got],
                send_sem=send.at[t, j], recv_sem=recv.at[t, j],
                device_id=(x, y, c), device_id_type=MESH).wait_recv()
        for cp, _, _, _ in mine:
            cp.wait_send()

    landing = [jax.ShapeDtypeStruct(b.shape, b.dtype) for b in buffers]
    return _Copies(list(buffers), landing, {t: t for t in range(n)}, (n, 3), start, finish)


def _scatter_sibling(parts):
    n = len(parts)

    def copies(ins, outs, send, recv):
        x, y, c = _position()
        return [pltpu.make_async_remote_copy(
            src_ref=ins[t].at[:, 1 - c], dst_ref=outs[t],
            send_sem=send.at[t, 0], recv_sem=recv.at[t, 0],
            device_id=(x, y, 1 - c), device_id_type=MESH) for t in range(n)]

    def start(ins, outs, send, recv):
        for cp in copies(ins, outs, send, recv):
            cp.start()

    def finish(ins, outs, send, recv):
        for cp in copies(ins, outs, send, recv):
            cp.wait()

    landing = [jax.ShapeDtypeStruct((p.shape[0],) + p.shape[2:], p.dtype) for p in parts]
    return _Copies(list(parts), landing, {}, (n, 1), start, finish)


def _scatter_chips(blocks):
    n = len(blocks)

    def start(ins, outs, send, recv):
        x, y, c = _position()
        for t in range(n):
            for j, (px, py) in enumerate(_other_chips(x, y)):
                pltpu.make_async_remote_copy(
                    src_ref=ins[t].at[2 * px + py], dst_ref=outs[t].at[2 * x + y],
                    send_sem=send.at[t, j], recv_sem=recv.at[t, j],
                    device_id=(px, py, c), device_id_type=MESH).start()

    def finish(ins, outs, send, recv):
        x, y, c = _position()
        for t in range(n):
            for j, (px, py) in enumerate(_other_chips(x, y)):
                cp = pltpu.make_async_remote_copy(
                    src_ref=ins[t].at[2 * px + py], dst_ref=outs[t].at[2 * px + py],
                    send_sem=send.at[t, j], recv_sem=recv.at[t, j],
                    device_id=(px, py, c), device_id_type=MESH)
                cp.wait_recv()
                cp.wait_send()

    landing = [jax.ShapeDtypeStruct(b.shape, b.dtype) for b in blocks]
    return _Copies(list(blocks), landing, {}, (n, 3), start, finish)


def _matmul(a, b, *, ta=False, tb=False, tm, tn, tk, out_dtype, name, bias=None, resid=None,
            resid_scale=1.0, copies=()):
    m, k = (a.shape[1], a.shape[0]) if ta else a.shape
    n = b.shape[0] if tb else b.shape[1]
    assert (b.shape[1] if tb else b.shape[0]) == k
    assert m % tm == 0 and n % tn == 0 and k % tk == 0
    ni, nj, nk = m // tm, n // tn, k // tk
    dn = (((0 if ta else 1,), (1 if tb else 0,)), ((), ()))
    n_in = 2 + (bias is not None) + (resid is not None)
    host = _Hosting(list(copies), n_in, 1)
    n_hin, n_hout = len(host.args), len(host.out_shapes)

    def body(*refs):
        a_ref, b_ref = refs[0], refs[1]
        pos = 2
        bias_ref = resid_ref = None
        if bias is not None:
            bias_ref = refs[pos]
            pos += 1
        if resid is not None:
            resid_ref = refs[pos]
            pos += 1
        h_in = refs[pos:pos + n_hin]
        pos += n_hin
        o_ref = refs[pos]
        h_out = refs[pos + 1:pos + 1 + n_hout]
        pos += 1 + n_hout
        acc_ref = refs[pos] if nk > 1 else None
        h_sems = refs[pos + (nk > 1):]
        step = (pl.program_id(0) * nj + pl.program_id(1)) * nk + pl.program_id(2)

        if copies:
            @pl.when(step == 0)
            def _():
                host.start(h_in, h_out, h_sems)

        def finish(acc):
            if bias_ref is not None:
                acc = acc + bias_ref[...]
            if resid_ref is not None:
                acc = acc + resid_scale * resid_ref[...]
            o_ref[...] = acc.astype(out_dtype)

        p = lax.dot_general(a_ref[...], b_ref[...], dn, preferred_element_type=F32)
        if nk == 1:
            finish(p)
        else:
            kk = pl.program_id(2)

            @pl.when(kk == 0)
            def _():
                acc_ref[...] = p

            @pl.when(kk > 0)
            def _():
                acc_ref[...] += p

            @pl.when(kk == nk - 1)
            def _():
                finish(acc_ref[...])

        if copies:
            @pl.when(step == ni * nj * nk - 1)
            def _():
                host.finish(h_in, h_out, h_sems)

    a_spec = (pl.BlockSpec((tk, tm), lambda i, j, kk: (kk, i)) if ta
              else pl.BlockSpec((tm, tk), lambda i, j, kk: (i, kk)))
    b_spec = (pl.BlockSpec((tn, tk), lambda i, j, kk: (j, kk)) if tb
              else pl.BlockSpec((tk, tn), lambda i, j, kk: (kk, j)))
    in_specs = [a_spec, b_spec]
    args = [a, b]
    if bias is not None:
        in_specs.append(pl.BlockSpec((1, tn), lambda i, j, kk: (0, j)))
        args.append(bias)
    if resid is not None:
        in_specs.append(pl.BlockSpec((tm, tn), lambda i, j, kk: (i, j)))
        args.append(resid)
    res = pl.pallas_call(
        body, name=name,
        out_shape=[jax.ShapeDtypeStruct((m, n), out_dtype)] + host.out_shapes,
        grid=(ni, nj, nk),
        in_specs=in_specs + [_ANY] * n_hin,
        out_specs=[pl.BlockSpec((tm, tn), lambda i, j, kk: (i, j))] + [_ANY] * n_hout,
        scratch_shapes=([pltpu.VMEM((tm, tn), F32)] if nk > 1 else []) + host.scratch,
        input_output_aliases=host.aliases,
        compiler_params=_cparams(("arbitrary",) * 3 if copies else
                                 ("parallel", "parallel", "arbitrary")),
    )(*args, *host.args)
    return (res[0], res[1:]) if copies else res[0]


class _Dims:
    def __init__(self, s, d, c, depth):
        self.s, self.d, self.c, self.depth = s, d, c, depth
        self.a = d - c
        self.nh = self.a // HEAD_DIM
        self.group = self.nh // N_KV
        self.din = 3 * c + 2 * self.a + 2 * KV_W
        self.o_q = 3 * c
        self.o_k = 3 * c + self.a
        self.o_v = self.o_k + KV_W
        self.o_ag = self.o_k + 2 * KV_W
        self.alpha = (2 * depth) ** 0.25
        assert self.nh % 2 == 0 and self.group % 2 == 0 and self.o_k % (2 * KV_W) == 0
        assert c % LANES == 0 and self.a % LANES == 0


SUBLANES = 8
TAP_ROWS = 64


def _shift_copies(src_ref, sh_ref, lanes, rows):
    for r in range(1, SUBLANES):
        sh_ref[r - 1, 0:rows, :] = src_ref[pl.ds(r, rows), lanes]


def _tap_rows(src_ref, sh_ref, lanes, off, start, rows):
    r = off % SUBLANES
    at = pl.multiple_of(start + (off - r), SUBLANES)
    if r == 0:
        return src_ref[pl.ds(at, rows), lanes]
    return sh_ref[r - 1, pl.ds(at, rows), :]


def _conv_chunk(w_ref, src_ref, sh_ref, out_ref, lanes, t, first_off, reverse, bias_ref=None):
    def block(it, carry):
        start = pl.multiple_of(it * TAP_ROWS, TAP_ROWS)
        acc = None
        for j in range(CONV_WIDTH):
            off = first_off - j if reverse else first_off + j
            term = w_ref[j:j + 1, lanes] * _tap_rows(src_ref, sh_ref, lanes, off, start, TAP_ROWS)
            acc = term if acc is None else acc + term
        if bias_ref is not None:
            acc = acc + bias_ref[:, lanes]
        out_ref[pl.ds(start, TAP_ROWS), lanes] = acc
        return carry

    lax.fori_loop(0, t // TAP_ROWS, block, 0)


def _conv_w_grad_chunk(src_ref, sh_ref, dconv_ref, acc_ref, lanes, t, first_off):
    def block(it, accs):
        start = pl.multiple_of(it * SUBLANES, SUBLANES)
        dv = dconv_ref[pl.ds(start, SUBLANES), lanes]
        return tuple(
            acc + _tap_rows(src_ref, sh_ref, lanes, first_off + j, start, SUBLANES) * dv
            for j, acc in enumerate(accs))

    zero = jnp.zeros((SUBLANES, LANES), F32)
    accs = lax.fori_loop(0, t // SUBLANES, block, (zero,) * CONV_WIDTH, unroll=2)
    for j in range(CONV_WIDTH):
        acc_ref[j, :, lanes] += accs[j]


def _kv_operands(kv_ref, lo):
    kext = kv_ref[:, 0:KV_W]
    vext = kv_ref[:, KV_W:2 * KV_W]
    ksw = pltpu.roll(kext, HEAD_DIM, 1)
    vsw = pltpu.roll(vext, HEAD_DIM, 1)
    zero = jnp.zeros_like(kext)
    k2 = [jnp.where(lo, kext, ksw).astype(BF16), jnp.where(lo, ksw, kext).astype(BF16)]
    khalf = [[jnp.where(lo, kext, zero).astype(BF16), jnp.where(lo, zero, ksw).astype(BF16)],
             [jnp.where(lo, ksw, zero).astype(BF16), jnp.where(lo, zero, kext).astype(BF16)]]
    vhalf = [[jnp.where(lo, vext, zero).astype(BF16), jnp.where(lo, zero, vsw).astype(BF16)],
             [jnp.where(lo, vsw, zero).astype(BF16), jnp.where(lo, zero, vext).astype(BF16)]]
    return k2, khalf, vhalf


SCALE = HEAD_DIM ** -0.5


def _from_previous(rows):
    row = lax.broadcasted_iota(jnp.int32, (rows, WINDOW), 0)
    col = lax.broadcasted_iota(jnp.int32, (rows, WINDOW), 1)
    return col > (row & (WINDOW - 1))


def _sink_column(sinks_ref, heads):
    block = lax.broadcasted_iota(jnp.int32, (len(heads) * WINDOW, 1), 0) // WINDOW
    out = jnp.zeros(block.shape, F32)
    for b, head in enumerate(heads):
        out = jnp.where(block == b, sinks_ref[head], out)
    return out


def _stacked_heads(dm, kvh):
    pairs = range(kvh * (dm.group // 2), (kvh + 1) * (dm.group // 2))
    return [(p, half) for half in range(2) for p in pairs]


def _band_merge(both, prev):
    return jnp.where(prev, both[:, 0:WINDOW], both[:, WINDOW:2 * WINDOW])


def _band_split(tile, prev):
    zero = jnp.zeros_like(tile)
    return jnp.concatenate([jnp.where(prev, tile, zero), jnp.where(prev, zero, tile)],
                           axis=1).astype(BF16)


def _softmax_with_sink(qm, k2rows, prev, no_previous, sink):
    both = lax.dot_general(qm, k2rows, (((1,), (1,)), ((), ())), preferred_element_type=F32)
    s_prev = both[:, 0:WINDOW]
    if no_previous is not None:
        s_prev = jnp.where(no_previous, NEG_INF, s_prev)
    s = jnp.where(prev, s_prev, both[:, WINDOW:2 * WINDOW])
    m = jnp.maximum(jnp.max(s, axis=1, keepdims=True), sink)
    e = jnp.exp(s - m)
    den = jnp.sum(e, axis=1, keepdims=True) + jnp.exp(sink - m)
    inv = 1.0 / den
    return e * inv, m, inv


def _mixer_specs(dm, t, idx):
    return [
        pl.BlockSpec((t, dm.din), lambda g: (idx(g), 0)),
        pl.BlockSpec((HALO, 2 * dm.c), lambda g: (jnp.maximum(idx(g) * (t // HALO) - 1, 0), 0)),
        pl.BlockSpec((WINDOW, 2 * KV_W),
                     lambda g: (jnp.maximum(idx(g) * (t // WINDOW) - 1, 0), dm.o_k // (2 * KV_W))),
    ]


def _mixer_fwd(dm, proj, conv_w, conv_b, cln_g, cln_b, sinks, t):
    c, nq = dm.c, t // WINDOW
    rs = dm.group * WINDOW

    def body(sinks_ref, pr_ref, ch_ref, kvh_ref, cw_ref, cb_ref, cg_ref, cbb_ref,
             y_ref, conv_ref, hext_ref, kv_ref, hs_ref, qs_ref):
        i = pl.program_id(0)
        first = i == 0
        h = pr_ref[:, 0:c] * _sigmoid(pr_ref[:, c:2 * c])
        hh = ch_ref[:, 0:c] * _sigmoid(ch_ref[:, c:2 * c])
        hext_ref[0:HALO, :] = jnp.where(first, 0.0, hh)
        hext_ref[HALO:HALO + t, :] = h
        for k in range(c // LANES):
            lanes = slice(LANES * k, LANES * (k + 1))
            _shift_copies(hext_ref, hs_ref, lanes, t + HALO - SUBLANES)
            _conv_chunk(cw_ref, hext_ref, hs_ref, conv_ref, lanes, t, HALO - (CONV_WIDTH - 1),
                        False, cb_ref)
        conv = conv_ref[...]
        mu = jnp.mean(conv, axis=1, keepdims=True)
        dlt = conv - mu
        var = jnp.mean(dlt * dlt, axis=1, keepdims=True)
        u = dlt * lax.rsqrt(var + LN_EPS) * cg_ref[...] + cbb_ref[...]
        gate = pr_ref[:, 2 * c:3 * c]
        y_ref[:, 0:c] = (u * _sigmoid(u) * (gate * _sigmoid(gate))).astype(BF16)

        kv_ref[0:WINDOW, :] = jnp.where(first, 0.0, kvh_ref[...])
        kv_ref[WINDOW:WINDOW + t, :] = pr_ref[:, dm.o_k:dm.o_k + 2 * KV_W]
        lo = lax.broadcasted_iota(jnp.int32, (1, LANES), 1) < HEAD_DIM
        k2, _, vhalf = _kv_operands(kv_ref, lo)
        stacks = [_stacked_heads(dm, kvh) for kvh in range(N_KV)]
        sink_cols = [_sink_column(sinks_ref, [2 * p + half for p, half in st]) for st in stacks]
        prev = _from_previous(rs)
        for qb in range(nq):
            r0 = qb * WINDOW
            rows = slice(r0, r0 + 2 * WINDOW)
            no_previous = first if qb == 0 else None
            for kvh in range(N_KV):
                for b, (p, half) in enumerate(stacks[kvh]):
                    qp = pr_ref[r0:r0 + WINDOW, dm.o_q + LANES * p:dm.o_q + LANES * (p + 1)] * SCALE
                    qs_ref[b * WINDOW:(b + 1) * WINDOW, :] = jnp.where(
                        lo if half == 0 else jnp.logical_not(lo), qp, 0.0).astype(BF16)
                prob, _, _ = _softmax_with_sink(qs_ref[...], k2[kvh][rows], prev, no_previous,
                                                sink_cols[kvh])
                pb = _band_split(prob, prev)
                o_lo = jnp.dot(pb[0:rs // 2], vhalf[kvh][0][rows], preferred_element_type=F32)
                o_hi = jnp.dot(pb[rs // 2:rs], vhalf[kvh][1][rows], preferred_element_type=F32)
                for b, (p, _) in enumerate(stacks[kvh][:len(stacks[kvh]) // 2]):
                    blk = slice(b * WINDOW, (b + 1) * WINDOW)
                    ag = pr_ref[r0:r0 + WINDOW, dm.o_ag + LANES * p:dm.o_ag + LANES * (p + 1)]
                    y_ref[r0:r0 + WINDOW, c + LANES * p:c + LANES * (p + 1)] = (
                        (o_lo[blk] + o_hi[blk]) * (ag * _sigmoid(ag))).astype(BF16)

    vec = pl.BlockSpec((1, c), lambda g: (0, 0))
    return pl.pallas_call(
        body, name="mixer_fwd",
        out_shape=(jax.ShapeDtypeStruct((dm.s, dm.d), BF16),
                   jax.ShapeDtypeStruct((dm.s, c), F32)),
        grid=(dm.s // t,),
        in_specs=[pl.BlockSpec(memory_space=pltpu.SMEM)] + _mixer_specs(dm, t, lambda g: g)
        + [pl.BlockSpec((CONV_ROWS, c), lambda g: (0, 0)), vec, vec, vec],
        out_specs=(pl.BlockSpec((t, dm.d), lambda g: (g, 0)), pl.BlockSpec((t, c), lambda g: (g, 0))),
        scratch_shapes=[pltpu.VMEM((t + HALO, c), F32), pltpu.VMEM((t + WINDOW, 2 * KV_W), F32),
                        pltpu.VMEM((SUBLANES - 1, t + HALO, LANES), F32),
                        pltpu.VMEM((rs, LANES), BF16)],
        compiler_params=_cparams(("arbitrary",)),
    )(sinks, proj, proj, proj, conv_w, conv_b, cln_g, cln_b)


def _mixer_bwd(dm, proj, dymix, conv, conv_w, conv_b, cln_g, cln_b, sinks, t):
    c, nq, nt = dm.c, t // WINDOW, dm.s // t
    rs = dm.group * WINDOW

    def body(sinks_ref, pr_ref, ch_ref, kvh_ref, dy_ref, cv_ref, cw_ref, cb_ref, cg_ref, cbb_ref,
             dpr_ref, dcw_ref, dsm_ref, dbin_ref, dsk_ref,
             hext_ref, kv_ref, dcx_ref, dkv_ref, carry_dc_ref, carry_kv_ref,
             hs_ref, ds_ref, dh_ref, dcw_acc_ref, qs_ref, do_ref):
        g = pl.program_id(0)
        i = nt - 1 - g
        first = i == 0

        @pl.when(g == 0)
        def _():
            dcw_acc_ref[...] = jnp.zeros_like(dcw_acc_ref)
            dsm_ref[...] = jnp.zeros_like(dsm_ref)
            dbin_ref[...] = jnp.zeros_like(dbin_ref)
            dsk_ref[...] = jnp.zeros_like(dsk_ref)
            carry_dc_ref[...] = jnp.zeros_like(carry_dc_ref)
            carry_kv_ref[...] = jnp.zeros_like(carry_kv_ref)

        def emit(col, width, val):
            dpr_ref[:, col:col + width] = val.astype(BF16)
            dbin_ref[0:1, col:col + width] += jnp.sum(val, axis=0, keepdims=True)

        val = pr_ref[:, 0:c]
        sg = _sigmoid(pr_ref[:, c:2 * c])
        h = val * sg
        hh = ch_ref[:, 0:c] * _sigmoid(ch_ref[:, c:2 * c])
        hext_ref[0:HALO, :] = jnp.where(first, 0.0, hh)
        hext_ref[HALO:HALO + t, :] = h
        conv = cv_ref[...]
        mu = jnp.mean(conv, axis=1, keepdims=True)
        dlt = conv - mu
        var = jnp.mean(dlt * dlt, axis=1, keepdims=True)
        rstd = lax.rsqrt(var + LN_EPS)
        xhat = dlt * rstd
        u = xhat * cg_ref[...] + cbb_ref[...]
        su = _sigmoid(u)
        gate = pr_ref[:, 2 * c:3 * c]
        sgate = _sigmoid(gate)
        dyc = dy_ref[:, 0:c]
        emit(2 * c, c, dyc * (u * su) * _dsilu(gate, sgate))
        du = dyc * (gate * sgate) * _dsilu(u, su)
        dsm_ref[1:2, :] += jnp.sum(du * xhat, axis=0, keepdims=True)
        dsm_ref[2:3, :] += jnp.sum(du, axis=0, keepdims=True)
        dxh = du * cg_ref[...]
        dconv = rstd * (dxh - jnp.mean(dxh, axis=1, keepdims=True)
                        - xhat * jnp.mean(dxh * xhat, axis=1, keepdims=True))
        dsm_ref[0:1, :] += jnp.sum(dconv, axis=0, keepdims=True)
        dcx_ref[0:t, :] = dconv
        dcx_ref[t:t + HALO, :] = carry_dc_ref[...]
        carry_dc_ref[...] = dconv[0:HALO, :]
        for k in range(c // LANES):
            lanes = slice(LANES * k, LANES * (k + 1))
            _shift_copies(hext_ref, hs_ref, lanes, t + HALO - SUBLANES)
            _shift_copies(dcx_ref, ds_ref, lanes, t + HALO - SUBLANES)
            _conv_chunk(cw_ref, dcx_ref, ds_ref, dh_ref, lanes, t, CONV_WIDTH - 1, True)
            _conv_w_grad_chunk(hext_ref, hs_ref, dcx_ref, dcw_acc_ref, lanes, t,
                               HALO - (CONV_WIDTH - 1))
        dh = dh_ref[...]
        emit(0, c, dh * sg)
        emit(c, c, dh * val * sg * (1.0 - sg))

        kv_ref[0:WINDOW, :] = jnp.where(first, 0.0, kvh_ref[...])
        kv_ref[WINDOW:WINDOW + t, :] = pr_ref[:, dm.o_k:dm.o_k + 2 * KV_W]
        dkv_ref[0:t, :] = jnp.zeros((t, 2 * KV_W), F32)
        dkv_ref[t:t + WINDOW, :] = carry_kv_ref[...]
        lane = lax.broadcasted_iota(jnp.int32, (1, LANES), 1)
        lo = lane < HEAD_DIM
        k2, khalf, vhalf = _kv_operands(kv_ref, lo)
        tn_dims = (((0,), (0,)), ((), ()))
        nt_dims = (((1,), (1,)), ((), ()))
        dsk = jnp.zeros((1, LANES), F32)
        stacks = [_stacked_heads(dm, kvh) for kvh in range(N_KV)]
        sink_cols = [_sink_column(sinks_ref, [2 * p + half for p, half in st]) for st in stacks]
        prev = _from_previous(rs)
        hs2 = rs // 2
        for qb in range(nq):
            r0 = qb * WINDOW
            rows = slice(r0, r0 + 2 * WINDOW)
            no_previous = first if qb == 0 else None
            dka = [None, None]
            dva = [None, None]
            for kvh in range(N_KV):
                pairs = [p for p, half in stacks[kvh] if half == 0]
                for b, (p, half) in enumerate(stacks[kvh]):
                    qp = pr_ref[r0:r0 + WINDOW, dm.o_q + LANES * p:dm.o_q + LANES * (p + 1)] * SCALE
                    qs_ref[b * WINDOW:(b + 1) * WINDOW, :] = jnp.where(
                        lo if half == 0 else jnp.logical_not(lo), qp, 0.0).astype(BF16)
                for b, p in enumerate(pairs):
                    ag = pr_ref[r0:r0 + WINDOW, dm.o_ag + LANES * p:dm.o_ag + LANES * (p + 1)]
                    dya = dy_ref[r0:r0 + WINDOW, c + LANES * p:c + LANES * (p + 1)]
                    do_ref[b * WINDOW:(b + 1) * WINDOW, :] = dya * (ag * _sigmoid(ag))
                qs = qs_ref[...]
                d_o = do_ref[...]
                d_o_b = d_o.astype(BF16)
                v_lo, v_hi = vhalf[kvh][0][rows], vhalf[kvh][1][rows]
                prob, m, inv = _softmax_with_sink(qs, k2[kvh][rows], prev, no_previous,
                                                  sink_cols[kvh])
                pb = _band_split(prob, prev)
                o_lo = jnp.dot(pb[0:hs2], v_lo, preferred_element_type=F32)
                o_hi = jnp.dot(pb[hs2:rs], v_hi, preferred_element_type=F32)
                delta = jnp.concatenate([jnp.sum(d_o * o_lo, axis=1, keepdims=True),
                                         jnp.sum(d_o * o_hi, axis=1, keepdims=True)], axis=0)
                dp = jnp.concatenate(
                    [lax.dot_general(d_o_b, v_lo, nt_dims, preferred_element_type=F32),
                     lax.dot_general(d_o_b, v_hi, nt_dims, preferred_element_type=F32)], axis=0)
                dsb = _band_split(prob * (_band_merge(dp, prev) - delta), prev)
                sink_grad = jnp.exp(sink_cols[kvh] - m) * inv * delta
                for b, (p, half) in enumerate(stacks[kvh]):
                    dsink = -jnp.sum(sink_grad[b * WINDOW:(b + 1) * WINDOW])
                    dsk = dsk + jnp.where(lane == 2 * p + half, dsink, 0.0)
                dq_lo = jnp.dot(dsb[0:hs2], khalf[kvh][0][rows], preferred_element_type=F32)
                dq_hi = jnp.dot(dsb[hs2:rs], khalf[kvh][1][rows], preferred_element_type=F32)
                dka[kvh] = lax.dot_general(dsb, qs, tn_dims, preferred_element_type=F32)
                d_o_half = jnp.concatenate([jnp.where(lo, d_o, 0.0), jnp.where(lo, 0.0, d_o)],
                                           axis=0).astype(BF16)
                dva[kvh] = lax.dot_general(pb, d_o_half, tn_dims, preferred_element_type=F32)
                for b, p in enumerate(pairs):
                    blk = slice(b * WINDOW, (b + 1) * WINDOW)
                    cols = slice(LANES * p, LANES * (p + 1))
                    ag = pr_ref[r0:r0 + WINDOW, dm.o_ag + cols.start:dm.o_ag + cols.stop]
                    dya = dy_ref[r0:r0 + WINDOW, c + cols.start:c + cols.stop]
                    dqpair = (dq_lo[blk] + dq_hi[blk]) * SCALE
                    d_ag = dya * (o_lo[blk] + o_hi[blk]) * _dsilu(ag, _sigmoid(ag))
                    dpr_ref[r0:r0 + WINDOW, dm.o_q + cols.start:dm.o_q + cols.stop] = (
                        dqpair.astype(BF16))
                    dbin_ref[0:1, dm.o_q + cols.start:dm.o_q + cols.stop] += jnp.sum(
                        dqpair, axis=0, keepdims=True)
                    dpr_ref[r0:r0 + WINDOW, dm.o_ag + cols.start:dm.o_ag + cols.stop] = (
                        d_ag.astype(BF16))
                    dbin_ref[0:1, dm.o_ag + cols.start:dm.o_ag + cols.stop] += jnp.sum(
                        d_ag, axis=0, keepdims=True)
            fold = [x + pltpu.roll(x, HEAD_DIM, 1) for x in (dka[0], dka[1], dva[0], dva[1])]
            dkv_ref[r0:r0 + 2 * WINDOW, 0:KV_W] += jnp.where(lo, fold[0], fold[1])
            dkv_ref[r0:r0 + 2 * WINDOW, KV_W:2 * KV_W] += jnp.where(lo, fold[2], fold[3])
        dsk_ref[0:1, :] += dsk
        carry_kv_ref[...] = dkv_ref[0:WINDOW, :]
        emit(dm.o_k, 2 * KV_W, dkv_ref[WINDOW:WINDOW + t, :])

        @pl.when(g == nt - 1)
        def _():
            for j in range(CONV_WIDTH):
                dcw_ref[j:j + 1, :] = jnp.sum(dcw_acc_ref[j], axis=0, keepdims=True)
            dcw_ref[CONV_WIDTH:CONV_ROWS, :] = jnp.zeros((CONV_ROWS - CONV_WIDTH, c), F32)

    rev = lambda g: nt - 1 - g
    vec = pl.BlockSpec((1, c), lambda g: (0, 0))
    const = lambda shape: pl.BlockSpec(shape, lambda g: (0, 0))
    return pl.pallas_call(
        body, name="mixer_bwd",
        out_shape=(jax.ShapeDtypeStruct((dm.s, dm.din), BF16),
                   jax.ShapeDtypeStruct((CONV_ROWS, c), F32),
                   jax.ShapeDtypeStruct((8, c), F32),
                   jax.ShapeDtypeStruct((8, dm.din), F32),
                   jax.ShapeDtypeStruct((8, LANES), F32)),
        grid=(nt,),
        in_specs=[pl.BlockSpec(memory_space=pltpu.SMEM)] + _mixer_specs(dm, t, rev)
        + [pl.BlockSpec((t, dm.d), lambda g: (rev(g), 0)), pl.BlockSpec((t, c), lambda g: (rev(g), 0)),
           pl.BlockSpec((CONV_ROWS, c), lambda g: (0, 0)), vec, vec, vec],
        out_specs=(pl.BlockSpec((t, dm.din), lambda g: (rev(g), 0)),
                   const((CONV_ROWS, c)), const((8, c)), const((8, dm.din)), const((8, LANES))),
        scratch_shapes=[pltpu.VMEM((t + HALO, c), F32), pltpu.VMEM((t + WINDOW, 2 * KV_W), F32),
                        pltpu.VMEM((t + HALO, c), F32), pltpu.VMEM((t + WINDOW, 2 * KV_W), F32),
                        pltpu.VMEM((HALO, c), F32), pltpu.VMEM((WINDOW, 2 * KV_W), F32),
                        pltpu.VMEM((SUBLANES - 1, t + HALO, LANES), F32),
                        pltpu.VMEM((SUBLANES - 1, t + HALO, LANES), F32),
                        pltpu.VMEM((t, c), F32), pltpu.VMEM((CONV_ROWS, SUBLANES, c), F32),
                        pltpu.VMEM((rs, LANES), BF16), pltpu.VMEM((rs // 2, LANES), F32)],
        compiler_params=_cparams(("arbitrary",)),
    )(sinks, proj, proj, proj, dymix, conv, conv_w, conv_b, cln_g, cln_b)


def _outproj_ln(dm, ymix, w_out, b_out, x, ln_g, ln_b, target, tm, copies=()):
    last = target is not None
    d = dm.d
    n_in = 7 if last else 6
    host = _Hosting(list(copies), n_in, 3)
    n_hin, n_hout = len(host.args), len(host.out_shapes)
    steps = dm.s // tm

    def body(*refs):
        y_ref, w_ref, bo_ref, x_ref, g_ref, b_ref = refs[:6]
        h_in = refs[n_in:n_in + n_hin]
        h_out = refs[n_in + n_hin + 3:n_in + n_hin + 3 + n_hout]
        h_sems = refs[n_in + n_hin + 3 + n_hout:]
        if copies:
            @pl.when(pl.program_id(0) == 0)
            def _():
                host.start(h_in, h_out, h_sems)

        z = dm.alpha * x_ref[...] + (
            jnp.dot(y_ref[...], w_ref[...], preferred_element_type=F32) + bo_ref[...])
        mu = jnp.mean(z, axis=1, keepdims=True)
        dlt = z - mu
        var = jnp.mean(dlt * dlt, axis=1, keepdims=True)
        out = dlt * lax.rsqrt(var + LN_EPS) * g_ref[...] + b_ref[...]
        if last:
            t_ref = refs[6]
            z_ref, dout_ref, loss_ref = refs[n_in + n_hin:n_in + n_hin + 3]
            z_ref[...] = z
            err = out - t_ref[...]
            dout_ref[...] = err * (1.0 / d)

            @pl.when(pl.program_id(0) == 0)
            def _():
                loss_ref[...] = jnp.zeros_like(loss_ref)

            loss_ref[...] += 0.5 * jnp.sum(jnp.mean(err * err, axis=1, keepdims=True), axis=0,
                                           keepdims=True)
        else:
            z_ref, o_ref, ob_ref = refs[n_in + n_hin:n_in + n_hin + 3]
            z_ref[...] = z
            o_ref[...] = out
            ob_ref[...] = out.astype(BF16)

        if copies:
            @pl.when(pl.program_id(0) == steps - 1)
            def _():
                host.finish(h_in, h_out, h_sems)

    row = pl.BlockSpec((tm, d), lambda i: (i, 0))
    vec = pl.BlockSpec((1, d), lambda i: (0, 0))
    in_specs = [row, pl.BlockSpec((d, d), lambda i: (0, 0)), vec, row, vec, vec]
    args = [ymix, w_out, b_out, x, ln_g, ln_b]
    act = jax.ShapeDtypeStruct((dm.s, d), F32)
    if last:
        in_specs.append(row)
        args.append(target)
        out_shape = [act, act, jax.ShapeDtypeStruct((8, LANES), F32)]
        out_specs = [row, row, pl.BlockSpec((8, LANES), lambda i: (0, 0))]
    else:
        out_shape = [act, act, jax.ShapeDtypeStruct((dm.s, d), BF16)]
        out_specs = [row, row, row]
    res = pl.pallas_call(
        body, name="outproj_ln_loss" if last else "outproj_ln",
        out_shape=out_shape + host.out_shapes, grid=(steps,),
        in_specs=in_specs + [_ANY] * n_hin, out_specs=out_specs + [_ANY] * n_hout,
        scratch_shapes=host.scratch, input_output_aliases=host.aliases,
        compiler_params=_cparams(("arbitrary",)),
    )(*args, *host.args)
    return res[:3], res[3:]


def _ln_bwd(dm, dout, z, ln_g, tm):
    d = dm.d

    def body(do_ref, z_ref, g_ref, dz_ref, dzb_ref, sm_ref):
        @pl.when(pl.program_id(0) == 0)
        def _():
            sm_ref[...] = jnp.zeros_like(sm_ref)

        z = z_ref[...]
        mu = jnp.mean(z, axis=1, keepdims=True)
        dlt = z - mu
        var = jnp.mean(dlt * dlt, axis=1, keepdims=True)
        rstd = lax.rsqrt(var + LN_EPS)
        zhat = dlt * rstd
        do = do_ref[...]
        dzh = do * g_ref[...]
        dz = rstd * (dzh - jnp.mean(dzh, axis=1, keepdims=True)
                     - zhat * jnp.mean(dzh * zhat, axis=1, keepdims=True))
        dz_ref[...] = dz
        dzb_ref[...] = dz.astype(BF16)
        sm_ref[0:1, :] += jnp.sum(do * zhat, axis=0, keepdims=True)
        sm_ref[1:2, :] += jnp.sum(do, axis=0, keepdims=True)
        sm_ref[2:3, :] += jnp.sum(dz, axis=0, keepdims=True)

    row = pl.BlockSpec((tm, d), lambda i: (i, 0))
    return pl.pallas_call(
        body, name="ln_bwd",
        out_shape=(jax.ShapeDtypeStruct((dm.s, d), F32), jax.ShapeDtypeStruct((dm.s, d), BF16),
                   jax.ShapeDtypeStruct((8, d), F32)),
        grid=(dm.s // tm,),
        in_specs=[row, row, pl.BlockSpec((1, d), lambda i: (0, 0))],
        out_specs=(row, row, pl.BlockSpec((8, d), lambda i: (0, 0))),
        compiler_params=_cparams(("arbitrary",)),
    )(dout, z, ln_g)


def _pair_sum(part, got, tr):
    _, _, r, w = part.shape

    def body(c_ref, p_ref, g_ref, o_ref):
        o_ref[...] = (p_ref[...] + g_ref[...]).astype(BF16)

    return pl.pallas_call(
        body, name="grad_pair_sum",
        out_shape=jax.ShapeDtypeStruct((N_CHIPS, r, w), BF16),
        grid_spec=pltpu.PrefetchScalarGridSpec(
            num_scalar_prefetch=1, grid=(N_CHIPS, r // tr),
            in_specs=[pl.BlockSpec((None, None, tr, w), lambda k, i, c_ref: (k, c_ref[0], i, 0)),
                      pl.BlockSpec((None, tr, w), lambda k, i, c_ref: (k, i, 0))],
            out_specs=pl.BlockSpec((None, tr, w), lambda k, i, c_ref: (k, i, 0))),
        compiler_params=_cparams(("parallel", "parallel")),
    )(lax.axis_index("c").reshape(1).astype(jnp.int32), part, got)


def _final_sum(part, got, recv, tr):
    _, _, r, w = part.shape

    def body(idx_ref, p_ref, g_ref, r0_ref, r1_ref, r2_ref, o_ref):
        acc = p_ref[...] + g_ref[...]
        for ref in (r0_ref, r1_ref, r2_ref):
            acc = acc + ref[...].astype(F32)
        o_ref[...] = acc

    x, y, c = _position()
    idx = jnp.stack([c, 2 * x + y, 2 * (1 - x) + y, 2 * x + (1 - y),
                     2 * (1 - x) + (1 - y)]).astype(jnp.int32)
    other = lambda j: pl.BlockSpec((None, tr, w), lambda i, s: (s[2 + j], i, 0))
    return pl.pallas_call(
        body, name="grad_final_sum",
        out_shape=jax.ShapeDtypeStruct((r, w), F32),
        grid_spec=pltpu.PrefetchScalarGridSpec(
            num_scalar_prefetch=1, grid=(r // tr,),
            in_specs=[pl.BlockSpec((None, None, tr, w), lambda i, s: (s[1], s[0], i, 0)),
                      pl.BlockSpec((None, tr, w), lambda i, s: (s[1], i, 0)),
                      other(0), other(1), other(2)],
            out_specs=pl.BlockSpec((tr, w), lambda i, s: (i, 0))),
        compiler_params=_cparams(("parallel",)),
    )(idx, part, got, recv, recv, recv)


def _adamw_math(w, g, m, v):
    m = ADAM_B1 * m + (1.0 - ADAM_B1) * g
    v = ADAM_B2 * v + (1.0 - ADAM_B2) * (g * g)
    m_hat = m / (1.0 - ADAM_B1 ** ADAM_STEP)
    v_hat = v / (1.0 - ADAM_B2 ** ADAM_STEP)
    delta = -ADAM_LR * (m_hat / (jnp.sqrt(v_hat) + ADAM_EPS) + ADAM_WD * w)
    return delta, m, v


def _adamw(w, g, m, v, tr):
    depth, r, width = w.shape

    def body(w_ref, g_ref, m_ref, v_ref, d_ref, nm_ref, nv_ref):
        d_ref[...], nm_ref[...], nv_ref[...] = _adamw_math(
            w_ref[...], g_ref[...], m_ref[...], v_ref[...])

    spec = pl.BlockSpec((None, tr, width), lambda l, i: (l, i, 0))
    shape = jax.ShapeDtypeStruct(w.shape, F32)
    return pl.pallas_call(
        body, name="adamw", out_shape=(shape, shape, shape), grid=(depth, r // tr),
        in_specs=[spec] * 4, out_specs=(spec, spec, spec),
        compiler_params=_cparams(("parallel", "parallel")),
    )(w, g, m, v)


def _small_allreduce_adamw(partial, w, m, v):
    rows = partial.shape[0]
    prows = w.shape[0]

    def body(p_ref, w_ref, m_ref, v_ref, tot_ref, d_ref, nm_ref, nv_ref, gath_ref,
             send_sems, recv_sems):
        x, y, c = _position()
        me = 4 * x + 2 * y + c
        copies = []
        for k in range(1, N_DEV):
            px = 1 - x if k & 4 else x
            py = 1 - y if k & 2 else y
            pc = 1 - c if k & 1 else c
            copies.append((pltpu.make_async_remote_copy(
                src_ref=p_ref, dst_ref=gath_ref.at[me],
                send_sem=send_sems.at[k - 1], recv_sem=recv_sems.at[k - 1],
                device_id=(px, py, pc), device_id_type=MESH), 4 * px + 2 * py + pc))
        for cp, _ in copies:
            cp.start()
        gath_ref[me] = p_ref[...]
        for k, (cp, peer) in enumerate(copies):
            pltpu.make_async_remote_copy(
                src_ref=p_ref, dst_ref=gath_ref.at[peer],
                send_sem=send_sems.at[k], recv_sem=recv_sems.at[k],
                device_id=(x, y, c), device_id_type=MESH).wait_recv()
        for cp, _ in copies:
            cp.wait_send()
        tot = gath_ref[0]
        for d in range(1, N_DEV):
            tot = tot + gath_ref[d]
        tot_ref[...] = tot
        d_ref[...], nm_ref[...], nv_ref[...] = _adamw_math(
            w_ref[...], tot[0:prows, :], m_ref[...], v_ref[...])

    vm = pl.BlockSpec(memory_space=pltpu.VMEM)
    pshape = jax.ShapeDtypeStruct(w.shape, F32)
    return pl.pallas_call(
        body, name="small_allreduce_adamw",
        out_shape=(jax.ShapeDtypeStruct(partial.shape, F32), pshape, pshape, pshape),
        in_specs=[vm] * 4, out_specs=(vm, vm, vm, vm),
        scratch_shapes=[pltpu.VMEM((N_DEV, rows, LANES), F32),
                        pltpu.SemaphoreType.DMA((N_DEV - 1,)), pltpu.SemaphoreType.DMA((N_DEV - 1,))],
        compiler_params=pltpu.CompilerParams(vmem_limit_bytes=VMEM_LIMIT),
    )(partial, w, m, v)


def _pack_rows(vec):
    depth, n = vec.shape
    rows = -(-n // LANES)
    rows = -(-rows // 8) * 8
    return jnp.pad(vec, ((0, 0), (0, rows * LANES - n))).reshape(depth, rows, LANES)


def _pack_small(named):
    blocks = [_pack_rows(a) for a in named]
    extents = [(b.shape[1], a.shape[1]) for b, a in zip(blocks, named)]
    depth = named[0].shape[0]
    packed = jnp.concatenate(blocks, axis=1).reshape(depth * sum(r for r, _ in extents), LANES)
    return packed, extents


def _unpack_small(packed, extents, depth):
    per_layer = sum(r for r, _ in extents)
    packed = packed.reshape(depth, per_layer, LANES)
    out, r0 = [], 0
    for rows, n in extents:
        out.append(packed[:, r0:r0 + rows, :].reshape(depth, rows * LANES)[:, :n])
        r0 += rows
    return out


def kernel(x, w_in, b_in, conv_w, conv_b, conv_ln_g, conv_ln_b, sinks, w_out, b_out, ln_g, ln_b, loss_target, m_w_in, m_b_in, m_conv_w, m_conv_b, m_conv_ln_g, m_conv_ln_b, m_sinks, m_w_out, m_b_out, m_ln_g, m_ln_b, v_w_in, v_b_in, v_conv_w, v_conv_b, v_conv_ln_g, v_conv_ln_b, v_sinks, v_w_out, v_b_out, v_ln_g, v_ln_b):
    depth, d, din_shard = w_in.shape
    s = x.shape[1]
    c_shard = conv_w.shape[2]
    dm = _Dims(s, d, N_DEV * c_shard, depth)
    assert dm.din == N_DEV * din_shard and x.shape[0] == 1 and sinks.shape[1] == dm.nh
    d_shard = w_out.shape[1]
    c, din = dm.c, dm.din

    t_mix = _tile(s, 256, WINDOW)
    tm_row = _tile(s, 256, 8)
    tm_big = _tile(s, 1024, 8)

    w_in_t, m_w_in_t, v_w_in_t = (a.transpose(0, 2, 1) for a in (w_in, m_w_in, v_w_in))
    w_in_b, w_out_b = w_in_t.astype(BF16), w_out.astype(BF16)
    conv_w_pad = jnp.pad(conv_w, ((0, 0), (0, CONV_ROWS - CONV_WIDTH), (0, 0)))
    first = _run_copies([_gather_own([w_in_b[0], w_out_b[0], conv_w_pad])], "weights_gather_own")
    g_in, g_out, g_conv = _run_copies([_gather_forward(first)], "weights_gather_forward")
    conv_w_full = g_conv.transpose(1, 2, 0, 3).reshape(depth, CONV_ROWS, c)
    by_cols = lambda g: g.reshape(din, d)
    by_rows = lambda g: g.reshape(d, d)

    xs = x[0]
    xb = xs.astype(BF16)
    saved = []
    loss_part = dout = None
    for l in range(depth):
        w_in_l = by_cols(g_in)
        riders = []
        if l + 1 < depth:
            riders.append(_gather_own([w_in_b[l + 1]]))
        if l > 0:
            riders.append(_gather_forward([g_out]))
        mm = dict(tb=True, tm=tm_big, tn=_tile(din, 768), tk=d, out_dtype=F32, name="in_proj",
                  bias=b_in[l][None, :])
        if riders:
            proj, landed = _matmul(xb, w_in_l, copies=riders, **mm)
            g_out = landed[-1] if l > 0 else g_out
            g_in_next = landed[0] if l + 1 < depth else None
        else:
            proj = _matmul(xb, w_in_l, **mm)
        ymix, conv = _mixer_fwd(dm, proj, conv_w_full[l], conv_b[l][None, :],
                                conv_ln_g[l][None, :], conv_ln_b[l][None, :], sinks[l], t_mix)
        w_out_l = by_rows(g_out)
        target = loss_target[0] if l == depth - 1 else None
        riders = []
        if l + 1 < depth:
            riders = [_gather_forward([g_in_next]), _gather_own([w_out_b[l + 1]])]
        res, landed = _outproj_ln(dm, ymix, w_out_l, b_out[l][None, :], xs, ln_g[l][None, :],
                                  ln_b[l][None, :], target, tm_row, copies=riders)
        saved.append((xb, proj, ymix, conv, res[0], w_in_l, w_out_l))
        if l + 1 < depth:
            g_in, g_out = landed
            xs, xb = res[1], res[2]
        else:
            dout, loss_part = res[1], res[2]

    g_w_in, g_w_out = [None] * depth, [None] * depth
    small_parts = [None] * depth
    dconv_w = [None] * depth
    tr_in, tr_out = _tile(din_shard, 512, 8), _tile(d_shard, 256, 8)
    parts_in = None
    for l in reversed(range(depth)):
        xb, proj, ymix, conv, z, w_in_l, w_out_l = saved[l]
        dz, dzb, ln_small = _ln_bwd(dm, dout, z, ln_g[l][None, :], tm_row)
        mm = dict(tb=True, tm=tm_big, tn=_tile(d, 1024), tk=d, out_dtype=F32, name="dymix")
        if parts_in is not None:
            dymix, (got_in,) = _matmul(dzb, w_out_l, copies=[_scatter_sibling([parts_in])], **mm)
            sums_in = _pair_sum(parts_in, got_in, tr_in)
        else:
            dymix = _matmul(dzb, w_out_l, **mm)
        dw_out = _matmul(ymix, dzb, ta=True, tm=_tile(d, 1024), tn=_tile(d, 1024), tk=s,
                         out_dtype=F32, name="dw_out")
        dproj, dcw, conv_small, dbin, dsk = _mixer_bwd(
            dm, proj, dymix, conv, conv_w_full[l], conv_b[l][None, :], conv_ln_g[l][None, :],
            conv_ln_b[l][None, :], sinks[l], t_mix)
        parts_out = dw_out.reshape(N_CHIPS, 2, d_shard, d)
        riders = [_scatter_sibling([parts_out])]
        if parts_in is not None:
            riders.append(_scatter_chips([sums_in]))
        dout, landed = _matmul(dproj, w_in_l, tm=_tile(s, 512), tn=_tile(d, 1024), tk=din,
                               out_dtype=F32, name="dx", resid=dz, resid_scale=dm.alpha,
                               copies=riders)
        got_out = landed[0]
        if parts_in is not None:
            g_w_in[l + 1] = _final_sum(parts_in, got_in, landed[1], tr_in)
        sums_out = _pair_sum(parts_out, got_out, tr_out)
        dw_in_t, (recv_out,) = _matmul(dproj, xb, ta=True, tm=_tile(din, 768), tn=_tile(d, 1024),
                                       tk=s, out_dtype=F32, name="dw_in",
                                       copies=[_scatter_chips([sums_out])])
        g_w_out[l] = _final_sum(parts_out, got_out, recv_out, tr_out)
        parts_in = dw_in_t.reshape(N_CHIPS, 2, din_shard, d)
        small_parts[l] = [dbin[0], conv_small[0], conv_small[1], conv_small[2], dsk[0, :dm.nh],
                          ln_small[2], ln_small[0], ln_small[1]]
        dconv_w[l] = dcw
    grad_x = dout[None]
    got_in, = _run_copies([_scatter_sibling([parts_in])], "grad_sibling_exchange")
    sums_in = _pair_sum(parts_in, got_in, tr_in)
    recv_in, = _run_copies([_scatter_chips([sums_in])], "grad_chip_exchange")
    g_w_in[0] = _final_sum(parts_in, got_in, recv_in, tr_in)

    small_w = [b_in, conv_b, conv_ln_g, conv_ln_b, sinks, b_out, ln_g, ln_b]
    small_m = [m_b_in, m_conv_b, m_conv_ln_g, m_conv_ln_b, m_sinks, m_b_out, m_ln_g, m_ln_b]
    small_v = [v_b_in, v_conv_b, v_conv_ln_g, v_conv_ln_b, v_sinks, v_b_out, v_ln_g, v_ln_b]
    n_small = len(small_w)
    packed_g, extents = _pack_small([jnp.stack([small_parts[l][k] for l in range(depth)])
                                     for k in range(n_small)])
    packed_w, _ = _pack_small(small_w)
    packed_m, _ = _pack_small(small_m)
    packed_v, _ = _pack_small(small_v)
    prows = packed_g.shape[0]
    conv_rows = depth * CONV_ROWS * c // LANES
    partial = jnp.concatenate(
        [packed_g, jnp.stack(dconv_w).reshape(conv_rows, LANES), loss_part], axis=0)
    total, sm_delta, sm_m, sm_v = _small_allreduce_adamw(partial, packed_w, packed_m, packed_v)
    loss = total[prows + conv_rows, 0]
    dconv_w_full = total[prows:prows + conv_rows].reshape(depth, CONV_ROWS, c)
    me = 4 * lax.axis_index("x") + 2 * lax.axis_index("y") + lax.axis_index("c")
    grad_conv_w = lax.dynamic_slice_in_dim(dconv_w_full, me * c_shard, c_shard, axis=2)[:, :CONV_WIDTH]

    grads_small = _unpack_small(total[:prows], extents, depth)
    delta_small = _unpack_small(sm_delta, extents, depth)
    newm_small = _unpack_small(sm_m, extents, depth)
    newv_small = _unpack_small(sm_v, extents, depth)

    grad_w_in_t = jnp.stack(g_w_in)
    grad_w_out = jnp.stack(g_w_out)
    grad_w_in, d_w_in, nm_w_in, nv_w_in = (a.transpose(0, 2, 1) for a in (
        grad_w_in_t, *_adamw(w_in_t, grad_w_in_t, m_w_in_t, v_w_in_t, tr_in)))
    d_w_out, nm_w_out, nv_w_out = _adamw(w_out, grad_w_out, m_w_out, v_w_out, _tile(d_shard, 256, 8))
    d_cw, nm_cw, nv_cw = _adamw(conv_w, grad_conv_w, m_conv_w, v_conv_w, CONV_WIDTH)

    def assemble(w_in_leaf, conv_w_leaf, w_out_leaf, small):
        b_in_, conv_b_, cg_, cb_, sinks_, b_out_, ln_g_, ln_b_ = small
        return [w_in_leaf, b_in_, conv_w_leaf, conv_b_, cg_, cb_, sinks_, w_out_leaf, b_out_,
                ln_g_, ln_b_]

    return (loss, grad_x,
            *assemble(grad_w_in, grad_conv_w, grad_w_out, grads_small),
            *assemble(d_w_in, d_cw, d_w_out, delta_small),
            *assemble(nm_w_in, nm_cw, nm_w_out, newm_small),
            *assemble(nv_w_in, nv_cw, nv_w_out, newv_small))
```

```python
import functools

import jax
import jax.numpy as jnp
from jax import lax
from jax.experimental import pallas as pl
from jax.experimental.pallas import tpu as pltpu

F32 = jnp.float32
BF16 = jnp.bfloat16
MESH = pl.DeviceIdType.MESH

N_DEV = 8
N_CHIPS = 4
HEAD_DIM = 64
N_KV = 2
KV_W = N_KV * HEAD_DIM
CONV_WIDTH = 31
CONV_ROWS = 32
HALO = 32
WINDOW = 128
LN_EPS = 1e-5
NEG_INF = -1e30
LANES = 128

ADAM_LR = 0.001
ADAM_B1 = 0.9
ADAM_B2 = 0.999
ADAM_EPS = 1e-08
ADAM_WD = 0.01
ADAM_STEP = 10

VMEM_LIMIT = 56 * 1024 * 1024


def _tile(n, target, align=LANES):
    best = None
    for t in range(align, min(n, target) + 1, align):
        if n % t == 0:
            best = t
    return n if best is None else best


def _sigmoid(x):
    return jax.nn.sigmoid(x)


def _dsilu(x, s):
    return s * (1.0 + x * (1.0 - s))


def _cparams(sem, vmem=VMEM_LIMIT):
    return pltpu.CompilerParams(dimension_semantics=sem, vmem_limit_bytes=vmem)


_ANY = pl.BlockSpec(memory_space=pl.ANY)


def _position():
    return lax.axis_index("x"), lax.axis_index("y"), lax.axis_index("c")


def _other_chips(x, y):
    return [(1 - x, y), (x, 1 - y), (1 - x, 1 - y)]


class _Copies:
    def __init__(self, operands, landing, alias, sems, start, finish):
        self.operands, self.landing, self.alias, self.sems = operands, landing, alias, sems
        self.start, self.finish = start, finish


class _Hosting:
    def __init__(self, groups, n_in, n_out):
        self.groups = groups
        self.args = [a for g in groups for a in g.operands]
        self.out_shapes = [s for g in groups for s in g.landing]
        self.scratch = [pltpu.SemaphoreType.DMA(g.sems) for g in groups for _ in range(2)]
        self.aliases = {}
        i0, o0 = n_in, n_out
        for g in groups:
            for a, b in g.alias.items():
                self.aliases[i0 + a] = o0 + b
            i0 += len(g.operands)
            o0 += len(g.landing)

    def _each(self, in_refs, out_refs, sem_refs):
        i0 = o0 = 0
        for n, g in enumerate(self.groups):
            yield (g, in_refs[i0:i0 + len(g.operands)], out_refs[o0:o0 + len(g.landing)],
                   sem_refs[2 * n], sem_refs[2 * n + 1])
            i0 += len(g.operands)
            o0 += len(g.landing)

    def start(self, in_refs, out_refs, sem_refs):
        for g, ins, outs, send, recv in self._each(in_refs, out_refs, sem_refs):
            g.start(ins, outs, send, recv)

    def finish(self, in_refs, out_refs, sem_refs):
        for g, ins, outs, send, recv in self._each(in_refs, out_refs, sem_refs):
            g.finish(ins, outs, send, recv)


def _ride(body, host, n_in, n_out, n_scratch, steps):
    n_hin, n_hout = len(host.args), len(host.out_shapes)

    def wrapped(*refs):
        pos = [0]

        def take(n):
            pos[0] += n
            return refs[pos[0] - n:pos[0]]

        ins, h_in, outs, h_out = take(n_in), take(n_hin), take(n_out), take(n_hout)
        scratch, h_sems = take(n_scratch), refs[pos[0]:]
        if host.groups:
            @pl.when(pl.program_id(0) == 0)
            def _():
                host.start(h_in, h_out, h_sems)

        body(*ins, *outs, *scratch)
        if host.groups:
            @pl.when(pl.program_id(0) == steps - 1)
            def _():
                host.finish(h_in, h_out, h_sems)

    return wrapped


def _run_copies(groups, name):
    host = _Hosting(groups, 0, 0)
    n_in, n_out = len(host.args), len(host.out_shapes)

    def body(*refs):
        ins, outs, sems = refs[:n_in], refs[n_in:n_in + n_out], refs[n_in + n_out:]
        host.start(ins, outs, sems)
        host.finish(ins, outs, sems)

    return pl.pallas_call(
        body, name=name, out_shape=host.out_shapes, in_specs=[_ANY] * n_in,
        out_specs=[_ANY] * n_out, scratch_shapes=host.scratch,
        input_output_aliases=host.aliases,
    )(*host.args)


def _gather_own(shards):
    n = len(shards)

    def copies(ins, outs, send, recv):
        x, y, c = _position()
        peers = [(x, y, 1 - c)] + [(px, py, c) for px, py in _other_chips(x, y)]
        out = []
        for t in range(n):
            for k, peer in enumerate(peers):
                out.append((pltpu.make_async_remote_copy(
                    src_ref=ins[t], dst_ref=outs[t].at[4 * x + 2 * y + c],
                    send_sem=send.at[t, k], recv_sem=recv.at[t, k],
                    device_id=peer, device_id_type=MESH), t, k, peer))
        local = [pltpu.make_async_copy(ins[t], outs[t].at[4 * x + 2 * y + c], send.at[t, 4])
                 for t in range(n)]
        return out, local

    def start(ins, outs, send, recv):
        remote, local = copies(ins, outs, send, recv)
        for cp in local:
            cp.start()
        for cp, _, _, _ in remote:
            cp.start()

    def finish(ins, outs, send, recv):
        remote, local = copies(ins, outs, send, recv)
        x, y, c = _position()
        for _, t, k, (px, py, pc) in remote:
            pltpu.make_async_remote_copy(
                src_ref=ins[t], dst_ref=outs[t].at[4 * px + 2 * py + pc],
                send_sem=send.at[t, k], recv_sem=recv.at[t, k],
                device_id=(x, y, c), device_id_type=MESH).wait_recv()
        for cp, _, _, _ in remote:
            cp.wait_send()
        for cp in local:
            cp.wait()

    landing = [jax.ShapeDtypeStruct((N_DEV,) + s.shape, s.dtype) for s in shards]
    return _Copies(list(shards), landing, {}, (n, 5), start, finish)


def _gather_forward(buffers):
    n = len(buffers)

    def copies(ins, outs, send, recv):
        x, y, c = _position()
        out = []
        for t in range(n):
            for j, (px, py) in enumerate(_other_chips(x, y)):
                slot = 4 * px + 2 * py + c
                out.append((pltpu.make_async_remote_copy(
                    src_ref=ins[t].at[slot], dst_ref=outs[t].at[slot],
                    send_sem=send.at[t, j], recv_sem=recv.at[t, j],
                    device_id=(x, y, 1 - c), device_id_type=MESH), t, j, 4 * px + 2 * py + 1 - c))
        return out

    def start(ins, outs, send, recv):
        for cp, _, _, _ in copies(ins, outs, send, recv):
            cp.start()

    def finish(ins, outs, send, recv):
        x, y, c = _position()
        mine = copies(ins, outs, send, recv)
        for _, t, j, got in mine:
            pltpu.make_async_remote_copy(
                src_ref=ins[t].at[got], dst_ref=outs[t].at[got],
                send_sem=send.at[t, j], recv_sem=recv.at[t, j],
                device_id=(x, y, c), device_id_type=MESH).wait_recv()
        for cp, _, _, _ in mine:
            cp.wait_send()

    landing = [jax.ShapeDtypeStruct(b.shape, b.dtype) for b in buffers]
    return _Copies(list(buffers), landing, {t: t for t in range(n)}, (n, 3), start, finish)


def _scatter_sibling(parts):
    n = len(parts)

    def copies(ins, outs, send, recv):
        x, y, c = _position()
        return [pltpu.make_async_remote_copy(
            src_ref=ins[t].at[:, 1 - c], dst_ref=outs[t],
            send_sem=send.at[t, 0], recv_sem=recv.at[t, 0],
            device_id=(x, y, 1 - c), device_id_type=MESH) for t in range(n)]

    def start(ins, outs, send, recv):
        for cp in copies(ins, outs, send, recv):
            cp.start()

    def finish(ins, outs, send, recv):
        for cp in copies(ins, outs, send, recv):
            cp.wait()

    landing = [jax.ShapeDtypeStruct((p.shape[0],) + p.shape[2:], p.dtype) for p in parts]
    return _Copies(list(parts), landing, {}, (n, 1), start, finish)


def _scatter_chips(blocks):
    n = len(blocks)

    def start(ins, outs, send, recv):
        x, y, c = _position()
        for t in range(n):
            for j, (px, py) in enumerate(_other_chips(x, y)):
                pltpu.make_async_remote_copy(
                    src_ref=ins[t].at[2 * px + py], dst_ref=outs[t].at[2 * x + y],
                    send_sem=send.at[t, j], recv_sem=recv.at[t, j],
                    device_id=(px, py, c), device_id_type=MESH).start()

    def finish(ins, outs, send, recv):
        x, y, c = _position()
        for t in range(n):
            for j, (px, py) in enumerate(_other_chips(x, y)):
                cp = pltpu.make_async_remote_copy(
                    src_ref=ins[t].at[2 * px + py], dst_ref=outs[t].at[2 * px + py],
                    send_sem=send.at[t, j], recv_sem=recv.at[t, j],
                    device_id=(px, py, c), device_id_type=MESH)
                cp.wait_recv()
                cp.wait_send()

    landing = [jax.ShapeDtypeStruct(b.shape, b.dtype) for b in blocks]
    return _Copies(list(blocks), landing, {}, (n, 3), start, finish)


def _matmul(a, b, *, ta=False, tb=False, tm, tn, tk, out_dtype, name, bias=None, resid=None,
            resid_scale=1.0, copies=()):
    m, k = (a.shape[1], a.shape[0]) if ta else a.shape
    n = b.shape[0] if tb else b.shape[1]
    assert (b.shape[1] if tb else b.shape[0]) == k
    assert m % tm == 0 and n % tn == 0 and k % tk == 0
    ni, nj, nk = m // tm, n // tn, k // tk
    dn = (((0 if ta else 1,), (1 if tb else 0,)), ((), ()))
    n_in = 2 + (bias is not None) + (resid is not None)
    host = _Hosting(list(copies), n_in, 1)
    n_hin, n_hout = len(host.args), len(host.out_shapes)

    def body(*refs):
        a_ref, b_ref = refs[0], refs[1]
        pos = 2
        bias_ref = resid_ref = None
        if bias is not None:
            bias_ref = refs[pos]
            pos += 1
        if resid is not None:
            resid_ref = refs[pos]
            pos += 1
        h_in = refs[pos:pos + n_hin]
        pos += n_hin
        o_ref = refs[pos]
        h_out = refs[pos + 1:pos + 1 + n_hout]
        pos += 1 + n_hout
        acc_ref = refs[pos] if nk > 1 else None
        h_sems = refs[pos + (nk > 1):]
        step = (pl.program_id(0) * nj + pl.program_id(1)) * nk + pl.program_id(2)

        if copies:
            @pl.when(step == 0)
            def _():
                host.start(h_in, h_out, h_sems)

        def finish(acc):
            if bias_ref is not None:
                acc = acc + bias_ref[...]
            if resid_ref is not None:
                acc = acc + resid_scale * resid_ref[...]
            o_ref[...] = acc.astype(out_dtype)

        p = lax.dot_general(a_ref[...], b_ref[...], dn, preferred_element_type=F32)
        if nk == 1:
            finish(p)
        else:
            kk = pl.program_id(2)

            @pl.when(kk == 0)
            def _():
                acc_ref[...] = p

            @pl.when(kk > 0)
            def _():
                acc_ref[...] += p

            @pl.when(kk == nk - 1)
            def _():
                finish(acc_ref[...])

        if copies:
            @pl.when(step == ni * nj * nk - 1)
            def _():
                host.finish(h_in, h_out, h_sems)

    a_spec = (pl.BlockSpec((tk, tm), lambda i, j, kk: (kk, i)) if ta
              else pl.BlockSpec((tm, tk), lambda i, j, kk: (i, kk)))
    b_spec = (pl.BlockSpec((tn, tk), lambda i, j, kk: (j, kk)) if tb
              else pl.BlockSpec((tk, tn), lambda i, j, kk: (kk, j)))
    in_specs = [a_spec, b_spec]
    args = [a, b]
    if bias is not None:
        in_specs.append(pl.BlockSpec((1, tn), lambda i, j, kk: (0, j)))
        args.append(bias)
    if resid is not None:
        in_specs.append(pl.BlockSpec((tm, tn), lambda i, j, kk: (i, j)))
        args.append(resid)
    res = pl.pallas_call(
        body, name=name,
        out_shape=[jax.ShapeDtypeStruct((m, n), out_dtype)] + host.out_shapes,
        grid=(ni, nj, nk),
        in_specs=in_specs + [_ANY] * n_hin,
        out_specs=[pl.BlockSpec((tm, tn), lambda i, j, kk: (i, j))] + [_ANY] * n_hout,
        scratch_shapes=([pltpu.VMEM((tm, tn), F32)] if nk > 1 else []) + host.scratch,
        input_output_aliases=host.aliases,
        compiler_params=_cparams(("arbitrary",) * 3 if copies else
                                 ("parallel", "parallel", "arbitrary")),
    )(*args, *host.args)
    return (res[0], res[1:]) if copies else res[0]


class _Dims:
    def __init__(self, s, d, c, depth):
        self.s, self.d, self.c, self.depth = s, d, c, depth
        self.a = d - c
        self.nh = self.a // HEAD_DIM
        self.group = self.nh // N_KV
        self.din = 3 * c + 2 * self.a + 2 * KV_W
        self.o_q = 3 * c
        self.o_k = 3 * c + self.a
        self.o_v = self.o_k + KV_W
        self.o_ag = self.o_k + 2 * KV_W
        self.alpha = (2 * depth) ** 0.25
        assert self.nh % 2 == 0 and self.group % 2 == 0 and self.o_k % (2 * KV_W) == 0
        assert c % LANES == 0 and self.a % LANES == 0


SUBLANES = 8
TAP_ROWS = 64


def _shift_copies(src_ref, sh_ref, lanes, rows):
    for r in range(1, SUBLANES):
        sh_ref[r - 1, 0:rows, :] = src_ref[pl.ds(r, rows), lanes]


def _tap_rows(src_ref, sh_ref, lanes, off, start, rows):
    r = off % SUBLANES
    at = pl.multiple_of(start + (off - r), SUBLANES)
    if r == 0:
        return src_ref[pl.ds(at, rows), lanes]
    return sh_ref[r - 1, pl.ds(at, rows), :]


def _conv_chunk(w_ref, src_ref, sh_ref, out_ref, lanes, t, first_off, reverse, bias_ref=None):
    def block(it, carry):
        start = pl.multiple_of(it * TAP_ROWS, TAP_ROWS)
        acc = None
        for j in range(CONV_WIDTH):
            off = first_off - j if reverse else first_off + j
            term = w_ref[j:j + 1, lanes] * _tap_rows(src_ref, sh_ref, lanes, off, start, TAP_ROWS)
            acc = term if acc is None else acc + term
        if bias_ref is not None:
            acc = acc + bias_ref[:, lanes]
        out_ref[pl.ds(start, TAP_ROWS), lanes] = acc
        return carry

    lax.fori_loop(0, t // TAP_ROWS, block, 0)


def _conv_w_grad_chunk(src_ref, sh_ref, dconv_ref, acc_ref, lanes, t, first_off):
    def block(it, accs):
        start = pl.multiple_of(it * SUBLANES, SUBLANES)
        dv = dconv_ref[pl.ds(start, SUBLANES), lanes]
        return tuple(
            acc + _tap_rows(src_ref, sh_ref, lanes, first_off + j, start, SUBLANES) * dv
            for j, acc in enumerate(accs))

    zero = jnp.zeros((SUBLANES, LANES), F32)
    accs = lax.fori_loop(0, t // SUBLANES, block, (zero,) * CONV_WIDTH, unroll=2)
    for j in range(CONV_WIDTH):
        acc_ref[j, :, lanes] += accs[j]


def _kv_operands(kv_ref, lo):
    kext = kv_ref[:, 0:KV_W]
    vext = kv_ref[:, KV_W:2 * KV_W]
    ksw = pltpu.roll(kext, HEAD_DIM, 1)
    vsw = pltpu.roll(vext, HEAD_DIM, 1)
    zero = jnp.zeros_like(kext)
    k2 = [jnp.where(lo, kext, ksw).astype(BF16), jnp.where(lo, ksw, kext).astype(BF16)]
    khalf = [[jnp.where(lo, kext, zero).astype(BF16), jnp.where(lo, zero, ksw).astype(BF16)],
             [jnp.where(lo, ksw, zero).astype(BF16), jnp.where(lo, zero, kext).astype(BF16)]]
    vhalf = [[jnp.where(lo, vext, zero).astype(BF16), jnp.where(lo, zero, vsw).astype(BF16)],
             [jnp.where(lo, vsw, zero).astype(BF16), jnp.where(lo, zero, vext).astype(BF16)]]
    return k2, khalf, vhalf


SCALE = HEAD_DIM ** -0.5


def _from_previous(rows):
    row = lax.broadcasted_iota(jnp.int32, (rows, WINDOW), 0)
    col = lax.broadcasted_iota(jnp.int32, (rows, WINDOW), 1)
    return col > (row & (WINDOW - 1))


def _sink_column(sinks_ref, heads):
    block = lax.broadcasted_iota(jnp.int32, (len(heads) * WINDOW, 1), 0) // WINDOW
    out = jnp.zeros(block.shape, F32)
    for b, head in enumerate(heads):
        out = jnp.where(block == b, sinks_ref[head], out)
    return out


def _stacked_heads(dm, kvh):
    pairs = range(kvh * (dm.group // 2), (kvh + 1) * (dm.group // 2))
    return [(p, half) for half in range(2) for p in pairs]


def _band_merge(both, prev):
    return jnp.where(prev, both[:, 0:WINDOW], both[:, WINDOW:2 * WINDOW])


def _band_split(tile, prev):
    zero = jnp.zeros_like(tile)
    return jnp.concatenate([jnp.where(prev, tile, zero), jnp.where(prev, zero, tile)],
                           axis=1).astype(BF16)


def _softmax_with_sink(qm, k2rows, prev, no_previous, sink):
    both = lax.dot_general(qm, k2rows, (((1,), (1,)), ((), ())), preferred_element_type=F32)
    s_prev = both[:, 0:WINDOW]
    if no_previous is not None:
        s_prev = jnp.where(no_previous, NEG_INF, s_prev)
    s = jnp.where(prev, s_prev, both[:, WINDOW:2 * WINDOW])
    m = jnp.maximum(jnp.max(s, axis=1, keepdims=True), sink)
    e = jnp.exp(s - m)
    den = jnp.sum(e, axis=1, keepdims=True) + jnp.exp(sink - m)
    inv = 1.0 / den
    return e * inv, m, inv


def _mixer_specs(dm, t, idx):
    return [
        pl.BlockSpec((t, dm.din), lambda g: (idx(g), 0)),
        pl.BlockSpec((HALO, 2 * dm.c), lambda g: (jnp.maximum(idx(g) * (t // HALO) - 1, 0), 0)),
        pl.BlockSpec((WINDOW, 2 * KV_W),
                     lambda g: (jnp.maximum(idx(g) * (t // WINDOW) - 1, 0), dm.o_k // (2 * KV_W))),
    ]


def _mixer_fwd(dm, proj, conv_w, conv_b, cln_g, cln_b, sinks, t, copies=()):
    c, nq = dm.c, t // WINDOW
    rs = dm.group * WINDOW
    host = _Hosting(list(copies), 8, 2)

    def body(sinks_ref, pr_ref, ch_ref, kvh_ref, cw_ref, cb_ref, cg_ref, cbb_ref,
             y_ref, conv_ref, hext_ref, kv_ref, hs_ref, qs_ref):
        i = pl.program_id(0)
        first = i == 0
        h = pr_ref[:, 0:c] * _sigmoid(pr_ref[:, c:2 * c])
        hh = ch_ref[:, 0:c] * _sigmoid(ch_ref[:, c:2 * c])
        hext_ref[0:HALO, :] = jnp.where(first, 0.0, hh)
        hext_ref[HALO:HALO + t, :] = h
        for k in range(c // LANES):
            lanes = slice(LANES * k, LANES * (k + 1))
            _shift_copies(hext_ref, hs_ref, lanes, t + HALO - SUBLANES)
            _conv_chunk(cw_ref, hext_ref, hs_ref, conv_ref, lanes, t, HALO - (CONV_WIDTH - 1),
                        False, cb_ref)
        conv = conv_ref[...]
        mu = jnp.mean(conv, axis=1, keepdims=True)
        dlt = conv - mu
        var = jnp.mean(dlt * dlt, axis=1, keepdims=True)
        u = dlt * lax.rsqrt(var + LN_EPS) * cg_ref[...] + cbb_ref[...]
        gate = pr_ref[:, 2 * c:3 * c]
        y_ref[:, 0:c] = (u * _sigmoid(u) * (gate * _sigmoid(gate))).astype(BF16)

        kv_ref[0:WINDOW, :] = jnp.where(first, 0.0, kvh_ref[...])
        kv_ref[WINDOW:WINDOW + t, :] = pr_ref[:, dm.o_k:dm.o_k + 2 * KV_W]
        lo = lax.broadcasted_iota(jnp.int32, (1, LANES), 1) < HEAD_DIM
        k2, _, vhalf = _kv_operands(kv_ref, lo)
        stacks = [_stacked_heads(dm, kvh) for kvh in range(N_KV)]
        sink_cols = [_sink_column(sinks_ref, [2 * p + half for p, half in st]) for st in stacks]
        prev = _from_previous(rs)
        for qb in range(nq):
            r0 = qb * WINDOW
            rows = slice(r0, r0 + 2 * WINDOW)
            no_previous = first if qb == 0 else None
            for kvh in range(N_KV):
                for b, (p, half) in enumerate(stacks[kvh]):
                    qp = pr_ref[r0:r0 + WINDOW, dm.o_q + LANES * p:dm.o_q + LANES * (p + 1)] * SCALE
                    qs_ref[b * WINDOW:(b + 1) * WINDOW, :] = jnp.where(
                        lo if half == 0 else jnp.logical_not(lo), qp, 0.0).astype(BF16)
                prob, _, _ = _softmax_with_sink(qs_ref[...], k2[kvh][rows], prev, no_previous,
                                                sink_cols[kvh])
                pb = _band_split(prob, prev)
                o_lo = jnp.dot(pb[0:rs // 2], vhalf[kvh][0][rows], preferred_element_type=F32)
                o_hi = jnp.dot(pb[rs // 2:rs], vhalf[kvh][1][rows], preferred_element_type=F32)
                for b, (p, _) in enumerate(stacks[kvh][:len(stacks[kvh]) // 2]):
                    blk = slice(b * WINDOW, (b + 1) * WINDOW)
                    ag = pr_ref[r0:r0 + WINDOW, dm.o_ag + LANES * p:dm.o_ag + LANES * (p + 1)]
                    y_ref[r0:r0 + WINDOW, c + LANES * p:c + LANES * (p + 1)] = (
                        (o_lo[blk] + o_hi[blk]) * (ag * _sigmoid(ag))).astype(BF16)

    vec = pl.BlockSpec((1, c), lambda g: (0, 0))
    scratch = [pltpu.VMEM((t + HALO, c), F32), pltpu.VMEM((t + WINDOW, 2 * KV_W), F32),
               pltpu.VMEM((SUBLANES - 1, t + HALO, LANES), F32), pltpu.VMEM((rs, LANES), BF16)]
    res = pl.pallas_call(
        _ride(body, host, 8, 2, len(scratch), dm.s // t), name="mixer_fwd",
        out_shape=[jax.ShapeDtypeStruct((dm.s, dm.d), BF16),
                   jax.ShapeDtypeStruct((dm.s, c), F32)]
        + host.out_shapes,
        grid=(dm.s // t,),
        in_specs=[pl.BlockSpec(memory_space=pltpu.SMEM)] + _mixer_specs(dm, t, lambda g: g)
        + [pl.BlockSpec((CONV_ROWS, c), lambda g: (0, 0)), vec, vec, vec] + [_ANY] * len(host.args),
        out_specs=[pl.BlockSpec((t, dm.d), lambda g: (g, 0)), pl.BlockSpec((t, c), lambda g: (g, 0))]
        + [_ANY] * len(host.out_shapes),
        scratch_shapes=scratch + host.scratch, input_output_aliases=host.aliases,
        compiler_params=_cparams(("arbitrary",)),
    )(sinks, proj, proj, proj, conv_w, conv_b, cln_g, cln_b, *host.args)
    return res[:2], res[2:]


def _mixer_bwd(dm, proj, dymix, conv, conv_w, conv_b, cln_g, cln_b, sinks, t, copies=()):
    c, nq, nt = dm.c, t // WINDOW, dm.s // t
    rs = dm.group * WINDOW

    def body(sinks_ref, pr_ref, ch_ref, kvh_ref, dy_ref, cv_ref, cw_ref, cb_ref, cg_ref, cbb_ref,
             dpr_ref, dcw_ref, dsm_ref, dbin_ref, dsk_ref,
             hext_ref, kv_ref, dcx_ref, dkv_ref, carry_dc_ref, carry_kv_ref,
             hs_ref, ds_ref, dh_ref, dcw_acc_ref, qs_ref, do_ref):
        g = pl.program_id(0)
        i = nt - 1 - g
        first = i == 0

        @pl.when(g == 0)
        def _():
            dcw_acc_ref[...] = jnp.zeros_like(dcw_acc_ref)
            dsm_ref[...] = jnp.zeros_like(dsm_ref)
            dbin_ref[...] = jnp.zeros_like(dbin_ref)
            dsk_ref[...] = jnp.zeros_like(dsk_ref)
            carry_dc_ref[...] = jnp.zeros_like(carry_dc_ref)
            carry_kv_ref[...] = jnp.zeros_like(carry_kv_ref)

        def emit(col, width, val):
            dpr_ref[:, col:col + width] = val.astype(BF16)
            dbin_ref[0:1, col:col + width] += jnp.sum(val, axis=0, keepdims=True)

        val = pr_ref[:, 0:c]
        sg = _sigmoid(pr_ref[:, c:2 * c])
        h = val * sg
        hh = ch_ref[:, 0:c] * _sigmoid(ch_ref[:, c:2 * c])
        hext_ref[0:HALO, :] = jnp.where(first, 0.0, hh)
        hext_ref[HALO:HALO + t, :] = h
        conv = cv_ref[...]
        mu = jnp.mean(conv, axis=1, keepdims=True)
        dlt = conv - mu
        var = jnp.mean(dlt * dlt, axis=1, keepdims=True)
        rstd = lax.rsqrt(var + LN_EPS)
        xhat = dlt * rstd
        u = xhat * cg_ref[...] + cbb_ref[...]
        su = _sigmoid(u)
        gate = pr_ref[:, 2 * c:3 * c]
        sgate = _sigmoid(gate)
        dyc = dy_ref[:, 0:c]
        emit(2 * c, c, dyc * (u * su) * _dsilu(gate, sgate))
        du = dyc * (gate * sgate) * _dsilu(u, su)
        dsm_ref[1:2, :] += jnp.sum(du * xhat, axis=0, keepdims=True)
        dsm_ref[2:3, :] += jnp.sum(du, axis=0, keepdims=True)
        dxh = du * cg_ref[...]
        dconv = rstd * (dxh - jnp.mean(dxh, axis=1, keepdims=True)
                        - xhat * jnp.mean(dxh * xhat, axis=1, keepdims=True))
        dsm_ref[0:1, :] += jnp.sum(dconv, axis=0, keepdims=True)
        dcx_ref[0:t, :] = dconv
        dcx_ref[t:t + HALO, :] = carry_dc_ref[...]
        carry_dc_ref[...] = dconv[0:HALO, :]
        for k in range(c // LANES):
            lanes = slice(LANES * k, LANES * (k + 1))
            _shift_copies(hext_ref, hs_ref, lanes, t + HALO - SUBLANES)
            _shift_copies(dcx_ref, ds_ref, lanes, t + HALO - SUBLANES)
            _conv_chunk(cw_ref, dcx_ref, ds_ref, dh_ref, lanes, t, CONV_WIDTH - 1, True)
            _conv_w_grad_chunk(hext_ref, hs_ref, dcx_ref, dcw_acc_ref, lanes, t,
                               HALO - (CONV_WIDTH - 1))
        dh = dh_ref[...]
        emit(0, c, dh * sg)
        emit(c, c, dh * val * sg * (1.0 - sg))

        kv_ref[0:WINDOW, :] = jnp.where(first, 0.0, kvh_ref[...])
        kv_ref[WINDOW:WINDOW + t, :] = pr_ref[:, dm.o_k:dm.o_k + 2 * KV_W]
        dkv_ref[0:t, :] = jnp.zeros((t, 2 * KV_W), F32)
        dkv_ref[t:t + WINDOW, :] = carry_kv_ref[...]
        lane = lax.broadcasted_iota(jnp.int32, (1, LANES), 1)
        lo = lane < HEAD_DIM
        k2, khalf, vhalf = _kv_operands(kv_ref, lo)
        tn_dims = (((0,), (0,)), ((), ()))
        nt_dims = (((1,), (1,)), ((), ()))
        dsk = jnp.zeros((1, LANES), F32)
        stacks = [_stacked_heads(dm, kvh) for kvh in range(N_KV)]
        sink_cols = [_sink_column(sinks_ref, [2 * p + half for p, half in st]) for st in stacks]
        prev = _from_previous(rs)
        hs2 = rs // 2
        for qb in range(nq):
            r0 = qb * WINDOW
            rows = slice(r0, r0 + 2 * WINDOW)
            no_previous = first if qb == 0 else None
            dka = [None, None]
            dva = [None, None]
            for kvh in range(N_KV):
                pairs = [p for p, half in stacks[kvh] if half == 0]
                for b, (p, half) in enumerate(stacks[kvh]):
                    qp = pr_ref[r0:r0 + WINDOW, dm.o_q + LANES * p:dm.o_q + LANES * (p + 1)] * SCALE
                    qs_ref[b * WINDOW:(b + 1) * WINDOW, :] = jnp.where(
                        lo if half == 0 else jnp.logical_not(lo), qp, 0.0).astype(BF16)
                for b, p in enumerate(pairs):
                    ag = pr_ref[r0:r0 + WINDOW, dm.o_ag + LANES * p:dm.o_ag + LANES * (p + 1)]
                    dya = dy_ref[r0:r0 + WINDOW, c + LANES * p:c + LANES * (p + 1)]
                    do_ref[b * WINDOW:(b + 1) * WINDOW, :] = dya * (ag * _sigmoid(ag))
                qs = qs_ref[...]
                d_o = do_ref[...]
                d_o_b = d_o.astype(BF16)
                v_lo, v_hi = vhalf[kvh][0][rows], vhalf[kvh][1][rows]
                prob, m, inv = _softmax_with_sink(qs, k2[kvh][rows], prev, no_previous,
                                                  sink_cols[kvh])
                pb = _band_split(prob, prev)
                o_lo = jnp.dot(pb[0:hs2], v_lo, preferred_element_type=F32)
                o_hi = jnp.dot(pb[hs2:rs], v_hi, preferred_element_type=F32)
                delta = jnp.concatenate([jnp.sum(d_o * o_lo, axis=1, keepdims=True),
                                         jnp.sum(d_o * o_hi, axis=1, keepdims=True)], axis=0)
                dp = jnp.concatenate(
                    [lax.dot_general(d_o_b, v_lo, nt_dims, preferred_element_type=F32),
                     lax.dot_general(d_o_b, v_hi, nt_dims, preferred_element_type=F32)], axis=0)
                dsb = _band_split(prob * (_band_merge(dp, prev) - delta), prev)
                sink_grad = jnp.exp(sink_cols[kvh] - m) * inv * delta
                for b, (p, half) in enumerate(stacks[kvh]):
                    dsink = -jnp.sum(sink_grad[b * WINDOW:(b + 1) * WINDOW])
                    dsk = dsk + jnp.where(lane == 2 * p + half, dsink, 0.0)
                dq_lo = jnp.dot(dsb[0:hs2], khalf[kvh][0][rows], preferred_element_type=F32)
                dq_hi = jnp.dot(dsb[hs2:rs], khalf[kvh][1][rows], preferred_element_type=F32)
                dka[kvh] = lax.dot_general(dsb, qs, tn_dims, preferred_element_type=F32)
                d_o_half = jnp.concatenate([jnp.where(lo, d_o, 0.0), jnp.where(lo, 0.0, d_o)],
                                           axis=0).astype(BF16)
                dva[kvh] = lax.dot_general(pb, d_o_half, tn_dims, preferred_element_type=F32)
                for b, p in enumerate(pairs):
                    blk = slice(b * WINDOW, (b + 1) * WINDOW)
                    cols = slice(LANES * p, LANES * (p + 1))
                    ag = pr_ref[r0:r0 + WINDOW, dm.o_ag + cols.start:dm.o_ag + cols.stop]
                    dya = dy_ref[r0:r0 + WINDOW, c + cols.start:c + cols.stop]
                    dqpair = (dq_lo[blk] + dq_hi[blk]) * SCALE
                    d_ag = dya * (o_lo[blk] + o_hi[blk]) * _dsilu(ag, _sigmoid(ag))
                    dpr_ref[r0:r0 + WINDOW, dm.o_q + cols.start:dm.o_q + cols.stop] = (
                        dqpair.astype(BF16))
                    dbin_ref[0:1, dm.o_q + cols.start:dm.o_q + cols.stop] += jnp.sum(
                        dqpair, axis=0, keepdims=True)
                    dpr_ref[r0:r0 + WINDOW, dm.o_ag + cols.start:dm.o_ag + cols.stop] = (
                        d_ag.astype(BF16))
                    dbin_ref[0:1, dm.o_ag + cols.start:dm.o_ag + cols.stop] += jnp.sum(
                        d_ag, axis=0, keepdims=True)
            fold = [x + pltpu.roll(x, HEAD_DIM, 1) for x in (dka[0], dka[1], dva[0], dva[1])]
            dkv_ref[r0:r0 + 2 * WINDOW, 0:KV_W] += jnp.where(lo, fold[0], fold[1])
            dkv_ref[r0:r0 + 2 * WINDOW, KV_W:2 * KV_W] += jnp.where(lo, fold[2], fold[3])
        dsk_ref[0:1, :] += dsk
        carry_kv_ref[...] = dkv_ref[0:WINDOW, :]
        emit(dm.o_k, 2 * KV_W, dkv_ref[WINDOW:WINDOW + t, :])

        @pl.when(g == nt - 1)
        def _():
            for j in range(CONV_WIDTH):
                dcw_ref[j:j + 1, :] = jnp.sum(dcw_acc_ref[j], axis=0, keepdims=True)
            dcw_ref[CONV_WIDTH:CONV_ROWS, :] = jnp.zeros((CONV_ROWS - CONV_WIDTH, c), F32)

    rev = lambda g: nt - 1 - g
    vec = pl.BlockSpec((1, c), lambda g: (0, 0))
    const = lambda shape: pl.BlockSpec(shape, lambda g: (0, 0))
    scratch = [pltpu.VMEM((t + HALO, c), F32), pltpu.VMEM((t + WINDOW, 2 * KV_W), F32),
               pltpu.VMEM((t + HALO, c), F32), pltpu.VMEM((t + WINDOW, 2 * KV_W), F32),
               pltpu.VMEM((HALO, c), F32), pltpu.VMEM((WINDOW, 2 * KV_W), F32),
               pltpu.VMEM((SUBLANES - 1, t + HALO, LANES), F32),
               pltpu.VMEM((SUBLANES - 1, t + HALO, LANES), F32),
               pltpu.VMEM((t, c), F32), pltpu.VMEM((CONV_ROWS, SUBLANES, c), F32),
               pltpu.VMEM((rs, LANES), BF16), pltpu.VMEM((rs // 2, LANES), F32)]
    host = _Hosting(list(copies), 10, 5)
    res = pl.pallas_call(
        _ride(body, host, 10, 5, len(scratch), nt), name="mixer_bwd",
        out_shape=[jax.ShapeDtypeStruct((dm.s, dm.din), BF16),
                   jax.ShapeDtypeStruct((CONV_ROWS, c), F32),
                   jax.ShapeDtypeStruct((8, c), F32),
                   jax.ShapeDtypeStruct((8, dm.din), F32),
                   jax.ShapeDtypeStruct((8, LANES), F32)]
        + host.out_shapes,
        grid=(nt,),
        in_specs=[pl.BlockSpec(memory_space=pltpu.SMEM)] + _mixer_specs(dm, t, rev)
        + [pl.BlockSpec((t, dm.d), lambda g: (rev(g), 0)), pl.BlockSpec((t, c), lambda g: (rev(g), 0)),
           pl.BlockSpec((CONV_ROWS, c), lambda g: (0, 0)), vec, vec, vec] + [_ANY] * len(host.args),
        out_specs=[pl.BlockSpec((t, dm.din), lambda g: (rev(g), 0)),
                   const((CONV_ROWS, c)), const((8, c)), const((8, dm.din)), const((8, LANES))]
        + [_ANY] * len(host.out_shapes),
        scratch_shapes=scratch + host.scratch, input_output_aliases=host.aliases,
        compiler_params=_cparams(("arbitrary",)),
    )(sinks, proj, proj, proj, dymix, conv, conv_w, conv_b, cln_g, cln_b, *host.args)
    return res[:5], res[5:]


def _outproj_ln(dm, ymix, w_out, b_out, x, ln_g, ln_b, target, tm, copies=()):
    last = target is not None
    d = dm.d
    n_in = 7 if last else 6
    host = _Hosting(list(copies), n_in, 3)
    n_hin, n_hout = len(host.args), len(host.out_shapes)
    steps = dm.s // tm

    def body(*refs):
        y_ref, w_ref, bo_ref, x_ref, g_ref, b_ref = refs[:6]
        h_in = refs[n_in:n_in + n_hin]
        h_out = refs[n_in + n_hin + 3:n_in + n_hin + 3 + n_hout]
        h_sems = refs[n_in + n_hin + 3 + n_hout:]
        if copies:
            @pl.when(pl.program_id(0) == 0)
            def _():
                host.start(h_in, h_out, h_sems)

        z = dm.alpha * x_ref[...] + (
            jnp.dot(y_ref[...], w_ref[...], preferred_element_type=F32) + bo_ref[...])
        mu = jnp.mean(z, axis=1, keepdims=True)
        dlt = z - mu
        var = jnp.mean(dlt * dlt, axis=1, keepdims=True)
        out = dlt * lax.rsqrt(var + LN_EPS) * g_ref[...] + b_ref[...]
        if last:
            t_ref = refs[6]
            z_ref, dout_ref, loss_ref = refs[n_in + n_hin:n_in + n_hin + 3]
            z_ref[...] = z
            err = out - t_ref[...]
            dout_ref[...] = err * (1.0 / d)

            @pl.when(pl.program_id(0) == 0)
            def _():
                loss_ref[...] = jnp.zeros_like(loss_ref)

            loss_ref[...] += 0.5 * jnp.sum(jnp.mean(err * err, axis=1, keepdims=True), axis=0,
                                           keepdims=True)
        else:
            z_ref, o_ref, ob_ref = refs[n_in + n_hin:n_in + n_hin + 3]
            z_ref[...] = z
            o_ref[...] = out
            ob_ref[...] = out.astype(BF16)

        if copies:
            @pl.when(pl.program_id(0) == steps - 1)
            def _():
                host.finish(h_in, h_out, h_sems)

    row = pl.BlockSpec((tm, d), lambda i: (i, 0))
    vec = pl.BlockSpec((1, d), lambda i: (0, 0))
    in_specs = [row, pl.BlockSpec((d, d), lambda i: (0, 0)), vec, row, vec, vec]
    args = [ymix, w_out, b_out, x, ln_g, ln_b]
    act = jax.ShapeDtypeStruct((dm.s, d), F32)
    if last:
        in_specs.append(row)
        args.append(target)
        out_shape = [act, act, jax.ShapeDtypeStruct((8, LANES), F32)]
        out_specs = [row, row, pl.BlockSpec((8, LANES), lambda i: (0, 0))]
    else:
        out_shape = [act, act, jax.ShapeDtypeStruct((dm.s, d), BF16)]
        out_specs = [row, row, row]
    res = pl.pallas_call(
        body, name="outproj_ln_loss" if last else "outproj_ln",
        out_shape=out_shape + host.out_shapes, grid=(steps,),
        in_specs=in_specs + [_ANY] * n_hin, out_specs=out_specs + [_ANY] * n_hout,
        scratch_shapes=host.scratch, input_output_aliases=host.aliases,
        compiler_params=_cparams(("arbitrary",)),
    )(*args, *host.args)
    return res[:3], res[3:]


def _ln_bwd(dm, dout, z, ln_g, tm):
    d = dm.d

    def body(do_ref, z_ref, g_ref, dz_ref, dzb_ref, sm_ref):
        @pl.when(pl.program_id(0) == 0)
        def _():
            sm_ref[...] = jnp.zeros_like(sm_ref)

        z = z_ref[...]
        mu = jnp.mean(z, axis=1, keepdims=True)
        dlt = z - mu
        var = jnp.mean(dlt * dlt, axis=1, keepdims=True)
        rstd = lax.rsqrt(var + LN_EPS)
        zhat = dlt * rstd
        do = do_ref[...]
        dzh = do * g_ref[...]
        dz = rstd * (dzh - jnp.mean(dzh, axis=1, keepdims=True)
                     - zhat * jnp.mean(dzh * zhat, axis=1, keepdims=True))
        dz_ref[...] = dz
        dzb_ref[...] = dz.astype(BF16)
        sm_ref[0:1, :] += jnp.sum(do * zhat, axis=0, keepdims=True)
        sm_ref[1:2, :] += jnp.sum(do, axis=0, keepdims=True)
        sm_ref[2:3, :] += jnp.sum(dz, axis=0, keepdims=True)

    row = pl.BlockSpec((tm, d), lambda i: (i, 0))
    return pl.pallas_call(
        body, name="ln_bwd",
        out_shape=(jax.ShapeDtypeStruct((dm.s, d), F32), jax.ShapeDtypeStruct((dm.s, d), BF16),
                   jax.ShapeDtypeStruct((8, d), F32)),
        grid=(dm.s // tm,),
        in_specs=[row, row, pl.BlockSpec((1, d), lambda i: (0, 0))],
        out_specs=(row, row, pl.BlockSpec((8, d), lambda i: (0, 0))),
        compiler_params=_cparams(("arbitrary",)),
    )(dout, z, ln_g)


def _pair_sum(part, got, tr):
    _, _, r, w = part.shape

    def body(c_ref, p_ref, g_ref, o_ref):
        o_ref[...] = (p_ref[...] + g_ref[...]).astype(BF16)

    return pl.pallas_call(
        body, name="grad_pair_sum",
        out_shape=jax.ShapeDtypeStruct((N_CHIPS, r, w), BF16),
        grid_spec=pltpu.PrefetchScalarGridSpec(
            num_scalar_prefetch=1, grid=(N_CHIPS, r // tr),
            in_specs=[pl.BlockSpec((None, None, tr, w), lambda k, i, c_ref: (k, c_ref[0], i, 0)),
                      pl.BlockSpec((None, tr, w), lambda k, i, c_ref: (k, i, 0))],
            out_specs=pl.BlockSpec((None, tr, w), lambda k, i, c_ref: (k, i, 0))),
        compiler_params=_cparams(("parallel", "parallel")),
    )(lax.axis_index("c").reshape(1).astype(jnp.int32), part, got)


def _final_sum(part, got, recv, tr):
    _, _, r, w = part.shape

    def body(idx_ref, p_ref, g_ref, r0_ref, r1_ref, r2_ref, o_ref):
        acc = p_ref[...] + g_ref[...]
        for ref in (r0_ref, r1_ref, r2_ref):
            acc = acc + ref[...].astype(F32)
        o_ref[...] = acc

    x, y, c = _position()
    idx = jnp.stack([c, 2 * x + y, 2 * (1 - x) + y, 2 * x + (1 - y),
                     2 * (1 - x) + (1 - y)]).astype(jnp.int32)
    other = lambda j: pl.BlockSpec((None, tr, w), lambda i, s: (s[2 + j], i, 0))
    return pl.pallas_call(
        body, name="grad_final_sum",
        out_shape=jax.ShapeDtypeStruct((r, w), F32),
        grid_spec=pltpu.PrefetchScalarGridSpec(
            num_scalar_prefetch=1, grid=(r // tr,),
            in_specs=[pl.BlockSpec((None, None, tr, w), lambda i, s: (s[1], s[0], i, 0)),
                      pl.BlockSpec((None, tr, w), lambda i, s: (s[1], i, 0)),
                      other(0), other(1), other(2)],
            out_specs=pl.BlockSpec((tr, w), lambda i, s: (i, 0))),
        compiler_params=_cparams(("parallel",)),
    )(idx, part, got, recv, recv, recv)


def _adamw_math(w, g, m, v):
    m = ADAM_B1 * m + (1.0 - ADAM_B1) * g
    v = ADAM_B2 * v + (1.0 - ADAM_B2) * (g * g)
    m_hat = m / (1.0 - ADAM_B1 ** ADAM_STEP)
    v_hat = v / (1.0 - ADAM_B2 ** ADAM_STEP)
    delta = -ADAM_LR * (m_hat / (jnp.sqrt(v_hat) + ADAM_EPS) + ADAM_WD * w)
    return delta, m, v


def _adamw(w, g, m, v, tr):
    depth, r, width = w.shape

    def body(w_ref, g_ref, m_ref, v_ref, d_ref, nm_ref, nv_ref):
        d_ref[...], nm_ref[...], nv_ref[...] = _adamw_math(
            w_ref[...], g_ref[...], m_ref[...], v_ref[...])

    spec = pl.BlockSpec((None, tr, width), lambda l, i: (l, i, 0))
    shape = jax.ShapeDtypeStruct(w.shape, F32)
    return pl.pallas_call(
        body, name="adamw", out_shape=(shape, shape, shape), grid=(depth, r // tr),
        in_specs=[spec] * 4, out_specs=(spec, spec, spec),
        compiler_params=_cparams(("parallel", "parallel")),
    )(w, g, m, v)


def _small_allreduce_adamw(partial, w, m, v):
    rows = partial.shape[0]
    prows = w.shape[0]

    def body(p_ref, w_ref, m_ref, v_ref, tot_ref, d_ref, nm_ref, nv_ref, gath_ref,
             send_sems, recv_sems):
        x, y, c = _position()
        me = 4 * x + 2 * y + c
        copies = []
        for k in range(1, N_DEV):
            px = 1 - x if k & 4 else x
            py = 1 - y if k & 2 else y
            pc = 1 - c if k & 1 else c
            copies.append((pltpu.make_async_remote_copy(
                src_ref=p_ref, dst_ref=gath_ref.at[me],
                send_sem=send_sems.at[k - 1], recv_sem=recv_sems.at[k - 1],
                device_id=(px, py, pc), device_id_type=MESH), 4 * px + 2 * py + pc))
        for cp, _ in copies:
            cp.start()
        gath_ref[me] = p_ref[...]
        for k, (cp, peer) in enumerate(copies):
            pltpu.make_async_remote_copy(
                src_ref=p_ref, dst_ref=gath_ref.at[peer],
                send_sem=send_sems.at[k], recv_sem=recv_sems.at[k],
                device_id=(x, y, c), device_id_type=MESH).wait_recv()
        for cp, _ in copies:
            cp.wait_send()
        tot = gath_ref[0]
        for d in range(1, N_DEV):
            tot = tot + gath_ref[d]
        tot_ref[...] = tot
        d_ref[...], nm_ref[...], nv_ref[...] = _adamw_math(
            w_ref[...], tot[0:prows, :], m_ref[...], v_ref[...])

    vm = pl.BlockSpec(memory_space=pltpu.VMEM)
    pshape = jax.ShapeDtypeStruct(w.shape, F32)
    return pl.pallas_call(
        body, name="small_allreduce_adamw",
        out_shape=(jax.ShapeDtypeStruct(partial.shape, F32), pshape, pshape, pshape),
        in_specs=[vm] * 4, out_specs=(vm, vm, vm, vm),
        scratch_shapes=[pltpu.VMEM((N_DEV, rows, LANES), F32),
                        pltpu.SemaphoreType.DMA((N_DEV - 1,)), pltpu.SemaphoreType.DMA((N_DEV - 1,))],
        compiler_params=pltpu.CompilerParams(vmem_limit_bytes=VMEM_LIMIT),
    )(partial, w, m, v)


def _pack_rows(vec):
    depth, n = vec.shape
    rows = -(-n // LANES)
    rows = -(-rows // 8) * 8
    return jnp.pad(vec, ((0, 0), (0, rows * LANES - n))).reshape(depth, rows, LANES)


def _pack_small(named):
    blocks = [_pack_rows(a) for a in named]
    extents = [(b.shape[1], a.shape[1]) for b, a in zip(blocks, named)]
    depth = named[0].shape[0]
    packed = jnp.concatenate(blocks, axis=1).reshape(depth * sum(r for r, _ in extents), LANES)
    return packed, extents


def _unpack_small(packed, extents, depth):
    per_layer = sum(r for r, _ in extents)
    packed = packed.reshape(depth, per_layer, LANES)
    out, r0 = [], 0
    for rows, n in extents:
        out.append(packed[:, r0:r0 + rows, :].reshape(depth, rows * LANES)[:, :n])
        r0 += rows
    return out


def kernel(x, w_in, b_in, conv_w, conv_b, conv_ln_g, conv_ln_b, sinks, w_out, b_out, ln_g, ln_b, loss_target, m_w_in, m_b_in, m_conv_w, m_conv_b, m_conv_ln_g, m_conv_ln_b, m_sinks, m_w_out, m_b_out, m_ln_g, m_ln_b, v_w_in, v_b_in, v_conv_w, v_conv_b, v_conv_ln_g, v_conv_ln_b, v_sinks, v_w_out, v_b_out, v_ln_g, v_ln_b):
    depth, d, din_shard = w_in.shape
    s = x.shape[1]
    c_shard = conv_w.shape[2]
    dm = _Dims(s, d, N_DEV * c_shard, depth)
    assert dm.din == N_DEV * din_shard and x.shape[0] == 1 and sinks.shape[1] == dm.nh
    d_shard = w_out.shape[1]
    c, din = dm.c, dm.din

    t_mix = _tile(s, 256, WINDOW)
    tm_row = _tile(s, 256, 8)
    tm_big = _tile(s, 1024, 8)

    w_in_t, m_w_in_t, v_w_in_t = (a.transpose(0, 2, 1) for a in (w_in, m_w_in, v_w_in))
    w_in_b, w_out_b = w_in_t.astype(BF16), w_out.astype(BF16)
    conv_w_pad = jnp.pad(conv_w, ((0, 0), (0, CONV_ROWS - CONV_WIDTH), (0, 0)))
    first = _run_copies([_gather_own([w_in_b[0], conv_w_pad])], "weights_gather_own")
    g_in, g_conv = _run_copies([_gather_forward(first)], "weights_gather_forward")
    g_out = None
    conv_w_full = g_conv.transpose(1, 2, 0, 3).reshape(depth, CONV_ROWS, c)
    by_cols = lambda g: g.reshape(din, d)
    by_rows = lambda g: g.reshape(d, d)

    xs = x[0]
    xb = xs.astype(BF16)
    saved = []
    loss_part = dout = None
    for l in range(depth):
        w_in_l = by_cols(g_in)
        riders = [_gather_own([w_out_b[0]])] if l == 0 else [_gather_forward([g_out])]
        proj, (g_out,) = _matmul(xb, w_in_l, tb=True, tm=tm_big, tn=_tile(din, 768), tk=d,
                                 out_dtype=F32, name="in_proj", bias=b_in[l][None, :],
                                 copies=riders)
        riders = [_gather_forward([g_out])] if l == 0 else []
        if l + 1 < depth:
            riders.append(_gather_own([w_in_b[l + 1]]))
        (ymix, conv), landed = _mixer_fwd(dm, proj, conv_w_full[l], conv_b[l][None, :],
                                          conv_ln_g[l][None, :], conv_ln_b[l][None, :], sinks[l],
                                          t_mix, copies=riders)
        if l == 0:
            g_out = landed[0]
        g_in_next = landed[-1] if l + 1 < depth else None
        w_out_l = by_rows(g_out)
        target = loss_target[0] if l == depth - 1 else None
        riders = []
        if l + 1 < depth:
            riders = [_gather_forward([g_in_next]), _gather_own([w_out_b[l + 1]])]
        res, landed = _outproj_ln(dm, ymix, w_out_l, b_out[l][None, :], xs, ln_g[l][None, :],
                                  ln_b[l][None, :], target, tm_row, copies=riders)
        saved.append((xb, proj, ymix, conv, res[0], w_in_l, w_out_l))
        if l + 1 < depth:
            g_in, g_out = landed
            xs, xb = res[1], res[2]
        else:
            dout, loss_part = res[1], res[2]

    g_w_in, g_w_out = [None] * depth, [None] * depth
    small_parts = [None] * depth
    dconv_w = [None] * depth
    tr_in, tr_out = _tile(din_shard, 512, 8), _tile(d_shard, 256, 8)
    parts_in = None
    for l in reversed(range(depth)):
        xb, proj, ymix, conv, z, w_in_l, w_out_l = saved[l]
        dz, dzb, ln_small = _ln_bwd(dm, dout, z, ln_g[l][None, :], tm_row)
        mm = dict(tb=True, tm=tm_big, tn=_tile(d, 1024), tk=d, out_dtype=F32, name="dymix")
        if parts_in is not None:
            dymix, (got_in,) = _matmul(dzb, w_out_l, copies=[_scatter_sibling([parts_in])], **mm)
            sums_in = _pair_sum(parts_in, got_in, tr_in)
        else:
            dymix = _matmul(dzb, w_out_l, **mm)
        dw_out = _matmul(ymix, dzb, ta=True, tm=_tile(d, 1024), tn=_tile(d, 1024), tk=s,
                         out_dtype=F32, name="dw_out")
        riders = [_scatter_chips([sums_in])] if parts_in is not None else []
        (dproj, dcw, conv_small, dbin, dsk), landed = _mixer_bwd(
            dm, proj, dymix, conv, conv_w_full[l], conv_b[l][None, :], conv_ln_g[l][None, :],
            conv_ln_b[l][None, :], sinks[l], t_mix, copies=riders)
        if parts_in is not None:
            g_w_in[l + 1] = _final_sum(parts_in, got_in, landed[0], tr_in)
        parts_out = dw_out.reshape(N_CHIPS, 2, d_shard, d)
        dx = dict(tm=_tile(s, 512), tn=_tile(d, 1024), tk=din, out_dtype=F32, name="dx", resid=dz,
                  resid_scale=dm.alpha)
        dw = dict(ta=True, tm=_tile(din, 768), tn=_tile(d, 1024), tk=s, out_dtype=F32, name="dw_in")
        if l > 0:
            dout, (got_out,) = _matmul(dproj, w_in_l, copies=[_scatter_sibling([parts_out])], **dx)
            sums_out = _pair_sum(parts_out, got_out, tr_out)
            dw_in_t, (recv_out,) = _matmul(dproj, xb, copies=[_scatter_chips([sums_out])], **dw)
            parts_in = dw_in_t.reshape(N_CHIPS, 2, din_shard, d)
        else:
            dw_in_t, (got_out,) = _matmul(dproj, xb, copies=[_scatter_sibling([parts_out])], **dw)
            parts_in = dw_in_t.reshape(N_CHIPS, 2, din_shard, d)
            got_in, = _run_copies([_scatter_sibling([parts_in])], "grad_sibling_exchange")
            sums_out = _pair_sum(parts_out, got_out, tr_out)
            sums_in = _pair_sum(parts_in, got_in, tr_in)
            dout, (recv_in, recv_out) = _matmul(
                dproj, w_in_l, copies=[_scatter_chips([sums_in, sums_out])], **dx)
            g_w_in[0] = _final_sum(parts_in, got_in, recv_in, tr_in)
        g_w_out[l] = _final_sum(parts_out, got_out, recv_out, tr_out)
        small_parts[l] = [dbin[0], conv_small[0], conv_small[1], conv_small[2], dsk[0, :dm.nh],
                          ln_small[2], ln_small[0], ln_small[1]]
        dconv_w[l] = dcw
    grad_x = dout[None]

    small_w = [b_in, conv_b, conv_ln_g, conv_ln_b, sinks, b_out, ln_g, ln_b]
    small_m = [m_b_in, m_conv_b, m_conv_ln_g, m_conv_ln_b, m_sinks, m_b_out, m_ln_g, m_ln_b]
    small_v = [v_b_in, v_conv_b, v_conv_ln_g, v_conv_ln_b, v_sinks, v_b_out, v_ln_g, v_ln_b]
    n_small = len(small_w)
    packed_g, extents = _pack_small([jnp.stack([small_parts[l][k] for l in range(depth)])
                                     for k in range(n_small)])
    packed_w, _ = _pack_small(small_w)
    packed_m, _ = _pack_small(small_m)
    packed_v, _ = _pack_small(small_v)
    prows = packed_g.shape[0]
    conv_rows = depth * CONV_ROWS * c // LANES
    partial = jnp.concatenate(
        [packed_g, jnp.stack(dconv_w).reshape(conv_rows, LANES), loss_part], axis=0)
    total, sm_delta, sm_m, sm_v = _small_allreduce_adamw(partial, packed_w, packed_m, packed_v)
    loss = total[prows + conv_rows, 0]
    dconv_w_full = total[prows:prows + conv_rows].reshape(depth, CONV_ROWS, c)
    me = 4 * lax.axis_index("x") + 2 * lax.axis_index("y") + lax.axis_index("c")
    grad_conv_w = lax.dynamic_slice_in_dim(dconv_w_full, me * c_shard, c_shard, axis=2)[:, :CONV_WIDTH]

    grads_small = _unpack_small(total[:prows], extents, depth)
    delta_small = _unpack_small(sm_delta, extents, depth)
    newm_small = _unpack_small(sm_m, extents, depth)
    newv_small = _unpack_small(sm_v, extents, depth)

    grad_w_in_t = jnp.stack(g_w_in)
    grad_w_out = jnp.stack(g_w_out)
    grad_w_in, d_w_in, nm_w_in, nv_w_in = (a.transpose(0, 2, 1) for a in (
        grad_w_in_t, *_adamw(w_in_t, grad_w_in_t, m_w_in_t, v_w_in_t, tr_in)))
    d_w_out, nm_w_out, nv_w_out = _adamw(w_out, grad_w_out, m_w_out, v_w_out, _tile(d_shard, 256, 8))
    d_cw, nm_cw, nv_cw = _adamw(conv_w, grad_conv_w, m_conv_w, v_conv_w, CONV_WIDTH)

    def assemble(w_in_leaf, conv_w_leaf, w_out_leaf, small):
        b_in_, conv_b_, cg_, cb_, sinks_, b_out_, ln_g_, ln_b_ = small
        return [w_in_leaf, b_in_, conv_w_leaf, conv_b_, cg_, cb_, sinks_, w_out_leaf, b_out_,
                ln_g_, ln_b_]

    return (loss, grad_x,
            *assemble(grad_w_in, grad_conv_w, grad_w_out, grads_small),
            *assemble(d_w_in, d_cw, d_w_out, delta_small),
            *assemble(nm_w_in, nm_cw, nm_w_out, newm_small),
            *assemble(nv_w_in, nv_cw, nv_w_out, newv_small))
```

```python
import functools

import jax
import jax.numpy as jnp
from jax import lax
from jax.experimental import pallas as pl
from jax.experimental.pallas import tpu as pltpu

F32 = jnp.float32
BF16 = jnp.bfloat16
MESH = pl.DeviceIdType.MESH

N_DEV = 8
N_CHIPS = 4
HEAD_DIM = 64
N_KV = 2
KV_W = N_KV * HEAD_DIM
CONV_WIDTH = 31
CONV_ROWS = 32
HALO = 32
WINDOW = 128
LN_EPS = 1e-5
NEG_INF = -1e30
LANES = 128

ADAM_LR = 0.001
ADAM_B1 = 0.9
ADAM_B2 = 0.999
ADAM_EPS = 1e-08
ADAM_WD = 0.01
ADAM_STEP = 10

VMEM_LIMIT = 56 * 1024 * 1024


def _tile(n, target, align=LANES):
    best = None
    for t in range(align, min(n, target) + 1, align):
        if n % t == 0:
            best = t
    return n if best is None else best


def _sigmoid(x):
    return jax.nn.sigmoid(x)


def _dsilu(x, s):
    return s * (1.0 + x * (1.0 - s))


def _cparams(sem, vmem=VMEM_LIMIT):
    return pltpu.CompilerParams(dimension_semantics=sem, vmem_limit_bytes=vmem)


_ANY = pl.BlockSpec(memory_space=pl.ANY)


def _position():
    return lax.axis_index("x"), lax.axis_index("y"), lax.axis_index("c")


def _other_chips(x, y):
    return [(1 - x, y), (x, 1 - y), (1 - x, 1 - y)]


class _Copies:
    def __init__(self, operands, landing, alias, sems, start, finish):
        self.operands, self.landing, self.alias, self.sems = operands, landing, alias, sems
        self.start, self.finish = start, finish


class _Hosting:
    def __init__(self, groups, n_in, n_out):
        self.groups = groups
        self.args = [a for g in groups for a in g.operands]
        self.out_shapes = [s for g in groups for s in g.landing]
        self.scratch = [pltpu.SemaphoreType.DMA(g.sems) for g in groups for _ in range(2)]
        self.aliases = {}
        i0, o0 = n_in, n_out
        for g in groups:
            for a, b in g.alias.items():
                self.aliases[i0 + a] = o0 + b
            i0 += len(g.operands)
            o0 += len(g.landing)

    def _each(self, in_refs, out_refs, sem_refs):
        i0 = o0 = 0
        for n, g in enumerate(self.groups):
            yield (g, in_refs[i0:i0 + len(g.operands)], out_refs[o0:o0 + len(g.landing)],
                   sem_refs[2 * n], sem_refs[2 * n + 1])
            i0 += len(g.operands)
            o0 += len(g.landing)

    def start(self, in_refs, out_refs, sem_refs):
        for g, ins, outs, send, recv in self._each(in_refs, out_refs, sem_refs):
            g.start(ins, outs, send, recv)

    def finish(self, in_refs, out_refs, sem_refs):
        for g, ins, outs, send, recv in self._each(in_refs, out_refs, sem_refs):
            g.finish(ins, outs, send, recv)


def _ride(body, host, n_in, n_out, n_scratch, steps):
    n_hin, n_hout = len(host.args), len(host.out_shapes)

    def wrapped(*refs):
        pos = [0]

        def take(n):
            pos[0] += n
            return refs[pos[0] - n:pos[0]]

        ins, h_in, outs, h_out = take(n_in), take(n_hin), take(n_out), take(n_hout)
        scratch, h_sems = take(n_scratch), refs[pos[0]:]
        if host.groups:
            @pl.when(pl.program_id(0) == 0)
            def _():
                host.start(h_in, h_out, h_sems)

        body(*ins, *outs, *scratch)
        if host.groups:
            @pl.when(pl.program_id(0) == steps - 1)
            def _():
                host.finish(h_in, h_out, h_sems)

    return wrapped


def _run_copies(groups, name):
    host = _Hosting(groups, 0, 0)
    n_in, n_out = len(host.args), len(host.out_shapes)

    def body(*refs):
        ins, outs, sems = refs[:n_in], refs[n_in:n_in + n_out], refs[n_in + n_out:]
        host.start(ins, outs, sems)
        host.finish(ins, outs, sems)

    return pl.pallas_call(
        body, name=name, out_shape=host.out_shapes, in_specs=[_ANY] * n_in,
        out_specs=[_ANY] * n_out, scratch_shapes=host.scratch,
        input_output_aliases=host.aliases,
    )(*host.args)


def _gather_own(shards):
    n = len(shards)

    def copies(ins, outs, send, recv):
        x, y, c = _position()
        peers = [(x, y, 1 - c)] + [(px, py, c) for px, py in _other_chips(x, y)]
        out = []
        for t in range(n):
            for k, peer in enumerate(peers):
                out.append((pltpu.make_async_remote_copy(
                    src_ref=ins[t], dst_ref=outs[t].at[4 * x + 2 * y + c],
                    send_sem=send.at[t, k], recv_sem=recv.at[t, k],
                    device_id=peer, device_id_type=MESH), t, k, peer))
        local = [pltpu.make_async_copy(ins[t], outs[t].at[4 * x + 2 * y + c], send.at[t, 4])
                 for t in range(n)]
        return out, local

    def start(ins, outs, send, recv):
        remote, local = copies(ins, outs, send, recv)
        for cp in local:
            cp.start()
        for cp, _, _, _ in remote:
            cp.start()

    def finish(ins, outs, send, recv):
        remote, local = copies(ins, outs, send, recv)
        x, y, c = _position()
        for _, t, k, (px, py, pc) in remote:
            pltpu.make_async_remote_copy(
                src_ref=ins[t], dst_ref=outs[t].at[4 * px + 2 * py + pc],
                send_sem=send.at[t, k], recv_sem=recv.at[t, k],
                device_id=(x, y, c), device_id_type=MESH).wait_recv()
        for cp, _, _, _ in remote:
            cp.wait_send()
        for cp in local:
            cp.wait()

    landing = [jax.ShapeDtypeStruct((N_DEV,) + s.shape, s.dtype) for s in shards]
    return _Copies(list(shards), landing, {}, (n, 5), start, finish)


def _gather_forward(buffers):
    n = len(buffers)

    def copies(ins, outs, send, recv):
        x, y, c = _position()
        out = []
        for t in range(n):
            for j, (px, py) in enumerate(_other_chips(x, y)):
                slot = 4 * px + 2 * py + c
                out.append((pltpu.make_async_remote_copy(
                    src_ref=ins[t].at[slot], dst_ref=outs[t].at[slot],
                    send_sem=send.at[t, j], recv_sem=recv.at[t, j],
                    device_id=(x, y, 1 - c), device_id_type=MESH), t, j, 4 * px + 2 * py + 1 - c))
        return out

    def start(ins, outs, send, recv):
        for cp, _, _, _ in copies(ins, outs, send, recv):
            cp.start()

    def finish(ins, outs, send, recv):
        x, y, c = _position()
        mine = copies(ins, outs, send, recv)
        for _, t, j, got in mine:
            pltpu.make_async_remote_copy(
                src_ref=ins[t].at[got], dst_ref=outs[t].at[got],
                send_sem=send.at[t, j], recv_sem=recv.at[t, j],
                device_id=(x, y, c), device_id_type=MESH).wait_recv()
        for cp, _, _, _ in mine:
            cp.wait_send()

    landing = [jax.ShapeDtypeStruct(b.shape, b.dtype) for b in buffers]
    return _Copies(list(buffers), landing, {t: t for t in range(n)}, (n, 3), start, finish)


def _scatter_sibling(parts):
    n = len(parts)

    def copies(ins, outs, send, recv):
        x, y, c = _position()
        return [pltpu.make_async_remote_copy(
            src_ref=ins[t].at[:, 1 - c], dst_ref=outs[t],
            send_sem=send.at[t, 0], recv_sem=recv.at[t, 0],
            device_id=(x, y, 1 - c), device_id_type=MESH) for t in range(n)]

    def start(ins, outs, send, recv):
        for cp in copies(ins, outs, send, recv):
            cp.start()

    def finish(ins, outs, send, recv):
        for cp in copies(ins, outs, send, recv):
            cp.wait()

    landing = [jax.ShapeDtypeStruct((p.shape[0],) + p.shape[2:], p.dtype) for p in parts]
    return _Copies(list(parts), landing, {}, (n, 1), start, finish)


def _scatter_chips(blocks):
    n = len(blocks)

    def start(ins, outs, send, recv):
        x, y, c = _position()
        for t in range(n):
            for j, (px, py) in enumerate(_other_chips(x, y)):
                pltpu.make_async_remote_copy(
                    src_ref=ins[t].at[2 * px + py], dst_ref=outs[t].at[2 * x + y],
                    send_sem=send.at[t, j], recv_sem=recv.at[t, j],
                    device_id=(px, py, c), device_id_type=MESH).start()

    def finish(ins, outs, send, recv):
        x, y, c = _position()
        for t in range(n):
            for j, (px, py) in enumerate(_other_chips(x, y)):
                cp = pltpu.make_async_remote_copy(
                    src_ref=ins[t].at[2 * px + py], dst_ref=outs[t].at[2 * px + py],
                    send_sem=send.at[t, j], recv_sem=recv.at[t, j],
                    device_id=(px, py, c), device_id_type=MESH)
                cp.wait_recv()
                cp.wait_send()

    landing = [jax.ShapeDtypeStruct(b.shape, b.dtype) for b in blocks]
    return _Copies(list(blocks), landing, {}, (n, 3), start, finish)


def _matmul(a, b, *, ta=False, tb=False, tm, tn, tk, out_dtype, name, bias=None, resid=None,
            resid_scale=1.0, copies=()):
    m, k = (a.shape[1], a.shape[0]) if ta else a.shape
    n = b.shape[0] if tb else b.shape[1]
    assert (b.shape[1] if tb else b.shape[0]) == k
    assert m % tm == 0 and n % tn == 0 and k % tk == 0
    ni, nj, nk = m // tm, n // tn, k // tk
    dn = (((0 if ta else 1,), (1 if tb else 0,)), ((), ()))
    n_in = 2 + (bias is not None) + (resid is not None)
    host = _Hosting(list(copies), n_in, 1)
    n_hin, n_hout = len(host.args), len(host.out_shapes)

    def body(*refs):
        a_ref, b_ref = refs[0], refs[1]
        pos = 2
        bias_ref = resid_ref = None
        if bias is not None:
            bias_ref = refs[pos]
            pos += 1
        if resid is not None:
            resid_ref = refs[pos]
            pos += 1
        h_in = refs[pos:pos + n_hin]
        pos += n_hin
        o_ref = refs[pos]
        h_out = refs[pos + 1:pos + 1 + n_hout]
        pos += 1 + n_hout
        acc_ref = refs[pos] if nk > 1 else None
        h_sems = refs[pos + (nk > 1):]
        step = (pl.program_id(0) * nj + pl.program_id(1)) * nk + pl.program_id(2)

        if copies:
            @pl.when(step == 0)
            def _():
                host.start(h_in, h_out, h_sems)

        def finish(acc):
            if bias_ref is not None:
                acc = acc + bias_ref[...]
            if resid_ref is not None:
                acc = acc + resid_scale * resid_ref[...]
            o_ref[...] = acc.astype(out_dtype)

        p = lax.dot_general(a_ref[...], b_ref[...], dn, preferred_element_type=F32)
        if nk == 1:
            finish(p)
        else:
            kk = pl.program_id(2)

            @pl.when(kk == 0)
            def _():
                acc_ref[...] = p

            @pl.when(kk > 0)
            def _():
                acc_ref[...] += p

            @pl.when(kk == nk - 1)
            def _():
                finish(acc_ref[...])

        if copies:
            @pl.when(step == ni * nj * nk - 1)
            def _():
                host.finish(h_in, h_out, h_sems)

    a_spec = (pl.BlockSpec((tk, tm), lambda i, j, kk: (kk, i)) if ta
              else pl.BlockSpec((tm, tk), lambda i, j, kk: (i, kk)))
    b_spec = (pl.BlockSpec((tn, tk), lambda i, j, kk: (j, kk)) if tb
              else pl.BlockSpec((tk, tn), lambda i, j, kk: (kk, j)))
    in_specs = [a_spec, b_spec]
    args = [a, b]
    if bias is not None:
        in_specs.append(pl.BlockSpec((1, tn), lambda i, j, kk: (0, j)))
        args.append(bias)
    if resid is not None:
        in_specs.append(pl.BlockSpec((tm, tn), lambda i, j, kk: (i, j)))
        args.append(resid)
    res = pl.pallas_call(
        body, name=name,
        out_shape=[jax.ShapeDtypeStruct((m, n), out_dtype)] + host.out_shapes,
        grid=(ni, nj, nk),
        in_specs=in_specs + [_ANY] * n_hin,
        out_specs=[pl.BlockSpec((tm, tn), lambda i, j, kk: (i, j))] + [_ANY] * n_hout,
        scratch_shapes=([pltpu.VMEM((tm, tn), F32)] if nk > 1 else []) + host.scratch,
        input_output_aliases=host.aliases,
        compiler_params=_cparams(("arbitrary",) * 3 if copies else
                                 ("parallel", "parallel", "arbitrary")),
    )(*args, *host.args)
    return (res[0], res[1:]) if copies else res[0]


class _Dims:
    def __init__(self, s, d, c, depth):
        self.s, self.d, self.c, self.depth = s, d, c, depth
        self.a = d - c
        self.nh = self.a // HEAD_DIM
        self.group = self.nh // N_KV
        self.din = 3 * c + 2 * self.a + 2 * KV_W
        self.o_q = 3 * c
        self.o_k = 3 * c + self.a
        self.o_v = self.o_k + KV_W
        self.o_ag = self.o_k + 2 * KV_W
        self.alpha = (2 * depth) ** 0.25
        assert self.nh % 2 == 0 and self.group % 2 == 0 and self.o_k % (2 * KV_W) == 0
        assert c % LANES == 0 and self.a % LANES == 0


SUBLANES = 8
TAP_ROWS = 64
ROW_CHUNK = 64


def _shift_copies(src_ref, sh_ref, lanes, rows):
    for r in range(1, SUBLANES):
        sh_ref[r - 1, 0:rows, :] = src_ref[pl.ds(r, rows), lanes]


def _tap_rows(src_ref, sh_ref, lanes, off, start, rows):
    r = off % SUBLANES
    at = pl.multiple_of(start + (off - r), SUBLANES)
    if r == 0:
        return src_ref[pl.ds(at, rows), lanes]
    return sh_ref[r - 1, pl.ds(at, rows), :]


def _conv_chunk(w_ref, src_ref, sh_ref, out_ref, lanes, t, first_off, reverse, bias_ref=None):
    def block(it, carry):
        start = pl.multiple_of(it * TAP_ROWS, TAP_ROWS)
        acc = None
        for j in range(CONV_WIDTH):
            off = first_off - j if reverse else first_off + j
            term = w_ref[j:j + 1, lanes] * _tap_rows(src_ref, sh_ref, lanes, off, start, TAP_ROWS)
            acc = term if acc is None else acc + term
        if bias_ref is not None:
            acc = acc + bias_ref[:, lanes]
        out_ref[pl.ds(start, TAP_ROWS), lanes] = acc
        return carry

    lax.fori_loop(0, t // TAP_ROWS, block, 0)


def _conv_w_grad_chunk(src_ref, sh_ref, dconv_ref, acc_ref, lanes, t, first_off):
    def block(it, accs):
        start = pl.multiple_of(it * SUBLANES, SUBLANES)
        dv = dconv_ref[pl.ds(start, SUBLANES), lanes]
        return tuple(
            acc + _tap_rows(src_ref, sh_ref, lanes, first_off + j, start, SUBLANES) * dv
            for j, acc in enumerate(accs))

    zero = jnp.zeros((SUBLANES, LANES), F32)
    accs = lax.fori_loop(0, t // SUBLANES, block, (zero,) * CONV_WIDTH, unroll=2)
    for j in range(CONV_WIDTH):
        acc_ref[j, :, lanes] += accs[j]


def _kv_operands(kv_ref, lo):
    kext = kv_ref[:, 0:KV_W]
    vext = kv_ref[:, KV_W:2 * KV_W]
    ksw = pltpu.roll(kext, HEAD_DIM, 1)
    vsw = pltpu.roll(vext, HEAD_DIM, 1)
    zero = jnp.zeros_like(kext)
    k2 = [jnp.where(lo, kext, ksw).astype(BF16), jnp.where(lo, ksw, kext).astype(BF16)]
    khalf = [[jnp.where(lo, kext, zero).astype(BF16), jnp.where(lo, zero, ksw).astype(BF16)],
             [jnp.where(lo, ksw, zero).astype(BF16), jnp.where(lo, zero, kext).astype(BF16)]]
    vhalf = [[jnp.where(lo, vext, zero).astype(BF16), jnp.where(lo, zero, vsw).astype(BF16)],
             [jnp.where(lo, vsw, zero).astype(BF16), jnp.where(lo, zero, vext).astype(BF16)]]
    return k2, khalf, vhalf


SCALE = HEAD_DIM ** -0.5


def _from_previous(rows):
    row = lax.broadcasted_iota(jnp.int32, (rows, WINDOW), 0)
    col = lax.broadcasted_iota(jnp.int32, (rows, WINDOW), 1)
    return col > (row & (WINDOW - 1))


def _sink_column(sinks_ref, heads):
    block = lax.broadcasted_iota(jnp.int32, (len(heads) * WINDOW, 1), 0) // WINDOW
    out = jnp.zeros(block.shape, F32)
    for b, head in enumerate(heads):
        out = jnp.where(block == b, sinks_ref[head], out)
    return out


def _stacked_heads(dm, kvh):
    pairs = range(kvh * (dm.group // 2), (kvh + 1) * (dm.group // 2))
    return [(p, half) for half in range(2) for p in pairs]


def _band_merge(both, prev):
    return jnp.where(prev, both[:, 0:WINDOW], both[:, WINDOW:2 * WINDOW])


def _band_split(tile, prev):
    zero = jnp.zeros_like(tile)
    return jnp.concatenate([jnp.where(prev, tile, zero), jnp.where(prev, zero, tile)],
                           axis=1).astype(BF16)


def _softmax_with_sink(qm, k2rows, prev, no_previous, sink):
    both = lax.dot_general(qm, k2rows, (((1,), (1,)), ((), ())), preferred_element_type=F32)
    s_prev = both[:, 0:WINDOW]
    if no_previous is not None:
        s_prev = jnp.where(no_previous, NEG_INF, s_prev)
    s = jnp.where(prev, s_prev, both[:, WINDOW:2 * WINDOW])
    m = jnp.maximum(jnp.max(s, axis=1, keepdims=True), sink)
    e = jnp.exp(s - m)
    den = jnp.sum(e, axis=1, keepdims=True) + jnp.exp(sink - m)
    inv = 1.0 / den
    return e * inv, m, inv


def _mixer_specs(dm, t, idx):
    return [
        pl.BlockSpec((t, dm.din), lambda g: (idx(g), 0)),
        pl.BlockSpec((HALO, 2 * dm.c), lambda g: (jnp.maximum(idx(g) * (t // HALO) - 1, 0), 0)),
        pl.BlockSpec((WINDOW, 2 * KV_W),
                     lambda g: (jnp.maximum(idx(g) * (t // WINDOW) - 1, 0), dm.o_k // (2 * KV_W))),
    ]


def _mixer_fwd(dm, proj, conv_w, conv_b, cln_g, cln_b, sinks, t, copies=()):
    c, nq = dm.c, t // WINDOW
    rs = dm.group * WINDOW
    host = _Hosting(list(copies), 8, 2)

    def body(sinks_ref, pr_ref, ch_ref, kvh_ref, cw_ref, cb_ref, cg_ref, cbb_ref,
             y_ref, conv_ref, hext_ref, kv_ref, hs_ref, qs_ref):
        i = pl.program_id(0)
        first = i == 0
        h = pr_ref[:, 0:c] * _sigmoid(pr_ref[:, c:2 * c])
        hh = ch_ref[:, 0:c] * _sigmoid(ch_ref[:, c:2 * c])
        hext_ref[0:HALO, :] = jnp.where(first, 0.0, hh)
        hext_ref[HALO:HALO + t, :] = h
        for k in range(c // LANES):
            lanes = slice(LANES * k, LANES * (k + 1))
            _shift_copies(hext_ref, hs_ref, lanes, t + HALO - SUBLANES)
            _conv_chunk(cw_ref, hext_ref, hs_ref, conv_ref, lanes, t, HALO - (CONV_WIDTH - 1),
                        False, cb_ref)
        conv = conv_ref[...]
        mu = jnp.mean(conv, axis=1, keepdims=True)
        dlt = conv - mu
        var = jnp.mean(dlt * dlt, axis=1, keepdims=True)
        u = dlt * lax.rsqrt(var + LN_EPS) * cg_ref[...] + cbb_ref[...]
        gate = pr_ref[:, 2 * c:3 * c]
        y_ref[:, 0:c] = (u * _sigmoid(u) * (gate * _sigmoid(gate))).astype(BF16)

        kv_ref[0:WINDOW, :] = jnp.where(first, 0.0, kvh_ref[...])
        kv_ref[WINDOW:WINDOW + t, :] = pr_ref[:, dm.o_k:dm.o_k + 2 * KV_W]
        lo = lax.broadcasted_iota(jnp.int32, (1, LANES), 1) < HEAD_DIM
        k2, _, vhalf = _kv_operands(kv_ref, lo)
        stacks = [_stacked_heads(dm, kvh) for kvh in range(N_KV)]
        sink_cols = [_sink_column(sinks_ref, [2 * p + half for p, half in st]) for st in stacks]
        prev = _from_previous(rs)
        for qb in range(nq):
            r0 = qb * WINDOW
            rows = slice(r0, r0 + 2 * WINDOW)
            no_previous = first if qb == 0 else None
            for kvh in range(N_KV):
                for b, (p, half) in enumerate(stacks[kvh]):
                    qp = pr_ref[r0:r0 + WINDOW, dm.o_q + LANES * p:dm.o_q + LANES * (p + 1)] * SCALE
                    qs_ref[b * WINDOW:(b + 1) * WINDOW, :] = jnp.where(
                        lo if half == 0 else jnp.logical_not(lo), qp, 0.0).astype(BF16)
                prob, _, _ = _softmax_with_sink(qs_ref[...], k2[kvh][rows], prev, no_previous,
                                                sink_cols[kvh])
                pb = _band_split(prob, prev)
                o_lo = jnp.dot(pb[0:rs // 2], vhalf[kvh][0][rows], preferred_element_type=F32)
                o_hi = jnp.dot(pb[rs // 2:rs], vhalf[kvh][1][rows], preferred_element_type=F32)
                for b, (p, _) in enumerate(stacks[kvh][:len(stacks[kvh]) // 2]):
                    blk = slice(b * WINDOW, (b + 1) * WINDOW)
                    ag = pr_ref[r0:r0 + WINDOW, dm.o_ag + LANES * p:dm.o_ag + LANES * (p + 1)]
                    y_ref[r0:r0 + WINDOW, c + LANES * p:c + LANES * (p + 1)] = (
                        (o_lo[blk] + o_hi[blk]) * (ag * _sigmoid(ag))).astype(BF16)

    vec = pl.BlockSpec((1, c), lambda g: (0, 0))
    scratch = [pltpu.VMEM((t + HALO, c), F32), pltpu.VMEM((t + WINDOW, 2 * KV_W), F32),
               pltpu.VMEM((SUBLANES - 1, t + HALO, LANES), F32), pltpu.VMEM((rs, LANES), BF16)]
    res = pl.pallas_call(
        _ride(body, host, 8, 2, len(scratch), dm.s // t), name="mixer_fwd",
        out_shape=[jax.ShapeDtypeStruct((dm.s, dm.d), BF16),
                   jax.ShapeDtypeStruct((dm.s, c), F32)]
        + host.out_shapes,
        grid=(dm.s // t,),
        in_specs=[pl.BlockSpec(memory_space=pltpu.SMEM)] + _mixer_specs(dm, t, lambda g: g)
        + [pl.BlockSpec((CONV_ROWS, c), lambda g: (0, 0)), vec, vec, vec] + [_ANY] * len(host.args),
        out_specs=[pl.BlockSpec((t, dm.d), lambda g: (g, 0)), pl.BlockSpec((t, c), lambda g: (g, 0))]
        + [_ANY] * len(host.out_shapes),
        scratch_shapes=scratch + host.scratch, input_output_aliases=host.aliases,
        compiler_params=_cparams(("arbitrary",)),
    )(sinks, proj, proj, proj, conv_w, conv_b, cln_g, cln_b, *host.args)
    return res[:2], res[2:]


def _mixer_bwd(dm, proj, dymix, conv, conv_w, conv_b, cln_g, cln_b, sinks, t, copies=()):
    c, nq, nt = dm.c, t // WINDOW, dm.s // t
    rs = dm.group * WINDOW

    def body(sinks_ref, pr_ref, ch_ref, kvh_ref, dy_ref, cv_ref, cw_ref, cb_ref, cg_ref, cbb_ref,
             dpr_ref, dcw_ref, dsm_ref, dbin_ref, dsk_ref,
             hext_ref, kv_ref, dcx_ref, dkv_ref, carry_dc_ref, carry_kv_ref,
             hs_ref, ds_ref, dh_ref, dcw_acc_ref, qs_ref, do_ref, sg_ref):
        g = pl.program_id(0)
        i = nt - 1 - g
        first = i == 0

        @pl.when(g == 0)
        def _():
            dcw_acc_ref[...] = jnp.zeros_like(dcw_acc_ref)
            dsm_ref[...] = jnp.zeros_like(dsm_ref)
            dbin_ref[...] = jnp.zeros_like(dbin_ref)
            dsk_ref[...] = jnp.zeros_like(dsk_ref)
            carry_dc_ref[...] = jnp.zeros_like(carry_dc_ref)
            carry_kv_ref[...] = jnp.zeros_like(carry_kv_ref)

        def emit(col, width, val, rows=slice(None)):
            dpr_ref[rows, col:col + width] = val.astype(BF16)
            dbin_ref[0:1, col:col + width] += jnp.sum(val, axis=0, keepdims=True)

        hh = ch_ref[:, 0:c] * _sigmoid(ch_ref[:, c:2 * c])
        hext_ref[0:HALO, :] = jnp.where(first, 0.0, hh)
        dcx_ref[t:t + HALO, :] = carry_dc_ref[...]
        for r0 in range(0, t, ROW_CHUNK):
            rows = slice(r0, r0 + ROW_CHUNK)
            sg = _sigmoid(pr_ref[rows, c:2 * c])
            sg_ref[rows, :] = sg
            hext_ref[HALO + r0:HALO + r0 + ROW_CHUNK, :] = pr_ref[rows, 0:c] * sg
            conv = cv_ref[rows, :]
            mu = jnp.mean(conv, axis=1, keepdims=True)
            dlt = conv - mu
            var = jnp.mean(dlt * dlt, axis=1, keepdims=True)
            rstd = lax.rsqrt(var + LN_EPS)
            xhat = dlt * rstd
            u = xhat * cg_ref[...] + cbb_ref[...]
            su = _sigmoid(u)
            gate = pr_ref[rows, 2 * c:3 * c]
            sgate = _sigmoid(gate)
            dyc = dy_ref[rows, 0:c]
            emit(2 * c, c, dyc * (u * su) * _dsilu(gate, sgate), rows)
            du = dyc * (gate * sgate) * _dsilu(u, su)
            dsm_ref[1:2, :] += jnp.sum(du * xhat, axis=0, keepdims=True)
            dsm_ref[2:3, :] += jnp.sum(du, axis=0, keepdims=True)
            dxh = du * cg_ref[...]
            dconv = rstd * (dxh - jnp.mean(dxh, axis=1, keepdims=True)
                            - xhat * jnp.mean(dxh * xhat, axis=1, keepdims=True))
            dsm_ref[0:1, :] += jnp.sum(dconv, axis=0, keepdims=True)
            dcx_ref[rows, :] = dconv
        carry_dc_ref[...] = dcx_ref[0:HALO, :]
        for k in range(c // LANES):
            lanes = slice(LANES * k, LANES * (k + 1))
            _shift_copies(hext_ref, hs_ref, lanes, t + HALO - SUBLANES)
            _shift_copies(dcx_ref, ds_ref, lanes, t + HALO - SUBLANES)
            _conv_chunk(cw_ref, dcx_ref, ds_ref, dh_ref, lanes, t, CONV_WIDTH - 1, True)
            _conv_w_grad_chunk(hext_ref, hs_ref, dcx_ref, dcw_acc_ref, lanes, t,
                               HALO - (CONV_WIDTH - 1))
        for r0 in range(0, t, ROW_CHUNK):
            rows = slice(r0, r0 + ROW_CHUNK)
            dh, sg = dh_ref[rows, :], sg_ref[rows, :]
            emit(0, c, dh * sg, rows)
            emit(c, c, dh * pr_ref[rows, 0:c] * sg * (1.0 - sg), rows)

        kv_ref[0:WINDOW, :] = jnp.where(first, 0.0, kvh_ref[...])
        kv_ref[WINDOW:WINDOW + t, :] = pr_ref[:, dm.o_k:dm.o_k + 2 * KV_W]
        dkv_ref[0:t, :] = jnp.zeros((t, 2 * KV_W), F32)
        dkv_ref[t:t + WINDOW, :] = carry_kv_ref[...]
        lane = lax.broadcasted_iota(jnp.int32, (1, LANES), 1)
        lo = lane < HEAD_DIM
        k2, khalf, vhalf = _kv_operands(kv_ref, lo)
        tn_dims = (((0,), (0,)), ((), ()))
        nt_dims = (((1,), (1,)), ((), ()))
        dsk = jnp.zeros((1, LANES), F32)
        stacks = [_stacked_heads(dm, kvh) for kvh in range(N_KV)]
        sink_cols = [_sink_column(sinks_ref, [2 * p + half for p, half in st]) for st in stacks]
        prev = _from_previous(rs)
        hs2 = rs // 2
        for qb in range(nq):
            r0 = qb * WINDOW
            rows = slice(r0, r0 + 2 * WINDOW)
            no_previous = first if qb == 0 else None
            dka = [None, None]
            dva = [None, None]
            for kvh in range(N_KV):
                pairs = [p for p, half in stacks[kvh] if half == 0]
                for b, (p, half) in enumerate(stacks[kvh]):
                    qp = pr_ref[r0:r0 + WINDOW, dm.o_q + LANES * p:dm.o_q + LANES * (p + 1)] * SCALE
                    qs_ref[b * WINDOW:(b + 1) * WINDOW, :] = jnp.where(
                        lo if half == 0 else jnp.logical_not(lo), qp, 0.0).astype(BF16)
                for b, p in enumerate(pairs):
                    ag = pr_ref[r0:r0 + WINDOW, dm.o_ag + LANES * p:dm.o_ag + LANES * (p + 1)]
                    dya = dy_ref[r0:r0 + WINDOW, c + LANES * p:c + LANES * (p + 1)]
                    do_ref[b * WINDOW:(b + 1) * WINDOW, :] = dya * (ag * _sigmoid(ag))
                qs = qs_ref[...]
                d_o = do_ref[...]
                d_o_b = d_o.astype(BF16)
                v_lo, v_hi = vhalf[kvh][0][rows], vhalf[kvh][1][rows]
                prob, m, inv = _softmax_with_sink(qs, k2[kvh][rows], prev, no_previous,
                                                  sink_cols[kvh])
                pb = _band_split(prob, prev)
                o_lo = jnp.dot(pb[0:hs2], v_lo, preferred_element_type=F32)
                o_hi = jnp.dot(pb[hs2:rs], v_hi, preferred_element_type=F32)
                delta = jnp.concatenate([jnp.sum(d_o * o_lo, axis=1, keepdims=True),
                                         jnp.sum(d_o * o_hi, axis=1, keepdims=True)], axis=0)
                dp = jnp.concatenate(
                    [lax.dot_general(d_o_b, v_lo, nt_dims, preferred_element_type=F32),
                     lax.dot_general(d_o_b, v_hi, nt_dims, preferred_element_type=F32)], axis=0)
                dsb = _band_split(prob * (_band_merge(dp, prev) - delta), prev)
                sink_grad = jnp.exp(sink_cols[kvh] - m) * inv * delta
                for b, (p, half) in enumerate(stacks[kvh]):
                    dsink = -jnp.sum(sink_grad[b * WINDOW:(b + 1) * WINDOW])
                    dsk = dsk + jnp.where(lane == 2 * p + half, dsink, 0.0)
                dq_lo = jnp.dot(dsb[0:hs2], khalf[kvh][0][rows], preferred_element_type=F32)
                dq_hi = jnp.dot(dsb[hs2:rs], khalf[kvh][1][rows], preferred_element_type=F32)
                dka[kvh] = lax.dot_general(dsb, qs, tn_dims, preferred_element_type=F32)
                d_o_half = jnp.concatenate([jnp.where(lo, d_o, 0.0), jnp.where(lo, 0.0, d_o)],
                                           axis=0).astype(BF16)
                dva[kvh] = lax.dot_general(pb, d_o_half, tn_dims, preferred_element_type=F32)
                for b, p in enumerate(pairs):
                    blk = slice(b * WINDOW, (b + 1) * WINDOW)
                    cols = slice(LANES * p, LANES * (p + 1))
                    ag = pr_ref[r0:r0 + WINDOW, dm.o_ag + cols.start:dm.o_ag + cols.stop]
                    dya = dy_ref[r0:r0 + WINDOW, c + cols.start:c + cols.stop]
                    dqpair = (dq_lo[blk] + dq_hi[blk]) * SCALE
                    d_ag = dya * (o_lo[blk] + o_hi[blk]) * _dsilu(ag, _sigmoid(ag))
                    dpr_ref[r0:r0 + WINDOW, dm.o_q + cols.start:dm.o_q + cols.stop] = (
                        dqpair.astype(BF16))
                    dbin_ref[0:1, dm.o_q + cols.start:dm.o_q + cols.stop] += jnp.sum(
                        dqpair, axis=0, keepdims=True)
                    dpr_ref[r0:r0 + WINDOW, dm.o_ag + cols.start:dm.o_ag + cols.stop] = (
                        d_ag.astype(BF16))
                    dbin_ref[0:1, dm.o_ag + cols.start:dm.o_ag + cols.stop] += jnp.sum(
                        d_ag, axis=0, keepdims=True)
            fold = [x + pltpu.roll(x, HEAD_DIM, 1) for x in (dka[0], dka[1], dva[0], dva[1])]
            dkv_ref[r0:r0 + 2 * WINDOW, 0:KV_W] += jnp.where(lo, fold[0], fold[1])
            dkv_ref[r0:r0 + 2 * WINDOW, KV_W:2 * KV_W] += jnp.where(lo, fold[2], fold[3])
        dsk_ref[0:1, :] += dsk
        carry_kv_ref[...] = dkv_ref[0:WINDOW, :]
        emit(dm.o_k, 2 * KV_W, dkv_ref[WINDOW:WINDOW + t, :])

        @pl.when(g == nt - 1)
        def _():
            for j in range(CONV_WIDTH):
                dcw_ref[j:j + 1, :] = jnp.sum(dcw_acc_ref[j], axis=0, keepdims=True)
            dcw_ref[CONV_WIDTH:CONV_ROWS, :] = jnp.zeros((CONV_ROWS - CONV_WIDTH, c), F32)

    rev = lambda g: nt - 1 - g
    vec = pl.BlockSpec((1, c), lambda g: (0, 0))
    const = lambda shape: pl.BlockSpec(shape, lambda g: (0, 0))
    scratch = [pltpu.VMEM((t + HALO, c), F32), pltpu.VMEM((t + WINDOW, 2 * KV_W), F32),
               pltpu.VMEM((t + HALO, c), F32), pltpu.VMEM((t + WINDOW, 2 * KV_W), F32),
               pltpu.VMEM((HALO, c), F32), pltpu.VMEM((WINDOW, 2 * KV_W), F32),
               pltpu.VMEM((SUBLANES - 1, t + HALO, LANES), F32),
               pltpu.VMEM((SUBLANES - 1, t + HALO, LANES), F32),
               pltpu.VMEM((t, c), F32), pltpu.VMEM((CONV_ROWS, SUBLANES, c), F32),
               pltpu.VMEM((rs, LANES), BF16), pltpu.VMEM((rs // 2, LANES), F32),
               pltpu.VMEM((t, c), F32)]
    host = _Hosting(list(copies), 10, 5)
    res = pl.pallas_call(
        _ride(body, host, 10, 5, len(scratch), nt), name="mixer_bwd",
        out_shape=[jax.ShapeDtypeStruct((dm.s, dm.din), BF16),
                   jax.ShapeDtypeStruct((CONV_ROWS, c), F32),
                   jax.ShapeDtypeStruct((8, c), F32),
                   jax.ShapeDtypeStruct((8, dm.din), F32),
                   jax.ShapeDtypeStruct((8, LANES), F32)]
        + host.out_shapes,
        grid=(nt,),
        in_specs=[pl.BlockSpec(memory_space=pltpu.SMEM)] + _mixer_specs(dm, t, rev)
        + [pl.BlockSpec((t, dm.d), lambda g: (rev(g), 0)), pl.BlockSpec((t, c), lambda g: (rev(g), 0)),
           pl.BlockSpec((CONV_ROWS, c), lambda g: (0, 0)), vec, vec, vec] + [_ANY] * len(host.args),
        out_specs=[pl.BlockSpec((t, dm.din), lambda g: (rev(g), 0)),
                   const((CONV_ROWS, c)), const((8, c)), const((8, dm.din)), const((8, LANES))]
        + [_ANY] * len(host.out_shapes),
        scratch_shapes=scratch + host.scratch, input_output_aliases=host.aliases,
        compiler_params=_cparams(("arbitrary",)),
    )(sinks, proj, proj, proj, dymix, conv, conv_w, conv_b, cln_g, cln_b, *host.args)
    return res[:5], res[5:]


def _outproj_ln(dm, ymix, w_out, b_out, x, ln_g, ln_b, target, tm, copies=()):
    last = target is not None
    d = dm.d
    n_in = 7 if last else 6
    host = _Hosting(list(copies), n_in, 3)
    n_hin, n_hout = len(host.args), len(host.out_shapes)
    steps = dm.s // tm

    def body(*refs):
        y_ref, w_ref, bo_ref, x_ref, g_ref, b_ref = refs[:6]
        h_in = refs[n_in:n_in + n_hin]
        h_out = refs[n_in + n_hin + 3:n_in + n_hin + 3 + n_hout]
        h_sems = refs[n_in + n_hin + 3 + n_hout:]
        if copies:
            @pl.when(pl.program_id(0) == 0)
            def _():
                host.start(h_in, h_out, h_sems)

        z = dm.alpha * x_ref[...] + (
            jnp.dot(y_ref[...], w_ref[...], preferred_element_type=F32) + bo_ref[...])
        mu = jnp.mean(z, axis=1, keepdims=True)
        dlt = z - mu
        var = jnp.mean(dlt * dlt, axis=1, keepdims=True)
        out = dlt * lax.rsqrt(var + LN_EPS) * g_ref[...] + b_ref[...]
        if last:
            t_ref = refs[6]
            z_ref, dout_ref, loss_ref = refs[n_in + n_hin:n_in + n_hin + 3]
            z_ref[...] = z
            err = out - t_ref[...]
            dout_ref[...] = err * (1.0 / d)

            @pl.when(pl.program_id(0) == 0)
            def _():
                loss_ref[...] = jnp.zeros_like(loss_ref)

            loss_ref[...] += 0.5 * jnp.sum(jnp.mean(err * err, axis=1, keepdims=True), axis=0,
                                           keepdims=True)
        else:
            z_ref, o_ref, ob_ref = refs[n_in + n_hin:n_in + n_hin + 3]
            z_ref[...] = z
            o_ref[...] = out
            ob_ref[...] = out.astype(BF16)

        if copies:
            @pl.when(pl.program_id(0) == steps - 1)
            def _():
                host.finish(h_in, h_out, h_sems)

    row = pl.BlockSpec((tm, d), lambda i: (i, 0))
    vec = pl.BlockSpec((1, d), lambda i: (0, 0))
    in_specs = [row, pl.BlockSpec((d, d), lambda i: (0, 0)), vec, row, vec, vec]
    args = [ymix, w_out, b_out, x, ln_g, ln_b]
    act = jax.ShapeDtypeStruct((dm.s, d), F32)
    if last:
        in_specs.append(row)
        args.append(target)
        out_shape = [act, act, jax.ShapeDtypeStruct((8, LANES), F32)]
        out_specs = [row, row, pl.BlockSpec((8, LANES), lambda i: (0, 0))]
    else:
        out_shape = [act, act, jax.ShapeDtypeStruct((dm.s, d), BF16)]
        out_specs = [row, row, row]
    res = pl.pallas_call(
        body, name="outproj_ln_loss" if last else "outproj_ln",
        out_shape=out_shape + host.out_shapes, grid=(steps,),
        in_specs=in_specs + [_ANY] * n_hin, out_specs=out_specs + [_ANY] * n_hout,
        scratch_shapes=host.scratch, input_output_aliases=host.aliases,
        compiler_params=_cparams(("arbitrary",)),
    )(*args, *host.args)
    return res[:3], res[3:]


def _ln_bwd(dm, dout, z, ln_g, tm):
    d = dm.d

    def body(do_ref, z_ref, g_ref, dz_ref, dzb_ref, sm_ref):
        @pl.when(pl.program_id(0) == 0)
        def _():
            sm_ref[...] = jnp.zeros_like(sm_ref)

        z = z_ref[...]
        mu = jnp.mean(z, axis=1, keepdims=True)
        dlt = z - mu
        var = jnp.mean(dlt * dlt, axis=1, keepdims=True)
        rstd = lax.rsqrt(var + LN_EPS)
        zhat = dlt * rstd
        do = do_ref[...]
        dzh = do * g_ref[...]
        dz = rstd * (dzh - jnp.mean(dzh, axis=1, keepdims=True)
                     - zhat * jnp.mean(dzh * zhat, axis=1, keepdims=True))
        dz_ref[...] = dz
        dzb_ref[...] = dz.astype(BF16)
        sm_ref[0:1, :] += jnp.sum(do * zhat, axis=0, keepdims=True)
        sm_ref[1:2, :] += jnp.sum(do, axis=0, keepdims=True)
        sm_ref[2:3, :] += jnp.sum(dz, axis=0, keepdims=True)

    row = pl.BlockSpec((tm, d), lambda i: (i, 0))
    return pl.pallas_call(
        body, name="ln_bwd",
        out_shape=(jax.ShapeDtypeStruct((dm.s, d), F32), jax.ShapeDtypeStruct((dm.s, d), BF16),
                   jax.ShapeDtypeStruct((8, d), F32)),
        grid=(dm.s // tm,),
        in_specs=[row, row, pl.BlockSpec((1, d), lambda i: (0, 0))],
        out_specs=(row, row, pl.BlockSpec((8, d), lambda i: (0, 0))),
        compiler_params=_cparams(("arbitrary",)),
    )(dout, z, ln_g)


def _pair_sum(part, got, tr):
    _, _, r, w = part.shape

    def body(c_ref, p_ref, g_ref, o_ref):
        o_ref[...] = (p_ref[...] + g_ref[...]).astype(BF16)

    return pl.pallas_call(
        body, name="grad_pair_sum",
        out_shape=jax.ShapeDtypeStruct((N_CHIPS, r, w), BF16),
        grid_spec=pltpu.PrefetchScalarGridSpec(
            num_scalar_prefetch=1, grid=(N_CHIPS, r // tr),
            in_specs=[pl.BlockSpec((None, None, tr, w), lambda k, i, c_ref: (k, c_ref[0], i, 0)),
                      pl.BlockSpec((None, tr, w), lambda k, i, c_ref: (k, i, 0))],
            out_specs=pl.BlockSpec((None, tr, w), lambda k, i, c_ref: (k, i, 0))),
        compiler_params=_cparams(("parallel", "parallel")),
    )(lax.axis_index("c").reshape(1).astype(jnp.int32), part, got)


def _final_sum(part, got, recv, tr):
    _, _, r, w = part.shape

    def body(idx_ref, p_ref, g_ref, r0_ref, r1_ref, r2_ref, o_ref):
        acc = p_ref[...] + g_ref[...]
        for ref in (r0_ref, r1_ref, r2_ref):
            acc = acc + ref[...].astype(F32)
        o_ref[...] = acc

    x, y, c = _position()
    idx = jnp.stack([c, 2 * x + y, 2 * (1 - x) + y, 2 * x + (1 - y),
                     2 * (1 - x) + (1 - y)]).astype(jnp.int32)
    other = lambda j: pl.BlockSpec((None, tr, w), lambda i, s: (s[2 + j], i, 0))
    return pl.pallas_call(
        body, name="grad_final_sum",
        out_shape=jax.ShapeDtypeStruct((r, w), F32),
        grid_spec=pltpu.PrefetchScalarGridSpec(
            num_scalar_prefetch=1, grid=(r // tr,),
            in_specs=[pl.BlockSpec((None, None, tr, w), lambda i, s: (s[1], s[0], i, 0)),
                      pl.BlockSpec((None, tr, w), lambda i, s: (s[1], i, 0)),
                      other(0), other(1), other(2)],
            out_specs=pl.BlockSpec((tr, w), lambda i, s: (i, 0))),
        compiler_params=_cparams(("parallel",)),
    )(idx, part, got, recv, recv, recv)


def _adamw_math(w, g, m, v):
    m = ADAM_B1 * m + (1.0 - ADAM_B1) * g
    v = ADAM_B2 * v + (1.0 - ADAM_B2) * (g * g)
    m_hat = m / (1.0 - ADAM_B1 ** ADAM_STEP)
    v_hat = v / (1.0 - ADAM_B2 ** ADAM_STEP)
    delta = -ADAM_LR * (m_hat / (jnp.sqrt(v_hat) + ADAM_EPS) + ADAM_WD * w)
    return delta, m, v


def _adamw(w, g, m, v, tr):
    depth, r, width = w.shape

    def body(w_ref, g_ref, m_ref, v_ref, d_ref, nm_ref, nv_ref):
        d_ref[...], nm_ref[...], nv_ref[...] = _adamw_math(
            w_ref[...], g_ref[...], m_ref[...], v_ref[...])

    spec = pl.BlockSpec((None, tr, width), lambda l, i: (l, i, 0))
    shape = jax.ShapeDtypeStruct(w.shape, F32)
    return pl.pallas_call(
        body, name="adamw", out_shape=(shape, shape, shape), grid=(depth, r // tr),
        in_specs=[spec] * 4, out_specs=(spec, spec, spec),
        compiler_params=_cparams(("parallel", "parallel")),
    )(w, g, m, v)


def _gather_direct(block):
    def copies(ins, outs, send, recv):
        x, y, c = _position()
        out = []
        for k in range(1, N_DEV):
            peer = (1 - x if k & 4 else x, 1 - y if k & 2 else y, 1 - c if k & 1 else c)
            out.append((pltpu.make_async_remote_copy(
                src_ref=ins[0], dst_ref=outs[0].at[4 * x + 2 * y + c],
                send_sem=send.at[0, k - 1], recv_sem=recv.at[0, k - 1],
                device_id=peer, device_id_type=MESH), k - 1, peer))
        local = pltpu.make_async_copy(ins[0], outs[0].at[4 * x + 2 * y + c], send.at[0, N_DEV - 1])
        return out, local

    def start(ins, outs, send, recv):
        remote, local = copies(ins, outs, send, recv)
        local.start()
        for cp, _, _ in remote:
            cp.start()

    def finish(ins, outs, send, recv):
        remote, local = copies(ins, outs, send, recv)
        x, y, c = _position()
        for _, k, (px, py, pc) in remote:
            pltpu.make_async_remote_copy(
                src_ref=ins[0], dst_ref=outs[0].at[4 * px + 2 * py + pc],
                send_sem=send.at[0, k], recv_sem=recv.at[0, k],
                device_id=(x, y, c), device_id_type=MESH).wait_recv()
        for cp, _, _ in remote:
            cp.wait_send()
        local.wait()

    landing = [jax.ShapeDtypeStruct((N_DEV,) + block.shape, block.dtype)]
    return _Copies([block], landing, {}, (1, N_DEV), start, finish)


def _small_sum_adamw(gathered, w, m, v):
    prows = w.shape[0]

    def body(g_ref, w_ref, m_ref, v_ref, tot_ref, d_ref, nm_ref, nv_ref):
        tot = g_ref[0]
        for d in range(1, N_DEV):
            tot = tot + g_ref[d]
        tot_ref[...] = tot
        d_ref[...], nm_ref[...], nv_ref[...] = _adamw_math(
            w_ref[...], tot[0:prows, :], m_ref[...], v_ref[...])

    vm = pl.BlockSpec(memory_space=pltpu.VMEM)
    pshape = jax.ShapeDtypeStruct(w.shape, F32)
    return pl.pallas_call(
        body, name="small_sum_adamw",
        out_shape=(jax.ShapeDtypeStruct(gathered.shape[1:], F32), pshape, pshape, pshape),
        in_specs=[vm] * 4, out_specs=(vm, vm, vm, vm),
        compiler_params=pltpu.CompilerParams(vmem_limit_bytes=VMEM_LIMIT),
    )(gathered, w, m, v)


def _pack_rows(vec):
    depth, n = vec.shape
    rows = -(-n // LANES)
    rows = -(-rows // 8) * 8
    return jnp.pad(vec, ((0, 0), (0, rows * LANES - n))).reshape(depth, rows, LANES)


def _pack_small(named):
    blocks = [_pack_rows(a) for a in named]
    extents = [(b.shape[1], a.shape[1]) for b, a in zip(blocks, named)]
    depth = named[0].shape[0]
    packed = jnp.concatenate(blocks, axis=1).reshape(depth * sum(r for r, _ in extents), LANES)
    return packed, extents


def _unpack_small(packed, extents, depth):
    per_layer = sum(r for r, _ in extents)
    packed = packed.reshape(depth, per_layer, LANES)
    out, r0 = [], 0
    for rows, n in extents:
        out.append(packed[:, r0:r0 + rows, :].reshape(depth, rows * LANES)[:, :n])
        r0 += rows
    return out


def kernel(x, w_in, b_in, conv_w, conv_b, conv_ln_g, conv_ln_b, sinks, w_out, b_out, ln_g, ln_b, loss_target, m_w_in, m_b_in, m_conv_w, m_conv_b, m_conv_ln_g, m_conv_ln_b, m_sinks, m_w_out, m_b_out, m_ln_g, m_ln_b, v_w_in, v_b_in, v_conv_w, v_conv_b, v_conv_ln_g, v_conv_ln_b, v_sinks, v_w_out, v_b_out, v_ln_g, v_ln_b):
    depth, d, din_shard = w_in.shape
    s = x.shape[1]
    c_shard = conv_w.shape[2]
    dm = _Dims(s, d, N_DEV * c_shard, depth)
    assert dm.din == N_DEV * din_shard and x.shape[0] == 1 and sinks.shape[1] == dm.nh
    d_shard = w_out.shape[1]
    c, din = dm.c, dm.din

    t_mix = _tile(s, 256, WINDOW)
    tm_row = _tile(s, 256, 8)
    tm_big = _tile(s, 1024, 8)

    w_in_t, m_w_in_t, v_w_in_t = (a.transpose(0, 2, 1) for a in (w_in, m_w_in, v_w_in))
    w_in_b, w_out_b = w_in_t.astype(BF16), w_out.astype(BF16)
    conv_w_pad = jnp.pad(conv_w, ((0, 0), (0, CONV_ROWS - CONV_WIDTH), (0, 0)))
    first = _run_copies([_gather_own([w_in_b[0], conv_w_pad])], "weights_gather_own")
    g_in, g_conv = _run_copies([_gather_forward(first)], "weights_gather_forward")
    g_out = None
    conv_w_full = g_conv.transpose(1, 2, 0, 3).reshape(depth, CONV_ROWS, c)
    by_cols = lambda g: g.reshape(din, d)
    by_rows = lambda g: g.reshape(d, d)

    xs = x[0]
    xb = xs.astype(BF16)
    saved = []
    loss_part = dout = None
    for l in range(depth):
        w_in_l = by_cols(g_in)
        riders = [_gather_own([w_out_b[0]])] if l == 0 else [_gather_forward([g_out])]
        proj, (g_out,) = _matmul(xb, w_in_l, tb=True, tm=tm_big, tn=_tile(din, 768), tk=d,
                                 out_dtype=F32, name="in_proj", bias=b_in[l][None, :],
                                 copies=riders)
        riders = [_gather_forward([g_out])] if l == 0 else []
        if l + 1 < depth:
            riders.append(_gather_own([w_in_b[l + 1]]))
        (ymix, conv), landed = _mixer_fwd(dm, proj, conv_w_full[l], conv_b[l][None, :],
                                          conv_ln_g[l][None, :], conv_ln_b[l][None, :], sinks[l],
                                          t_mix, copies=riders)
        if l == 0:
            g_out = landed[0]
        g_in_next = landed[-1] if l + 1 < depth else None
        w_out_l = by_rows(g_out)
        target = loss_target[0] if l == depth - 1 else None
        riders = []
        if l + 1 < depth:
            riders = [_gather_forward([g_in_next]), _gather_own([w_out_b[l + 1]])]
        res, landed = _outproj_ln(dm, ymix, w_out_l, b_out[l][None, :], xs, ln_g[l][None, :],
                                  ln_b[l][None, :], target, tm_row, copies=riders)
        saved.append((xb, proj, ymix, conv, res[0], w_in_l, w_out_l))
        if l + 1 < depth:
            g_in, g_out = landed
            xs, xb = res[1], res[2]
        else:
            dout, loss_part = res[1], res[2]

    g_w_in, g_w_out = [None] * depth, [None] * depth
    small_parts = [None] * depth
    dconv_w = [None] * depth
    tr_in, tr_out = _tile(din_shard, 512, 8), _tile(d_shard, 256, 8)
    parts_in = None
    for l in reversed(range(depth)):
        xb, proj, ymix, conv, z, w_in_l, w_out_l = saved[l]
        dz, dzb, ln_small = _ln_bwd(dm, dout, z, ln_g[l][None, :], tm_row)
        mm = dict(tb=True, tm=tm_big, tn=_tile(d, 1024), tk=d, out_dtype=F32, name="dymix")
        if parts_in is not None:
            dymix, (got_in,) = _matmul(dzb, w_out_l, copies=[_scatter_sibling([parts_in])], **mm)
            sums_in = _pair_sum(parts_in, got_in, tr_in)
        else:
            dymix = _matmul(dzb, w_out_l, **mm)
        dw_out = _matmul(ymix, dzb, ta=True, tm=_tile(d, 1024), tn=_tile(d, 1024), tk=s,
                         out_dtype=F32, name="dw_out")
        riders = [_scatter_chips([sums_in])] if parts_in is not None else []
        (dproj, dcw, conv_small, dbin, dsk), landed = _mixer_bwd(
            dm, proj, dymix, conv, conv_w_full[l], conv_b[l][None, :], conv_ln_g[l][None, :],
            conv_ln_b[l][None, :], sinks[l], t_mix, copies=riders)
        if parts_in is not None:
            g_w_in[l + 1] = _final_sum(parts_in, got_in, landed[0], tr_in)
        small_parts[l] = [dbin[0], conv_small[0], conv_small[1], conv_small[2], dsk[0, :dm.nh],
                          ln_small[2], ln_small[0], ln_small[1]]
        dconv_w[l] = dcw
        parts_out = dw_out.reshape(N_CHIPS, 2, d_shard, d)
        dx = dict(tm=_tile(s, 512), tn=_tile(d, 1024), tk=din, out_dtype=F32, name="dx", resid=dz,
                  resid_scale=dm.alpha)
        dw = dict(ta=True, tm=_tile(din, 768), tn=_tile(d, 1024), tk=s, out_dtype=F32, name="dw_in")
        if l > 0:
            dout, (got_out,) = _matmul(dproj, w_in_l, copies=[_scatter_sibling([parts_out])], **dx)
            sums_out = _pair_sum(parts_out, got_out, tr_out)
            dw_in_t, (recv_out,) = _matmul(dproj, xb, copies=[_scatter_chips([sums_out])], **dw)
            parts_in = dw_in_t.reshape(N_CHIPS, 2, din_shard, d)
        else:
            packed_g, extents = _pack_small([jnp.stack([small_parts[k][n] for k in range(depth)])
                                             for n in range(len(small_parts[0]))])
            prows = packed_g.shape[0]
            conv_rows = depth * CONV_ROWS * c // LANES
            partial = jnp.concatenate(
                [packed_g, jnp.stack(dconv_w).reshape(conv_rows, LANES), loss_part], axis=0)
            dw_in_t, (got_out, small_all) = _matmul(
                dproj, xb, copies=[_scatter_sibling([parts_out]), _gather_direct(partial)], **dw)
            parts_in = dw_in_t.reshape(N_CHIPS, 2, din_shard, d)
            sums_out = _pair_sum(parts_out, got_out, tr_out)
            got_in, recv_out = _run_copies(
                [_scatter_sibling([parts_in]), _scatter_chips([sums_out])], "grad_exchange_tail")
            sums_in = _pair_sum(parts_in, got_in, tr_in)
            dout, (recv_in,) = _matmul(dproj, w_in_l, copies=[_scatter_chips([sums_in])], **dx)
            g_w_in[0] = _final_sum(parts_in, got_in, recv_in, tr_in)
        g_w_out[l] = _final_sum(parts_out, got_out, recv_out, tr_out)
    grad_x = dout[None]

    small_w = [b_in, conv_b, conv_ln_g, conv_ln_b, sinks, b_out, ln_g, ln_b]
    small_m = [m_b_in, m_conv_b, m_conv_ln_g, m_conv_ln_b, m_sinks, m_b_out, m_ln_g, m_ln_b]
    small_v = [v_b_in, v_conv_b, v_conv_ln_g, v_conv_ln_b, v_sinks, v_b_out, v_ln_g, v_ln_b]
    packed_w, _ = _pack_small(small_w)
    packed_m, _ = _pack_small(small_m)
    packed_v, _ = _pack_small(small_v)
    total, sm_delta, sm_m, sm_v = _small_sum_adamw(small_all, packed_w, packed_m, packed_v)
    loss = total[prows + conv_rows, 0]
    dconv_w_full = total[prows:prows + conv_rows].reshape(depth, CONV_ROWS, c)
    me = 4 * lax.axis_index("x") + 2 * lax.axis_index("y") + lax.axis_index("c")
    grad_conv_w = lax.dynamic_slice_in_dim(dconv_w_full, me * c_shard, c_shard, axis=2)[:, :CONV_WIDTH]

    grads_small = _unpack_small(total[:prows], extents, depth)
    delta_small = _unpack_small(sm_delta, extents, depth)
    newm_small = _unpack_small(sm_m, extents, depth)
    newv_small = _unpack_small(sm_v, extents, depth)

    grad_w_in_t = jnp.stack(g_w_in)
    grad_w_out = jnp.stack(g_w_out)
    grad_w_in, d_w_in, nm_w_in, nv_w_in = (a.transpose(0, 2, 1) for a in (
        grad_w_in_t, *_adamw(w_in_t, grad_w_in_t, m_w_in_t, v_w_in_t, tr_in)))
    d_w_out, nm_w_out, nv_w_out = _adamw(w_out, grad_w_out, m_w_out, v_w_out, _tile(d_shard, 256, 8))
    d_cw, nm_cw, nv_cw = _adamw(conv_w, grad_conv_w, m_conv_w, v_conv_w, CONV_WIDTH)

    def assemble(w_in_leaf, conv_w_leaf, w_out_leaf, small):
        b_in_, conv_b_, cg_, cb_, sinks_, b_out_, ln_g_, ln_b_ = small
        return [w_in_leaf, b_in_, conv_w_leaf, conv_b_, cg_, cb_, sinks_, w_out_leaf, b_out_,
                ln_g_, ln_b_]

    return (loss, grad_x,
            *assemble(grad_w_in, grad_conv_w, grad_w_out, grads_small),
            *assemble(d_w_in, d_cw, d_w_out, delta_small),
            *assemble(nm_w_in, nm_cw, nm_w_out, newm_small),
            *assemble(nv_w_in, nv_cw, nv_w_out, newv_small))
```

```python
import functools

import jax
import jax.numpy as jnp
from jax import lax
from jax.experimental import pallas as pl
from jax.experimental.pallas import tpu as pltpu

F32 = jnp.float32
BF16 = jnp.bfloat16
MESH = pl.DeviceIdType.MESH

N_DEV = 8
N_CHIPS = 4
HEAD_DIM = 64
N_KV = 2
KV_W = N_KV * HEAD_DIM
CONV_WIDTH = 31
CONV_ROWS = 32
HALO = 32
WINDOW = 128
LN_EPS = 1e-5
NEG_INF = -1e30
LANES = 128

ADAM_LR = 0.001
ADAM_B1 = 0.9
ADAM_B2 = 0.999
ADAM_EPS = 1e-08
ADAM_WD = 0.01
ADAM_STEP = 10

VMEM_LIMIT = 56 * 1024 * 1024


def _tile(n, target, align=LANES):
    best = None
    for t in range(align, min(n, target) + 1, align):
        if n % t == 0:
            best = t
    return n if best is None else best


def _sigmoid(x):
    return jax.nn.sigmoid(x)


def _dsilu(x, s):
    return s * (1.0 + x * (1.0 - s))


def _cparams(sem, vmem=VMEM_LIMIT):
    return pltpu.CompilerParams(dimension_semantics=sem, vmem_limit_bytes=vmem)


_ANY = pl.BlockSpec(memory_space=pl.ANY)


def _position():
    return lax.axis_index("x"), lax.axis_index("y"), lax.axis_index("c")


def _other_chips(x, y):
    return [(1 - x, y), (x, 1 - y), (1 - x, 1 - y)]


class _Copies:
    def __init__(self, operands, landing, alias, sems, start, finish):
        self.operands, self.landing, self.alias, self.sems = operands, landing, alias, sems
        self.start, self.finish = start, finish


class _Hosting:
    def __init__(self, groups, n_in, n_out):
        self.groups = groups
        self.args = [a for g in groups for a in g.operands]
        self.out_shapes = [s for g in groups for s in g.landing]
        self.scratch = [pltpu.SemaphoreType.DMA(g.sems) for g in groups for _ in range(2)]
        self.aliases = {}
        i0, o0 = n_in, n_out
        for g in groups:
            for a, b in g.alias.items():
                self.aliases[i0 + a] = o0 + b
            i0 += len(g.operands)
            o0 += len(g.landing)

    def _each(self, in_refs, out_refs, sem_refs):
        i0 = o0 = 0
        for n, g in enumerate(self.groups):
            yield (g, in_refs[i0:i0 + len(g.operands)], out_refs[o0:o0 + len(g.landing)],
                   sem_refs[2 * n], sem_refs[2 * n + 1])
            i0 += len(g.operands)
            o0 += len(g.landing)

    def start(self, in_refs, out_refs, sem_refs):
        for g, ins, outs, send, recv in self._each(in_refs, out_refs, sem_refs):
            g.start(ins, outs, send, recv)

    def finish(self, in_refs, out_refs, sem_refs):
        for g, ins, outs, send, recv in self._each(in_refs, out_refs, sem_refs):
            g.finish(ins, outs, send, recv)


def _ride(body, host, n_in, n_out, n_scratch, steps):
    n_hin, n_hout = len(host.args), len(host.out_shapes)

    def wrapped(*refs):
        pos = [0]

        def take(n):
            pos[0] += n
            return refs[pos[0] - n:pos[0]]

        ins, h_in, outs, h_out = take(n_in), take(n_hin), take(n_out), take(n_hout)
        scratch, h_sems = take(n_scratch), refs[pos[0]:]
        if host.groups:
            @pl.when(pl.program_id(0) == 0)
            def _():
                host.start(h_in, h_out, h_sems)

        body(*ins, *outs, *scratch)
        if host.groups:
            @pl.when(pl.program_id(0) == steps - 1)
            def _():
                host.finish(h_in, h_out, h_sems)

    return wrapped


def _run_copies(groups, name):
    host = _Hosting(groups, 0, 0)
    n_in, n_out = len(host.args), len(host.out_shapes)

    def body(*refs):
        ins, outs, sems = refs[:n_in], refs[n_in:n_in + n_out], refs[n_in + n_out:]
        host.start(ins, outs, sems)
        host.finish(ins, outs, sems)

    return pl.pallas_call(
        body, name=name, out_shape=host.out_shapes, in_specs=[_ANY] * n_in,
        out_specs=[_ANY] * n_out, scratch_shapes=host.scratch,
        input_output_aliases=host.aliases,
    )(*host.args)


def _gather_own(shards):
    n = len(shards)

    def copies(ins, outs, send, recv):
        x, y, c = _position()
        peers = [(x, y, 1 - c)] + [(px, py, c) for px, py in _other_chips(x, y)]
        out = []
        for t in range(n):
            for k, peer in enumerate(peers):
                out.append((pltpu.make_async_remote_copy(
                    src_ref=ins[t], dst_ref=outs[t].at[4 * x + 2 * y + c],
                    send_sem=send.at[t, k], recv_sem=recv.at[t, k],
                    device_id=peer, device_id_type=MESH), t, k, peer))
        local = [pltpu.make_async_copy(ins[t], outs[t].at[4 * x + 2 * y + c], send.at[t, 4])
                 for t in range(n)]
        return out, local

    def start(ins, outs, send, recv):
        remote, local = copies(ins, outs, send, recv)
        for cp in local:
            cp.start()
        for cp, _, _, _ in remote:
            cp.start()

    def finish(ins, outs, send, recv):
        remote, local = copies(ins, outs, send, recv)
        x, y, c = _position()
        for _, t, k, (px, py, pc) in remote:
            pltpu.make_async_remote_copy(
                src_ref=ins[t], dst_ref=outs[t].at[4 * px + 2 * py + pc],
                send_sem=send.at[t, k], recv_sem=recv.at[t, k],
                device_id=(x, y, c), device_id_type=MESH).wait_recv()
        for cp, _, _, _ in remote:
            cp.wait_send()
        for cp in local:
            cp.wait()

    landing = [jax.ShapeDtypeStruct((N_DEV,) + s.shape, s.dtype) for s in shards]
    return _Copies(list(shards), landing, {}, (n, 5), start, finish)


def _gather_forward(buffers):
    n = len(buffers)

    def copies(ins, outs, send, recv):
        x, y, c = _position()
        out = []
        for t in range(n):
            for j, (px, py) in enumerate(_other_chips(x, y)):
                slot = 4 * px + 2 * py + c
                out.append((pltpu.make_async_remote_copy(
                    src_ref=ins[t].at[slot], dst_ref=outs[t].at[slot],
                    send_sem=send.at[t, j], recv_sem=recv.at[t, j],
                    device_id=(x, y, 1 - c), device_id_type=MESH), t, j, 4 * px + 2 * py + 1 - c))
        return out

    def start(ins, outs, send, recv):
        for cp, _, _, _ in copies(ins, outs, send, recv):
            cp.start()

    def finish(ins, outs, send, recv):
        x, y, c = _position()
        mine = copies(ins, outs, send, recv)
        for _, t, j, got in mine:
            pltpu.make_async_remote_copy(
                src_ref=ins[t].at[got], dst_ref=outs[t].at[got],
                send_sem=send.at[t, j], recv_sem=recv.at[t, j],
                device_id=(x, y, c), device_id_type=MESH).wait_recv()
        for cp, _, _, _ in mine:
            cp.wait_send()

    landing = [jax.ShapeDtypeStruct(b.shape, b.dtype) for b in buffers]
    return _Copies(list(buffers), landing, {t: t for t in range(n)}, (n, 3), start, finish)


def _scatter_sibling(parts):
    n = len(parts)

    def copies(ins, outs, send, recv):
        x, y, c = _position()
        return [pltpu.make_async_remote_copy(
            src_ref=ins[t].at[:, 1 - c], dst_ref=outs[t],
            send_sem=send.at[t, 0], recv_sem=recv.at[t, 0],
            device_id=(x, y, 1 - c), device_id_type=MESH) for t in range(n)]

    def start(ins, outs, send, recv):
        for cp in copies(ins, outs, send, recv):
            cp.start()

    def finish(ins, outs, send, recv):
        for cp in copies(ins, outs, send, recv):
            cp.wait()

    landing = [jax.ShapeDtypeStruct((p.shape[0],) + p.shape[2:], p.dtype) for p in parts]
    return _Copies(list(parts), landing, {}, (n, 1), start, finish)


def _scatter_chips(blocks):
    n = len(blocks)

    def start(ins, outs, send, recv):
        x, y, c = _position()
        for t in range(n):
            for j, (px, py) in enumerate(_other_chips(x, y)):
                pltpu.make_async_remote_copy(
                    src_ref=ins[t].at[2 * px + py], dst_ref=outs[t].at[2 * x + y],
                    send_sem=send.at[t, j], recv_sem=recv.at[t, j],
                    device_id=(px, py, c), device_id_type=MESH).start()

    def finish(ins, outs, send, recv):
        x, y, c = _position()
        for t in range(n):
            for j, (px, py) in enumerate(_other_chips(x, y)):
                cp = pltpu.make_async_remote_copy(
                    src_ref=ins[t].at[2 * px + py], dst_ref=outs[t].at[2 * px + py],
                    send_sem=send.at[t, j], recv_sem=recv.at[t, j],
                    device_id=(px, py, c), device_id_type=MESH)
                cp.wait_recv()
                cp.wait_send()

    landing = [jax.ShapeDtypeStruct(b.shape, b.dtype) for b in blocks]
    return _Copies(list(blocks), landing, {}, (n, 3), start, finish)


def _matmul(a, b, *, ta=False, tb=False, tm, tn, tk, out_dtype, name, bias=None, resid=None,
            resid_scale=1.0, copies=()):
    m, k = (a.shape[1], a.shape[0]) if ta else a.shape
    n = b.shape[0] if tb else b.shape[1]
    assert (b.shape[1] if tb else b.shape[0]) == k
    assert m % tm == 0 and n % tn == 0 and k % tk == 0
    ni, nj, nk = m // tm, n // tn, k // tk
    dn = (((0 if ta else 1,), (1 if tb else 0,)), ((), ()))
    n_in = 2 + (bias is not None) + (resid is not None)
    host = _Hosting(list(copies), n_in, 1)
    n_hin, n_hout = len(host.args), len(host.out_shapes)

    def body(*refs):
        a_ref, b_ref = refs[0], refs[1]
        pos = 2
        bias_ref = resid_ref = None
        if bias is not None:
            bias_ref = refs[pos]
            pos += 1
        if resid is not None:
            resid_ref = refs[pos]
            pos += 1
        h_in = refs[pos:pos + n_hin]
        pos += n_hin
        o_ref = refs[pos]
        h_out = refs[pos + 1:pos + 1 + n_hout]
        pos += 1 + n_hout
        acc_ref = refs[pos] if nk > 1 else None
        h_sems = refs[pos + (nk > 1):]
        step = (pl.program_id(0) * nj + pl.program_id(1)) * nk + pl.program_id(2)

        if copies:
            @pl.when(step == 0)
            def _():
                host.start(h_in, h_out, h_sems)

        def finish(acc):
            if bias_ref is not None:
                acc = acc + bias_ref[...]
            if resid_ref is not None:
                acc = acc + resid_scale * resid_ref[...]
            o_ref[...] = acc.astype(out_dtype)

        p = lax.dot_general(a_ref[...], b_ref[...], dn, preferred_element_type=F32)
        if nk == 1:
            finish(p)
        else:
            kk = pl.program_id(2)

            @pl.when(kk == 0)
            def _():
                acc_ref[...] = p

            @pl.when(kk > 0)
            def _():
                acc_ref[...] += p

            @pl.when(kk == nk - 1)
            def _():
                finish(acc_ref[...])

        if copies:
            @pl.when(step == ni * nj * nk - 1)
            def _():
                host.finish(h_in, h_out, h_sems)

    a_spec = (pl.BlockSpec((tk, tm), lambda i, j, kk: (kk, i)) if ta
              else pl.BlockSpec((tm, tk), lambda i, j, kk: (i, kk)))
    b_spec = (pl.BlockSpec((tn, tk), lambda i, j, kk: (j, kk)) if tb
              else pl.BlockSpec((tk, tn), lambda i, j, kk: (kk, j)))
    in_specs = [a_spec, b_spec]
    args = [a, b]
    if bias is not None:
        in_specs.append(pl.BlockSpec((1, tn), lambda i, j, kk: (0, j)))
        args.append(bias)
    if resid is not None:
        in_specs.append(pl.BlockSpec((tm, tn), lambda i, j, kk: (i, j)))
        args.append(resid)
    res = pl.pallas_call(
        body, name=name,
        out_shape=[jax.ShapeDtypeStruct((m, n), out_dtype)] + host.out_shapes,
        grid=(ni, nj, nk),
        in_specs=in_specs + [_ANY] * n_hin,
        out_specs=[pl.BlockSpec((tm, tn), lambda i, j, kk: (i, j))] + [_ANY] * n_hout,
        scratch_shapes=([pltpu.VMEM((tm, tn), F32)] if nk > 1 else []) + host.scratch,
        input_output_aliases=host.aliases,
        compiler_params=_cparams(("arbitrary",) * 3 if copies else
                                 ("parallel", "parallel", "arbitrary")),
    )(*args, *host.args)
    return (res[0], res[1:]) if copies else res[0]


class _Dims:
    def __init__(self, s, d, c, depth):
        self.s, self.d, self.c, self.depth = s, d, c, depth
        self.a = d - c
        self.nh = self.a // HEAD_DIM
        self.group = self.nh // N_KV
        self.din = 3 * c + 2 * self.a + 2 * KV_W
        self.o_q = 3 * c
        self.o_k = 3 * c + self.a
        self.o_v = self.o_k + KV_W
        self.o_ag = self.o_k + 2 * KV_W
        self.alpha = (2 * depth) ** 0.25
        assert self.nh % 2 == 0 and self.group % 2 == 0 and self.o_k % (2 * KV_W) == 0
        assert c % LANES == 0 and self.a % LANES == 0


SUBLANES = 8
TAP_ROWS = 64
ROW_CHUNK = 64


def _shift_copies(src_ref, sh_ref, lanes, rows):
    for r in range(1, SUBLANES):
        sh_ref[r - 1, 0:rows, :] = src_ref[pl.ds(r, rows), lanes]


def _tap_rows(src_ref, sh_ref, lanes, off, start, rows):
    r = off % SUBLANES
    at = pl.multiple_of(start + (off - r), SUBLANES)
    if r == 0:
        return src_ref[pl.ds(at, rows), lanes]
    return sh_ref[r - 1, pl.ds(at, rows), :]


def _conv_chunk(w_ref, src_ref, sh_ref, out_ref, lanes, t, first_off, reverse, bias_ref=None):
    def block(it, carry):
        start = pl.multiple_of(it * TAP_ROWS, TAP_ROWS)
        acc = None
        for j in range(CONV_WIDTH):
            off = first_off - j if reverse else first_off + j
            term = w_ref[j:j + 1, lanes] * _tap_rows(src_ref, sh_ref, lanes, off, start, TAP_ROWS)
            acc = term if acc is None else acc + term
        if bias_ref is not None:
            acc = acc + bias_ref[:, lanes]
        out_ref[pl.ds(start, TAP_ROWS), lanes] = acc
        return carry

    lax.fori_loop(0, t // TAP_ROWS, block, 0)


def _conv_w_grad_chunk(src_ref, sh_ref, dconv_ref, acc_ref, lanes, t, first_off):
    def block(it, accs):
        start = pl.multiple_of(it * SUBLANES, SUBLANES)
        dv = dconv_ref[pl.ds(start, SUBLANES), lanes]
        return tuple(
            acc + _tap_rows(src_ref, sh_ref, lanes, first_off + j, start, SUBLANES) * dv
            for j, acc in enumerate(accs))

    zero = jnp.zeros((SUBLANES, LANES), F32)
    accs = lax.fori_loop(0, t // SUBLANES, block, (zero,) * CONV_WIDTH, unroll=2)
    for j in range(CONV_WIDTH):
        acc_ref[j, :, lanes] += accs[j]


def _kv_operands(kv_ref, lo):
    kext = kv_ref[:, 0:KV_W]
    vext = kv_ref[:, KV_W:2 * KV_W]
    ksw = pltpu.roll(kext, HEAD_DIM, 1)
    vsw = pltpu.roll(vext, HEAD_DIM, 1)
    zero = jnp.zeros_like(kext)
    k2 = [jnp.where(lo, kext, ksw).astype(BF16), jnp.where(lo, ksw, kext).astype(BF16)]
    khalf = [[jnp.where(lo, kext, zero).astype(BF16), jnp.where(lo, zero, ksw).astype(BF16)],
             [jnp.where(lo, ksw, zero).astype(BF16), jnp.where(lo, zero, kext).astype(BF16)]]
    vhalf = [[jnp.where(lo, vext, zero).astype(BF16), jnp.where(lo, zero, vsw).astype(BF16)],
             [jnp.where(lo, vsw, zero).astype(BF16), jnp.where(lo, zero, vext).astype(BF16)]]
    return k2, khalf, vhalf


SCALE = HEAD_DIM ** -0.5


def _from_previous(rows):
    row = lax.broadcasted_iota(jnp.int32, (rows, WINDOW), 0)
    col = lax.broadcasted_iota(jnp.int32, (rows, WINDOW), 1)
    return col > (row & (WINDOW - 1))


def _sink_column(sinks_ref, heads):
    block = lax.broadcasted_iota(jnp.int32, (len(heads) * WINDOW, 1), 0) // WINDOW
    out = jnp.zeros(block.shape, F32)
    for b, head in enumerate(heads):
        out = jnp.where(block == b, sinks_ref[head], out)
    return out


def _stacked_heads(dm, kvh):
    pairs = range(kvh * (dm.group // 2), (kvh + 1) * (dm.group // 2))
    return [(p, half) for half in range(2) for p in pairs]


def _band_merge(both, prev):
    return jnp.where(prev, both[:, 0:WINDOW], both[:, WINDOW:2 * WINDOW])


def _band_split(tile, prev):
    zero = jnp.zeros_like(tile)
    return jnp.concatenate([jnp.where(prev, tile, zero), jnp.where(prev, zero, tile)],
                           axis=1).astype(BF16)


def _softmax_with_sink(qm, k2rows, prev, no_previous, sink):
    both = lax.dot_general(qm, k2rows, (((1,), (1,)), ((), ())), preferred_element_type=F32)
    s_prev = both[:, 0:WINDOW]
    if no_previous is not None:
        s_prev = jnp.where(no_previous, NEG_INF, s_prev)
    s = jnp.where(prev, s_prev, both[:, WINDOW:2 * WINDOW])
    m = jnp.maximum(jnp.max(s, axis=1, keepdims=True), sink)
    e = jnp.exp(s - m)
    den = jnp.sum(e, axis=1, keepdims=True) + jnp.exp(sink - m)
    inv = 1.0 / den
    return e * inv, m, inv


def _mixer_specs(dm, t, idx):
    return [
        pl.BlockSpec((t, dm.din), lambda g: (idx(g), 0)),
        pl.BlockSpec((HALO, 2 * dm.c), lambda g: (jnp.maximum(idx(g) * (t // HALO) - 1, 0), 0)),
        pl.BlockSpec((WINDOW, 2 * KV_W),
                     lambda g: (jnp.maximum(idx(g) * (t // WINDOW) - 1, 0), dm.o_k // (2 * KV_W))),
    ]


def _mixer_fwd(dm, proj, conv_w, conv_b, cln_g, cln_b, sinks, t, copies=()):
    c, nq = dm.c, t // WINDOW
    rs = dm.group * WINDOW
    host = _Hosting(list(copies), 8, 2)

    def body(sinks_ref, pr_ref, ch_ref, kvh_ref, cw_ref, cb_ref, cg_ref, cbb_ref,
             y_ref, conv_ref, hext_ref, kv_ref, hs_ref, qs_ref):
        i = pl.program_id(0)
        first = i == 0
        h = pr_ref[:, 0:c] * _sigmoid(pr_ref[:, c:2 * c])
        hh = ch_ref[:, 0:c] * _sigmoid(ch_ref[:, c:2 * c])
        hext_ref[0:HALO, :] = jnp.where(first, 0.0, hh)
        hext_ref[HALO:HALO + t, :] = h
        for k in range(c // LANES):
            lanes = slice(LANES * k, LANES * (k + 1))
            _shift_copies(hext_ref, hs_ref, lanes, t + HALO - SUBLANES)
            _conv_chunk(cw_ref, hext_ref, hs_ref, conv_ref, lanes, t, HALO - (CONV_WIDTH - 1),
                        False, cb_ref)
        conv = conv_ref[...]
        mu = jnp.mean(conv, axis=1, keepdims=True)
        dlt = conv - mu
        var = jnp.mean(dlt * dlt, axis=1, keepdims=True)
        u = dlt * lax.rsqrt(var + LN_EPS) * cg_ref[...] + cbb_ref[...]
        gate = pr_ref[:, 2 * c:3 * c]
        y_ref[:, 0:c] = (u * _sigmoid(u) * (gate * _sigmoid(gate))).astype(BF16)

        kv_ref[0:WINDOW, :] = jnp.where(first, 0.0, kvh_ref[...])
        kv_ref[WINDOW:WINDOW + t, :] = pr_ref[:, dm.o_k:dm.o_k + 2 * KV_W]
        lo = lax.broadcasted_iota(jnp.int32, (1, LANES), 1) < HEAD_DIM
        k2, _, vhalf = _kv_operands(kv_ref, lo)
        stacks = [_stacked_heads(dm, kvh) for kvh in range(N_KV)]
        sink_cols = [_sink_column(sinks_ref, [2 * p + half for p, half in st]) for st in stacks]
        prev = _from_previous(rs)
        for qb in range(nq):
            r0 = qb * WINDOW
            rows = slice(r0, r0 + 2 * WINDOW)
            no_previous = first if qb == 0 else None
            for kvh in range(N_KV):
                for b, (p, half) in enumerate(stacks[kvh]):
                    qp = pr_ref[r0:r0 + WINDOW, dm.o_q + LANES * p:dm.o_q + LANES * (p + 1)] * SCALE
                    qs_ref[b * WINDOW:(b + 1) * WINDOW, :] = jnp.where(
                        lo if half == 0 else jnp.logical_not(lo), qp, 0.0).astype(BF16)
                prob, _, _ = _softmax_with_sink(qs_ref[...], k2[kvh][rows], prev, no_previous,
                                                sink_cols[kvh])
                pb = _band_split(prob, prev)
                o_lo = jnp.dot(pb[0:rs // 2], vhalf[kvh][0][rows], preferred_element_type=F32)
                o_hi = jnp.dot(pb[rs // 2:rs], vhalf[kvh][1][rows], preferred_element_type=F32)
                for b, (p, _) in enumerate(stacks[kvh][:len(stacks[kvh]) // 2]):
                    blk = slice(b * WINDOW, (b + 1) * WINDOW)
                    ag = pr_ref[r0:r0 + WINDOW, dm.o_ag + LANES * p:dm.o_ag + LANES * (p + 1)]
                    y_ref[r0:r0 + WINDOW, c + LANES * p:c + LANES * (p + 1)] = (
                        (o_lo[blk] + o_hi[blk]) * (ag * _sigmoid(ag))).astype(BF16)

    vec = pl.BlockSpec((1, c), lambda g: (0, 0))
    scratch = [pltpu.VMEM((t + HALO, c), F32), pltpu.VMEM((t + WINDOW, 2 * KV_W), F32),
               pltpu.VMEM((SUBLANES - 1, t + HALO, LANES), F32), pltpu.VMEM((rs, LANES), BF16)]
    res = pl.pallas_call(
        _ride(body, host, 8, 2, len(scratch), dm.s // t), name="mixer_fwd",
        out_shape=[jax.ShapeDtypeStruct((dm.s, dm.d), BF16),
                   jax.ShapeDtypeStruct((dm.s, c), F32)]
        + host.out_shapes,
        grid=(dm.s // t,),
        in_specs=[pl.BlockSpec(memory_space=pltpu.SMEM)] + _mixer_specs(dm, t, lambda g: g)
        + [pl.BlockSpec((CONV_ROWS, c), lambda g: (0, 0)), vec, vec, vec] + [_ANY] * len(host.args),
        out_specs=[pl.BlockSpec((t, dm.d), lambda g: (g, 0)), pl.BlockSpec((t, c), lambda g: (g, 0))]
        + [_ANY] * len(host.out_shapes),
        scratch_shapes=scratch + host.scratch, input_output_aliases=host.aliases,
        compiler_params=_cparams(("arbitrary",)),
    )(sinks, proj, proj, proj, conv_w, conv_b, cln_g, cln_b, *host.args)
    return res[:2], res[2:]


def _mixer_bwd(dm, proj, dymix, conv, conv_w, cln_g, cln_b, sinks, t, copies=()):
    c, nq, nt = dm.c, t // WINDOW, dm.s // t
    rs = dm.group * WINDOW

    def body(sinks_ref, pr_ref, ch_ref, kvh_ref, dy_ref, cv_ref, cw_ref, cg_ref, cbb_ref,
             dpr_ref, dcw_ref, dsm_ref, dbin_ref, dsk_ref,
             hext_ref, kv_ref, dcx_ref, dkv_ref, carry_dc_ref, carry_kv_ref,
             hs_ref, ds_ref, dh_ref, dcw_acc_ref, qs_ref, do_ref, sg_ref):
        g = pl.program_id(0)
        i = nt - 1 - g
        first = i == 0

        @pl.when(g == 0)
        def _():
            dcw_acc_ref[...] = jnp.zeros_like(dcw_acc_ref)
            dsm_ref[...] = jnp.zeros_like(dsm_ref)
            dbin_ref[...] = jnp.zeros_like(dbin_ref)
            dsk_ref[...] = jnp.zeros_like(dsk_ref)
            carry_dc_ref[...] = jnp.zeros_like(carry_dc_ref)
            carry_kv_ref[...] = jnp.zeros_like(carry_kv_ref)

        def emit(col, width, val, rows=slice(None)):
            dpr_ref[rows, col:col + width] = val.astype(BF16)
            dbin_ref[0:1, col:col + width] += jnp.sum(val, axis=0, keepdims=True)

        hh = ch_ref[:, 0:c] * _sigmoid(ch_ref[:, c:2 * c])
        hext_ref[0:HALO, :] = jnp.where(first, 0.0, hh)
        dcx_ref[t:t + HALO, :] = carry_dc_ref[...]
        for r0 in range(0, t, ROW_CHUNK):
            rows = slice(r0, r0 + ROW_CHUNK)
            sg = _sigmoid(pr_ref[rows, c:2 * c])
            sg_ref[rows, :] = sg
            hext_ref[HALO + r0:HALO + r0 + ROW_CHUNK, :] = pr_ref[rows, 0:c] * sg
            conv = cv_ref[rows, :]
            mu = jnp.mean(conv, axis=1, keepdims=True)
            dlt = conv - mu
            var = jnp.mean(dlt * dlt, axis=1, keepdims=True)
            rstd = lax.rsqrt(var + LN_EPS)
            xhat = dlt * rstd
            u = xhat * cg_ref[...] + cbb_ref[...]
            su = _sigmoid(u)
            gate = pr_ref[rows, 2 * c:3 * c]
            sgate = _sigmoid(gate)
            dyc = dy_ref[rows, 0:c]
            emit(2 * c, c, dyc * (u * su) * _dsilu(gate, sgate), rows)
            du = dyc * (gate * sgate) * _dsilu(u, su)
            dsm_ref[1:2, :] += jnp.sum(du * xhat, axis=0, keepdims=True)
            dsm_ref[2:3, :] += jnp.sum(du, axis=0, keepdims=True)
            dxh = du * cg_ref[...]
            dconv = rstd * (dxh - jnp.mean(dxh, axis=1, keepdims=True)
                            - xhat * jnp.mean(dxh * xhat, axis=1, keepdims=True))
            dsm_ref[0:1, :] += jnp.sum(dconv, axis=0, keepdims=True)
            dcx_ref[rows, :] = dconv
        carry_dc_ref[...] = dcx_ref[0:HALO, :]
        for k in range(c // LANES):
            lanes = slice(LANES * k, LANES * (k + 1))
            _shift_copies(hext_ref, hs_ref, lanes, t + HALO - SUBLANES)
            _shift_copies(dcx_ref, ds_ref, lanes, t + HALO - SUBLANES)
            _conv_chunk(cw_ref, dcx_ref, ds_ref, dh_ref, lanes, t, CONV_WIDTH - 1, True)
            _conv_w_grad_chunk(hext_ref, hs_ref, dcx_ref, dcw_acc_ref, lanes, t,
                               HALO - (CONV_WIDTH - 1))
        for r0 in range(0, t, ROW_CHUNK):
            rows = slice(r0, r0 + ROW_CHUNK)
            dh, sg = dh_ref[rows, :], sg_ref[rows, :]
            emit(0, c, dh * sg, rows)
            emit(c, c, dh * pr_ref[rows, 0:c] * sg * (1.0 - sg), rows)

        kv_ref[0:WINDOW, :] = jnp.where(first, 0.0, kvh_ref[...])
        kv_ref[WINDOW:WINDOW + t, :] = pr_ref[:, dm.o_k:dm.o_k + 2 * KV_W]
        dkv_ref[0:t, :] = jnp.zeros((t, 2 * KV_W), F32)
        dkv_ref[t:t + WINDOW, :] = carry_kv_ref[...]
        lane = lax.broadcasted_iota(jnp.int32, (1, LANES), 1)
        lo = lane < HEAD_DIM
        k2, khalf, vhalf = _kv_operands(kv_ref, lo)
        tn_dims = (((0,), (0,)), ((), ()))
        nt_dims = (((1,), (1,)), ((), ()))
        dsk = jnp.zeros((1, LANES), F32)
        stacks = [_stacked_heads(dm, kvh) for kvh in range(N_KV)]
        sink_cols = [_sink_column(sinks_ref, [2 * p + half for p, half in st]) for st in stacks]
        prev = _from_previous(rs)
        hs2 = rs // 2
        for qb in range(nq):
            r0 = qb * WINDOW
            rows = slice(r0, r0 + 2 * WINDOW)
            no_previous = first if qb == 0 else None
            dka = [None, None]
            dva = [None, None]
            for kvh in range(N_KV):
                pairs = [p for p, half in stacks[kvh] if half == 0]
                for b, (p, half) in enumerate(stacks[kvh]):
                    qp = pr_ref[r0:r0 + WINDOW, dm.o_q + LANES * p:dm.o_q + LANES * (p + 1)] * SCALE
                    qs_ref[b * WINDOW:(b + 1) * WINDOW, :] = jnp.where(
                        lo if half == 0 else jnp.logical_not(lo), qp, 0.0).astype(BF16)
                for b, p in enumerate(pairs):
                    ag = pr_ref[r0:r0 + WINDOW, dm.o_ag + LANES * p:dm.o_ag + LANES * (p + 1)]
                    dya = dy_ref[r0:r0 + WINDOW, c + LANES * p:c + LANES * (p + 1)]
                    do_ref[b * WINDOW:(b + 1) * WINDOW, :] = dya * (ag * _sigmoid(ag))
                qs = qs_ref[...]
                d_o = do_ref[...]
                d_o_b = d_o.astype(BF16)
                v_lo, v_hi = vhalf[kvh][0][rows], vhalf[kvh][1][rows]
                prob, m, inv = _softmax_with_sink(qs, k2[kvh][rows], prev, no_previous,
                                                  sink_cols[kvh])
                pb = _band_split(prob, prev)
                o_lo = jnp.dot(pb[0:hs2], v_lo, preferred_element_type=F32)
                o_hi = jnp.dot(pb[hs2:rs], v_hi, preferred_element_type=F32)
                delta = jnp.concatenate([jnp.sum(d_o * o_lo, axis=1, keepdims=True),
                                         jnp.sum(d_o * o_hi, axis=1, keepdims=True)], axis=0)
                dp = jnp.concatenate(
                    [lax.dot_general(d_o_b, v_lo, nt_dims, preferred_element_type=F32),
                     lax.dot_general(d_o_b, v_hi, nt_dims, preferred_element_type=F32)], axis=0)
                dsb = _band_split(prob * (_band_merge(dp, prev) - delta), prev)
                sink_grad = jnp.exp(sink_cols[kvh] - m) * inv * delta
                for b, (p, half) in enumerate(stacks[kvh]):
                    dsink = -jnp.sum(sink_grad[b * WINDOW:(b + 1) * WINDOW])
                    dsk = dsk + jnp.where(lane == 2 * p + half, dsink, 0.0)
                dq_lo = jnp.dot(dsb[0:hs2], khalf[kvh][0][rows], preferred_element_type=F32)
                dq_hi = jnp.dot(dsb[hs2:rs], khalf[kvh][1][rows], preferred_element_type=F32)
                dka[kvh] = lax.dot_general(dsb, qs, tn_dims, preferred_element_type=F32)
                d_o_half = jnp.concatenate([jnp.where(lo, d_o, 0.0), jnp.where(lo, 0.0, d_o)],
                                           axis=0).astype(BF16)
                dva[kvh] = lax.dot_general(pb, d_o_half, tn_dims, preferred_element_type=F32)
                for b, p in enumerate(pairs):
                    blk = slice(b * WINDOW, (b + 1) * WINDOW)
                    cols = slice(LANES * p, LANES * (p + 1))
                    ag = pr_ref[r0:r0 + WINDOW, dm.o_ag + cols.start:dm.o_ag + cols.stop]
                    dya = dy_ref[r0:r0 + WINDOW, c + cols.start:c + cols.stop]
                    dqpair = (dq_lo[blk] + dq_hi[blk]) * SCALE
                    d_ag = dya * (o_lo[blk] + o_hi[blk]) * _dsilu(ag, _sigmoid(ag))
                    dpr_ref[r0:r0 + WINDOW, dm.o_q + cols.start:dm.o_q + cols.stop] = (
                        dqpair.astype(BF16))
                    dbin_ref[0:1, dm.o_q + cols.start:dm.o_q + cols.stop] += jnp.sum(
                        dqpair, axis=0, keepdims=True)
                    dpr_ref[r0:r0 + WINDOW, dm.o_ag + cols.start:dm.o_ag + cols.stop] = (
                        d_ag.astype(BF16))
                    dbin_ref[0:1, dm.o_ag + cols.start:dm.o_ag + cols.stop] += jnp.sum(
                        d_ag, axis=0, keepdims=True)
            fold = [x + pltpu.roll(x, HEAD_DIM, 1) for x in (dka[0], dka[1], dva[0], dva[1])]
            dkv_ref[r0:r0 + 2 * WINDOW, 0:KV_W] += jnp.where(lo, fold[0], fold[1])
            dkv_ref[r0:r0 + 2 * WINDOW, KV_W:2 * KV_W] += jnp.where(lo, fold[2], fold[3])
        dsk_ref[0:1, :] += dsk
        carry_kv_ref[...] = dkv_ref[0:WINDOW, :]
        emit(dm.o_k, 2 * KV_W, dkv_ref[WINDOW:WINDOW + t, :])

        @pl.when(g == nt - 1)
        def _():
            for j in range(CONV_WIDTH):
                dcw_ref[j:j + 1, :] = jnp.sum(dcw_acc_ref[j], axis=0, keepdims=True)
            dcw_ref[CONV_WIDTH:CONV_ROWS, :] = jnp.zeros((CONV_ROWS - CONV_WIDTH, c), F32)

    rev = lambda g: nt - 1 - g
    vec = pl.BlockSpec((1, c), lambda g: (0, 0))
    const = lambda shape: pl.BlockSpec(shape, lambda g: (0, 0))
    scratch = [pltpu.VMEM((t + HALO, c), F32), pltpu.VMEM((t + WINDOW, 2 * KV_W), F32),
               pltpu.VMEM((t + HALO, c), F32), pltpu.VMEM((t + WINDOW, 2 * KV_W), F32),
               pltpu.VMEM((HALO, c), F32), pltpu.VMEM((WINDOW, 2 * KV_W), F32),
               pltpu.VMEM((SUBLANES - 1, t + HALO, LANES), F32),
               pltpu.VMEM((SUBLANES - 1, t + HALO, LANES), F32),
               pltpu.VMEM((t, c), F32), pltpu.VMEM((CONV_ROWS, SUBLANES, c), F32),
               pltpu.VMEM((rs, LANES), BF16), pltpu.VMEM((rs // 2, LANES), F32),
               pltpu.VMEM((t, c), F32)]
    host = _Hosting(list(copies), 9, 5)
    res = pl.pallas_call(
        _ride(body, host, 9, 5, len(scratch), nt), name="mixer_bwd",
        out_shape=[jax.ShapeDtypeStruct((dm.s, dm.din), BF16),
                   jax.ShapeDtypeStruct((CONV_ROWS, c), F32),
                   jax.ShapeDtypeStruct((8, c), F32),
                   jax.ShapeDtypeStruct((8, dm.din), F32),
                   jax.ShapeDtypeStruct((8, LANES), F32)]
        + host.out_shapes,
        grid=(nt,),
        in_specs=[pl.BlockSpec(memory_space=pltpu.SMEM)] + _mixer_specs(dm, t, rev)
        + [pl.BlockSpec((t, dm.d), lambda g: (rev(g), 0)), pl.BlockSpec((t, c), lambda g: (rev(g), 0)),
           pl.BlockSpec((CONV_ROWS, c), lambda g: (0, 0)), vec, vec] + [_ANY] * len(host.args),
        out_specs=[pl.BlockSpec((t, dm.din), lambda g: (rev(g), 0)),
                   const((CONV_ROWS, c)), const((8, c)), const((8, dm.din)), const((8, LANES))]
        + [_ANY] * len(host.out_shapes),
        scratch_shapes=scratch + host.scratch, input_output_aliases=host.aliases,
        compiler_params=_cparams(("arbitrary",)),
    )(sinks, proj, proj, proj, dymix, conv, conv_w, cln_g, cln_b, *host.args)
    return res[:5], res[5:]


def _outproj_ln(dm, ymix, w_out, b_out, x, ln_g, ln_b, target, tm, copies=()):
    last = target is not None
    d = dm.d
    n_in = 7 if last else 6
    host = _Hosting(list(copies), n_in, 3)
    n_hin, n_hout = len(host.args), len(host.out_shapes)
    steps = dm.s // tm

    def body(*refs):
        y_ref, w_ref, bo_ref, x_ref, g_ref, b_ref = refs[:6]
        h_in = refs[n_in:n_in + n_hin]
        h_out = refs[n_in + n_hin + 3:n_in + n_hin + 3 + n_hout]
        h_sems = refs[n_in + n_hin + 3 + n_hout:]
        if copies:
            @pl.when(pl.program_id(0) == 0)
            def _():
                host.start(h_in, h_out, h_sems)

        z = dm.alpha * x_ref[...] + (
            jnp.dot(y_ref[...], w_ref[...], preferred_element_type=F32) + bo_ref[...])
        mu = jnp.mean(z, axis=1, keepdims=True)
        dlt = z - mu
        var = jnp.mean(dlt * dlt, axis=1, keepdims=True)
        out = dlt * lax.rsqrt(var + LN_EPS) * g_ref[...] + b_ref[...]
        if last:
            t_ref = refs[6]
            z_ref, dout_ref, loss_ref = refs[n_in + n_hin:n_in + n_hin + 3]
            z_ref[...] = z
            err = out - t_ref[...]
            dout_ref[...] = err * (1.0 / d)

            @pl.when(pl.program_id(0) == 0)
            def _():
                loss_ref[...] = jnp.zeros_like(loss_ref)

            loss_ref[...] += 0.5 * jnp.sum(jnp.mean(err * err, axis=1, keepdims=True), axis=0,
                                           keepdims=True)
        else:
            z_ref, o_ref, ob_ref = refs[n_in + n_hin:n_in + n_hin + 3]
            z_ref[...] = z
            o_ref[...] = out
            ob_ref[...] = out.astype(BF16)

        if copies:
            @pl.when(pl.program_id(0) == steps - 1)
            def _():
                host.finish(h_in, h_out, h_sems)

    row = pl.BlockSpec((tm, d), lambda i: (i, 0))
    vec = pl.BlockSpec((1, d), lambda i: (0, 0))
    in_specs = [row, pl.BlockSpec((d, d), lambda i: (0, 0)), vec, row, vec, vec]
    args = [ymix, w_out, b_out, x, ln_g, ln_b]
    act = jax.ShapeDtypeStruct((dm.s, d), F32)
    if last:
        in_specs.append(row)
        args.append(target)
        out_shape = [act, act, jax.ShapeDtypeStruct((8, LANES), F32)]
        out_specs = [row, row, pl.BlockSpec((8, LANES), lambda i: (0, 0))]
    else:
        out_shape = [act, act, jax.ShapeDtypeStruct((dm.s, d), BF16)]
        out_specs = [row, row, row]
    res = pl.pallas_call(
        body, name="outproj_ln_loss" if last else "outproj_ln",
        out_shape=out_shape + host.out_shapes, grid=(steps,),
        in_specs=in_specs + [_ANY] * n_hin, out_specs=out_specs + [_ANY] * n_hout,
        scratch_shapes=host.scratch, input_output_aliases=host.aliases,
        compiler_params=_cparams(("arbitrary",)),
    )(*args, *host.args)
    return res[:3], res[3:]


def _ln_bwd(dm, dout, z, ln_g, w_out, tm, copies=()):
    d = dm.d
    host = _Hosting(list(copies), 4, 4)

    def body(do_ref, z_ref, g_ref, w_ref, dz_ref, dzb_ref, sm_ref, dy_ref):
        @pl.when(pl.program_id(0) == 0)
        def _():
            sm_ref[...] = jnp.zeros_like(sm_ref)

        z = z_ref[...]
        mu = jnp.mean(z, axis=1, keepdims=True)
        dlt = z - mu
        var = jnp.mean(dlt * dlt, axis=1, keepdims=True)
        rstd = lax.rsqrt(var + LN_EPS)
        zhat = dlt * rstd
        do = do_ref[...]
        dzh = do * g_ref[...]
        dz = rstd * (dzh - jnp.mean(dzh, axis=1, keepdims=True)
                     - zhat * jnp.mean(dzh * zhat, axis=1, keepdims=True))
        dzb = dz.astype(BF16)
        dz_ref[...] = dz
        dzb_ref[...] = dzb
        dy_ref[...] = lax.dot_general(dzb, w_ref[...], (((1,), (1,)), ((), ())),
                                      preferred_element_type=F32)
        sm_ref[0:1, :] += jnp.sum(do * zhat, axis=0, keepdims=True)
        sm_ref[1:2, :] += jnp.sum(do, axis=0, keepdims=True)
        sm_ref[2:3, :] += jnp.sum(dz, axis=0, keepdims=True)

    row = pl.BlockSpec((tm, d), lambda i: (i, 0))
    act = jax.ShapeDtypeStruct((dm.s, d), F32)
    res = pl.pallas_call(
        _ride(body, host, 4, 4, 0, dm.s // tm), name="ln_bwd_dymix",
        out_shape=[act, jax.ShapeDtypeStruct((dm.s, d), BF16), jax.ShapeDtypeStruct((8, d), F32), act]
        + host.out_shapes,
        grid=(dm.s // tm,),
        in_specs=[row, row, pl.BlockSpec((1, d), lambda i: (0, 0)),
                  pl.BlockSpec((d, d), lambda i: (0, 0))] + [_ANY] * len(host.args),
        out_specs=[row, row, pl.BlockSpec((8, d), lambda i: (0, 0)), row]
        + [_ANY] * len(host.out_shapes),
        scratch_shapes=host.scratch, input_output_aliases=host.aliases,
        compiler_params=_cparams(("arbitrary",)),
    )(dout, z, ln_g, w_out, *host.args)
    return res[:4], res[4:]


def _pair_sum(part, got, tr):
    _, _, r, w = part.shape

    def body(c_ref, p_ref, g_ref, o_ref):
        o_ref[...] = (p_ref[...] + g_ref[...]).astype(BF16)

    return pl.pallas_call(
        body, name="grad_pair_sum",
        out_shape=jax.ShapeDtypeStruct((N_CHIPS, r, w), BF16),
        grid_spec=pltpu.PrefetchScalarGridSpec(
            num_scalar_prefetch=1, grid=(N_CHIPS, r // tr),
            in_specs=[pl.BlockSpec((None, None, tr, w), lambda k, i, c_ref: (k, c_ref[0], i, 0)),
                      pl.BlockSpec((None, tr, w), lambda k, i, c_ref: (k, i, 0))],
            out_specs=pl.BlockSpec((None, tr, w), lambda k, i, c_ref: (k, i, 0))),
        compiler_params=_cparams(("parallel", "parallel")),
    )(lax.axis_index("c").reshape(1).astype(jnp.int32), part, got)


def _final_sum(part, got, recv, tr):
    _, _, r, w = part.shape

    def body(idx_ref, p_ref, g_ref, r0_ref, r1_ref, r2_ref, o_ref):
        acc = p_ref[...] + g_ref[...]
        for ref in (r0_ref, r1_ref, r2_ref):
            acc = acc + ref[...].astype(F32)
        o_ref[...] = acc

    x, y, c = _position()
    idx = jnp.stack([c, 2 * x + y, 2 * (1 - x) + y, 2 * x + (1 - y),
                     2 * (1 - x) + (1 - y)]).astype(jnp.int32)
    other = lambda j: pl.BlockSpec((None, tr, w), lambda i, s: (s[2 + j], i, 0))
    return pl.pallas_call(
        body, name="grad_final_sum",
        out_shape=jax.ShapeDtypeStruct((r, w), F32),
        grid_spec=pltpu.PrefetchScalarGridSpec(
            num_scalar_prefetch=1, grid=(r // tr,),
            in_specs=[pl.BlockSpec((None, None, tr, w), lambda i, s: (s[1], s[0], i, 0)),
                      pl.BlockSpec((None, tr, w), lambda i, s: (s[1], i, 0)),
                      other(0), other(1), other(2)],
            out_specs=pl.BlockSpec((tr, w), lambda i, s: (i, 0))),
        compiler_params=_cparams(("parallel",)),
    )(idx, part, got, recv, recv, recv)


def _adamw_math(w, g, m, v):
    m = ADAM_B1 * m + (1.0 - ADAM_B1) * g
    v = ADAM_B2 * v + (1.0 - ADAM_B2) * (g * g)
    m_hat = m / (1.0 - ADAM_B1 ** ADAM_STEP)
    v_hat = v / (1.0 - ADAM_B2 ** ADAM_STEP)
    delta = -ADAM_LR * (m_hat / (jnp.sqrt(v_hat) + ADAM_EPS) + ADAM_WD * w)
    return delta, m, v


def _adamw(w, g, m, v, tr):
    depth, r, width = w.shape

    def body(w_ref, g_ref, m_ref, v_ref, d_ref, nm_ref, nv_ref):
        d_ref[...], nm_ref[...], nv_ref[...] = _adamw_math(
            w_ref[...], g_ref[...], m_ref[...], v_ref[...])

    spec = pl.BlockSpec((None, tr, width), lambda l, i: (l, i, 0))
    shape = jax.ShapeDtypeStruct(w.shape, F32)
    return pl.pallas_call(
        body, name="adamw", out_shape=(shape, shape, shape), grid=(depth, r // tr),
        in_specs=[spec] * 4, out_specs=(spec, spec, spec),
        compiler_params=_cparams(("parallel", "parallel")),
    )(w, g, m, v)


def _gather_direct(block):
    def copies(ins, outs, send, recv):
        x, y, c = _position()
        out = []
        for k in range(1, N_DEV):
            peer = (1 - x if k & 4 else x, 1 - y if k & 2 else y, 1 - c if k & 1 else c)
            out.append((pltpu.make_async_remote_copy(
                src_ref=ins[0], dst_ref=outs[0].at[4 * x + 2 * y + c],
                send_sem=send.at[0, k - 1], recv_sem=recv.at[0, k - 1],
                device_id=peer, device_id_type=MESH), k - 1, peer))
        local = pltpu.make_async_copy(ins[0], outs[0].at[4 * x + 2 * y + c], send.at[0, N_DEV - 1])
        return out, local

    def start(ins, outs, send, recv):
        remote, local = copies(ins, outs, send, recv)
        local.start()
        for cp, _, _ in remote:
            cp.start()

    def finish(ins, outs, send, recv):
        remote, local = copies(ins, outs, send, recv)
        x, y, c = _position()
        for _, k, (px, py, pc) in remote:
            pltpu.make_async_remote_copy(
                src_ref=ins[0], dst_ref=outs[0].at[4 * px + 2 * py + pc],
                send_sem=send.at[0, k], recv_sem=recv.at[0, k],
                device_id=(x, y, c), device_id_type=MESH).wait_recv()
        for cp, _, _ in remote:
            cp.wait_send()
        local.wait()

    landing = [jax.ShapeDtypeStruct((N_DEV,) + block.shape, block.dtype)]
    return _Copies([block], landing, {}, (1, N_DEV), start, finish)


def _small_sum_adamw(gathered, w, m, v):
    prows = w.shape[0]

    def body(g_ref, w_ref, m_ref, v_ref, tot_ref, d_ref, nm_ref, nv_ref):
        tot = g_ref[0]
        for d in range(1, N_DEV):
            tot = tot + g_ref[d]
        tot_ref[...] = tot
        d_ref[...], nm_ref[...], nv_ref[...] = _adamw_math(
            w_ref[...], tot[0:prows, :], m_ref[...], v_ref[...])

    vm = pl.BlockSpec(memory_space=pltpu.VMEM)
    pshape = jax.ShapeDtypeStruct(w.shape, F32)
    return pl.pallas_call(
        body, name="small_sum_adamw",
        out_shape=(jax.ShapeDtypeStruct(gathered.shape[1:], F32), pshape, pshape, pshape),
        in_specs=[vm] * 4, out_specs=(vm, vm, vm, vm),
        compiler_params=pltpu.CompilerParams(vmem_limit_bytes=VMEM_LIMIT),
    )(gathered, w, m, v)


def _pack_rows(vec):
    depth, n = vec.shape
    rows = -(-n // LANES)
    rows = -(-rows // 8) * 8
    return jnp.pad(vec, ((0, 0), (0, rows * LANES - n))).reshape(depth, rows, LANES)


def _pack_small(named):
    blocks = [_pack_rows(a) for a in named]
    extents = [(b.shape[1], a.shape[1]) for b, a in zip(blocks, named)]
    depth = named[0].shape[0]
    packed = jnp.concatenate(blocks, axis=1).reshape(depth * sum(r for r, _ in extents), LANES)
    return packed, extents


def _unpack_small(packed, extents, depth):
    per_layer = sum(r for r, _ in extents)
    packed = packed.reshape(depth, per_layer, LANES)
    out, r0 = [], 0
    for rows, n in extents:
        out.append(packed[:, r0:r0 + rows, :].reshape(depth, rows * LANES)[:, :n])
        r0 += rows
    return out


def kernel(x, w_in, b_in, conv_w, conv_b, conv_ln_g, conv_ln_b, sinks, w_out, b_out, ln_g, ln_b, loss_target, m_w_in, m_b_in, m_conv_w, m_conv_b, m_conv_ln_g, m_conv_ln_b, m_sinks, m_w_out, m_b_out, m_ln_g, m_ln_b, v_w_in, v_b_in, v_conv_w, v_conv_b, v_conv_ln_g, v_conv_ln_b, v_sinks, v_w_out, v_b_out, v_ln_g, v_ln_b):
    depth, d, din_shard = w_in.shape
    s = x.shape[1]
    c_shard = conv_w.shape[2]
    dm = _Dims(s, d, N_DEV * c_shard, depth)
    assert dm.din == N_DEV * din_shard and x.shape[0] == 1 and sinks.shape[1] == dm.nh
    d_shard = w_out.shape[1]
    c, din = dm.c, dm.din

    t_mix = _tile(s, 256, WINDOW)
    tm_row = _tile(s, 256, 8)
    tm_big = _tile(s, 1024, 8)

    w_in_t, m_w_in_t, v_w_in_t = (a.transpose(0, 2, 1) for a in (w_in, m_w_in, v_w_in))
    w_in_b, w_out_b = w_in_t.astype(BF16), w_out.astype(BF16)
    conv_w_pad = jnp.pad(conv_w, ((0, 0), (0, CONV_ROWS - CONV_WIDTH), (0, 0)))
    first = _run_copies([_gather_own([w_in_b[0], conv_w_pad])], "weights_gather_own")
    g_in, g_conv = _run_copies([_gather_forward(first)], "weights_gather_forward")
    g_out = None
    conv_w_full = g_conv.transpose(1, 2, 0, 3).reshape(depth, CONV_ROWS, c)
    by_cols = lambda g: g.reshape(din, d)
    by_rows = lambda g: g.reshape(d, d)

    xs = x[0]
    xb = xs.astype(BF16)
    saved = []
    loss_part = dout = None
    for l in range(depth):
        w_in_l = by_cols(g_in)
        riders = [_gather_own([w_out_b[0]])] if l == 0 else [_gather_forward([g_out])]
        proj, (g_out,) = _matmul(xb, w_in_l, tb=True, tm=tm_big, tn=_tile(din, 768), tk=d,
                                 out_dtype=F32, name="in_proj", bias=b_in[l][None, :],
                                 copies=riders)
        riders = [_gather_forward([g_out])] if l == 0 else []
        if l + 1 < depth:
            riders.append(_gather_own([w_in_b[l + 1]]))
        (ymix, conv), landed = _mixer_fwd(dm, proj, conv_w_full[l], conv_b[l][None, :],
                                          conv_ln_g[l][None, :], conv_ln_b[l][None, :], sinks[l],
                                          t_mix, copies=riders)
        if l == 0:
            g_out = landed[0]
        g_in_next = landed[-1] if l + 1 < depth else None
        w_out_l = by_rows(g_out)
        target = loss_target[0] if l == depth - 1 else None
        riders = []
        if l + 1 < depth:
            riders = [_gather_forward([g_in_next]), _gather_own([w_out_b[l + 1]])]
        res, landed = _outproj_ln(dm, ymix, w_out_l, b_out[l][None, :], xs, ln_g[l][None, :],
                                  ln_b[l][None, :], target, tm_row, copies=riders)
        saved.append((xb, proj, ymix, conv, res[0], w_in_l, w_out_l))
        if l + 1 < depth:
            g_in, g_out = landed
            xs, xb = res[1], res[2]
        else:
            dout, loss_part = res[1], res[2]

    g_w_in, g_w_out = [None] * depth, [None] * depth
    small_parts = [None] * depth
    dconv_w = [None] * depth
    tr_in, tr_out = _tile(din_shard, 512, 8), _tile(d_shard, 256, 8)
    parts_in = None
    for l in reversed(range(depth)):
        xb, proj, ymix, conv, z, w_in_l, w_out_l = saved[l]
        riders = [_scatter_sibling([parts_in])] if parts_in is not None else []
        (dz, dzb, ln_small, dymix), landed = _ln_bwd(dm, dout, z, ln_g[l][None, :], w_out_l, tm_row,
                                                     copies=riders)
        if parts_in is not None:
            got_in = landed[0]
            sums_in = _pair_sum(parts_in, got_in, tr_in)
        dw_out = _matmul(ymix, dzb, ta=True, tm=_tile(d, 1024), tn=_tile(d, 1024), tk=s,
                         out_dtype=F32, name="dw_out")
        riders = [_scatter_chips([sums_in])] if parts_in is not None else []
        (dproj, dcw, conv_small, dbin, dsk), landed = _mixer_bwd(
            dm, proj, dymix, conv, conv_w_full[l], conv_ln_g[l][None, :], conv_ln_b[l][None, :],
            sinks[l], t_mix, copies=riders)
        if parts_in is not None:
            g_w_in[l + 1] = _final_sum(parts_in, got_in, landed[0], tr_in)
        small_parts[l] = [dbin[0], conv_small[0], conv_small[1], conv_small[2], dsk[0, :dm.nh],
                          ln_small[2], ln_small[0], ln_small[1]]
        dconv_w[l] = dcw
        parts_out = dw_out.reshape(N_CHIPS, 2, d_shard, d)
        dx = dict(tm=_tile(s, 512), tn=_tile(d, 1024), tk=din, out_dtype=F32, name="dx", resid=dz,
                  resid_scale=dm.alpha)
        dw = dict(ta=True, tm=_tile(din, 768), tn=_tile(d, 1024), tk=s, out_dtype=F32, name="dw_in")
        if l > 0:
            dout, (got_out,) = _matmul(dproj, w_in_l, copies=[_scatter_sibling([parts_out])], **dx)
            sums_out = _pair_sum(parts_out, got_out, tr_out)
            dw_in_t, (recv_out,) = _matmul(dproj, xb, copies=[_scatter_chips([sums_out])], **dw)
            parts_in = dw_in_t.reshape(N_CHIPS, 2, din_shard, d)
        else:
            packed_g, extents = _pack_small([jnp.stack([small_parts[k][n] for k in range(depth)])
                                             for n in range(len(small_parts[0]))])
            prows = packed_g.shape[0]
            conv_rows = depth * CONV_ROWS * c // LANES
            partial = jnp.concatenate(
                [packed_g, jnp.stack(dconv_w).reshape(conv_rows, LANES), loss_part], axis=0)
            dw_in_t, (got_out, small_all) = _matmul(
                dproj, xb, copies=[_scatter_sibling([parts_out]), _gather_direct(partial)], **dw)
            parts_in = dw_in_t.reshape(N_CHIPS, 2, din_shard, d)
            sums_out = _pair_sum(parts_out, got_out, tr_out)
            got_in, recv_out = _run_copies(
                [_scatter_sibling([parts_in]), _scatter_chips([sums_out])], "grad_exchange_tail")
            sums_in = _pair_sum(parts_in, got_in, tr_in)
            dout, (recv_in,) = _matmul(dproj, w_in_l, copies=[_scatter_chips([sums_in])], **dx)
            g_w_in[0] = _final_sum(parts_in, got_in, recv_in, tr_in)
        g_w_out[l] = _final_sum(parts_out, got_out, recv_out, tr_out)
    grad_x = dout[None]

    small_w = [b_in, conv_b, conv_ln_g, conv_ln_b, sinks, b_out, ln_g, ln_b]
    small_m = [m_b_in, m_conv_b, m_conv_ln_g, m_conv_ln_b, m_sinks, m_b_out, m_ln_g, m_ln_b]
    small_v = [v_b_in, v_conv_b, v_conv_ln_g, v_conv_ln_b, v_sinks, v_b_out, v_ln_g, v_ln_b]
    packed_w, _ = _pack_small(small_w)
    packed_m, _ = _pack_small(small_m)
    packed_v, _ = _pack_small(small_v)
    total, sm_delta, sm_m, sm_v = _small_sum_adamw(small_all, packed_w, packed_m, packed_v)
    loss = total[prows + conv_rows, 0]
    dconv_w_full = total[prows:prows + conv_rows].reshape(depth, CONV_ROWS, c)
    me = 4 * lax.axis_index("x") + 2 * lax.axis_index("y") + lax.axis_index("c")
    grad_conv_w = lax.dynamic_slice_in_dim(dconv_w_full, me * c_shard, c_shard, axis=2)[:, :CONV_WIDTH]

    grads_small = _unpack_small(total[:prows], extents, depth)
    delta_small = _unpack_small(sm_delta, extents, depth)
    newm_small = _unpack_small(sm_m, extents, depth)
    newv_small = _unpack_small(sm_v, extents, depth)

    grad_w_in_t = jnp.stack(g_w_in)
    grad_w_out = jnp.stack(g_w_out)
    grad_w_in, d_w_in, nm_w_in, nv_w_in = (a.transpose(0, 2, 1) for a in (
        grad_w_in_t, *_adamw(w_in_t, grad_w_in_t, m_w_in_t, v_w_in_t, tr_in)))
    d_w_out, nm_w_out, nv_w_out = _adamw(w_out, grad_w_out, m_w_out, v_w_out, _tile(d_shard, 256, 8))
    d_cw, nm_cw, nv_cw = _adamw(conv_w, grad_conv_w, m_conv_w, v_conv_w, CONV_WIDTH)

    def assemble(w_in_leaf, conv_w_leaf, w_out_leaf, small):
        b_in_, conv_b_, cg_, cb_, sinks_, b_out_, ln_g_, ln_b_ = small
        return [w_in_leaf, b_in_, conv_w_leaf, conv_b_, cg_, cb_, sinks_, w_out_leaf, b_out_,
                ln_g_, ln_b_]

    return (loss, grad_x,
            *assemble(grad_w_in, grad_conv_w, grad_w_out, grads_small),
            *assemble(d_w_in, d_cw, d_w_out, delta_small),
            *assemble(nm_w_in, nm_cw, nm_w_out, newm_small),
            *assemble(nv_w_in, nv_cw, nv_w_out, newv_small))
```

```python
import functools

import jax
import jax.numpy as jnp
from jax import lax
from jax.experimental import pallas as pl
from jax.experimental.pallas import tpu as pltpu

F32 = jnp.float32
BF16 = jnp.bfloat16
MESH = pl.DeviceIdType.MESH

N_DEV = 8
N_CHIPS = 4
HEAD_DIM = 64
N_KV = 2
KV_W = N_KV * HEAD_DIM
CONV_WIDTH = 31
CONV_ROWS = 32
HALO = 32
WINDOW = 128
LN_EPS = 1e-5
NEG_INF = -1e30
LANES = 128

ADAM_LR = 0.001
ADAM_B1 = 0.9
ADAM_B2 = 0.999
ADAM_EPS = 1e-08
ADAM_WD = 0.01
ADAM_STEP = 10

VMEM_LIMIT = 56 * 1024 * 1024


def _tile(n, target, align=LANES):
    best = None
    for t in range(align, min(n, target) + 1, align):
        if n % t == 0:
            best = t
    return n if best is None else best


def _sigmoid(x):
    return jax.nn.sigmoid(x)


def _dsilu(x, s):
    return s * (1.0 + x * (1.0 - s))


def _cparams(sem, vmem=VMEM_LIMIT):
    return pltpu.CompilerParams(dimension_semantics=sem, vmem_limit_bytes=vmem)


_ANY = pl.BlockSpec(memory_space=pl.ANY)


def _position():
    return lax.axis_index("x"), lax.axis_index("y"), lax.axis_index("c")


def _other_chips(x, y):
    return [(1 - x, y), (x, 1 - y), (1 - x, 1 - y)]


class _Copies:
    def __init__(self, operands, landing, alias, sems, start, finish):
        self.operands, self.landing, self.alias, self.sems = operands, landing, alias, sems
        self.start, self.finish = start, finish


class _Hosting:
    def __init__(self, groups, n_in, n_out):
        self.groups = groups
        self.args = [a for g in groups for a in g.operands]
        self.out_shapes = [s for g in groups for s in g.landing]
        self.scratch = [pltpu.SemaphoreType.DMA(g.sems) for g in groups for _ in range(2)]
        self.aliases = {}
        i0, o0 = n_in, n_out
        for g in groups:
            for a, b in g.alias.items():
                self.aliases[i0 + a] = o0 + b
            i0 += len(g.operands)
            o0 += len(g.landing)

    def _each(self, in_refs, out_refs, sem_refs):
        i0 = o0 = 0
        for n, g in enumerate(self.groups):
            yield (g, in_refs[i0:i0 + len(g.operands)], out_refs[o0:o0 + len(g.landing)],
                   sem_refs[2 * n], sem_refs[2 * n + 1])
            i0 += len(g.operands)
            o0 += len(g.landing)

    def start(self, in_refs, out_refs, sem_refs):
        for g, ins, outs, send, recv in self._each(in_refs, out_refs, sem_refs):
            g.start(ins, outs, send, recv)

    def finish(self, in_refs, out_refs, sem_refs):
        for g, ins, outs, send, recv in self._each(in_refs, out_refs, sem_refs):
            g.finish(ins, outs, send, recv)


def _ride(body, host, n_in, n_out, n_scratch, steps):
    n_hin, n_hout = len(host.args), len(host.out_shapes)

    def wrapped(*refs):
        pos = [0]

        def take(n):
            pos[0] += n
            return refs[pos[0] - n:pos[0]]

        ins, h_in, outs, h_out = take(n_in), take(n_hin), take(n_out), take(n_hout)
        scratch, h_sems = take(n_scratch), refs[pos[0]:]
        if host.groups:
            @pl.when(pl.program_id(0) == 0)
            def _():
                host.start(h_in, h_out, h_sems)

        body(*ins, *outs, *scratch)
        if host.groups:
            @pl.when(pl.program_id(0) == steps - 1)
            def _():
                host.finish(h_in, h_out, h_sems)

    return wrapped


def _run_copies(groups, name):
    host = _Hosting(groups, 0, 0)
    n_in, n_out = len(host.args), len(host.out_shapes)

    def body(*refs):
        ins, outs, sems = refs[:n_in], refs[n_in:n_in + n_out], refs[n_in + n_out:]
        host.start(ins, outs, sems)
        host.finish(ins, outs, sems)

    return pl.pallas_call(
        body, name=name, out_shape=host.out_shapes, in_specs=[_ANY] * n_in,
        out_specs=[_ANY] * n_out, scratch_shapes=host.scratch,
        input_output_aliases=host.aliases,
    )(*host.args)


def _gather_own(shards):
    n = len(shards)

    def copies(ins, outs, send, recv):
        x, y, c = _position()
        peers = [(x, y, 1 - c)] + [(px, py, c) for px, py in _other_chips(x, y)]
        out = []
        for t in range(n):
            for k, peer in enumerate(peers):
                out.append((pltpu.make_async_remote_copy(
                    src_ref=ins[t], dst_ref=outs[t].at[4 * x + 2 * y + c],
                    send_sem=send.at[t, k], recv_sem=recv.at[t, k],
                    device_id=peer, device_id_type=MESH), t, k, peer))
        local = [pltpu.make_async_copy(ins[t], outs[t].at[4 * x + 2 * y + c], send.at[t, 4])
                 for t in range(n)]
        return out, local

    def start(ins, outs, send, recv):
        remote, local = copies(ins, outs, send, recv)
        for cp in local:
            cp.start()
        for cp, _, _, _ in remote:
            cp.start()

    def finish(ins, outs, send, recv):
        remote, local = copies(ins, outs, send, recv)
        x, y, c = _position()
        for _, t, k, (px, py, pc) in remote:
            pltpu.make_async_remote_copy(
                src_ref=ins[t], dst_ref=outs[t].at[4 * px + 2 * py + pc],
                send_sem=send.at[t, k], recv_sem=recv.at[t, k],
                device_id=(x, y, c), device_id_type=MESH).wait_recv()
        for cp, _, _, _ in remote:
            cp.wait_send()
        for cp in local:
            cp.wait()

    landing = [jax.ShapeDtypeStruct((N_DEV,) + s.shape, s.dtype) for s in shards]
    return _Copies(list(shards), landing, {}, (n, 5), start, finish)


def _gather_forward(buffers):
    n = len(buffers)

    def copies(ins, outs, send, recv):
        x, y, c = _position()
        out = []
        for t in range(n):
            for j, (px, py) in enumerate(_other_chips(x, y)):
                slot = 4 * px + 2 * py + c
                out.append((pltpu.make_async_remote_copy(
                    src_ref=ins[t].at[slot], dst_ref=outs[t].at[slot],
                    send_sem=send.at[t, j], recv_sem=recv.at[t, j],
                    device_id=(x, y, 1 - c), device_id_type=MESH), t, j, 4 * px + 2 * py + 1 - c))
        return out

    def start(ins, outs, send, recv):
        for cp, _, _, _ in copies(ins, outs, send, recv):
            cp.start()

    def finish(ins, outs, send, recv):
        x, y, c = _position()
        mine = copies(ins, outs, send, recv)
        for _, t, j, got in mine:
            pltpu.make_async_remote_copy(
                src_ref=ins[t].at[got], dst_ref=outs[t].at[got],
                send_sem=send.at[t, j], recv_sem=recv.at[t, j],
                device_id=(x, y, c), device_id_type=MESH).wait_recv()
        for cp, _, _, _ in mine:
            cp.wait_send()

    landing = [jax.ShapeDtypeStruct(b.shape, b.dtype) for b in buffers]
    return _Copies(list(buffers), landing, {t: t for t in range(n)}, (n, 3), start, finish)


def _scatter_sibling(parts):
    n = len(parts)

    def copies(ins, outs, send, recv):
        x, y, c = _position()
        return [pltpu.make_async_remote_copy(
            src_ref=ins[t].at[:, 1 - c], dst_ref=outs[t],
            send_sem=send.at[t, 0], recv_sem=recv.at[t, 0],
            device_id=(x, y, 1 - c), device_id_type=MESH) for t in range(n)]

    def start(ins, outs, send, recv):
        for cp in copies(ins, outs, send, recv):
            cp.start()

    def finish(ins, outs, send, recv):
        for cp in copies(ins, outs, send, recv):
            cp.wait()

    landing = [jax.ShapeDtypeStruct((p.shape[0],) + p.shape[2:], p.dtype) for p in parts]
    return _Copies(list(parts), landing, {}, (n, 1), start, finish)


def _scatter_chips(blocks):
    n = len(blocks)

    def start(ins, outs, send, recv):
        x, y, c = _position()
        for t in range(n):
            for j, (px, py) in enumerate(_other_chips(x, y)):
                pltpu.make_async_remote_copy(
                    src_ref=ins[t].at[2 * px + py], dst_ref=outs[t].at[2 * x + y],
                    send_sem=send.at[t, j], recv_sem=recv.at[t, j],
                    device_id=(px, py, c), device_id_type=MESH).start()

    def finish(ins, outs, send, recv):
        x, y, c = _position()
        for t in range(n):
            for j, (px, py) in enumerate(_other_chips(x, y)):
                cp = pltpu.make_async_remote_copy(
                    src_ref=ins[t].at[2 * px + py], dst_ref=outs[t].at[2 * px + py],
                    send_sem=send.at[t, j], recv_sem=recv.at[t, j],
                    device_id=(px, py, c), device_id_type=MESH)
                cp.wait_recv()
                cp.wait_send()

    landing = [jax.ShapeDtypeStruct(b.shape, b.dtype) for b in blocks]
    return _Copies(list(blocks), landing, {}, (n, 3), start, finish)


def _matmul(a, b, *, ta=False, tb=False, tm, tn, tk, out_dtype, name, bias=None, resid=None,
            resid_scale=1.0, copies=()):
    m, k = (a.shape[1], a.shape[0]) if ta else a.shape
    n = b.shape[0] if tb else b.shape[1]
    assert (b.shape[1] if tb else b.shape[0]) == k
    assert m % tm == 0 and n % tn == 0 and k % tk == 0
    ni, nj, nk = m // tm, n // tn, k // tk
    dn = (((0 if ta else 1,), (1 if tb else 0,)), ((), ()))
    n_in = 2 + (bias is not None) + (resid is not None)
    host = _Hosting(list(copies), n_in, 1)
    n_hin, n_hout = len(host.args), len(host.out_shapes)

    def body(*refs):
        a_ref, b_ref = refs[0], refs[1]
        pos = 2
        bias_ref = resid_ref = None
        if bias is not None:
            bias_ref = refs[pos]
            pos += 1
        if resid is not None:
            resid_ref = refs[pos]
            pos += 1
        h_in = refs[pos:pos + n_hin]
        pos += n_hin
        o_ref = refs[pos]
        h_out = refs[pos + 1:pos + 1 + n_hout]
        pos += 1 + n_hout
        acc_ref = refs[pos] if nk > 1 else None
        h_sems = refs[pos + (nk > 1):]
        step = (pl.program_id(0) * nj + pl.program_id(1)) * nk + pl.program_id(2)

        if copies:
            @pl.when(step == 0)
            def _():
                host.start(h_in, h_out, h_sems)

        def finish(acc):
            if bias_ref is not None:
                acc = acc + bias_ref[...]
            if resid_ref is not None:
                acc = acc + resid_scale * resid_ref[...]
            o_ref[...] = acc.astype(out_dtype)

        p = lax.dot_general(a_ref[...], b_ref[...], dn, preferred_element_type=F32)
        if nk == 1:
            finish(p)
        else:
            kk = pl.program_id(2)

            @pl.when(kk == 0)
            def _():
                acc_ref[...] = p

            @pl.when(kk > 0)
            def _():
                acc_ref[...] += p

            @pl.when(kk == nk - 1)
            def _():
                finish(acc_ref[...])

        if copies:
            @pl.when(step == ni * nj * nk - 1)
            def _():
                host.finish(h_in, h_out, h_sems)

    a_spec = (pl.BlockSpec((tk, tm), lambda i, j, kk: (kk, i)) if ta
              else pl.BlockSpec((tm, tk), lambda i, j, kk: (i, kk)))
    b_spec = (pl.BlockSpec((tn, tk), lambda i, j, kk: (j, kk)) if tb
              else pl.BlockSpec((tk, tn), lambda i, j, kk: (kk, j)))
    in_specs = [a_spec, b_spec]
    args = [a, b]
    if bias is not None:
        in_specs.append(pl.BlockSpec((1, tn), lambda i, j, kk: (0, j)))
        args.append(bias)
    if resid is not None:
        in_specs.append(pl.BlockSpec((tm, tn), lambda i, j, kk: (i, j)))
        args.append(resid)
    res = pl.pallas_call(
        body, name=name,
        out_shape=[jax.ShapeDtypeStruct((m, n), out_dtype)] + host.out_shapes,
        grid=(ni, nj, nk),
        in_specs=in_specs + [_ANY] * n_hin,
        out_specs=[pl.BlockSpec((tm, tn), lambda i, j, kk: (i, j))] + [_ANY] * n_hout,
        scratch_shapes=([pltpu.VMEM((tm, tn), F32)] if nk > 1 else []) + host.scratch,
        input_output_aliases=host.aliases,
        compiler_params=_cparams(("arbitrary",) * 3 if copies else
                                 ("parallel", "parallel", "arbitrary")),
    )(*args, *host.args)
    return (res[0], res[1:]) if copies else res[0]


class _Dims:
    def __init__(self, s, d, c, depth):
        self.s, self.d, self.c, self.depth = s, d, c, depth
        self.a = d - c
        self.nh = self.a // HEAD_DIM
        self.group = self.nh // N_KV
        self.din = 3 * c + 2 * self.a + 2 * KV_W
        self.o_q = 3 * c
        self.o_k = 3 * c + self.a
        self.o_v = self.o_k + KV_W
        self.o_ag = self.o_k + 2 * KV_W
        self.alpha = (2 * depth) ** 0.25
        assert self.nh % 2 == 0 and self.group % 2 == 0 and self.o_k % (2 * KV_W) == 0
        assert c % LANES == 0 and self.a % LANES == 0


SUBLANES = 8
TAP_ROWS = 64
ROW_CHUNK = 64


def _shift_copies(src_ref, sh_ref, lanes, rows):
    for r in range(1, SUBLANES):
        sh_ref[r - 1, 0:rows, :] = src_ref[pl.ds(r, rows), lanes]


def _tap_rows(src_ref, sh_ref, lanes, off, start, rows):
    r = off % SUBLANES
    at = pl.multiple_of(start + (off - r), SUBLANES)
    if r == 0:
        return src_ref[pl.ds(at, rows), lanes]
    return sh_ref[r - 1, pl.ds(at, rows), :]


def _conv_chunk(w_ref, src_ref, sh_ref, out_ref, lanes, t, first_off, reverse, bias_ref=None):
    def block(it, carry):
        start = pl.multiple_of(it * TAP_ROWS, TAP_ROWS)
        acc = None
        for j in range(CONV_WIDTH):
            off = first_off - j if reverse else first_off + j
            term = w_ref[j:j + 1, lanes] * _tap_rows(src_ref, sh_ref, lanes, off, start, TAP_ROWS)
            acc = term if acc is None else acc + term
        if bias_ref is not None:
            acc = acc + bias_ref[:, lanes]
        out_ref[pl.ds(start, TAP_ROWS), lanes] = acc
        return carry

    lax.fori_loop(0, t // TAP_ROWS, block, 0)


def _conv_w_grad_chunk(src_ref, sh_ref, dconv_ref, acc_ref, lanes, t, first_off):
    def block(it, accs):
        start = pl.multiple_of(it * SUBLANES, SUBLANES)
        dv = dconv_ref[pl.ds(start, SUBLANES), lanes]
        return tuple(
            acc + _tap_rows(src_ref, sh_ref, lanes, first_off + j, start, SUBLANES) * dv
            for j, acc in enumerate(accs))

    zero = jnp.zeros((SUBLANES, LANES), F32)
    accs = lax.fori_loop(0, t // SUBLANES, block, (zero,) * CONV_WIDTH, unroll=2)
    for j in range(CONV_WIDTH):
        acc_ref[j, :, lanes] += accs[j]


def _kv_operands(kv_ref, lo):
    kext = kv_ref[:, 0:KV_W]
    vext = kv_ref[:, KV_W:2 * KV_W]
    ksw = pltpu.roll(kext, HEAD_DIM, 1)
    vsw = pltpu.roll(vext, HEAD_DIM, 1)
    zero = jnp.zeros_like(kext)
    k2 = [jnp.where(lo, kext, ksw).astype(BF16), jnp.where(lo, ksw, kext).astype(BF16)]
    khalf = [[jnp.where(lo, kext, zero).astype(BF16), jnp.where(lo, zero, ksw).astype(BF16)],
             [jnp.where(lo, ksw, zero).astype(BF16), jnp.where(lo, zero, kext).astype(BF16)]]
    vhalf = [[jnp.where(lo, vext, zero).astype(BF16), jnp.where(lo, zero, vsw).astype(BF16)],
             [jnp.where(lo, vsw, zero).astype(BF16), jnp.where(lo, zero, vext).astype(BF16)]]
    return k2, khalf, vhalf


SCALE = HEAD_DIM ** -0.5


def _from_previous(rows):
    row = lax.broadcasted_iota(jnp.int32, (rows, WINDOW), 0)
    col = lax.broadcasted_iota(jnp.int32, (rows, WINDOW), 1)
    return col > (row & (WINDOW - 1))


def _sink_column(sinks_ref, heads):
    block = lax.broadcasted_iota(jnp.int32, (len(heads) * WINDOW, 1), 0) // WINDOW
    out = jnp.zeros(block.shape, F32)
    for b, head in enumerate(heads):
        out = jnp.where(block == b, sinks_ref[head], out)
    return out


def _stacked_heads(dm, kvh):
    pairs = range(kvh * (dm.group // 2), (kvh + 1) * (dm.group // 2))
    return [(p, half) for half in range(2) for p in pairs]


def _band_merge(both, prev):
    return jnp.where(prev, both[:, 0:WINDOW], both[:, WINDOW:2 * WINDOW])


def _band_split(tile, prev):
    zero = jnp.zeros_like(tile)
    return jnp.concatenate([jnp.where(prev, tile, zero), jnp.where(prev, zero, tile)],
                           axis=1).astype(BF16)


def _softmax_with_sink(qm, k2rows, prev, no_previous, sink):
    both = lax.dot_general(qm, k2rows, (((1,), (1,)), ((), ())), preferred_element_type=F32)
    s_prev = both[:, 0:WINDOW]
    if no_previous is not None:
        s_prev = jnp.where(no_previous, NEG_INF, s_prev)
    s = jnp.where(prev, s_prev, both[:, WINDOW:2 * WINDOW])
    m = jnp.maximum(jnp.max(s, axis=1, keepdims=True), sink)
    e = jnp.exp(s - m)
    den = jnp.sum(e, axis=1, keepdims=True) + jnp.exp(sink - m)
    inv = 1.0 / den
    return e * inv, m, inv


def _mixer_specs(dm, t, idx):
    return [
        pl.BlockSpec((t, dm.din), lambda g: (idx(g), 0)),
        pl.BlockSpec((HALO, 2 * dm.c), lambda g: (jnp.maximum(idx(g) * (t // HALO) - 1, 0), 0)),
        pl.BlockSpec((WINDOW, 2 * KV_W),
                     lambda g: (jnp.maximum(idx(g) * (t // WINDOW) - 1, 0), dm.o_k // (2 * KV_W))),
    ]


def _mixer_fwd(dm, proj, conv_w, conv_b, cln_g, cln_b, sinks, t, copies=()):
    c, nq = dm.c, t // WINDOW
    rs = dm.group * WINDOW
    host = _Hosting(list(copies), 8, 2)

    def body(sinks_ref, pr_ref, ch_ref, kvh_ref, cw_ref, cb_ref, cg_ref, cbb_ref,
             y_ref, conv_ref, hext_ref, kv_ref, hs_ref, qs_ref):
        i = pl.program_id(0)
        first = i == 0
        h = pr_ref[:, 0:c] * _sigmoid(pr_ref[:, c:2 * c])
        hh = ch_ref[:, 0:c] * _sigmoid(ch_ref[:, c:2 * c])
        hext_ref[0:HALO, :] = jnp.where(first, 0.0, hh)
        hext_ref[HALO:HALO + t, :] = h
        for k in range(c // LANES):
            lanes = slice(LANES * k, LANES * (k + 1))
            _shift_copies(hext_ref, hs_ref, lanes, t + HALO - SUBLANES)
            _conv_chunk(cw_ref, hext_ref, hs_ref, conv_ref, lanes, t, HALO - (CONV_WIDTH - 1),
                        False, cb_ref)
        conv = conv_ref[...]
        mu = jnp.mean(conv, axis=1, keepdims=True)
        dlt = conv - mu
        var = jnp.mean(dlt * dlt, axis=1, keepdims=True)
        u = dlt * lax.rsqrt(var + LN_EPS) * cg_ref[...] + cbb_ref[...]
        gate = pr_ref[:, 2 * c:3 * c]
        y_ref[:, 0:c] = (u * _sigmoid(u) * (gate * _sigmoid(gate))).astype(BF16)

        kv_ref[0:WINDOW, :] = jnp.where(first, 0.0, kvh_ref[...])
        kv_ref[WINDOW:WINDOW + t, :] = pr_ref[:, dm.o_k:dm.o_k + 2 * KV_W]
        lo = lax.broadcasted_iota(jnp.int32, (1, LANES), 1) < HEAD_DIM
        k2, _, vhalf = _kv_operands(kv_ref, lo)
        stacks = [_stacked_heads(dm, kvh) for kvh in range(N_KV)]
        sink_cols = [_sink_column(sinks_ref, [2 * p + half for p, half in st]) for st in stacks]
        prev = _from_previous(rs)
        for qb in range(nq):
            r0 = qb * WINDOW
            rows = slice(r0, r0 + 2 * WINDOW)
            no_previous = first if qb == 0 else None
            for kvh in range(N_KV):
                for b, (p, half) in enumerate(stacks[kvh]):
                    qp = pr_ref[r0:r0 + WINDOW, dm.o_q + LANES * p:dm.o_q + LANES * (p + 1)] * SCALE
                    qs_ref[b * WINDOW:(b + 1) * WINDOW, :] = jnp.where(
                        lo if half == 0 else jnp.logical_not(lo), qp, 0.0).astype(BF16)
                prob, _, _ = _softmax_with_sink(qs_ref[...], k2[kvh][rows], prev, no_previous,
                                                sink_cols[kvh])
                pb = _band_split(prob, prev)
                o_lo = jnp.dot(pb[0:rs // 2], vhalf[kvh][0][rows], preferred_element_type=F32)
                o_hi = jnp.dot(pb[rs // 2:rs], vhalf[kvh][1][rows], preferred_element_type=F32)
                for b, (p, _) in enumerate(stacks[kvh][:len(stacks[kvh]) // 2]):
                    blk = slice(b * WINDOW, (b + 1) * WINDOW)
                    ag = pr_ref[r0:r0 + WINDOW, dm.o_ag + LANES * p:dm.o_ag + LANES * (p + 1)]
                    y_ref[r0:r0 + WINDOW, c + LANES * p:c + LANES * (p + 1)] = (
                        (o_lo[blk] + o_hi[blk]) * (ag * _sigmoid(ag))).astype(BF16)

    vec = pl.BlockSpec((1, c), lambda g: (0, 0))
    scratch = [pltpu.VMEM((t + HALO, c), F32), pltpu.VMEM((t + WINDOW, 2 * KV_W), F32),
               pltpu.VMEM((SUBLANES - 1, t + HALO, LANES), F32), pltpu.VMEM((rs, LANES), BF16)]
    res = pl.pallas_call(
        _ride(body, host, 8, 2, len(scratch), dm.s // t), name="mixer_fwd",
        out_shape=[jax.ShapeDtypeStruct((dm.s, dm.d), BF16),
                   jax.ShapeDtypeStruct((dm.s, c), F32)]
        + host.out_shapes,
        grid=(dm.s // t,),
        in_specs=[pl.BlockSpec(memory_space=pltpu.SMEM)] + _mixer_specs(dm, t, lambda g: g)
        + [pl.BlockSpec((CONV_ROWS, c), lambda g: (0, 0)), vec, vec, vec] + [_ANY] * len(host.args),
        out_specs=[pl.BlockSpec((t, dm.d), lambda g: (g, 0)), pl.BlockSpec((t, c), lambda g: (g, 0))]
        + [_ANY] * len(host.out_shapes),
        scratch_shapes=scratch + host.scratch, input_output_aliases=host.aliases,
        compiler_params=_cparams(("arbitrary",)),
    )(sinks, proj, proj, proj, conv_w, conv_b, cln_g, cln_b, *host.args)
    return res[:2], res[2:]


def _mixer_bwd(dm, proj, dymix, conv, conv_w, cln_g, cln_b, sinks, t, copies=()):
    c, nq, nt = dm.c, t // WINDOW, dm.s // t
    rs = dm.group * WINDOW

    def body(sinks_ref, pr_ref, ch_ref, kvh_ref, dy_ref, cv_ref, cw_ref, cg_ref, cbb_ref,
             dpr_ref, dcw_ref, dsm_ref, dbin_ref, dsk_ref,
             hext_ref, kv_ref, dcx_ref, dkv_ref, carry_dc_ref, carry_kv_ref,
             hs_ref, ds_ref, dh_ref, dcw_acc_ref, qs_ref, do_ref, sg_ref):
        g = pl.program_id(0)
        i = nt - 1 - g
        first = i == 0

        @pl.when(g == 0)
        def _():
            dcw_acc_ref[...] = jnp.zeros_like(dcw_acc_ref)
            dsm_ref[...] = jnp.zeros_like(dsm_ref)
            dbin_ref[...] = jnp.zeros_like(dbin_ref)
            dsk_ref[...] = jnp.zeros_like(dsk_ref)
            carry_dc_ref[...] = jnp.zeros_like(carry_dc_ref)
            carry_kv_ref[...] = jnp.zeros_like(carry_kv_ref)

        def emit(col, width, val, rows=slice(None)):
            dpr_ref[rows, col:col + width] = val.astype(BF16)
            dbin_ref[0:1, col:col + width] += jnp.sum(val, axis=0, keepdims=True)

        hh = ch_ref[:, 0:c] * _sigmoid(ch_ref[:, c:2 * c])
        hext_ref[0:HALO, :] = jnp.where(first, 0.0, hh)
        dcx_ref[t:t + HALO, :] = carry_dc_ref[...]
        for r0 in range(0, t, ROW_CHUNK):
            rows = slice(r0, r0 + ROW_CHUNK)
            sg = _sigmoid(pr_ref[rows, c:2 * c])
            sg_ref[rows, :] = sg
            hext_ref[HALO + r0:HALO + r0 + ROW_CHUNK, :] = pr_ref[rows, 0:c] * sg
            conv = cv_ref[rows, :]
            mu = jnp.mean(conv, axis=1, keepdims=True)
            dlt = conv - mu
            var = jnp.mean(dlt * dlt, axis=1, keepdims=True)
            rstd = lax.rsqrt(var + LN_EPS)
            xhat = dlt * rstd
            u = xhat * cg_ref[...] + cbb_ref[...]
            su = _sigmoid(u)
            gate = pr_ref[rows, 2 * c:3 * c]
            sgate = _sigmoid(gate)
            dyc = dy_ref[rows, 0:c]
            emit(2 * c, c, dyc * (u * su) * _dsilu(gate, sgate), rows)
            du = dyc * (gate * sgate) * _dsilu(u, su)
            dsm_ref[1:2, :] += jnp.sum(du * xhat, axis=0, keepdims=True)
            dsm_ref[2:3, :] += jnp.sum(du, axis=0, keepdims=True)
            dxh = du * cg_ref[...]
            dconv = rstd * (dxh - jnp.mean(dxh, axis=1, keepdims=True)
                            - xhat * jnp.mean(dxh * xhat, axis=1, keepdims=True))
            dsm_ref[0:1, :] += jnp.sum(dconv, axis=0, keepdims=True)
            dcx_ref[rows, :] = dconv
        carry_dc_ref[...] = dcx_ref[0:HALO, :]
        for k in range(c // LANES):
            lanes = slice(LANES * k, LANES * (k + 1))
            _shift_copies(hext_ref, hs_ref, lanes, t + HALO - SUBLANES)
            _shift_copies(dcx_ref, ds_ref, lanes, t + HALO - SUBLANES)
            _conv_chunk(cw_ref, dcx_ref, ds_ref, dh_ref, lanes, t, CONV_WIDTH - 1, True)
            _conv_w_grad_chunk(hext_ref, hs_ref, dcx_ref, dcw_acc_ref, lanes, t,
                               HALO - (CONV_WIDTH - 1))
        for r0 in range(0, t, ROW_CHUNK):
            rows = slice(r0, r0 + ROW_CHUNK)
            dh, sg = dh_ref[rows, :], sg_ref[rows, :]
            emit(0, c, dh * sg, rows)
            emit(c, c, dh * pr_ref[rows, 0:c] * sg * (1.0 - sg), rows)

        kv_ref[0:WINDOW, :] = jnp.where(first, 0.0, kvh_ref[...])
        kv_ref[WINDOW:WINDOW + t, :] = pr_ref[:, dm.o_k:dm.o_k + 2 * KV_W]
        dkv_ref[0:t, :] = jnp.zeros((t, 2 * KV_W), F32)
        dkv_ref[t:t + WINDOW, :] = carry_kv_ref[...]
        lane = lax.broadcasted_iota(jnp.int32, (1, LANES), 1)
        lo = lane < HEAD_DIM
        k2, khalf, vhalf = _kv_operands(kv_ref, lo)
        tn_dims = (((0,), (0,)), ((), ()))
        nt_dims = (((1,), (1,)), ((), ()))
        dsk = jnp.zeros((1, LANES), F32)
        stacks = [_stacked_heads(dm, kvh) for kvh in range(N_KV)]
        sink_cols = [_sink_column(sinks_ref, [2 * p + half for p, half in st]) for st in stacks]
        prev = _from_previous(rs)
        hs2 = rs // 2
        for qb in range(nq):
            r0 = qb * WINDOW
            rows = slice(r0, r0 + 2 * WINDOW)
            no_previous = first if qb == 0 else None
            dka = [None, None]
            dva = [None, None]
            for kvh in range(N_KV):
                pairs = [p for p, half in stacks[kvh] if half == 0]
                for b, (p, half) in enumerate(stacks[kvh]):
                    qp = pr_ref[r0:r0 + WINDOW, dm.o_q + LANES * p:dm.o_q + LANES * (p + 1)] * SCALE
                    qs_ref[b * WINDOW:(b + 1) * WINDOW, :] = jnp.where(
                        lo if half == 0 else jnp.logical_not(lo), qp, 0.0).astype(BF16)
                for b, p in enumerate(pairs):
                    ag = pr_ref[r0:r0 + WINDOW, dm.o_ag + LANES * p:dm.o_ag + LANES * (p + 1)]
                    dya = dy_ref[r0:r0 + WINDOW, c + LANES * p:c + LANES * (p + 1)]
                    do_ref[b * WINDOW:(b + 1) * WINDOW, :] = dya * (ag * _sigmoid(ag))
                qs = qs_ref[...]
                d_o = do_ref[...]
                d_o_b = d_o.astype(BF16)
                v_lo, v_hi = vhalf[kvh][0][rows], vhalf[kvh][1][rows]
                prob, m, inv = _softmax_with_sink(qs, k2[kvh][rows], prev, no_previous,
                                                  sink_cols[kvh])
                pb = _band_split(prob, prev)
                o_lo = jnp.dot(pb[0:hs2], v_lo, preferred_element_type=F32)
                o_hi = jnp.dot(pb[hs2:rs], v_hi, preferred_element_type=F32)
                delta = jnp.concatenate([jnp.sum(d_o * o_lo, axis=1, keepdims=True),
                                         jnp.sum(d_o * o_hi, axis=1, keepdims=True)], axis=0)
                dp = jnp.concatenate(
                    [lax.dot_general(d_o_b, v_lo, nt_dims, preferred_element_type=F32),
                     lax.dot_general(d_o_b, v_hi, nt_dims, preferred_element_type=F32)], axis=0)
                dsb = _band_split(prob * (_band_merge(dp, prev) - delta), prev)
                sink_grad = jnp.exp(sink_cols[kvh] - m) * inv * delta
                for b, (p, half) in enumerate(stacks[kvh]):
                    dsink = -jnp.sum(sink_grad[b * WINDOW:(b + 1) * WINDOW])
                    dsk = dsk + jnp.where(lane == 2 * p + half, dsink, 0.0)
                dq_lo = jnp.dot(dsb[0:hs2], khalf[kvh][0][rows], preferred_element_type=F32)
                dq_hi = jnp.dot(dsb[hs2:rs], khalf[kvh][1][rows], preferred_element_type=F32)
                dka[kvh] = lax.dot_general(dsb, qs, tn_dims, preferred_element_type=F32)
                d_o_half = jnp.concatenate([jnp.where(lo, d_o, 0.0), jnp.where(lo, 0.0, d_o)],
                                           axis=0).astype(BF16)
                dva[kvh] = lax.dot_general(pb, d_o_half, tn_dims, preferred_element_type=F32)
                for b, p in enumerate(pairs):
                    blk = slice(b * WINDOW, (b + 1) * WINDOW)
                    cols = slice(LANES * p, LANES * (p + 1))
                    ag = pr_ref[r0:r0 + WINDOW, dm.o_ag + cols.start:dm.o_ag + cols.stop]
                    dya = dy_ref[r0:r0 + WINDOW, c + cols.start:c + cols.stop]
                    dqpair = (dq_lo[blk] + dq_hi[blk]) * SCALE
                    d_ag = dya * (o_lo[blk] + o_hi[blk]) * _dsilu(ag, _sigmoid(ag))
                    dpr_ref[r0:r0 + WINDOW, dm.o_q + cols.start:dm.o_q + cols.stop] = (
                        dqpair.astype(BF16))
                    dbin_ref[0:1, dm.o_q + cols.start:dm.o_q + cols.stop] += jnp.sum(
                        dqpair, axis=0, keepdims=True)
                    dpr_ref[r0:r0 + WINDOW, dm.o_ag + cols.start:dm.o_ag + cols.stop] = (
                        d_ag.astype(BF16))
                    dbin_ref[0:1, dm.o_ag + cols.start:dm.o_ag + cols.stop] += jnp.sum(
                        d_ag, axis=0, keepdims=True)
            fold = [x + pltpu.roll(x, HEAD_DIM, 1) for x in (dka[0], dka[1], dva[0], dva[1])]
            dkv_ref[r0:r0 + 2 * WINDOW, 0:KV_W] += jnp.where(lo, fold[0], fold[1])
            dkv_ref[r0:r0 + 2 * WINDOW, KV_W:2 * KV_W] += jnp.where(lo, fold[2], fold[3])
        dsk_ref[0:1, :] += dsk
        carry_kv_ref[...] = dkv_ref[0:WINDOW, :]
        emit(dm.o_k, 2 * KV_W, dkv_ref[WINDOW:WINDOW + t, :])

        @pl.when(g == nt - 1)
        def _():
            for j in range(CONV_WIDTH):
                dcw_ref[j:j + 1, :] = jnp.sum(dcw_acc_ref[j], axis=0, keepdims=True)
            dcw_ref[CONV_WIDTH:CONV_ROWS, :] = jnp.zeros((CONV_ROWS - CONV_WIDTH, c), F32)

    rev = lambda g: nt - 1 - g
    vec = pl.BlockSpec((1, c), lambda g: (0, 0))
    const = lambda shape: pl.BlockSpec(shape, lambda g: (0, 0))
    scratch = [pltpu.VMEM((t + HALO, c), F32), pltpu.VMEM((t + WINDOW, 2 * KV_W), F32),
               pltpu.VMEM((t + HALO, c), F32), pltpu.VMEM((t + WINDOW, 2 * KV_W), F32),
               pltpu.VMEM((HALO, c), F32), pltpu.VMEM((WINDOW, 2 * KV_W), F32),
               pltpu.VMEM((SUBLANES - 1, t + HALO, LANES), F32),
               pltpu.VMEM((SUBLANES - 1, t + HALO, LANES), F32),
               pltpu.VMEM((t, c), F32), pltpu.VMEM((CONV_ROWS, SUBLANES, c), F32),
               pltpu.VMEM((rs, LANES), BF16), pltpu.VMEM((rs // 2, LANES), F32),
               pltpu.VMEM((t, c), F32)]
    host = _Hosting(list(copies), 9, 5)
    res = pl.pallas_call(
        _ride(body, host, 9, 5, len(scratch), nt), name="mixer_bwd",
        out_shape=[jax.ShapeDtypeStruct((dm.s, dm.din), BF16),
                   jax.ShapeDtypeStruct((CONV_ROWS, c), F32),
                   jax.ShapeDtypeStruct((8, c), F32),
                   jax.ShapeDtypeStruct((8, dm.din), F32),
                   jax.ShapeDtypeStruct((8, LANES), F32)]
        + host.out_shapes,
        grid=(nt,),
        in_specs=[pl.BlockSpec(memory_space=pltpu.SMEM)] + _mixer_specs(dm, t, rev)
        + [pl.BlockSpec((t, dm.d), lambda g: (rev(g), 0)), pl.BlockSpec((t, c), lambda g: (rev(g), 0)),
           pl.BlockSpec((CONV_ROWS, c), lambda g: (0, 0)), vec, vec] + [_ANY] * len(host.args),
        out_specs=[pl.BlockSpec((t, dm.din), lambda g: (rev(g), 0)),
                   const((CONV_ROWS, c)), const((8, c)), const((8, dm.din)), const((8, LANES))]
        + [_ANY] * len(host.out_shapes),
        scratch_shapes=scratch + host.scratch, input_output_aliases=host.aliases,
        compiler_params=_cparams(("arbitrary",)),
    )(sinks, proj, proj, proj, dymix, conv, conv_w, cln_g, cln_b, *host.args)
    return res[:5], res[5:]


def _outproj_ln(dm, ymix, w_out, b_out, x, ln_g, ln_b, target, tm, copies=()):
    last = target is not None
    d = dm.d
    n_in = 7 if last else 6
    host = _Hosting(list(copies), n_in, 3)
    n_hin, n_hout = len(host.args), len(host.out_shapes)
    steps = dm.s // tm

    def body(*refs):
        y_ref, w_ref, bo_ref, x_ref, g_ref, b_ref = refs[:6]
        h_in = refs[n_in:n_in + n_hin]
        h_out = refs[n_in + n_hin + 3:n_in + n_hin + 3 + n_hout]
        h_sems = refs[n_in + n_hin + 3 + n_hout:]
        if copies:
            @pl.when(pl.program_id(0) == 0)
            def _():
                host.start(h_in, h_out, h_sems)

        z = dm.alpha * x_ref[...] + (
            jnp.dot(y_ref[...], w_ref[...], preferred_element_type=F32) + bo_ref[...])
        mu = jnp.mean(z, axis=1, keepdims=True)
        dlt = z - mu
        var = jnp.mean(dlt * dlt, axis=1, keepdims=True)
        out = dlt * lax.rsqrt(var + LN_EPS) * g_ref[...] + b_ref[...]
        if last:
            t_ref = refs[6]
            z_ref, dout_ref, loss_ref = refs[n_in + n_hin:n_in + n_hin + 3]
            z_ref[...] = z
            err = out - t_ref[...]
            dout_ref[...] = err * (1.0 / d)

            @pl.when(pl.program_id(0) == 0)
            def _():
                loss_ref[...] = jnp.zeros_like(loss_ref)

            loss_ref[...] += 0.5 * jnp.sum(jnp.mean(err * err, axis=1, keepdims=True), axis=0,
                                           keepdims=True)
        else:
            z_ref, o_ref, ob_ref = refs[n_in + n_hin:n_in + n_hin + 3]
            z_ref[...] = z
            o_ref[...] = out
            ob_ref[...] = out.astype(BF16)

        if copies:
            @pl.when(pl.program_id(0) == steps - 1)
            def _():
                host.finish(h_in, h_out, h_sems)

    row = pl.BlockSpec((tm, d), lambda i: (i, 0))
    vec = pl.BlockSpec((1, d), lambda i: (0, 0))
    in_specs = [row, pl.BlockSpec((d, d), lambda i: (0, 0)), vec, row, vec, vec]
    args = [ymix, w_out, b_out, x, ln_g, ln_b]
    act = jax.ShapeDtypeStruct((dm.s, d), F32)
    if last:
        in_specs.append(row)
        args.append(target)
        out_shape = [act, act, jax.ShapeDtypeStruct((8, LANES), F32)]
        out_specs = [row, row, pl.BlockSpec((8, LANES), lambda i: (0, 0))]
    else:
        out_shape = [act, act, jax.ShapeDtypeStruct((dm.s, d), BF16)]
        out_specs = [row, row, row]
    res = pl.pallas_call(
        body, name="outproj_ln_loss" if last else "outproj_ln",
        out_shape=out_shape + host.out_shapes, grid=(steps,),
        in_specs=in_specs + [_ANY] * n_hin, out_specs=out_specs + [_ANY] * n_hout,
        scratch_shapes=host.scratch, input_output_aliases=host.aliases,
        compiler_params=_cparams(("arbitrary",)),
    )(*args, *host.args)
    return res[:3], res[3:]


def _ln_bwd(dm, dout, z, ln_g, w_out, tm, copies=()):
    d = dm.d
    host = _Hosting(list(copies), 4, 4)
    n = dm.s // tm

    def body(do_ref, z_ref, g_ref, w_ref, dz_ref, dzb_ref, sm_ref, dy_ref, prev_ref):
        @pl.when(pl.program_id(0) == 0)
        def _():
            sm_ref[...] = jnp.zeros_like(sm_ref)
            prev_ref[...] = jnp.zeros_like(prev_ref)

        dy_ref[...] = lax.dot_general(prev_ref[...], w_ref[...], (((1,), (1,)), ((), ())),
                                      preferred_element_type=F32)
        counted = (pl.program_id(0) < n).astype(F32)
        z = z_ref[...]
        mu = jnp.mean(z, axis=1, keepdims=True)
        dlt = z - mu
        var = jnp.mean(dlt * dlt, axis=1, keepdims=True)
        rstd = lax.rsqrt(var + LN_EPS)
        zhat = dlt * rstd
        do = do_ref[...]
        dzh = do * g_ref[...]
        dz = rstd * (dzh - jnp.mean(dzh, axis=1, keepdims=True)
                     - zhat * jnp.mean(dzh * zhat, axis=1, keepdims=True))
        dzb = dz.astype(BF16)
        dz_ref[...] = dz
        dzb_ref[...] = dzb
        prev_ref[...] = dzb
        sm_ref[0:1, :] += counted * jnp.sum(do * zhat, axis=0, keepdims=True)
        sm_ref[1:2, :] += counted * jnp.sum(do, axis=0, keepdims=True)
        sm_ref[2:3, :] += counted * jnp.sum(dz, axis=0, keepdims=True)

    row = pl.BlockSpec((tm, d), lambda i: (jnp.minimum(i, n - 1), 0))
    behind = pl.BlockSpec((tm, d), lambda i: (jnp.maximum(i - 1, 0), 0))
    act = jax.ShapeDtypeStruct((dm.s, d), F32)
    res = pl.pallas_call(
        _ride(body, host, 4, 4, 1, n + 1), name="ln_bwd_dymix",
        out_shape=[act, jax.ShapeDtypeStruct((dm.s, d), BF16), jax.ShapeDtypeStruct((8, d), F32), act]
        + host.out_shapes,
        grid=(n + 1,),
        in_specs=[row, row, pl.BlockSpec((1, d), lambda i: (0, 0)),
                  pl.BlockSpec((d, d), lambda i: (0, 0))] + [_ANY] * len(host.args),
        out_specs=[row, row, pl.BlockSpec((8, d), lambda i: (0, 0)), behind]
        + [_ANY] * len(host.out_shapes),
        scratch_shapes=[pltpu.VMEM((tm, d), BF16)] + host.scratch,
        input_output_aliases=host.aliases,
        compiler_params=_cparams(("arbitrary",)),
    )(dout, z, ln_g, w_out, *host.args)
    return res[:4], res[4:]


def _pair_sum(part, got, tr):
    _, _, r, w = part.shape

    def body(c_ref, p_ref, g_ref, o_ref):
        o_ref[...] = (p_ref[...] + g_ref[...]).astype(BF16)

    return pl.pallas_call(
        body, name="grad_pair_sum",
        out_shape=jax.ShapeDtypeStruct((N_CHIPS, r, w), BF16),
        grid_spec=pltpu.PrefetchScalarGridSpec(
            num_scalar_prefetch=1, grid=(N_CHIPS, r // tr),
            in_specs=[pl.BlockSpec((None, None, tr, w), lambda k, i, c_ref: (k, c_ref[0], i, 0)),
                      pl.BlockSpec((None, tr, w), lambda k, i, c_ref: (k, i, 0))],
            out_specs=pl.BlockSpec((None, tr, w), lambda k, i, c_ref: (k, i, 0))),
        compiler_params=_cparams(("parallel", "parallel")),
    )(lax.axis_index("c").reshape(1).astype(jnp.int32), part, got)


def _final_sum(part, got, recv, tr):
    _, _, r, w = part.shape

    def body(idx_ref, p_ref, g_ref, r0_ref, r1_ref, r2_ref, o_ref):
        acc = p_ref[...] + g_ref[...]
        for ref in (r0_ref, r1_ref, r2_ref):
            acc = acc + ref[...].astype(F32)
        o_ref[...] = acc

    x, y, c = _position()
    idx = jnp.stack([c, 2 * x + y, 2 * (1 - x) + y, 2 * x + (1 - y),
                     2 * (1 - x) + (1 - y)]).astype(jnp.int32)
    other = lambda j: pl.BlockSpec((None, tr, w), lambda i, s: (s[2 + j], i, 0))
    return pl.pallas_call(
        body, name="grad_final_sum",
        out_shape=jax.ShapeDtypeStruct((r, w), F32),
        grid_spec=pltpu.PrefetchScalarGridSpec(
            num_scalar_prefetch=1, grid=(r // tr,),
            in_specs=[pl.BlockSpec((None, None, tr, w), lambda i, s: (s[1], s[0], i, 0)),
                      pl.BlockSpec((None, tr, w), lambda i, s: (s[1], i, 0)),
                      other(0), other(1), other(2)],
            out_specs=pl.BlockSpec((tr, w), lambda i, s: (i, 0))),
        compiler_params=_cparams(("parallel",)),
    )(idx, part, got, recv, recv, recv)


def _adamw_math(w, g, m, v):
    m = ADAM_B1 * m + (1.0 - ADAM_B1) * g
    v = ADAM_B2 * v + (1.0 - ADAM_B2) * (g * g)
    m_hat = m / (1.0 - ADAM_B1 ** ADAM_STEP)
    v_hat = v / (1.0 - ADAM_B2 ** ADAM_STEP)
    delta = -ADAM_LR * (m_hat / (jnp.sqrt(v_hat) + ADAM_EPS) + ADAM_WD * w)
    return delta, m, v


def _adamw(w, g, m, v, tr):
    depth, r, width = w.shape

    def body(w_ref, g_ref, m_ref, v_ref, d_ref, nm_ref, nv_ref):
        d_ref[...], nm_ref[...], nv_ref[...] = _adamw_math(
            w_ref[...], g_ref[...], m_ref[...], v_ref[...])

    spec = pl.BlockSpec((None, tr, width), lambda l, i: (l, i, 0))
    shape = jax.ShapeDtypeStruct(w.shape, F32)
    return pl.pallas_call(
        body, name="adamw", out_shape=(shape, shape, shape), grid=(depth, r // tr),
        in_specs=[spec] * 4, out_specs=(spec, spec, spec),
        compiler_params=_cparams(("parallel", "parallel")),
    )(w, g, m, v)


def _gather_direct(block):
    def copies(ins, outs, send, recv):
        x, y, c = _position()
        out = []
        for k in range(1, N_DEV):
            peer = (1 - x if k & 4 else x, 1 - y if k & 2 else y, 1 - c if k & 1 else c)
            out.append((pltpu.make_async_remote_copy(
                src_ref=ins[0], dst_ref=outs[0].at[4 * x + 2 * y + c],
                send_sem=send.at[0, k - 1], recv_sem=recv.at[0, k - 1],
                device_id=peer, device_id_type=MESH), k - 1, peer))
        local = pltpu.make_async_copy(ins[0], outs[0].at[4 * x + 2 * y + c], send.at[0, N_DEV - 1])
        return out, local

    def start(ins, outs, send, recv):
        remote, local = copies(ins, outs, send, recv)
        local.start()
        for cp, _, _ in remote:
            cp.start()

    def finish(ins, outs, send, recv):
        remote, local = copies(ins, outs, send, recv)
        x, y, c = _position()
        for _, k, (px, py, pc) in remote:
            pltpu.make_async_remote_copy(
                src_ref=ins[0], dst_ref=outs[0].at[4 * px + 2 * py + pc],
                send_sem=send.at[0, k], recv_sem=recv.at[0, k],
                device_id=(x, y, c), device_id_type=MESH).wait_recv()
        for cp, _, _ in remote:
            cp.wait_send()
        local.wait()

    landing = [jax.ShapeDtypeStruct((N_DEV,) + block.shape, block.dtype)]
    return _Copies([block], landing, {}, (1, N_DEV), start, finish)


def _small_sum_adamw(gathered, w, m, v):
    prows = w.shape[0]

    def body(g_ref, w_ref, m_ref, v_ref, tot_ref, d_ref, nm_ref, nv_ref):
        tot = g_ref[0]
        for d in range(1, N_DEV):
            tot = tot + g_ref[d]
        tot_ref[...] = tot
        d_ref[...], nm_ref[...], nv_ref[...] = _adamw_math(
            w_ref[...], tot[0:prows, :], m_ref[...], v_ref[...])

    vm = pl.BlockSpec(memory_space=pltpu.VMEM)
    pshape = jax.ShapeDtypeStruct(w.shape, F32)
    return pl.pallas_call(
        body, name="small_sum_adamw",
        out_shape=(jax.ShapeDtypeStruct(gathered.shape[1:], F32), pshape, pshape, pshape),
        in_specs=[vm] * 4, out_specs=(vm, vm, vm, vm),
        compiler_params=pltpu.CompilerParams(vmem_limit_bytes=VMEM_LIMIT),
    )(gathered, w, m, v)


def _pack_rows(vec):
    depth, n = vec.shape
    rows = -(-n // LANES)
    rows = -(-rows // 8) * 8
    return jnp.pad(vec, ((0, 0), (0, rows * LANES - n))).reshape(depth, rows, LANES)


def _pack_small(named):
    blocks = [_pack_rows(a) for a in named]
    extents = [(b.shape[1], a.shape[1]) for b, a in zip(blocks, named)]
    depth = named[0].shape[0]
    packed = jnp.concatenate(blocks, axis=1).reshape(depth * sum(r for r, _ in extents), LANES)
    return packed, extents


def _unpack_small(packed, extents, depth):
    per_layer = sum(r for r, _ in extents)
    packed = packed.reshape(depth, per_layer, LANES)
    out, r0 = [], 0
    for rows, n in extents:
        out.append(packed[:, r0:r0 + rows, :].reshape(depth, rows * LANES)[:, :n])
        r0 += rows
    return out


def kernel(x, w_in, b_in, conv_w, conv_b, conv_ln_g, conv_ln_b, sinks, w_out, b_out, ln_g, ln_b, loss_target, m_w_in, m_b_in, m_conv_w, m_conv_b, m_conv_ln_g, m_conv_ln_b, m_sinks, m_w_out, m_b_out, m_ln_g, m_ln_b, v_w_in, v_b_in, v_conv_w, v_conv_b, v_conv_ln_g, v_conv_ln_b, v_sinks, v_w_out, v_b_out, v_ln_g, v_ln_b):
    depth, d, din_shard = w_in.shape
    s = x.shape[1]
    c_shard = conv_w.shape[2]
    dm = _Dims(s, d, N_DEV * c_shard, depth)
    assert dm.din == N_DEV * din_shard and x.shape[0] == 1 and sinks.shape[1] == dm.nh
    d_shard = w_out.shape[1]
    c, din = dm.c, dm.din

    t_mix = _tile(s, 256, WINDOW)
    tm_row = _tile(s, 256, 8)
    tm_big = _tile(s, 1024, 8)

    w_in_t, m_w_in_t, v_w_in_t = (a.transpose(0, 2, 1) for a in (w_in, m_w_in, v_w_in))
    w_in_b, w_out_b = w_in_t.astype(BF16), w_out.astype(BF16)
    conv_w_pad = jnp.pad(conv_w, ((0, 0), (0, CONV_ROWS - CONV_WIDTH), (0, 0)))
    first = _run_copies([_gather_own([w_in_b[0], conv_w_pad])], "weights_gather_own")
    g_in, g_conv = _run_copies([_gather_forward(first)], "weights_gather_forward")
    g_out = None
    conv_w_full = g_conv.transpose(1, 2, 0, 3).reshape(depth, CONV_ROWS, c)
    by_cols = lambda g: g.reshape(din, d)
    by_rows = lambda g: g.reshape(d, d)

    xs = x[0]
    xb = xs.astype(BF16)
    saved = []
    loss_part = dout = None
    for l in range(depth):
        w_in_l = by_cols(g_in)
        riders = [_gather_own([w_out_b[0]])] if l == 0 else [_gather_forward([g_out])]
        proj, (g_out,) = _matmul(xb, w_in_l, tb=True, tm=tm_big, tn=_tile(din, 768), tk=d,
                                 out_dtype=F32, name="in_proj", bias=b_in[l][None, :],
                                 copies=riders)
        riders = [_gather_forward([g_out])] if l == 0 else []
        if l + 1 < depth:
            riders.append(_gather_own([w_in_b[l + 1]]))
        (ymix, conv), landed = _mixer_fwd(dm, proj, conv_w_full[l], conv_b[l][None, :],
                                          conv_ln_g[l][None, :], conv_ln_b[l][None, :], sinks[l],
                                          t_mix, copies=riders)
        if l == 0:
            g_out = landed[0]
        g_in_next = landed[-1] if l + 1 < depth else None
        w_out_l = by_rows(g_out)
        target = loss_target[0] if l == depth - 1 else None
        riders = []
        if l + 1 < depth:
            riders = [_gather_forward([g_in_next]), _gather_own([w_out_b[l + 1]])]
        res, landed = _outproj_ln(dm, ymix, w_out_l, b_out[l][None, :], xs, ln_g[l][None, :],
                                  ln_b[l][None, :], target, tm_row, copies=riders)
        saved.append((xb, proj, ymix, conv, res[0], w_in_l, w_out_l))
        if l + 1 < depth:
            g_in, g_out = landed
            xs, xb = res[1], res[2]
        else:
            dout, loss_part = res[1], res[2]

    g_w_in, g_w_out = [None] * depth, [None] * depth
    small_parts = [None] * depth
    dconv_w = [None] * depth
    tr_in, tr_out = _tile(din_shard, 512, 8), _tile(d_shard, 256, 8)
    parts_in = None
    for l in reversed(range(depth)):
        xb, proj, ymix, conv, z, w_in_l, w_out_l = saved[l]
        riders = [_scatter_sibling([parts_in])] if parts_in is not None else []
        (dz, dzb, ln_small, dymix), landed = _ln_bwd(dm, dout, z, ln_g[l][None, :], w_out_l, tm_row,
                                                     copies=riders)
        if parts_in is not None:
            got_in = landed[0]
            sums_in = _pair_sum(parts_in, got_in, tr_in)
        dw_out = _matmul(ymix, dzb, ta=True, tm=_tile(d, 1024), tn=_tile(d, 1024), tk=s,
                         out_dtype=F32, name="dw_out")
        riders = [_scatter_chips([sums_in])] if parts_in is not None else []
        (dproj, dcw, conv_small, dbin, dsk), landed = _mixer_bwd(
            dm, proj, dymix, conv, conv_w_full[l], conv_ln_g[l][None, :], conv_ln_b[l][None, :],
            sinks[l], t_mix, copies=riders)
        if parts_in is not None:
            g_w_in[l + 1] = _final_sum(parts_in, got_in, landed[0], tr_in)
        small_parts[l] = [dbin[0], conv_small[0], conv_small[1], conv_small[2], dsk[0, :dm.nh],
                          ln_small[2], ln_small[0], ln_small[1]]
        dconv_w[l] = dcw
        parts_out = dw_out.reshape(N_CHIPS, 2, d_shard, d)
        dx = dict(tm=_tile(s, 512), tn=_tile(d, 1024), tk=din, out_dtype=F32, name="dx", resid=dz,
                  resid_scale=dm.alpha)
        dw = dict(ta=True, tm=_tile(din, 768), tn=_tile(d, 1024), tk=s, out_dtype=F32, name="dw_in")
        if l > 0:
            dout, (got_out,) = _matmul(dproj, w_in_l, copies=[_scatter_sibling([parts_out])], **dx)
            sums_out = _pair_sum(parts_out, got_out, tr_out)
            dw_in_t, (recv_out,) = _matmul(dproj, xb, copies=[_scatter_chips([sums_out])], **dw)
            parts_in = dw_in_t.reshape(N_CHIPS, 2, din_shard, d)
        else:
            packed_g, extents = _pack_small([jnp.stack([small_parts[k][n] for k in range(depth)])
                                             for n in range(len(small_parts[0]))])
            prows = packed_g.shape[0]
            conv_rows = depth * CONV_ROWS * c // LANES
            partial = jnp.concatenate(
                [packed_g, jnp.stack(dconv_w).reshape(conv_rows, LANES), loss_part], axis=0)
            dw_in_t, (got_out, small_all) = _matmul(
                dproj, xb, copies=[_scatter_sibling([parts_out]), _gather_direct(partial)], **dw)
            parts_in = dw_in_t.reshape(N_CHIPS, 2, din_shard, d)
            sums_out = _pair_sum(parts_out, got_out, tr_out)
            got_in, recv_out = _run_copies(
                [_scatter_sibling([parts_in]), _scatter_chips([sums_out])], "grad_exchange_tail")
            sums_in = _pair_sum(parts_in, got_in, tr_in)
            dout, (recv_in,) = _matmul(dproj, w_in_l, copies=[_scatter_chips([sums_in])], **dx)
            g_w_in[0] = _final_sum(parts_in, got_in, recv_in, tr_in)
        g_w_out[l] = _final_sum(parts_out, got_out, recv_out, tr_out)
    grad_x = dout[None]

    small_w = [b_in, conv_b, conv_ln_g, conv_ln_b, sinks, b_out, ln_g, ln_b]
    small_m = [m_b_in, m_conv_b, m_conv_ln_g, m_conv_ln_b, m_sinks, m_b_out, m_ln_g, m_ln_b]
    small_v = [v_b_in, v_conv_b, v_conv_ln_g, v_conv_ln_b, v_sinks, v_b_out, v_ln_g, v_ln_b]
    packed_w, _ = _pack_small(small_w)
    packed_m, _ = _pack_small(small_m)
    packed_v, _ = _pack_small(small_v)
    total, sm_delta, sm_m, sm_v = _small_sum_adamw(small_all, packed_w, packed_m, packed_v)
    loss = total[prows + conv_rows, 0]
    dconv_w_full = total[prows:prows + conv_rows].reshape(depth, CONV_ROWS, c)
    me = 4 * lax.axis_index("x") + 2 * lax.axis_index("y") + lax.axis_index("c")
    grad_conv_w = lax.dynamic_slice_in_dim(dconv_w_full, me * c_shard, c_shard, axis=2)[:, :CONV_WIDTH]

    grads_small = _unpack_small(total[:prows], extents, depth)
    delta_small = _unpack_small(sm_delta, extents, depth)
    newm_small = _unpack_small(sm_m, extents, depth)
    newv_small = _unpack_small(sm_v, extents, depth)

    grad_w_in_t = jnp.stack(g_w_in)
    grad_w_out = jnp.stack(g_w_out)
    grad_w_in, d_w_in, nm_w_in, nv_w_in = (a.transpose(0, 2, 1) for a in (
        grad_w_in_t, *_adamw(w_in_t, grad_w_in_t, m_w_in_t, v_w_in_t, tr_in)))
    d_w_out, nm_w_out, nv_w_out = _adamw(w_out, grad_w_out, m_w_out, v_w_out, _tile(d_shard, 256, 8))
    d_cw, nm_cw, nv_cw = _adamw(conv_w, grad_conv_w, m_conv_w, v_conv_w, CONV_WIDTH)

    def assemble(w_in_leaf, conv_w_leaf, w_out_leaf, small):
        b_in_, conv_b_, cg_, cb_, sinks_, b_out_, ln_g_, ln_b_ = small
        return [w_in_leaf, b_in_, conv_w_leaf, conv_b_, cg_, cb_, sinks_, w_out_leaf, b_out_,
                ln_g_, ln_b_]

    return (loss, grad_x,
            *assemble(grad_w_in, grad_conv_w, grad_w_out, grads_small),
            *assemble(d_w_in, d_cw, d_w_out, delta_small),
            *assemble(nm_w_in, nm_cw, nm_w_out, newm_small),
            *assemble(nv_w_in, nv_cw, nv_w_out, newv_small))
```

```python
import functools

import jax
import jax.numpy as jnp
from jax import lax
from jax.experimental import pallas as pl
from jax.experimental.pallas import tpu as pltpu

F32 = jnp.float32
BF16 = jnp.bfloat16
MESH = pl.DeviceIdType.MESH

N_DEV = 8
N_CHIPS = 4
HEAD_DIM = 64
N_KV = 2
KV_W = N_KV * HEAD_DIM
CONV_WIDTH = 31
CONV_ROWS = 32
HALO = 32
WINDOW = 128
LN_EPS = 1e-5
NEG_INF = -1e30
LANES = 128

ADAM_LR = 0.001
ADAM_B1 = 0.9
ADAM_B2 = 0.999
ADAM_EPS = 1e-08
ADAM_WD = 0.01
ADAM_STEP = 10

VMEM_LIMIT = 56 * 1024 * 1024


def _tile(n, target, align=LANES):
    best = None
    for t in range(align, min(n, target) + 1, align):
        if n % t == 0:
            best = t
    return n if best is None else best


def _sigmoid(x):
    return jax.nn.sigmoid(x)


def _dsilu(x, s):
    return s * (1.0 + x * (1.0 - s))


def _cparams(sem, vmem=VMEM_LIMIT):
    return pltpu.CompilerParams(dimension_semantics=sem, vmem_limit_bytes=vmem)


_ANY = pl.BlockSpec(memory_space=pl.ANY)


def _position():
    return lax.axis_index("x"), lax.axis_index("y"), lax.axis_index("c")


def _other_chips(x, y):
    return [(1 - x, y), (x, 1 - y), (1 - x, 1 - y)]


class _Copies:
    def __init__(self, operands, landing, alias, sems, start, finish):
        self.operands, self.landing, self.alias, self.sems = operands, landing, alias, sems
        self.start, self.finish = start, finish


class _Hosting:
    def __init__(self, groups, n_in, n_out):
        self.groups = groups
        self.args = [a for g in groups for a in g.operands]
        self.out_shapes = [s for g in groups for s in g.landing]
        self.scratch = [pltpu.SemaphoreType.DMA(g.sems) for g in groups for _ in range(2)]
        self.aliases = {}
        i0, o0 = n_in, n_out
        for g in groups:
            for a, b in g.alias.items():
                self.aliases[i0 + a] = o0 + b
            i0 += len(g.operands)
            o0 += len(g.landing)

    def _each(self, in_refs, out_refs, sem_refs):
        i0 = o0 = 0
        for n, g in enumerate(self.groups):
            yield (g, in_refs[i0:i0 + len(g.operands)], out_refs[o0:o0 + len(g.landing)],
                   sem_refs[2 * n], sem_refs[2 * n + 1])
            i0 += len(g.operands)
            o0 += len(g.landing)

    def start(self, in_refs, out_refs, sem_refs):
        for g, ins, outs, send, recv in self._each(in_refs, out_refs, sem_refs):
            g.start(ins, outs, send, recv)

    def finish(self, in_refs, out_refs, sem_refs):
        for g, ins, outs, send, recv in self._each(in_refs, out_refs, sem_refs):
            g.finish(ins, outs, send, recv)


def _ride(body, host, n_in, n_out, n_scratch, steps):
    n_hin, n_hout = len(host.args), len(host.out_shapes)

    def wrapped(*refs):
        pos = [0]

        def take(n):
            pos[0] += n
            return refs[pos[0] - n:pos[0]]

        ins, h_in, outs, h_out = take(n_in), take(n_hin), take(n_out), take(n_hout)
        scratch, h_sems = take(n_scratch), refs[pos[0]:]
        if host.groups:
            @pl.when(pl.program_id(0) == 0)
            def _():
                host.start(h_in, h_out, h_sems)

        body(*ins, *outs, *scratch)
        if host.groups:
            @pl.when(pl.program_id(0) == steps - 1)
            def _():
                host.finish(h_in, h_out, h_sems)

    return wrapped


def _run_copies(groups, name):
    host = _Hosting(groups, 0, 0)
    n_in, n_out = len(host.args), len(host.out_shapes)

    def body(*refs):
        ins, outs, sems = refs[:n_in], refs[n_in:n_in + n_out], refs[n_in + n_out:]
        host.start(ins, outs, sems)
        host.finish(ins, outs, sems)

    return pl.pallas_call(
        body, name=name, out_shape=host.out_shapes, in_specs=[_ANY] * n_in,
        out_specs=[_ANY] * n_out, scratch_shapes=host.scratch,
        input_output_aliases=host.aliases,
    )(*host.args)


def _gather_own(shards):
    n = len(shards)

    def copies(ins, outs, send, recv):
        x, y, c = _position()
        peers = [(x, y, 1 - c)] + [(px, py, c) for px, py in _other_chips(x, y)]
        out = []
        for t in range(n):
            for k, peer in enumerate(peers):
                out.append((pltpu.make_async_remote_copy(
                    src_ref=ins[t], dst_ref=outs[t].at[4 * x + 2 * y + c],
                    send_sem=send.at[t, k], recv_sem=recv.at[t, k],
                    device_id=peer, device_id_type=MESH), t, k, peer))
        local = [pltpu.make_async_copy(ins[t], outs[t].at[4 * x + 2 * y + c], send.at[t, 4])
                 for t in range(n)]
        return out, local

    def start(ins, outs, send, recv):
        remote, local = copies(ins, outs, send, recv)
        for cp in local:
            cp.start()
        for cp, _, _, _ in remote:
            cp.start()

    def finish(ins, outs, send, recv):
        remote, local = copies(ins, outs, send, recv)
        x, y, c = _position()
        for _, t, k, (px, py, pc) in remote:
            pltpu.make_async_remote_copy(
                src_ref=ins[t], dst_ref=outs[t].at[4 * px + 2 * py + pc],
                send_sem=send.at[t, k], recv_sem=recv.at[t, k],
                device_id=(x, y, c), device_id_type=MESH).wait_recv()
        for cp, _, _, _ in remote:
            cp.wait_send()
        for cp in local:
            cp.wait()

    landing = [jax.ShapeDtypeStruct((N_DEV,) + s.shape, s.dtype) for s in shards]
    return _Copies(list(shards), landing, {}, (n, 5), start, finish)


def _gather_forward(buffers):
    n = len(buffers)

    def copies(ins, outs, send, recv):
        x, y, c = _position()
        out = []
        for t in range(n):
            for j, (px, py) in enumerate(_other_chips(x, y)):
                slot = 4 * px + 2 * py + c
                out.append((pltpu.make_async_remote_copy(
                    src_ref=ins[t].at[slot], dst_ref=outs[t].at[slot],
                    send_sem=send.at[t, j], recv_sem=recv.at[t, j],
                    device_id=(x, y, 1 - c), device_id_type=MESH), t, j, 4 * px + 2 * py + 1 - c))
        return out

    def start(ins, outs, send, recv):
        for cp, _, _, _ in copies(ins, outs, send, recv):
            cp.start()

    def finish(ins, outs, send, recv):
        x, y, c = _position()
        mine = copies(ins, outs, send, recv)
        for _, t, j, got in mine:
            pltpu.make_async_remote_copy(
                src_ref=ins[t].at[got], dst_ref=outs[t].at[got],
                send_sem=send.at[t, j], recv_sem=recv.at[t, j],
                device_id=(x, y, c), device_id_type=MESH).wait_recv()
        for cp, _, _, _ in mine:
            cp.wait_send()

    landing = [jax.ShapeDtypeStruct(b.shape, b.dtype) for b in buffers]
    return _Copies(list(buffers), landing, {t: t for t in range(n)}, (n, 3), start, finish)


def _scatter_sibling(parts):
    n = len(parts)

    def copies(ins, outs, send, recv):
        x, y, c = _position()
        return [pltpu.make_async_remote_copy(
            src_ref=ins[t].at[:, 1 - c], dst_ref=outs[t],
            send_sem=send.at[t, 0], recv_sem=recv.at[t, 0],
            device_id=(x, y, 1 - c), device_id_type=MESH) for t in range(n)]

    def start(ins, outs, send, recv):
        for cp in copies(ins, outs, send, recv):
            cp.start()

    def finish(ins, outs, send, recv):
        for cp in copies(ins, outs, send, recv):
            cp.wait()

    landing = [jax.ShapeDtypeStruct((p.shape[0],) + p.shape[2:], p.dtype) for p in parts]
    return _Copies(list(parts), landing, {}, (n, 1), start, finish)


def _scatter_chips(blocks):
    n = len(blocks)

    def start(ins, outs, send, recv):
        x, y, c = _position()
        for t in range(n):
            for j, (px, py) in enumerate(_other_chips(x, y)):
                pltpu.make_async_remote_copy(
                    src_ref=ins[t].at[2 * px + py], dst_ref=outs[t].at[2 * x + y],
                    send_sem=send.at[t, j], recv_sem=recv.at[t, j],
                    device_id=(px, py, c), device_id_type=MESH).start()

    def finish(ins, outs, send, recv):
        x, y, c = _position()
        for t in range(n):
            for j, (px, py) in enumerate(_other_chips(x, y)):
                cp = pltpu.make_async_remote_copy(
                    src_ref=ins[t].at[2 * px + py], dst_ref=outs[t].at[2 * px + py],
                    send_sem=send.at[t, j], recv_sem=recv.at[t, j],
                    device_id=(px, py, c), device_id_type=MESH)
                cp.wait_recv()
                cp.wait_send()

    landing = [jax.ShapeDtypeStruct(b.shape, b.dtype) for b in blocks]
    return _Copies(list(blocks), landing, {}, (n, 3), start, finish)


def _matmul(a, b, *, ta=False, tb=False, tm, tn, tk, out_dtype, name, bias=None, resid=None,
            resid_scale=1.0, copies=()):
    m, k = (a.shape[1], a.shape[0]) if ta else a.shape
    n = b.shape[0] if tb else b.shape[1]
    assert (b.shape[1] if tb else b.shape[0]) == k
    assert m % tm == 0 and n % tn == 0 and k % tk == 0
    ni, nj, nk = m // tm, n // tn, k // tk
    dn = (((0 if ta else 1,), (1 if tb else 0,)), ((), ()))
    n_in = 2 + (bias is not None) + (resid is not None)
    host = _Hosting(list(copies), n_in, 1)
    n_hin, n_hout = len(host.args), len(host.out_shapes)

    def body(*refs):
        a_ref, b_ref = refs[0], refs[1]
        pos = 2
        bias_ref = resid_ref = None
        if bias is not None:
            bias_ref = refs[pos]
            pos += 1
        if resid is not None:
            resid_ref = refs[pos]
            pos += 1
        h_in = refs[pos:pos + n_hin]
        pos += n_hin
        o_ref = refs[pos]
        h_out = refs[pos + 1:pos + 1 + n_hout]
        pos += 1 + n_hout
        acc_ref = refs[pos] if nk > 1 else None
        h_sems = refs[pos + (nk > 1):]
        step = (pl.program_id(0) * nj + pl.program_id(1)) * nk + pl.program_id(2)

        if copies:
            @pl.when(step == 0)
            def _():
                host.start(h_in, h_out, h_sems)

        def finish(acc):
            if bias_ref is not None:
                acc = acc + bias_ref[...]
            if resid_ref is not None:
                acc = acc + resid_scale * resid_ref[...]
            o_ref[...] = acc.astype(out_dtype)

        p = lax.dot_general(a_ref[...], b_ref[...], dn, preferred_element_type=F32)
        if nk == 1:
            finish(p)
        else:
            kk = pl.program_id(2)

            @pl.when(kk == 0)
            def _():
                acc_ref[...] = p

            @pl.when(kk > 0)
            def _():
                acc_ref[...] += p

            @pl.when(kk == nk - 1)
            def _():
                finish(acc_ref[...])

        if copies:
            @pl.when(step == ni * nj * nk - 1)
            def _():
                host.finish(h_in, h_out, h_sems)

    a_spec = (pl.BlockSpec((tk, tm), lambda i, j, kk: (kk, i)) if ta
              else pl.BlockSpec((tm, tk), lambda i, j, kk: (i, kk)))
    b_spec = (pl.BlockSpec((tn, tk), lambda i, j, kk: (j, kk)) if tb
              else pl.BlockSpec((tk, tn), lambda i, j, kk: (kk, j)))
    in_specs = [a_spec, b_spec]
    args = [a, b]
    if bias is not None:
        in_specs.append(pl.BlockSpec((1, tn), lambda i, j, kk: (0, j)))
        args.append(bias)
    if resid is not None:
        in_specs.append(pl.BlockSpec((tm, tn), lambda i, j, kk: (i, j)))
        args.append(resid)
    res = pl.pallas_call(
        body, name=name,
        out_shape=[jax.ShapeDtypeStruct((m, n), out_dtype)] + host.out_shapes,
        grid=(ni, nj, nk),
        in_specs=in_specs + [_ANY] * n_hin,
        out_specs=[pl.BlockSpec((tm, tn), lambda i, j, kk: (i, j))] + [_ANY] * n_hout,
        scratch_shapes=([pltpu.VMEM((tm, tn), F32)] if nk > 1 else []) + host.scratch,
        input_output_aliases=host.aliases,
        compiler_params=_cparams(("arbitrary",) * 3 if copies else
                                 ("parallel", "parallel", "arbitrary")),
    )(*args, *host.args)
    return (res[0], res[1:]) if copies else res[0]


class _Dims:
    def __init__(self, s, d, c, depth):
        self.s, self.d, self.c, self.depth = s, d, c, depth
        self.a = d - c
        self.nh = self.a // HEAD_DIM
        self.group = self.nh // N_KV
        self.din = 3 * c + 2 * self.a + 2 * KV_W
        self.o_q = 3 * c
        self.o_k = 3 * c + self.a
        self.o_v = self.o_k + KV_W
        self.o_ag = self.o_k + 2 * KV_W
        self.alpha = (2 * depth) ** 0.25
        assert self.nh % 2 == 0 and self.group % 2 == 0 and self.o_k % (2 * KV_W) == 0
        assert c % LANES == 0 and self.a % LANES == 0


SUBLANES = 8
TAP_ROWS = 128
ROW_CHUNK = 64


def _shift_copies(src_ref, sh_ref, lanes, rows):
    for r in range(1, SUBLANES):
        sh_ref[r - 1, 0:rows, :] = src_ref[pl.ds(r, rows), lanes]


def _tap_rows(src_ref, sh_ref, lanes, off, start, rows):
    r = off % SUBLANES
    at = pl.multiple_of(start + (off - r), SUBLANES)
    if r == 0:
        return src_ref[pl.ds(at, rows), lanes]
    return sh_ref[r - 1, pl.ds(at, rows), :]


def _conv_chunk(w_ref, src_ref, sh_ref, out_ref, lanes, t, first_off, reverse, bias_ref=None):
    def block(it, carry):
        start = pl.multiple_of(it * TAP_ROWS, TAP_ROWS)
        acc = None
        for j in range(CONV_WIDTH):
            off = first_off - j if reverse else first_off + j
            term = w_ref[j:j + 1, lanes] * _tap_rows(src_ref, sh_ref, lanes, off, start, TAP_ROWS)
            acc = term if acc is None else acc + term
        if bias_ref is not None:
            acc = acc + bias_ref[:, lanes]
        out_ref[pl.ds(start, TAP_ROWS), lanes] = acc
        return carry

    lax.fori_loop(0, t // TAP_ROWS, block, 0)


def _conv_w_grad_chunk(src_ref, sh_ref, dconv_ref, acc_ref, lanes, t, first_off):
    def block(it, accs):
        start = pl.multiple_of(it * SUBLANES, SUBLANES)
        dv = dconv_ref[pl.ds(start, SUBLANES), lanes]
        return tuple(
            acc + _tap_rows(src_ref, sh_ref, lanes, first_off + j, start, SUBLANES) * dv
            for j, acc in enumerate(accs))

    zero = jnp.zeros((SUBLANES, LANES), F32)
    accs = lax.fori_loop(0, t // SUBLANES, block, (zero,) * CONV_WIDTH, unroll=2)
    for j in range(CONV_WIDTH):
        acc_ref[j, :, lanes] += accs[j]


def _kv_operands(kv_ref, lo):
    kext = kv_ref[:, 0:KV_W]
    vext = kv_ref[:, KV_W:2 * KV_W]
    ksw = pltpu.roll(kext, HEAD_DIM, 1)
    vsw = pltpu.roll(vext, HEAD_DIM, 1)
    zero = jnp.zeros_like(kext)
    k2 = [jnp.where(lo, kext, ksw).astype(BF16), jnp.where(lo, ksw, kext).astype(BF16)]
    khalf = [[jnp.where(lo, kext, zero).astype(BF16), jnp.where(lo, zero, ksw).astype(BF16)],
             [jnp.where(lo, ksw, zero).astype(BF16), jnp.where(lo, zero, kext).astype(BF16)]]
    vhalf = [[jnp.where(lo, vext, zero).astype(BF16), jnp.where(lo, zero, vsw).astype(BF16)],
             [jnp.where(lo, vsw, zero).astype(BF16), jnp.where(lo, zero, vext).astype(BF16)]]
    return k2, khalf, vhalf


SCALE = HEAD_DIM ** -0.5


def _from_previous(rows):
    row = lax.broadcasted_iota(jnp.int32, (rows, WINDOW), 0)
    col = lax.broadcasted_iota(jnp.int32, (rows, WINDOW), 1)
    return col > (row & (WINDOW - 1))


def _sink_column(sinks_ref, heads):
    block = lax.broadcasted_iota(jnp.int32, (len(heads) * WINDOW, 1), 0) // WINDOW
    out = jnp.zeros(block.shape, F32)
    for b, head in enumerate(heads):
        out = jnp.where(block == b, sinks_ref[head], out)
    return out


def _stacked_heads(dm, kvh):
    pairs = range(kvh * (dm.group // 2), (kvh + 1) * (dm.group // 2))
    return [(p, half) for half in range(2) for p in pairs]


def _band_merge(both, prev):
    return jnp.where(prev, both[:, 0:WINDOW], both[:, WINDOW:2 * WINDOW])


def _band_split(tile, prev):
    zero = jnp.zeros_like(tile)
    return jnp.concatenate([jnp.where(prev, tile, zero), jnp.where(prev, zero, tile)],
                           axis=1).astype(BF16)


def _softmax_with_sink(qm, k2rows, prev, no_previous, sink):
    both = lax.dot_general(qm, k2rows, (((1,), (1,)), ((), ())), preferred_element_type=F32)
    s_prev = both[:, 0:WINDOW]
    if no_previous is not None:
        s_prev = jnp.where(no_previous, NEG_INF, s_prev)
    s = jnp.where(prev, s_prev, both[:, WINDOW:2 * WINDOW])
    m = jnp.maximum(jnp.max(s, axis=1, keepdims=True), sink)
    e = jnp.exp(s - m)
    den = jnp.sum(e, axis=1, keepdims=True) + jnp.exp(sink - m)
    inv = 1.0 / den
    return e * inv, m, inv


def _mixer_specs(dm, t, idx):
    return [
        pl.BlockSpec((t, dm.din), lambda g: (idx(g), 0)),
        pl.BlockSpec((HALO, 2 * dm.c), lambda g: (jnp.maximum(idx(g) * (t // HALO) - 1, 0), 0)),
        pl.BlockSpec((WINDOW, 2 * KV_W),
                     lambda g: (jnp.maximum(idx(g) * (t // WINDOW) - 1, 0), dm.o_k // (2 * KV_W))),
    ]


def _mixer_fwd(dm, proj, conv_w, conv_b, cln_g, cln_b, sinks, t, copies=()):
    c, nq = dm.c, t // WINDOW
    rs = dm.group * WINDOW
    host = _Hosting(list(copies), 8, 2)

    def body(sinks_ref, pr_ref, ch_ref, kvh_ref, cw_ref, cb_ref, cg_ref, cbb_ref,
             y_ref, conv_ref, hext_ref, kv_ref, hs_ref, qs_ref):
        i = pl.program_id(0)
        first = i == 0
        h = pr_ref[:, 0:c] * _sigmoid(pr_ref[:, c:2 * c])
        hh = ch_ref[:, 0:c] * _sigmoid(ch_ref[:, c:2 * c])
        hext_ref[0:HALO, :] = jnp.where(first, 0.0, hh)
        hext_ref[HALO:HALO + t, :] = h
        for k in range(c // LANES):
            lanes = slice(LANES * k, LANES * (k + 1))
            _shift_copies(hext_ref, hs_ref, lanes, t + HALO - SUBLANES)
            _conv_chunk(cw_ref, hext_ref, hs_ref, conv_ref, lanes, t, HALO - (CONV_WIDTH - 1),
                        False, cb_ref)
        conv = conv_ref[...]
        mu = jnp.mean(conv, axis=1, keepdims=True)
        dlt = conv - mu
        var = jnp.mean(dlt * dlt, axis=1, keepdims=True)
        u = dlt * lax.rsqrt(var + LN_EPS) * cg_ref[...] + cbb_ref[...]
        gate = pr_ref[:, 2 * c:3 * c]
        y_ref[:, 0:c] = (u * _sigmoid(u) * (gate * _sigmoid(gate))).astype(BF16)

        kv_ref[0:WINDOW, :] = jnp.where(first, 0.0, kvh_ref[...])
        kv_ref[WINDOW:WINDOW + t, :] = pr_ref[:, dm.o_k:dm.o_k + 2 * KV_W]
        lo = lax.broadcasted_iota(jnp.int32, (1, LANES), 1) < HEAD_DIM
        k2, _, vhalf = _kv_operands(kv_ref, lo)
        stacks = [_stacked_heads(dm, kvh) for kvh in range(N_KV)]
        sink_cols = [_sink_column(sinks_ref, [2 * p + half for p, half in st]) for st in stacks]
        prev = _from_previous(rs)
        for qb in range(nq):
            r0 = qb * WINDOW
            rows = slice(r0, r0 + 2 * WINDOW)
            no_previous = first if qb == 0 else None
            for kvh in range(N_KV):
                for b, (p, half) in enumerate(stacks[kvh]):
                    qp = pr_ref[r0:r0 + WINDOW, dm.o_q + LANES * p:dm.o_q + LANES * (p + 1)] * SCALE
                    qs_ref[b * WINDOW:(b + 1) * WINDOW, :] = jnp.where(
                        lo if half == 0 else jnp.logical_not(lo), qp, 0.0).astype(BF16)
                prob, _, _ = _softmax_with_sink(qs_ref[...], k2[kvh][rows], prev, no_previous,
                                                sink_cols[kvh])
                pb = _band_split(prob, prev)
                o_lo = jnp.dot(pb[0:rs // 2], vhalf[kvh][0][rows], preferred_element_type=F32)
                o_hi = jnp.dot(pb[rs // 2:rs], vhalf[kvh][1][rows], preferred_element_type=F32)
                for b, (p, _) in enumerate(stacks[kvh][:len(stacks[kvh]) // 2]):
                    blk = slice(b * WINDOW, (b + 1) * WINDOW)
                    ag = pr_ref[r0:r0 + WINDOW, dm.o_ag + LANES * p:dm.o_ag + LANES * (p + 1)]
                    y_ref[r0:r0 + WINDOW, c + LANES * p:c + LANES * (p + 1)] = (
                        (o_lo[blk] + o_hi[blk]) * (ag * _sigmoid(ag))).astype(BF16)

    vec = pl.BlockSpec((1, c), lambda g: (0, 0))
    scratch = [pltpu.VMEM((t + HALO, c), F32), pltpu.VMEM((t + WINDOW, 2 * KV_W), F32),
               pltpu.VMEM((SUBLANES - 1, t + HALO, LANES), F32), pltpu.VMEM((rs, LANES), BF16)]
    res = pl.pallas_call(
        _ride(body, host, 8, 2, len(scratch), dm.s // t), name="mixer_fwd",
        out_shape=[jax.ShapeDtypeStruct((dm.s, dm.d), BF16),
                   jax.ShapeDtypeStruct((dm.s, c), F32)]
        + host.out_shapes,
        grid=(dm.s // t,),
        in_specs=[pl.BlockSpec(memory_space=pltpu.SMEM)] + _mixer_specs(dm, t, lambda g: g)
        + [pl.BlockSpec((CONV_ROWS, c), lambda g: (0, 0)), vec, vec, vec] + [_ANY] * len(host.args),
        out_specs=[pl.BlockSpec((t, dm.d), lambda g: (g, 0)), pl.BlockSpec((t, c), lambda g: (g, 0))]
        + [_ANY] * len(host.out_shapes),
        scratch_shapes=scratch + host.scratch, input_output_aliases=host.aliases,
        compiler_params=_cparams(("arbitrary",)),
    )(sinks, proj, proj, proj, conv_w, conv_b, cln_g, cln_b, *host.args)
    return res[:2], res[2:]


def _mixer_bwd(dm, proj, dymix, conv, conv_w, cln_g, cln_b, sinks, t, copies=()):
    c, nq, nt = dm.c, t // WINDOW, dm.s // t
    rs = dm.group * WINDOW

    def body(sinks_ref, pr_ref, ch_ref, kvh_ref, dy_ref, cv_ref, cw_ref, cg_ref, cbb_ref,
             dpr_ref, dcw_ref, dsm_ref, dbin_ref, dsk_ref,
             hext_ref, kv_ref, dcx_ref, dkv_ref, carry_dc_ref, carry_kv_ref,
             hs_ref, ds_ref, dh_ref, dcw_acc_ref, qs_ref, do_ref, sg_ref):
        g = pl.program_id(0)
        i = nt - 1 - g
        first = i == 0

        @pl.when(g == 0)
        def _():
            dcw_acc_ref[...] = jnp.zeros_like(dcw_acc_ref)
            dsm_ref[...] = jnp.zeros_like(dsm_ref)
            dbin_ref[...] = jnp.zeros_like(dbin_ref)
            dsk_ref[...] = jnp.zeros_like(dsk_ref)
            carry_dc_ref[...] = jnp.zeros_like(carry_dc_ref)
            carry_kv_ref[...] = jnp.zeros_like(carry_kv_ref)

        def emit(col, width, val, rows=slice(None)):
            dpr_ref[rows, col:col + width] = val.astype(BF16)
            dbin_ref[0:1, col:col + width] += jnp.sum(val, axis=0, keepdims=True)

        hh = ch_ref[:, 0:c] * _sigmoid(ch_ref[:, c:2 * c])
        hext_ref[0:HALO, :] = jnp.where(first, 0.0, hh)
        dcx_ref[t:t + HALO, :] = carry_dc_ref[...]
        for r0 in range(0, t, ROW_CHUNK):
            rows = slice(r0, r0 + ROW_CHUNK)
            sg = _sigmoid(pr_ref[rows, c:2 * c])
            sg_ref[rows, :] = sg
            hext_ref[HALO + r0:HALO + r0 + ROW_CHUNK, :] = pr_ref[rows, 0:c] * sg
            conv = cv_ref[rows, :]
            mu = jnp.mean(conv, axis=1, keepdims=True)
            dlt = conv - mu
            var = jnp.mean(dlt * dlt, axis=1, keepdims=True)
            rstd = lax.rsqrt(var + LN_EPS)
            xhat = dlt * rstd
            u = xhat * cg_ref[...] + cbb_ref[...]
            su = _sigmoid(u)
            gate = pr_ref[rows, 2 * c:3 * c]
            sgate = _sigmoid(gate)
            dyc = dy_ref[rows, 0:c]
            emit(2 * c, c, dyc * (u * su) * _dsilu(gate, sgate), rows)
            du = dyc * (gate * sgate) * _dsilu(u, su)
            dsm_ref[1:2, :] += jnp.sum(du * xhat, axis=0, keepdims=True)
            dsm_ref[2:3, :] += jnp.sum(du, axis=0, keepdims=True)
            dxh = du * cg_ref[...]
            dconv = rstd * (dxh - jnp.mean(dxh, axis=1, keepdims=True)
                            - xhat * jnp.mean(dxh * xhat, axis=1, keepdims=True))
            dsm_ref[0:1, :] += jnp.sum(dconv, axis=0, keepdims=True)
            dcx_ref[rows, :] = dconv
        carry_dc_ref[...] = dcx_ref[0:HALO, :]
        for k in range(c // LANES):
            lanes = slice(LANES * k, LANES * (k + 1))
            _shift_copies(hext_ref, hs_ref, lanes, t + HALO - SUBLANES)
            _shift_copies(dcx_ref, ds_ref, lanes, t + HALO - SUBLANES)
            _conv_chunk(cw_ref, dcx_ref, ds_ref, dh_ref, lanes, t, CONV_WIDTH - 1, True)
            _conv_w_grad_chunk(hext_ref, hs_ref, dcx_ref, dcw_acc_ref, lanes, t,
                               HALO - (CONV_WIDTH - 1))
        for r0 in range(0, t, ROW_CHUNK):
            rows = slice(r0, r0 + ROW_CHUNK)
            dh, sg = dh_ref[rows, :], sg_ref[rows, :]
            emit(0, c, dh * sg, rows)
            emit(c, c, dh * pr_ref[rows, 0:c] * sg * (1.0 - sg), rows)

        kv_ref[0:WINDOW, :] = jnp.where(first, 0.0, kvh_ref[...])
        kv_ref[WINDOW:WINDOW + t, :] = pr_ref[:, dm.o_k:dm.o_k + 2 * KV_W]
        dkv_ref[0:t, :] = jnp.zeros((t, 2 * KV_W), F32)
        dkv_ref[t:t + WINDOW, :] = carry_kv_ref[...]
        lane = lax.broadcasted_iota(jnp.int32, (1, LANES), 1)
        lo = lane < HEAD_DIM
        k2, khalf, vhalf = _kv_operands(kv_ref, lo)
        tn_dims = (((0,), (0,)), ((), ()))
        nt_dims = (((1,), (1,)), ((), ()))
        dsk = jnp.zeros((1, LANES), F32)
        stacks = [_stacked_heads(dm, kvh) for kvh in range(N_KV)]
        sink_cols = [_sink_column(sinks_ref, [2 * p + half for p, half in st]) for st in stacks]
        prev = _from_previous(rs)
        hs2 = rs // 2
        for qb in range(nq):
            r0 = qb * WINDOW
            rows = slice(r0, r0 + 2 * WINDOW)
            no_previous = first if qb == 0 else None
            dka = [None, None]
            dva = [None, None]
            for kvh in range(N_KV):
                pairs = [p for p, half in stacks[kvh] if half == 0]
                for b, (p, half) in enumerate(stacks[kvh]):
                    qp = pr_ref[r0:r0 + WINDOW, dm.o_q + LANES * p:dm.o_q + LANES * (p + 1)] * SCALE
                    qs_ref[b * WINDOW:(b + 1) * WINDOW, :] = jnp.where(
                        lo if half == 0 else jnp.logical_not(lo), qp, 0.0).astype(BF16)
                for b, p in enumerate(pairs):
                    ag = pr_ref[r0:r0 + WINDOW, dm.o_ag + LANES * p:dm.o_ag + LANES * (p + 1)]
                    dya = dy_ref[r0:r0 + WINDOW, c + LANES * p:c + LANES * (p + 1)]
                    do_ref[b * WINDOW:(b + 1) * WINDOW, :] = dya * (ag * _sigmoid(ag))
                qs = qs_ref[...]
                d_o = do_ref[...]
                d_o_b = d_o.astype(BF16)
                v_lo, v_hi = vhalf[kvh][0][rows], vhalf[kvh][1][rows]
                prob, m, inv = _softmax_with_sink(qs, k2[kvh][rows], prev, no_previous,
                                                  sink_cols[kvh])
                pb = _band_split(prob, prev)
                o_lo = jnp.dot(pb[0:hs2], v_lo, preferred_element_type=F32)
                o_hi = jnp.dot(pb[hs2:rs], v_hi, preferred_element_type=F32)
                delta = jnp.concatenate([jnp.sum(d_o * o_lo, axis=1, keepdims=True),
                                         jnp.sum(d_o * o_hi, axis=1, keepdims=True)], axis=0)
                dp = jnp.concatenate(
                    [lax.dot_general(d_o_b, v_lo, nt_dims, preferred_element_type=F32),
                     lax.dot_general(d_o_b, v_hi, nt_dims, preferred_element_type=F32)], axis=0)
                dsb = _band_split(prob * (_band_merge(dp, prev) - delta), prev)
                sink_grad = jnp.exp(sink_cols[kvh] - m) * inv * delta
                for b, (p, half) in enumerate(stacks[kvh]):
                    dsink = -jnp.sum(sink_grad[b * WINDOW:(b + 1) * WINDOW])
                    dsk = dsk + jnp.where(lane == 2 * p + half, dsink, 0.0)
                dq_lo = jnp.dot(dsb[0:hs2], khalf[kvh][0][rows], preferred_element_type=F32)
                dq_hi = jnp.dot(dsb[hs2:rs], khalf[kvh][1][rows], preferred_element_type=F32)
                dka[kvh] = lax.dot_general(dsb, qs, tn_dims, preferred_element_type=F32)
                d_o_half = jnp.concatenate([jnp.where(lo, d_o, 0.0), jnp.where(lo, 0.0, d_o)],
                                           axis=0).astype(BF16)
                dva[kvh] = lax.dot_general(pb, d_o_half, tn_dims, preferred_element_type=F32)
                for b, p in enumerate(pairs):
                    blk = slice(b * WINDOW, (b + 1) * WINDOW)
                    cols = slice(LANES * p, LANES * (p + 1))
                    ag = pr_ref[r0:r0 + WINDOW, dm.o_ag + cols.start:dm.o_ag + cols.stop]
                    dya = dy_ref[r0:r0 + WINDOW, c + cols.start:c + cols.stop]
                    dqpair = (dq_lo[blk] + dq_hi[blk]) * SCALE
                    d_ag = dya * (o_lo[blk] + o_hi[blk]) * _dsilu(ag, _sigmoid(ag))
                    dpr_ref[r0:r0 + WINDOW, dm.o_q + cols.start:dm.o_q + cols.stop] = (
                        dqpair.astype(BF16))
                    dbin_ref[0:1, dm.o_q + cols.start:dm.o_q + cols.stop] += jnp.sum(
                        dqpair, axis=0, keepdims=True)
                    dpr_ref[r0:r0 + WINDOW, dm.o_ag + cols.start:dm.o_ag + cols.stop] = (
                        d_ag.astype(BF16))
                    dbin_ref[0:1, dm.o_ag + cols.start:dm.o_ag + cols.stop] += jnp.sum(
                        d_ag, axis=0, keepdims=True)
            fold = [x + pltpu.roll(x, HEAD_DIM, 1) for x in (dka[0], dka[1], dva[0], dva[1])]
            dkv_ref[r0:r0 + 2 * WINDOW, 0:KV_W] += jnp.where(lo, fold[0], fold[1])
            dkv_ref[r0:r0 + 2 * WINDOW, KV_W:2 * KV_W] += jnp.where(lo, fold[2], fold[3])
        dsk_ref[0:1, :] += dsk
        carry_kv_ref[...] = dkv_ref[0:WINDOW, :]
        emit(dm.o_k, 2 * KV_W, dkv_ref[WINDOW:WINDOW + t, :])

        @pl.when(g == nt - 1)
        def _():
            for j in range(CONV_WIDTH):
                dcw_ref[j:j + 1, :] = jnp.sum(dcw_acc_ref[j], axis=0, keepdims=True)
            dcw_ref[CONV_WIDTH:CONV_ROWS, :] = jnp.zeros((CONV_ROWS - CONV_WIDTH, c), F32)

    rev = lambda g: nt - 1 - g
    vec = pl.BlockSpec((1, c), lambda g: (0, 0))
    const = lambda shape: pl.BlockSpec(shape, lambda g: (0, 0))
    scratch = [pltpu.VMEM((t + HALO, c), F32), pltpu.VMEM((t + WINDOW, 2 * KV_W), F32),
               pltpu.VMEM((t + HALO, c), F32), pltpu.VMEM((t + WINDOW, 2 * KV_W), F32),
               pltpu.VMEM((HALO, c), F32), pltpu.VMEM((WINDOW, 2 * KV_W), F32),
               pltpu.VMEM((SUBLANES - 1, t + HALO, LANES), F32),
               pltpu.VMEM((SUBLANES - 1, t + HALO, LANES), F32),
               pltpu.VMEM((t, c), F32), pltpu.VMEM((CONV_ROWS, SUBLANES, c), F32),
               pltpu.VMEM((rs, LANES), BF16), pltpu.VMEM((rs // 2, LANES), F32),
               pltpu.VMEM((t, c), F32)]
    host = _Hosting(list(copies), 9, 5)
    res = pl.pallas_call(
        _ride(body, host, 9, 5, len(scratch), nt), name="mixer_bwd",
        out_shape=[jax.ShapeDtypeStruct((dm.s, dm.din), BF16),
                   jax.ShapeDtypeStruct((CONV_ROWS, c), F32),
                   jax.ShapeDtypeStruct((8, c), F32),
                   jax.ShapeDtypeStruct((8, dm.din), F32),
                   jax.ShapeDtypeStruct((8, LANES), F32)]
        + host.out_shapes,
        grid=(nt,),
        in_specs=[pl.BlockSpec(memory_space=pltpu.SMEM)] + _mixer_specs(dm, t, rev)
        + [pl.BlockSpec((t, dm.d), lambda g: (rev(g), 0)), pl.BlockSpec((t, c), lambda g: (rev(g), 0)),
           pl.BlockSpec((CONV_ROWS, c), lambda g: (0, 0)), vec, vec] + [_ANY] * len(host.args),
        out_specs=[pl.BlockSpec((t, dm.din), lambda g: (rev(g), 0)),
                   const((CONV_ROWS, c)), const((8, c)), const((8, dm.din)), const((8, LANES))]
        + [_ANY] * len(host.out_shapes),
        scratch_shapes=scratch + host.scratch, input_output_aliases=host.aliases,
        compiler_params=_cparams(("arbitrary",)),
    )(sinks, proj, proj, proj, dymix, conv, conv_w, cln_g, cln_b, *host.args)
    return res[:5], res[5:]


def _outproj_ln(dm, ymix, w_out, b_out, x, ln_g, ln_b, target, tm, copies=()):
    last = target is not None
    d = dm.d
    n_in = 7 if last else 6
    host = _Hosting(list(copies), n_in, 3)
    n_hin, n_hout = len(host.args), len(host.out_shapes)
    steps = dm.s // tm

    def body(*refs):
        y_ref, w_ref, bo_ref, x_ref, g_ref, b_ref = refs[:6]
        h_in = refs[n_in:n_in + n_hin]
        h_out = refs[n_in + n_hin + 3:n_in + n_hin + 3 + n_hout]
        h_sems = refs[n_in + n_hin + 3 + n_hout:]
        if copies:
            @pl.when(pl.program_id(0) == 0)
            def _():
                host.start(h_in, h_out, h_sems)

        z = dm.alpha * x_ref[...] + (
            jnp.dot(y_ref[...], w_ref[...], preferred_element_type=F32) + bo_ref[...])
        mu = jnp.mean(z, axis=1, keepdims=True)
        dlt = z - mu
        var = jnp.mean(dlt * dlt, axis=1, keepdims=True)
        out = dlt * lax.rsqrt(var + LN_EPS) * g_ref[...] + b_ref[...]
        if last:
            t_ref = refs[6]
            z_ref, dout_ref, loss_ref = refs[n_in + n_hin:n_in + n_hin + 3]
            z_ref[...] = z
            err = out - t_ref[...]
            dout_ref[...] = err * (1.0 / d)

            @pl.when(pl.program_id(0) == 0)
            def _():
                loss_ref[...] = jnp.zeros_like(loss_ref)

            loss_ref[...] += 0.5 * jnp.sum(jnp.mean(err * err, axis=1, keepdims=True), axis=0,
                                           keepdims=True)
        else:
            z_ref, o_ref, ob_ref = refs[n_in + n_hin:n_in + n_hin + 3]
            z_ref[...] = z
            o_ref[...] = out
            ob_ref[...] = out.astype(BF16)

        if copies:
            @pl.when(pl.program_id(0) == steps - 1)
            def _():
                host.finish(h_in, h_out, h_sems)

    row = pl.BlockSpec((tm, d), lambda i: (i, 0))
    vec = pl.BlockSpec((1, d), lambda i: (0, 0))
    in_specs = [row, pl.BlockSpec((d, d), lambda i: (0, 0)), vec, row, vec, vec]
    args = [ymix, w_out, b_out, x, ln_g, ln_b]
    act = jax.ShapeDtypeStruct((dm.s, d), F32)
    if last:
        in_specs.append(row)
        args.append(target)
        out_shape = [act, act, jax.ShapeDtypeStruct((8, LANES), F32)]
        out_specs = [row, row, pl.BlockSpec((8, LANES), lambda i: (0, 0))]
    else:
        out_shape = [act, act, jax.ShapeDtypeStruct((dm.s, d), BF16)]
        out_specs = [row, row, row]
    res = pl.pallas_call(
        body, name="outproj_ln_loss" if last else "outproj_ln",
        out_shape=out_shape + host.out_shapes, grid=(steps,),
        in_specs=in_specs + [_ANY] * n_hin, out_specs=out_specs + [_ANY] * n_hout,
        scratch_shapes=host.scratch, input_output_aliases=host.aliases,
        compiler_params=_cparams(("arbitrary",)),
    )(*args, *host.args)
    return res[:3], res[3:]


def _ln_bwd(dm, dout, z, ln_g, w_out, tm, copies=()):
    d = dm.d
    host = _Hosting(list(copies), 4, 4)

    def body(do_ref, z_ref, g_ref, w_ref, dz_ref, dzb_ref, sm_ref, dy_ref):
        @pl.when(pl.program_id(0) == 0)
        def _():
            sm_ref[...] = jnp.zeros_like(sm_ref)

        z = z_ref[...]
        mu = jnp.mean(z, axis=1, keepdims=True)
        dlt = z - mu
        var = jnp.mean(dlt * dlt, axis=1, keepdims=True)
        rstd = lax.rsqrt(var + LN_EPS)
        zhat = dlt * rstd
        do = do_ref[...]
        dzh = do * g_ref[...]
        dz = rstd * (dzh - jnp.mean(dzh, axis=1, keepdims=True)
                     - zhat * jnp.mean(dzh * zhat, axis=1, keepdims=True))
        dzb = dz.astype(BF16)
        dz_ref[...] = dz
        dzb_ref[...] = dzb
        dy_ref[...] = lax.dot_general(dzb, w_ref[...], (((1,), (1,)), ((), ())),
                                      preferred_element_type=F32)
        sm_ref[0:1, :] += jnp.sum(do * zhat, axis=0, keepdims=True)
        sm_ref[1:2, :] += jnp.sum(do, axis=0, keepdims=True)
        sm_ref[2:3, :] += jnp.sum(dz, axis=0, keepdims=True)

    row = pl.BlockSpec((tm, d), lambda i: (i, 0))
    act = jax.ShapeDtypeStruct((dm.s, d), F32)
    res = pl.pallas_call(
        _ride(body, host, 4, 4, 0, dm.s // tm), name="ln_bwd_dymix",
        out_shape=[act, jax.ShapeDtypeStruct((dm.s, d), BF16), jax.ShapeDtypeStruct((8, d), F32), act]
        + host.out_shapes,
        grid=(dm.s // tm,),
        in_specs=[row, row, pl.BlockSpec((1, d), lambda i: (0, 0)),
                  pl.BlockSpec((d, d), lambda i: (0, 0))] + [_ANY] * len(host.args),
        out_specs=[row, row, pl.BlockSpec((8, d), lambda i: (0, 0)), row]
        + [_ANY] * len(host.out_shapes),
        scratch_shapes=host.scratch, input_output_aliases=host.aliases,
        compiler_params=_cparams(("arbitrary",)),
    )(dout, z, ln_g, w_out, *host.args)
    return res[:4], res[4:]


def _pair_sum(part, got, tr):
    _, _, r, w = part.shape

    def body(c_ref, p_ref, g_ref, o_ref):
        o_ref[...] = (p_ref[...] + g_ref[...]).astype(BF16)

    return pl.pallas_call(
        body, name="grad_pair_sum",
        out_shape=jax.ShapeDtypeStruct((N_CHIPS, r, w), BF16),
        grid_spec=pltpu.PrefetchScalarGridSpec(
            num_scalar_prefetch=1, grid=(N_CHIPS, r // tr),
            in_specs=[pl.BlockSpec((None, None, tr, w), lambda k, i, c_ref: (k, c_ref[0], i, 0)),
                      pl.BlockSpec((None, tr, w), lambda k, i, c_ref: (k, i, 0))],
            out_specs=pl.BlockSpec((None, tr, w), lambda k, i, c_ref: (k, i, 0))),
        compiler_params=_cparams(("parallel", "parallel")),
    )(lax.axis_index("c").reshape(1).astype(jnp.int32), part, got)


def _final_sum(part, got, recv, tr):
    _, _, r, w = part.shape

    def body(idx_ref, p_ref, g_ref, r0_ref, r1_ref, r2_ref, o_ref):
        acc = p_ref[...] + g_ref[...]
        for ref in (r0_ref, r1_ref, r2_ref):
            acc = acc + ref[...].astype(F32)
        o_ref[...] = acc

    x, y, c = _position()
    idx = jnp.stack([c, 2 * x + y, 2 * (1 - x) + y, 2 * x + (1 - y),
                     2 * (1 - x) + (1 - y)]).astype(jnp.int32)
    other = lambda j: pl.BlockSpec((None, tr, w), lambda i, s: (s[2 + j], i, 0))
    return pl.pallas_call(
        body, name="grad_final_sum",
        out_shape=jax.ShapeDtypeStruct((r, w), F32),
        grid_spec=pltpu.PrefetchScalarGridSpec(
            num_scalar_prefetch=1, grid=(r // tr,),
            in_specs=[pl.BlockSpec((None, None, tr, w), lambda i, s: (s[1], s[0], i, 0)),
                      pl.BlockSpec((None, tr, w), lambda i, s: (s[1], i, 0)),
                      other(0), other(1), other(2)],
            out_specs=pl.BlockSpec((tr, w), lambda i, s: (i, 0))),
        compiler_params=_cparams(("parallel",)),
    )(idx, part, got, recv, recv, recv)


def _adamw_math(w, g, m, v):
    m = ADAM_B1 * m + (1.0 - ADAM_B1) * g
    v = ADAM_B2 * v + (1.0 - ADAM_B2) * (g * g)
    m_hat = m / (1.0 - ADAM_B1 ** ADAM_STEP)
    v_hat = v / (1.0 - ADAM_B2 ** ADAM_STEP)
    delta = -ADAM_LR * (m_hat / (jnp.sqrt(v_hat) + ADAM_EPS) + ADAM_WD * w)
    return delta, m, v


def _adamw(w, g, m, v, tr):
    depth, r, width = w.shape

    def body(w_ref, g_ref, m_ref, v_ref, d_ref, nm_ref, nv_ref):
        d_ref[...], nm_ref[...], nv_ref[...] = _adamw_math(
            w_ref[...], g_ref[...], m_ref[...], v_ref[...])

    spec = pl.BlockSpec((None, tr, width), lambda l, i: (l, i, 0))
    shape = jax.ShapeDtypeStruct(w.shape, F32)
    return pl.pallas_call(
        body, name="adamw", out_shape=(shape, shape, shape), grid=(depth, r // tr),
        in_specs=[spec] * 4, out_specs=(spec, spec, spec),
        compiler_params=_cparams(("parallel", "parallel")),
    )(w, g, m, v)


def _gather_direct(block):
    def copies(ins, outs, send, recv):
        x, y, c = _position()
        out = []
        for k in range(1, N_DEV):
            peer = (1 - x if k & 4 else x, 1 - y if k & 2 else y, 1 - c if k & 1 else c)
            out.append((pltpu.make_async_remote_copy(
                src_ref=ins[0], dst_ref=outs[0].at[4 * x + 2 * y + c],
                send_sem=send.at[0, k - 1], recv_sem=recv.at[0, k - 1],
                device_id=peer, device_id_type=MESH), k - 1, peer))
        local = pltpu.make_async_copy(ins[0], outs[0].at[4 * x + 2 * y + c], send.at[0, N_DEV - 1])
        return out, local

    def start(ins, outs, send, recv):
        remote, local = copies(ins, outs, send, recv)
        local.start()
        for cp, _, _ in remote:
            cp.start()

    def finish(ins, outs, send, recv):
        remote, local = copies(ins, outs, send, recv)
        x, y, c = _position()
        for _, k, (px, py, pc) in remote:
            pltpu.make_async_remote_copy(
                src_ref=ins[0], dst_ref=outs[0].at[4 * px + 2 * py + pc],
                send_sem=send.at[0, k], recv_sem=recv.at[0, k],
                device_id=(x, y, c), device_id_type=MESH).wait_recv()
        for cp, _, _ in remote:
            cp.wait_send()
        local.wait()

    landing = [jax.ShapeDtypeStruct((N_DEV,) + block.shape, block.dtype)]
    return _Copies([block], landing, {}, (1, N_DEV), start, finish)


def _small_sum_adamw(gathered, w, m, v):
    prows = w.shape[0]

    def body(g_ref, w_ref, m_ref, v_ref, tot_ref, d_ref, nm_ref, nv_ref):
        tot = g_ref[0]
        for d in range(1, N_DEV):
            tot = tot + g_ref[d]
        tot_ref[...] = tot
        d_ref[...], nm_ref[...], nv_ref[...] = _adamw_math(
            w_ref[...], tot[0:prows, :], m_ref[...], v_ref[...])

    vm = pl.BlockSpec(memory_space=pltpu.VMEM)
    pshape = jax.ShapeDtypeStruct(w.shape, F32)
    return pl.pallas_call(
        body, name="small_sum_adamw",
        out_shape=(jax.ShapeDtypeStruct(gathered.shape[1:], F32), pshape, pshape, pshape),
        in_specs=[vm] * 4, out_specs=(vm, vm, vm, vm),
        compiler_params=pltpu.CompilerParams(vmem_limit_bytes=VMEM_LIMIT),
    )(gathered, w, m, v)


def _pack_rows(vec):
    depth, n = vec.shape
    rows = -(-n // LANES)
    rows = -(-rows // 8) * 8
    return jnp.pad(vec, ((0, 0), (0, rows * LANES - n))).reshape(depth, rows, LANES)


def _pack_small(named):
    blocks = [_pack_rows(a) for a in named]
    extents = [(b.shape[1], a.shape[1]) for b, a in zip(blocks, named)]
    depth = named[0].shape[0]
    packed = jnp.concatenate(blocks, axis=1).reshape(depth * sum(r for r, _ in extents), LANES)
    return packed, extents


def _unpack_small(packed, extents, depth):
    per_layer = sum(r for r, _ in extents)
    packed = packed.reshape(depth, per_layer, LANES)
    out, r0 = [], 0
    for rows, n in extents:
        out.append(packed[:, r0:r0 + rows, :].reshape(depth, rows * LANES)[:, :n])
        r0 += rows
    return out


def kernel(x, w_in, b_in, conv_w, conv_b, conv_ln_g, conv_ln_b, sinks, w_out, b_out, ln_g, ln_b, loss_target, m_w_in, m_b_in, m_conv_w, m_conv_b, m_conv_ln_g, m_conv_ln_b, m_sinks, m_w_out, m_b_out, m_ln_g, m_ln_b, v_w_in, v_b_in, v_conv_w, v_conv_b, v_conv_ln_g, v_conv_ln_b, v_sinks, v_w_out, v_b_out, v_ln_g, v_ln_b):
    depth, d, din_shard = w_in.shape
    s = x.shape[1]
    c_shard = conv_w.shape[2]
    dm = _Dims(s, d, N_DEV * c_shard, depth)
    assert dm.din == N_DEV * din_shard and x.shape[0] == 1 and sinks.shape[1] == dm.nh
    d_shard = w_out.shape[1]
    c, din = dm.c, dm.din

    t_mix = _tile(s, 256, WINDOW)
    tm_row = _tile(s, 256, 8)
    tm_big = _tile(s, 1024, 8)

    w_in_t, m_w_in_t, v_w_in_t = (a.transpose(0, 2, 1) for a in (w_in, m_w_in, v_w_in))
    w_in_b, w_out_b = w_in_t.astype(BF16), w_out.astype(BF16)
    conv_w_pad = jnp.pad(conv_w, ((0, 0), (0, CONV_ROWS - CONV_WIDTH), (0, 0)))
    first = _run_copies([_gather_own([w_in_b[0], conv_w_pad])], "weights_gather_own")
    g_in, g_conv = _run_copies([_gather_forward(first)], "weights_gather_forward")
    g_out = None
    conv_w_full = g_conv.transpose(1, 2, 0, 3).reshape(depth, CONV_ROWS, c)
    by_cols = lambda g: g.reshape(din, d)
    by_rows = lambda g: g.reshape(d, d)

    xs = x[0]
    xb = xs.astype(BF16)
    saved = []
    loss_part = dout = None
    for l in range(depth):
        w_in_l = by_cols(g_in)
        riders = [_gather_own([w_out_b[0]])] if l == 0 else [_gather_forward([g_out])]
        proj, (g_out,) = _matmul(xb, w_in_l, tb=True, tm=tm_big, tn=_tile(din, 768), tk=d,
                                 out_dtype=F32, name="in_proj", bias=b_in[l][None, :],
                                 copies=riders)
        riders = [_gather_forward([g_out])] if l == 0 else []
        if l + 1 < depth:
            riders.append(_gather_own([w_in_b[l + 1]]))
        (ymix, conv), landed = _mixer_fwd(dm, proj, conv_w_full[l], conv_b[l][None, :],
                                          conv_ln_g[l][None, :], conv_ln_b[l][None, :], sinks[l],
                                          t_mix, copies=riders)
        if l == 0:
            g_out = landed[0]
        g_in_next = landed[-1] if l + 1 < depth else None
        w_out_l = by_rows(g_out)
        target = loss_target[0] if l == depth - 1 else None
        riders = []
        if l + 1 < depth:
            riders = [_gather_forward([g_in_next]), _gather_own([w_out_b[l + 1]])]
        res, landed = _outproj_ln(dm, ymix, w_out_l, b_out[l][None, :], xs, ln_g[l][None, :],
                                  ln_b[l][None, :], target, tm_row, copies=riders)
        saved.append((xb, proj, ymix, conv, res[0], w_in_l, w_out_l))
        if l + 1 < depth:
            g_in, g_out = landed
            xs, xb = res[1], res[2]
        else:
            dout, loss_part = res[1], res[2]

    g_w_in, g_w_out = [None] * depth, [None] * depth
    small_parts = [None] * depth
    dconv_w = [None] * depth
    tr_in, tr_out = _tile(din_shard, 512, 8), _tile(d_shard, 256, 8)
    parts_in = None
    for l in reversed(range(depth)):
        xb, proj, ymix, conv, z, w_in_l, w_out_l = saved[l]
        riders = [_scatter_sibling([parts_in])] if parts_in is not None else []
        (dz, dzb, ln_small, dymix), landed = _ln_bwd(dm, dout, z, ln_g[l][None, :], w_out_l, tm_row,
                                                     copies=riders)
        if parts_in is not None:
            got_in = landed[0]
            sums_in = _pair_sum(parts_in, got_in, tr_in)
        dw_out = _matmul(ymix, dzb, ta=True, tm=_tile(d, 1024), tn=_tile(d, 1024), tk=s,
                         out_dtype=F32, name="dw_out")
        riders = [_scatter_chips([sums_in])] if parts_in is not None else []
        (dproj, dcw, conv_small, dbin, dsk), landed = _mixer_bwd(
            dm, proj, dymix, conv, conv_w_full[l], conv_ln_g[l][None, :], conv_ln_b[l][None, :],
            sinks[l], t_mix, copies=riders)
        if parts_in is not None:
            g_w_in[l + 1] = _final_sum(parts_in, got_in, landed[0], tr_in)
        small_parts[l] = [dbin[0], conv_small[0], conv_small[1], conv_small[2], dsk[0, :dm.nh],
                          ln_small[2], ln_small[0], ln_small[1]]
        dconv_w[l] = dcw
        parts_out = dw_out.reshape(N_CHIPS, 2, d_shard, d)
        dx = dict(tm=_tile(s, 512), tn=_tile(d, 1024), tk=din, out_dtype=F32, name="dx", resid=dz,
                  resid_scale=dm.alpha)
        dw = dict(ta=True, tm=_tile(din, 768), tn=_tile(d, 1024), tk=s, out_dtype=F32, name="dw_in")
        if l > 0:
            dout, (got_out,) = _matmul(dproj, w_in_l, copies=[_scatter_sibling([parts_out])], **dx)
            sums_out = _pair_sum(parts_out, got_out, tr_out)
            dw_in_t, (recv_out,) = _matmul(dproj, xb, copies=[_scatter_chips([sums_out])], **dw)
            parts_in = dw_in_t.reshape(N_CHIPS, 2, din_shard, d)
        else:
            packed_g, extents = _pack_small([jnp.stack([small_parts[k][n] for k in range(depth)])
                                             for n in range(len(small_parts[0]))])
            prows = packed_g.shape[0]
            conv_rows = depth * CONV_ROWS * c // LANES
            partial = jnp.concatenate(
                [packed_g, jnp.stack(dconv_w).reshape(conv_rows, LANES), loss_part], axis=0)
            dw_in_t, (got_out, small_all) = _matmul(
                dproj, xb, copies=[_scatter_sibling([parts_out]), _gather_direct(partial)], **dw)
            parts_in = dw_in_t.reshape(N_CHIPS, 2, din_shard, d)
            sums_out = _pair_sum(parts_out, got_out, tr_out)
            got_in, recv_out = _run_copies(
                [_scatter_sibling([parts_in]), _scatter_chips([sums_out])], "grad_exchange_tail")
            sums_in = _pair_sum(parts_in, got_in, tr_in)
            dout, (recv_in,) = _matmul(dproj, w_in_l, copies=[_scatter_chips([sums_in])], **dx)
            g_w_in[0] = _final_sum(parts_in, got_in, recv_in, tr_in)
        g_w_out[l] = _final_sum(parts_out, got_out, recv_out, tr_out)
    grad_x = dout[None]

    small_w = [b_in, conv_b, conv_ln_g, conv_ln_b, sinks, b_out, ln_g, ln_b]
    small_m = [m_b_in, m_conv_b, m_conv_ln_g, m_conv_ln_b, m_sinks, m_b_out, m_ln_g, m_ln_b]
    small_v = [v_b_in, v_conv_b, v_conv_ln_g, v_conv_ln_b, v_sinks, v_b_out, v_ln_g, v_ln_b]
    packed_w, _ = _pack_small(small_w)
    packed_m, _ = _pack_small(small_m)
    packed_v, _ = _pack_small(small_v)
    total, sm_delta, sm_m, sm_v = _small_sum_adamw(small_all, packed_w, packed_m, packed_v)
    loss = total[prows + conv_rows, 0]
    dconv_w_full = total[prows:prows + conv_rows].reshape(depth, CONV_ROWS, c)
    me = 4 * lax.axis_index("x") + 2 * lax.axis_index("y") + lax.axis_index("c")
    grad_conv_w = lax.dynamic_slice_in_dim(dconv_w_full, me * c_shard, c_shard, axis=2)[:, :CONV_WIDTH]

    grads_small = _unpack_small(total[:prows], extents, depth)
    delta_small = _unpack_small(sm_delta, extents, depth)
    newm_small = _unpack_small(sm_m, extents, depth)
    newv_small = _unpack_small(sm_v, extents, depth)

    grad_w_in_t = jnp.stack(g_w_in)
    grad_w_out = jnp.stack(g_w_out)
    grad_w_in, d_w_in, nm_w_in, nv_w_in = (a.transpose(0, 2, 1) for a in (
        grad_w_in_t, *_adamw(w_in_t, grad_w_in_t, m_w_in_t, v_w_in_t, tr_in)))
    d_w_out, nm_w_out, nv_w_out = _adamw(w_out, grad_w_out, m_w_out, v_w_out, _tile(d_shard, 256, 8))
    d_cw, nm_cw, nv_cw = _adamw(conv_w, grad_conv_w, m_conv_w, v_conv_w, CONV_WIDTH)

    def assemble(w_in_leaf, conv_w_leaf, w_out_leaf, small):
        b_in_, conv_b_, cg_, cb_, sinks_, b_out_, ln_g_, ln_b_ = small
        return [w_in_leaf, b_in_, conv_w_leaf, conv_b_, cg_, cb_, sinks_, w_out_leaf, b_out_,
                ln_g_, ln_b_]

    return (loss, grad_x,
            *assemble(grad_w_in, grad_conv_w, grad_w_out, grads_small),
            *assemble(d_w_in, d_cw, d_w_out, delta_small),
            *assemble(nm_w_in, nm_cw, nm_w_out, newm_small),
            *assemble(nv_w_in, nv_cw, nv_w_out, newv_small))
```

```python
import functools

import jax
import jax.numpy as jnp
from jax import lax
from jax.experimental import pallas as pl
from jax.experimental.pallas import tpu as pltpu

F32 = jnp.float32
BF16 = jnp.bfloat16
MESH = pl.DeviceIdType.MESH

N_DEV = 8
N_CHIPS = 4
HEAD_DIM = 64
N_KV = 2
KV_W = N_KV * HEAD_DIM
CONV_WIDTH = 31
CONV_ROWS = 32
HALO = 32
WINDOW = 128
LN_EPS = 1e-5
NEG_INF = -1e30
LANES = 128

ADAM_LR = 0.001
ADAM_B1 = 0.9
ADAM_B2 = 0.999
ADAM_EPS = 1e-08
ADAM_WD = 0.01
ADAM_STEP = 10

VMEM_LIMIT = 56 * 1024 * 1024


def _tile(n, target, align=LANES):
    best = None
    for t in range(align, min(n, target) + 1, align):
        if n % t == 0:
            best = t
    return n if best is None else best


def _sigmoid(x):
    return jax.nn.sigmoid(x)


def _dsilu(x, s):
    return s * (1.0 + x * (1.0 - s))


def _cparams(sem, vmem=VMEM_LIMIT):
    return pltpu.CompilerParams(dimension_semantics=sem, vmem_limit_bytes=vmem)


_ANY = pl.BlockSpec(memory_space=pl.ANY)


def _position():
    return lax.axis_index("x"), lax.axis_index("y"), lax.axis_index("c")


def _other_chips(x, y):
    return [(1 - x, y), (x, 1 - y), (1 - x, 1 - y)]


class _Copies:
    def __init__(self, operands, landing, alias, sems, start, finish):
        self.operands, self.landing, self.alias, self.sems = operands, landing, alias, sems
        self.start, self.finish = start, finish


class _Hosting:
    def __init__(self, groups, n_in, n_out):
        self.groups = groups
        self.args = [a for g in groups for a in g.operands]
        self.out_shapes = [s for g in groups for s in g.landing]
        self.scratch = [pltpu.SemaphoreType.DMA(g.sems) for g in groups for _ in range(2)]
        self.aliases = {}
        i0, o0 = n_in, n_out
        for g in groups:
            for a, b in g.alias.items():
                self.aliases[i0 + a] = o0 + b
            i0 += len(g.operands)
            o0 += len(g.landing)

    def _each(self, in_refs, out_refs, sem_refs):
        i0 = o0 = 0
        for n, g in enumerate(self.groups):
            yield (g, in_refs[i0:i0 + len(g.operands)], out_refs[o0:o0 + len(g.landing)],
                   sem_refs[2 * n], sem_refs[2 * n + 1])
            i0 += len(g.operands)
            o0 += len(g.landing)

    def start(self, in_refs, out_refs, sem_refs):
        for g, ins, outs, send, recv in self._each(in_refs, out_refs, sem_refs):
            g.start(ins, outs, send, recv)

    def finish(self, in_refs, out_refs, sem_refs):
        for g, ins, outs, send, recv in self._each(in_refs, out_refs, sem_refs):
            g.finish(ins, outs, send, recv)


def _ride(body, host, n_in, n_out, n_scratch, steps):
    n_hin, n_hout = len(host.args), len(host.out_shapes)

    def wrapped(*refs):
        pos = [0]

        def take(n):
            pos[0] += n
            return refs[pos[0] - n:pos[0]]

        ins, h_in, outs, h_out = take(n_in), take(n_hin), take(n_out), take(n_hout)
        scratch, h_sems = take(n_scratch), refs[pos[0]:]
        if host.groups:
            @pl.when(pl.program_id(0) == 0)
            def _():
                host.start(h_in, h_out, h_sems)

        body(*ins, *outs, *scratch)
        if host.groups:
            @pl.when(pl.program_id(0) == steps - 1)
            def _():
                host.finish(h_in, h_out, h_sems)

    return wrapped


def _run_copies(groups, name):
    host = _Hosting(groups, 0, 0)
    n_in, n_out = len(host.args), len(host.out_shapes)

    def body(*refs):
        ins, outs, sems = refs[:n_in], refs[n_in:n_in + n_out], refs[n_in + n_out:]
        host.start(ins, outs, sems)
        host.finish(ins, outs, sems)

    return pl.pallas_call(
        body, name=name, out_shape=host.out_shapes, in_specs=[_ANY] * n_in,
        out_specs=[_ANY] * n_out, scratch_shapes=host.scratch,
        input_output_aliases=host.aliases,
    )(*host.args)


def _gather_own(shards):
    n = len(shards)

    def copies(ins, outs, send, recv):
        x, y, c = _position()
        peers = [(x, y, 1 - c)] + [(px, py, c) for px, py in _other_chips(x, y)]
        out = []
        for t in range(n):
            for k, peer in enumerate(peers):
                out.append((pltpu.make_async_remote_copy(
                    src_ref=ins[t], dst_ref=outs[t].at[4 * x + 2 * y + c],
                    send_sem=send.at[t, k], recv_sem=recv.at[t, k],
                    device_id=peer, device_id_type=MESH), t, k, peer))
        local = [pltpu.make_async_copy(ins[t], outs[t].at[4 * x + 2 * y + c], send.at[t, 4])
                 for t in range(n)]
        return out, local

    def start(ins, outs, send, recv):
        remote, local = copies(ins, outs, send, recv)
        for cp in local:
            cp.start()
        for cp, _, _, _ in remote:
            cp.start()

    def finish(ins, outs, send, recv):
        remote, local = copies(ins, outs, send, recv)
        x, y, c = _position()
        for _, t, k, (px, py, pc) in remote:
            pltpu.make_async_remote_copy(
                src_ref=ins[t], dst_ref=outs[t].at[4 * px + 2 * py + pc],
                send_sem=send.at[t, k], recv_sem=recv.at[t, k],
                device_id=(x, y, c), device_id_type=MESH).wait_recv()
        for cp, _, _, _ in remote:
            cp.wait_send()
        for cp in local:
            cp.wait()

    landing = [jax.ShapeDtypeStruct((N_DEV,) + s.shape, s.dtype) for s in shards]
    return _Copies(list(shards), landing, {}, (n, 5), start, finish)


def _gather_forward(buffers):
    n = len(buffers)

    def copies(ins, outs, send, recv):
        x, y, c = _position()
        out = []
        for t in range(n):
            for j, (px, py) in enumerate(_other_chips(x, y)):
                slot = 4 * px + 2 * py + c
                out.append((pltpu.make_async_remote_copy(
                    src_ref=ins[t].at[slot], dst_ref=outs[t].at[slot],
                    send_sem=send.at[t, j], recv_sem=recv.at[t, j],
                    device_id=(x, y, 1 - c), device_id_type=MESH), t, j, 4 * px + 2 * py + 1 - c))
        return out

    def start(ins, outs, send, recv):
        for cp, _, _, _ in copies(ins, outs, send, recv):
            cp.start()

    def finish(ins, outs, send, recv):
        x, y, c = _position()
        mine = copies(ins, outs, send, recv)
        for _, t, j, got in mine:
            pltpu.make_async_remote_copy(
                src_ref=ins[t].at[got], dst_ref=outs[t].at[got],
                send_sem=send.at[t, j], recv_sem=recv.at[t, j],
                device_id=(x, y, c), device_id_type=MESH).wait_recv()
        for cp, _, _, _ in mine:
            cp.wait_send()

    landing = [jax.ShapeDtypeStruct(b.shape, b.dtype) for b in buffers]
    return _Copies(list(buffers), landing, {t: t for t in range(n)}, (n, 3), start, finish)


def _scatter_sibling(parts):
    n = len(parts)

    def copies(ins, outs, send, recv):
        x, y, c = _position()
        return [pltpu.make_async_remote_copy(
            src_ref=ins[t].at[:, 1 - c], dst_ref=outs[t],
            send_sem=send.at[t, 0], recv_sem=recv.at[t, 0],
            device_id=(x, y, 1 - c), device_id_type=MESH) for t in range(n)]

    def start(ins, outs, send, recv):
        for cp in copies(ins, outs, send, recv):
            cp.start()

    def finish(ins, outs, send, recv):
        for cp in copies(ins, outs, send, recv):
            cp.wait()

    landing = [jax.ShapeDtypeStruct((p.shape[0],) + p.shape[2:], p.dtype) for p in parts]
    return _Copies(list(parts), landing, {}, (n, 1), start, finish)


def _scatter_chips(blocks):
    n = len(blocks)

    def start(ins, outs, send, recv):
        x, y, c = _position()
        for t in range(n):
            for j, (px, py) in enumerate(_other_chips(x, y)):
                pltpu.make_async_remote_copy(
                    src_ref=ins[t].at[2 * px + py], dst_ref=outs[t].at[2 * x + y],
                    send_sem=send.at[t, j], recv_sem=recv.at[t, j],
                    device_id=(px, py, c), device_id_type=MESH).start()

    def finish(ins, outs, send, recv):
        x, y, c = _position()
        for t in range(n):
            for j, (px, py) in enumerate(_other_chips(x, y)):
                cp = pltpu.make_async_remote_copy(
                    src_ref=ins[t].at[2 * px + py], dst_ref=outs[t].at[2 * px + py],
                    send_sem=send.at[t, j], recv_sem=recv.at[t, j],
                    device_id=(px, py, c), device_id_type=MESH)
                cp.wait_recv()
                cp.wait_send()

    landing = [jax.ShapeDtypeStruct(b.shape, b.dtype) for b in blocks]
    return _Copies(list(blocks), landing, {}, (n, 3), start, finish)


def _matmul(a, b, *, ta=False, tb=False, tm, tn, tk, out_dtype, name, bias=None, resid=None,
            resid_scale=1.0, copies=()):
    m, k = (a.shape[1], a.shape[0]) if ta else a.shape
    n = b.shape[0] if tb else b.shape[1]
    assert (b.shape[1] if tb else b.shape[0]) == k
    assert m % tm == 0 and n % tn == 0 and k % tk == 0
    ni, nj, nk = m // tm, n // tn, k // tk
    dn = (((0 if ta else 1,), (1 if tb else 0,)), ((), ()))
    n_in = 2 + (bias is not None) + (resid is not None)
    host = _Hosting(list(copies), n_in, 1)
    n_hin, n_hout = len(host.args), len(host.out_shapes)

    def body(*refs):
        a_ref, b_ref = refs[0], refs[1]
        pos = 2
        bias_ref = resid_ref = None
        if bias is not None:
            bias_ref = refs[pos]
            pos += 1
        if resid is not None:
            resid_ref = refs[pos]
            pos += 1
        h_in = refs[pos:pos + n_hin]
        pos += n_hin
        o_ref = refs[pos]
        h_out = refs[pos + 1:pos + 1 + n_hout]
        pos += 1 + n_hout
        acc_ref = refs[pos] if nk > 1 else None
        h_sems = refs[pos + (nk > 1):]
        step = (pl.program_id(0) * nj + pl.program_id(1)) * nk + pl.program_id(2)

        if copies:
            @pl.when(step == 0)
            def _():
                host.start(h_in, h_out, h_sems)

        def finish(acc):
            if bias_ref is not None:
                acc = acc + bias_ref[...]
            if resid_ref is not None:
                acc = acc + resid_scale * resid_ref[...]
            o_ref[...] = acc.astype(out_dtype)

        p = lax.dot_general(a_ref[...], b_ref[...], dn, preferred_element_type=F32)
        if nk == 1:
            finish(p)
        else:
            kk = pl.program_id(2)

            @pl.when(kk == 0)
            def _():
                acc_ref[...] = p

            @pl.when(kk > 0)
            def _():
                acc_ref[...] += p

            @pl.when(kk == nk - 1)
            def _():
                finish(acc_ref[...])

        if copies:
            @pl.when(step == ni * nj * nk - 1)
            def _():
                host.finish(h_in, h_out, h_sems)

    a_spec = (pl.BlockSpec((tk, tm), lambda i, j, kk: (kk, i)) if ta
              else pl.BlockSpec((tm, tk), lambda i, j, kk: (i, kk)))
    b_spec = (pl.BlockSpec((tn, tk), lambda i, j, kk: (j, kk)) if tb
              else pl.BlockSpec((tk, tn), lambda i, j, kk: (kk, j)))
    in_specs = [a_spec, b_spec]
    args = [a, b]
    if bias is not None:
        in_specs.append(pl.BlockSpec((1, tn), lambda i, j, kk: (0, j)))
        args.append(bias)
    if resid is not None:
        in_specs.append(pl.BlockSpec((tm, tn), lambda i, j, kk: (i, j)))
        args.append(resid)
    res = pl.pallas_call(
        body, name=name,
        out_shape=[jax.ShapeDtypeStruct((m, n), out_dtype)] + host.out_shapes,
        grid=(ni, nj, nk),
        in_specs=in_specs + [_ANY] * n_hin,
        out_specs=[pl.BlockSpec((tm, tn), lambda i, j, kk: (i, j))] + [_ANY] * n_hout,
        scratch_shapes=([pltpu.VMEM((tm, tn), F32)] if nk > 1 else []) + host.scratch,
        input_output_aliases=host.aliases,
        compiler_params=_cparams(("arbitrary",) * 3 if copies else
                                 ("parallel", "parallel", "arbitrary")),
    )(*args, *host.args)
    return (res[0], res[1:]) if copies else res[0]


class _Dims:
    def __init__(self, s, d, c, depth):
        self.s, self.d, self.c, self.depth = s, d, c, depth
        self.a = d - c
        self.nh = self.a // HEAD_DIM
        self.group = self.nh // N_KV
        self.din = 3 * c + 2 * self.a + 2 * KV_W
        self.o_q = 3 * c
        self.o_k = 3 * c + self.a
        self.o_v = self.o_k + KV_W
        self.o_ag = self.o_k + 2 * KV_W
        self.alpha = (2 * depth) ** 0.25
        assert self.nh % 2 == 0 and self.group % 2 == 0 and self.o_k % (2 * KV_W) == 0
        assert c % LANES == 0 and self.a % LANES == 0


SUBLANES = 8
TAP_ROWS = 128
ROW_CHUNK = 64


def _shift_copies(src_ref, sh_ref, lanes, rows):
    for r in range(1, SUBLANES):
        sh_ref[r - 1, 0:rows, :] = src_ref[pl.ds(r, rows), lanes]


def _tap_rows(src_ref, sh_ref, lanes, off, start, rows):
    r = off % SUBLANES
    at = pl.multiple_of(start + (off - r), SUBLANES)
    if r == 0:
        return src_ref[pl.ds(at, rows), lanes]
    return sh_ref[r - 1, pl.ds(at, rows), :]


def _conv_chunk(w_ref, src_ref, sh_ref, out_ref, lanes, t, first_off, reverse, bias_ref=None):
    def block(it, carry):
        start = pl.multiple_of(it * TAP_ROWS, TAP_ROWS)
        acc = None
        for j in range(CONV_WIDTH):
            off = first_off - j if reverse else first_off + j
            term = w_ref[j:j + 1, lanes] * _tap_rows(src_ref, sh_ref, lanes, off, start, TAP_ROWS)
            acc = term if acc is None else acc + term
        if bias_ref is not None:
            acc = acc + bias_ref[:, lanes]
        out_ref[pl.ds(start, TAP_ROWS), lanes] = acc
        return carry

    lax.fori_loop(0, t // TAP_ROWS, block, 0)


def _conv_w_grad_chunk(src_ref, sh_ref, dconv_ref, acc_ref, lanes, t, first_off):
    def block(it, accs):
        start = pl.multiple_of(it * SUBLANES, SUBLANES)
        dv = dconv_ref[pl.ds(start, SUBLANES), lanes]
        return tuple(
            acc + _tap_rows(src_ref, sh_ref, lanes, first_off + j, start, SUBLANES) * dv
            for j, acc in enumerate(accs))

    zero = jnp.zeros((SUBLANES, LANES), F32)
    accs = lax.fori_loop(0, t // SUBLANES, block, (zero,) * CONV_WIDTH, unroll=2)
    for j in range(CONV_WIDTH):
        acc_ref[j, :, lanes] += accs[j]


def _kv_operands(kv_ref, lo):
    kext = kv_ref[:, 0:KV_W]
    vext = kv_ref[:, KV_W:2 * KV_W]
    ksw = pltpu.roll(kext, HEAD_DIM, 1)
    vsw = pltpu.roll(vext, HEAD_DIM, 1)
    zero = jnp.zeros_like(kext)
    k2 = [jnp.where(lo, kext, ksw).astype(BF16), jnp.where(lo, ksw, kext).astype(BF16)]
    khalf = [[jnp.where(lo, kext, zero).astype(BF16), jnp.where(lo, zero, ksw).astype(BF16)],
             [jnp.where(lo, ksw, zero).astype(BF16), jnp.where(lo, zero, kext).astype(BF16)]]
    vhalf = [[jnp.where(lo, vext, zero).astype(BF16), jnp.where(lo, zero, vsw).astype(BF16)],
             [jnp.where(lo, vsw, zero).astype(BF16), jnp.where(lo, zero, vext).astype(BF16)]]
    return k2, khalf, vhalf


SCALE = HEAD_DIM ** -0.5


def _from_previous(rows):
    row = lax.broadcasted_iota(jnp.int32, (rows, WINDOW), 0)
    col = lax.broadcasted_iota(jnp.int32, (rows, WINDOW), 1)
    return col > (row & (WINDOW - 1))


def _sink_column(sinks_ref, heads):
    block = lax.broadcasted_iota(jnp.int32, (len(heads) * WINDOW, 1), 0) // WINDOW
    out = jnp.zeros(block.shape, F32)
    for b, head in enumerate(heads):
        out = jnp.where(block == b, sinks_ref[head], out)
    return out


def _stacked_heads(dm, kvh):
    pairs = range(kvh * (dm.group // 2), (kvh + 1) * (dm.group // 2))
    return [(p, half) for half in range(2) for p in pairs]


def _band_merge(both, prev):
    return jnp.where(prev, both[:, 0:WINDOW], both[:, WINDOW:2 * WINDOW])


def _band_split(tile, prev):
    zero = jnp.zeros_like(tile)
    return jnp.concatenate([jnp.where(prev, tile, zero), jnp.where(prev, zero, tile)],
                           axis=1).astype(BF16)


def _softmax_with_sink(qm, k2rows, prev, no_previous, sink):
    both = lax.dot_general(qm, k2rows, (((1,), (1,)), ((), ())), preferred_element_type=F32)
    s_prev = both[:, 0:WINDOW]
    if no_previous is not None:
        s_prev = jnp.where(no_previous, NEG_INF, s_prev)
    s = jnp.where(prev, s_prev, both[:, WINDOW:2 * WINDOW])
    m = jnp.maximum(jnp.max(s, axis=1, keepdims=True), sink)
    e = jnp.exp(s - m)
    den = jnp.sum(e, axis=1, keepdims=True) + jnp.exp(sink - m)
    inv = 1.0 / den
    return e * inv, m, inv


def _mixer_specs(dm, t, idx):
    return [
        pl.BlockSpec((t, dm.din), lambda g: (idx(g), 0)),
        pl.BlockSpec((HALO, 2 * dm.c), lambda g: (jnp.maximum(idx(g) * (t // HALO) - 1, 0), 0)),
        pl.BlockSpec((WINDOW, 2 * KV_W),
                     lambda g: (jnp.maximum(idx(g) * (t // WINDOW) - 1, 0), dm.o_k // (2 * KV_W))),
    ]


def _mixer_fwd(dm, proj, conv_w, conv_b, cln_g, cln_b, sinks, t, copies=()):
    c, nq = dm.c, t // WINDOW
    rs = dm.group * WINDOW
    host = _Hosting(list(copies), 8, 2)

    def body(sinks_ref, pr_ref, ch_ref, kvh_ref, cw_ref, cb_ref, cg_ref, cbb_ref,
             y_ref, conv_ref, hext_ref, kv_ref, hs_ref, qs_ref):
        i = pl.program_id(0)
        first = i == 0
        h = pr_ref[:, 0:c] * _sigmoid(pr_ref[:, c:2 * c])
        hh = ch_ref[:, 0:c] * _sigmoid(ch_ref[:, c:2 * c])
        hext_ref[0:HALO, :] = jnp.where(first, 0.0, hh)
        hext_ref[HALO:HALO + t, :] = h
        for k in range(c // LANES):
            lanes = slice(LANES * k, LANES * (k + 1))
            _shift_copies(hext_ref, hs_ref, lanes, t + HALO - SUBLANES)
            _conv_chunk(cw_ref, hext_ref, hs_ref, conv_ref, lanes, t, HALO - (CONV_WIDTH - 1),
                        False, cb_ref)
        conv = conv_ref[...]
        mu = jnp.mean(conv, axis=1, keepdims=True)
        dlt = conv - mu
        var = jnp.mean(dlt * dlt, axis=1, keepdims=True)
        u = dlt * lax.rsqrt(var + LN_EPS) * cg_ref[...] + cbb_ref[...]
        gate = pr_ref[:, 2 * c:3 * c]
        y_ref[:, 0:c] = (u * _sigmoid(u) * (gate * _sigmoid(gate))).astype(BF16)

        kv_ref[0:WINDOW, :] = jnp.where(first, 0.0, kvh_ref[...])
        kv_ref[WINDOW:WINDOW + t, :] = pr_ref[:, dm.o_k:dm.o_k + 2 * KV_W]
        lo = lax.broadcasted_iota(jnp.int32, (1, LANES), 1) < HEAD_DIM
        k2, _, vhalf = _kv_operands(kv_ref, lo)
        stacks = [_stacked_heads(dm, kvh) for kvh in range(N_KV)]
        sink_cols = [_sink_column(sinks_ref, [2 * p + half for p, half in st]) for st in stacks]
        prev = _from_previous(rs)
        for qb in range(nq):
            r0 = qb * WINDOW
            rows = slice(r0, r0 + 2 * WINDOW)
            no_previous = first if qb == 0 else None
            for kvh in range(N_KV):
                for b, (p, half) in enumerate(stacks[kvh]):
                    qp = pr_ref[r0:r0 + WINDOW, dm.o_q + LANES * p:dm.o_q + LANES * (p + 1)] * SCALE
                    qs_ref[b * WINDOW:(b + 1) * WINDOW, :] = jnp.where(
                        lo if half == 0 else jnp.logical_not(lo), qp, 0.0).astype(BF16)
                prob, _, _ = _softmax_with_sink(qs_ref[...], k2[kvh][rows], prev, no_previous,
                                                sink_cols[kvh])
                pb = _band_split(prob, prev)
                o_lo = jnp.dot(pb[0:rs // 2], vhalf[kvh][0][rows], preferred_element_type=F32)
                o_hi = jnp.dot(pb[rs // 2:rs], vhalf[kvh][1][rows], preferred_element_type=F32)
                for b, (p, _) in enumerate(stacks[kvh][:len(stacks[kvh]) // 2]):
                    blk = slice(b * WINDOW, (b + 1) * WINDOW)
                    ag = pr_ref[r0:r0 + WINDOW, dm.o_ag + LANES * p:dm.o_ag + LANES * (p + 1)]
                    y_ref[r0:r0 + WINDOW, c + LANES * p:c + LANES * (p + 1)] = (
                        (o_lo[blk] + o_hi[blk]) * (ag * _sigmoid(ag))).astype(BF16)

    vec = pl.BlockSpec((1, c), lambda g: (0, 0))
    scratch = [pltpu.VMEM((t + HALO, c), F32), pltpu.VMEM((t + WINDOW, 2 * KV_W), F32),
               pltpu.VMEM((SUBLANES - 1, t + HALO, LANES), F32), pltpu.VMEM((rs, LANES), BF16)]
    res = pl.pallas_call(
        _ride(body, host, 8, 2, len(scratch), dm.s // t), name="mixer_fwd",
        out_shape=[jax.ShapeDtypeStruct((dm.s, dm.d), BF16),
                   jax.ShapeDtypeStruct((dm.s, c), F32)]
        + host.out_shapes,
        grid=(dm.s // t,),
        in_specs=[pl.BlockSpec(memory_space=pltpu.SMEM)] + _mixer_specs(dm, t, lambda g: g)
        + [pl.BlockSpec((CONV_ROWS, c), lambda g: (0, 0)), vec, vec, vec] + [_ANY] * len(host.args),
        out_specs=[pl.BlockSpec((t, dm.d), lambda g: (g, 0)), pl.BlockSpec((t, c), lambda g: (g, 0))]
        + [_ANY] * len(host.out_shapes),
        scratch_shapes=scratch + host.scratch, input_output_aliases=host.aliases,
        compiler_params=_cparams(("arbitrary",)),
    )(sinks, proj, proj, proj, conv_w, conv_b, cln_g, cln_b, *host.args)
    return res[:2], res[2:]


def _mixer_bwd(dm, proj, dymix, conv, conv_w, cln_g, cln_b, sinks, t, copies=()):
    c, nq, nt = dm.c, t // WINDOW, dm.s // t
    rs = dm.group * WINDOW

    def body(sinks_ref, pr_ref, ch_ref, kvh_ref, dy_ref, cv_ref, cw_ref, cg_ref, cbb_ref,
             dpr_ref, dcw_ref, dsm_ref, dbin_ref, dsk_ref,
             hext_ref, kv_ref, dcx_ref, dkv_ref, carry_dc_ref, carry_kv_ref,
             hs_ref, ds_ref, dh_ref, dcw_acc_ref, qs_ref, do_ref, sg_ref):
        g = pl.program_id(0)
        i = nt - 1 - g
        first = i == 0

        @pl.when(g == 0)
        def _():
            dcw_acc_ref[...] = jnp.zeros_like(dcw_acc_ref)
            dsm_ref[...] = jnp.zeros_like(dsm_ref)
            dbin_ref[...] = jnp.zeros_like(dbin_ref)
            dsk_ref[...] = jnp.zeros_like(dsk_ref)
            carry_dc_ref[...] = jnp.zeros_like(carry_dc_ref)
            carry_kv_ref[...] = jnp.zeros_like(carry_kv_ref)

        def emit(col, width, val, rows=slice(None)):
            dpr_ref[rows, col:col + width] = val.astype(BF16)
            dbin_ref[0:1, col:col + width] += jnp.sum(val, axis=0, keepdims=True)

        hh = ch_ref[:, 0:c] * _sigmoid(ch_ref[:, c:2 * c])
        hext_ref[0:HALO, :] = jnp.where(first, 0.0, hh)
        dcx_ref[t:t + HALO, :] = carry_dc_ref[...]
        for r0 in range(0, t, ROW_CHUNK):
            rows = slice(r0, r0 + ROW_CHUNK)
            sg = _sigmoid(pr_ref[rows, c:2 * c])
            sg_ref[rows, :] = sg
            hext_ref[HALO + r0:HALO + r0 + ROW_CHUNK, :] = pr_ref[rows, 0:c] * sg
            conv = cv_ref[rows, :]
            mu = jnp.mean(conv, axis=1, keepdims=True)
            dlt = conv - mu
            var = jnp.mean(dlt * dlt, axis=1, keepdims=True)
            rstd = lax.rsqrt(var + LN_EPS)
            xhat = dlt * rstd
            u = xhat * cg_ref[...] + cbb_ref[...]
            su = _sigmoid(u)
            gate = pr_ref[rows, 2 * c:3 * c]
            sgate = _sigmoid(gate)
            dyc = dy_ref[rows, 0:c]
            emit(2 * c, c, dyc * (u * su) * _dsilu(gate, sgate), rows)
            du = dyc * (gate * sgate) * _dsilu(u, su)
            dsm_ref[1:2, :] += jnp.sum(du * xhat, axis=0, keepdims=True)
            dsm_ref[2:3, :] += jnp.sum(du, axis=0, keepdims=True)
            dxh = du * cg_ref[...]
            dconv = rstd * (dxh - jnp.mean(dxh, axis=1, keepdims=True)
                            - xhat * jnp.mean(dxh * xhat, axis=1, keepdims=True))
            dsm_ref[0:1, :] += jnp.sum(dconv, axis=0, keepdims=True)
            dcx_ref[rows, :] = dconv
        carry_dc_ref[...] = dcx_ref[0:HALO, :]
        for k in range(c // LANES):
            lanes = slice(LANES * k, LANES * (k + 1))
            _shift_copies(hext_ref, hs_ref, lanes, t + HALO - SUBLANES)
            _shift_copies(dcx_ref, ds_ref, lanes, t + HALO - SUBLANES)
            _conv_chunk(cw_ref, dcx_ref, ds_ref, dh_ref, lanes, t, CONV_WIDTH - 1, True)
            _conv_w_grad_chunk(hext_ref, hs_ref, dcx_ref, dcw_acc_ref, lanes, t,
                               HALO - (CONV_WIDTH - 1))
        for r0 in range(0, t, ROW_CHUNK):
            rows = slice(r0, r0 + ROW_CHUNK)
            dh, sg = dh_ref[rows, :], sg_ref[rows, :]
            emit(0, c, dh * sg, rows)
            emit(c, c, dh * pr_ref[rows, 0:c] * sg * (1.0 - sg), rows)

        kv_ref[0:WINDOW, :] = jnp.where(first, 0.0, kvh_ref[...])
        kv_ref[WINDOW:WINDOW + t, :] = pr_ref[:, dm.o_k:dm.o_k + 2 * KV_W]
        dkv_ref[0:t, :] = jnp.zeros((t, 2 * KV_W), F32)
        dkv_ref[t:t + WINDOW, :] = carry_kv_ref[...]
        lane = lax.broadcasted_iota(jnp.int32, (1, LANES), 1)
        lo = lane < HEAD_DIM
        k2, khalf, vhalf = _kv_operands(kv_ref, lo)
        tn_dims = (((0,), (0,)), ((), ()))
        nt_dims = (((1,), (1,)), ((), ()))
        dsk = jnp.zeros((1, LANES), F32)
        stacks = [_stacked_heads(dm, kvh) for kvh in range(N_KV)]
        sink_cols = [_sink_column(sinks_ref, [2 * p + half for p, half in st]) for st in stacks]
        prev = _from_previous(rs)
        hs2 = rs // 2
        for qb in range(nq):
            r0 = qb * WINDOW
            rows = slice(r0, r0 + 2 * WINDOW)
            no_previous = first if qb == 0 else None
            dka = [None, None]
            dva = [None, None]
            for kvh in range(N_KV):
                pairs = [p for p, half in stacks[kvh] if half == 0]
                for b, (p, half) in enumerate(stacks[kvh]):
                    qp = pr_ref[r0:r0 + WINDOW, dm.o_q + LANES * p:dm.o_q + LANES * (p + 1)] * SCALE
                    qs_ref[b * WINDOW:(b + 1) * WINDOW, :] = jnp.where(
                        lo if half == 0 else jnp.logical_not(lo), qp, 0.0).astype(BF16)
                for b, p in enumerate(pairs):
                    ag = pr_ref[r0:r0 + WINDOW, dm.o_ag + LANES * p:dm.o_ag + LANES * (p + 1)]
                    dya = dy_ref[r0:r0 + WINDOW, c + LANES * p:c + LANES * (p + 1)]
                    do_ref[b * WINDOW:(b + 1) * WINDOW, :] = dya * (ag * _sigmoid(ag))
                qs = qs_ref[...]
                d_o = do_ref[...]
                d_o_b = d_o.astype(BF16)
                v_lo, v_hi = vhalf[kvh][0][rows], vhalf[kvh][1][rows]
                prob, m, inv = _softmax_with_sink(qs, k2[kvh][rows], prev, no_previous,
                                                  sink_cols[kvh])
                pb = _band_split(prob, prev)
                o_lo = jnp.dot(pb[0:hs2], v_lo, preferred_element_type=F32)
                o_hi = jnp.dot(pb[hs2:rs], v_hi, preferred_element_type=F32)
                delta = jnp.concatenate([jnp.sum(d_o * o_lo, axis=1, keepdims=True),
                                         jnp.sum(d_o * o_hi, axis=1, keepdims=True)], axis=0)
                dp = jnp.concatenate(
                    [lax.dot_general(d_o_b, v_lo, nt_dims, preferred_element_type=F32),
                     lax.dot_general(d_o_b, v_hi, nt_dims, preferred_element_type=F32)], axis=0)
                dsb = _band_split(prob * (_band_merge(dp, prev) - delta), prev)
                sink_grad = jnp.exp(sink_cols[kvh] - m) * inv * delta
                for b, (p, half) in enumerate(stacks[kvh]):
                    dsink = -jnp.sum(sink_grad[b * WINDOW:(b + 1) * WINDOW])
                    dsk = dsk + jnp.where(lane == 2 * p + half, dsink, 0.0)
                dq_lo = jnp.dot(dsb[0:hs2], khalf[kvh][0][rows], preferred_element_type=F32)
                dq_hi = jnp.dot(dsb[hs2:rs], khalf[kvh][1][rows], preferred_element_type=F32)
                dka[kvh] = lax.dot_general(dsb, qs, tn_dims, preferred_element_type=F32)
                d_o_half = jnp.concatenate([jnp.where(lo, d_o, 0.0), jnp.where(lo, 0.0, d_o)],
                                           axis=0).astype(BF16)
                dva[kvh] = lax.dot_general(pb, d_o_half, tn_dims, preferred_element_type=F32)
                for b, p in enumerate(pairs):
                    blk = slice(b * WINDOW, (b + 1) * WINDOW)
                    cols = slice(LANES * p, LANES * (p + 1))
                    ag = pr_ref[r0:r0 + WINDOW, dm.o_ag + cols.start:dm.o_ag + cols.stop]
                    dya = dy_ref[r0:r0 + WINDOW, c + cols.start:c + cols.stop]
                    dqpair = (dq_lo[blk] + dq_hi[blk]) * SCALE
                    d_ag = dya * (o_lo[blk] + o_hi[blk]) * _dsilu(ag, _sigmoid(ag))
                    dpr_ref[r0:r0 + WINDOW, dm.o_q + cols.start:dm.o_q + cols.stop] = (
                        dqpair.astype(BF16))
                    dbin_ref[0:1, dm.o_q + cols.start:dm.o_q + cols.stop] += jnp.sum(
                        dqpair, axis=0, keepdims=True)
                    dpr_ref[r0:r0 + WINDOW, dm.o_ag + cols.start:dm.o_ag + cols.stop] = (
                        d_ag.astype(BF16))
                    dbin_ref[0:1, dm.o_ag + cols.start:dm.o_ag + cols.stop] += jnp.sum(
                        d_ag, axis=0, keepdims=True)
            fold = [x + pltpu.roll(x, HEAD_DIM, 1) for x in (dka[0], dka[1], dva[0], dva[1])]
            dkv_ref[r0:r0 + 2 * WINDOW, 0:KV_W] += jnp.where(lo, fold[0], fold[1])
            dkv_ref[r0:r0 + 2 * WINDOW, KV_W:2 * KV_W] += jnp.where(lo, fold[2], fold[3])
        dsk_ref[0:1, :] += dsk
        carry_kv_ref[...] = dkv_ref[0:WINDOW, :]
        emit(dm.o_k, 2 * KV_W, dkv_ref[WINDOW:WINDOW + t, :])

        @pl.when(g == nt - 1)
        def _():
            for j in range(CONV_WIDTH):
                dcw_ref[j:j + 1, :] = jnp.sum(dcw_acc_ref[j], axis=0, keepdims=True)
            dcw_ref[CONV_WIDTH:CONV_ROWS, :] = jnp.zeros((CONV_ROWS - CONV_WIDTH, c), F32)

    rev = lambda g: nt - 1 - g
    vec = pl.BlockSpec((1, c), lambda g: (0, 0))
    const = lambda shape: pl.BlockSpec(shape, lambda g: (0, 0))
    scratch = [pltpu.VMEM((t + HALO, c), F32), pltpu.VMEM((t + WINDOW, 2 * KV_W), F32),
               pltpu.VMEM((t + HALO, c), F32), pltpu.VMEM((t + WINDOW, 2 * KV_W), F32),
               pltpu.VMEM((HALO, c), F32), pltpu.VMEM((WINDOW, 2 * KV_W), F32),
               pltpu.VMEM((SUBLANES - 1, t + HALO, LANES), F32),
               pltpu.VMEM((SUBLANES - 1, t + HALO, LANES), F32),
               pltpu.VMEM((t, c), F32), pltpu.VMEM((CONV_ROWS, SUBLANES, c), F32),
               pltpu.VMEM((rs, LANES), BF16), pltpu.VMEM((rs // 2, LANES), F32),
               pltpu.VMEM((t, c), F32)]
    host = _Hosting(list(copies), 9, 5)
    res = pl.pallas_call(
        _ride(body, host, 9, 5, len(scratch), nt), name="mixer_bwd",
        out_shape=[jax.ShapeDtypeStruct((dm.s, dm.din), BF16),
                   jax.ShapeDtypeStruct((CONV_ROWS, c), F32),
                   jax.ShapeDtypeStruct((8, c), F32),
                   jax.ShapeDtypeStruct((8, dm.din), F32),
                   jax.ShapeDtypeStruct((8, LANES), F32)]
        + host.out_shapes,
        grid=(nt,),
        in_specs=[pl.BlockSpec(memory_space=pltpu.SMEM)] + _mixer_specs(dm, t, rev)
        + [pl.BlockSpec((t, dm.d), lambda g: (rev(g), 0)), pl.BlockSpec((t, c), lambda g: (rev(g), 0)),
           pl.BlockSpec((CONV_ROWS, c), lambda g: (0, 0)), vec, vec] + [_ANY] * len(host.args),
        out_specs=[pl.BlockSpec((t, dm.din), lambda g: (rev(g), 0)),
                   const((CONV_ROWS, c)), const((8, c)), const((8, dm.din)), const((8, LANES))]
        + [_ANY] * len(host.out_shapes),
        scratch_shapes=scratch + host.scratch, input_output_aliases=host.aliases,
        compiler_params=_cparams(("arbitrary",)),
    )(sinks, proj, proj, proj, dymix, conv, conv_w, cln_g, cln_b, *host.args)
    return res[:5], res[5:]


def _outproj_ln(dm, ymix, w_out, b_out, x, ln_g, ln_b, target, tm, copies=()):
    last = target is not None
    d = dm.d
    n_in = 7 if last else 6
    host = _Hosting(list(copies), n_in, 3)
    n_hin, n_hout = len(host.args), len(host.out_shapes)
    steps = dm.s // tm

    def body(*refs):
        y_ref, w_ref, bo_ref, x_ref, g_ref, b_ref = refs[:6]
        h_in = refs[n_in:n_in + n_hin]
        h_out = refs[n_in + n_hin + 3:n_in + n_hin + 3 + n_hout]
        h_sems = refs[n_in + n_hin + 3 + n_hout:]
        if copies:
            @pl.when(pl.program_id(0) == 0)
            def _():
                host.start(h_in, h_out, h_sems)

        z = dm.alpha * x_ref[...] + (
            jnp.dot(y_ref[...], w_ref[...], preferred_element_type=F32) + bo_ref[...])
        mu = jnp.mean(z, axis=1, keepdims=True)
        dlt = z - mu
        var = jnp.mean(dlt * dlt, axis=1, keepdims=True)
        out = dlt * lax.rsqrt(var + LN_EPS) * g_ref[...] + b_ref[...]
        if last:
            t_ref = refs[6]
            z_ref, dout_ref, loss_ref = refs[n_in + n_hin:n_in + n_hin + 3]
            z_ref[...] = z
            err = out - t_ref[...]
            dout_ref[...] = err * (1.0 / d)

            @pl.when(pl.program_id(0) == 0)
            def _():
                loss_ref[...] = jnp.zeros_like(loss_ref)

            loss_ref[...] += 0.5 * jnp.sum(jnp.mean(err * err, axis=1, keepdims=True), axis=0,
                                           keepdims=True)
        else:
            z_ref, o_ref, ob_ref = refs[n_in + n_hin:n_in + n_hin + 3]
            z_ref[...] = z
            o_ref[...] = out
            ob_ref[...] = out.astype(BF16)

        if copies:
            @pl.when(pl.program_id(0) == steps - 1)
            def _():
                host.finish(h_in, h_out, h_sems)

    row = pl.BlockSpec((tm, d), lambda i: (i, 0))
    vec = pl.BlockSpec((1, d), lambda i: (0, 0))
    in_specs = [row, pl.BlockSpec((d, d), lambda i: (0, 0)), vec, row, vec, vec]
    args = [ymix, w_out, b_out, x, ln_g, ln_b]
    act = jax.ShapeDtypeStruct((dm.s, d), F32)
    if last:
        in_specs.append(row)
        args.append(target)
        out_shape = [act, act, jax.ShapeDtypeStruct((8, LANES), F32)]
        out_specs = [row, row, pl.BlockSpec((8, LANES), lambda i: (0, 0))]
    else:
        out_shape = [act, act, jax.ShapeDtypeStruct((dm.s, d), BF16)]
        out_specs = [row, row, row]
    res = pl.pallas_call(
        body, name="outproj_ln_loss" if last else "outproj_ln",
        out_shape=out_shape + host.out_shapes, grid=(steps,),
        in_specs=in_specs + [_ANY] * n_hin, out_specs=out_specs + [_ANY] * n_hout,
        scratch_shapes=host.scratch, input_output_aliases=host.aliases,
        compiler_params=_cparams(("arbitrary",)),
    )(*args, *host.args)
    return res[:3], res[3:]


def _ln_bwd(dm, dout, z, ln_g, w_out, tm, copies=()):
    d = dm.d
    host = _Hosting(list(copies), 4, 4)

    def body(do_ref, z_ref, g_ref, w_ref, dz_ref, dzb_ref, sm_ref, dy_ref):
        @pl.when(pl.program_id(0) == 0)
        def _():
            sm_ref[...] = jnp.zeros_like(sm_ref)

        z = z_ref[...]
        mu = jnp.mean(z, axis=1, keepdims=True)
        dlt = z - mu
        var = jnp.mean(dlt * dlt, axis=1, keepdims=True)
        rstd = lax.rsqrt(var + LN_EPS)
        zhat = dlt * rstd
        do = do_ref[...]
        dzh = do * g_ref[...]
        dz = rstd * (dzh - jnp.mean(dzh, axis=1, keepdims=True)
                     - zhat * jnp.mean(dzh * zhat, axis=1, keepdims=True))
        dzb = dz.astype(BF16)
        dz_ref[...] = dz
        dzb_ref[...] = dzb
        dy_ref[...] = lax.dot_general(dzb, w_ref[...], (((1,), (1,)), ((), ())),
                                      preferred_element_type=F32)
        sm_ref[0:1, :] += jnp.sum(do * zhat, axis=0, keepdims=True)
        sm_ref[1:2, :] += jnp.sum(do, axis=0, keepdims=True)
        sm_ref[2:3, :] += jnp.sum(dz, axis=0, keepdims=True)

    row = pl.BlockSpec((tm, d), lambda i: (i, 0))
    act = jax.ShapeDtypeStruct((dm.s, d), F32)
    res = pl.pallas_call(
        _ride(body, host, 4, 4, 0, dm.s // tm), name="ln_bwd_dymix",
        out_shape=[act, jax.ShapeDtypeStruct((dm.s, d), BF16), jax.ShapeDtypeStruct((8, d), F32), act]
        + host.out_shapes,
        grid=(dm.s // tm,),
        in_specs=[row, row, pl.BlockSpec((1, d), lambda i: (0, 0)),
                  pl.BlockSpec((d, d), lambda i: (0, 0))] + [_ANY] * len(host.args),
        out_specs=[row, row, pl.BlockSpec((8, d), lambda i: (0, 0)), row]
        + [_ANY] * len(host.out_shapes),
        scratch_shapes=host.scratch, input_output_aliases=host.aliases,
        compiler_params=_cparams(("arbitrary",)),
    )(dout, z, ln_g, w_out, *host.args)
    return res[:4], res[4:]


def _pair_sum(part, got, tr):
    _, _, r, w = part.shape

    def body(c_ref, p_ref, g_ref, o_ref):
        o_ref[...] = (p_ref[...] + g_ref[...]).astype(BF16)

    return pl.pallas_call(
        body, name="grad_pair_sum",
        out_shape=jax.ShapeDtypeStruct((N_CHIPS, r, w), BF16),
        grid_spec=pltpu.PrefetchScalarGridSpec(
            num_scalar_prefetch=1, grid=(N_CHIPS, r // tr),
            in_specs=[pl.BlockSpec((None, None, tr, w), lambda k, i, c_ref: (k, c_ref[0], i, 0)),
                      pl.BlockSpec((None, tr, w), lambda k, i, c_ref: (k, i, 0))],
            out_specs=pl.BlockSpec((None, tr, w), lambda k, i, c_ref: (k, i, 0))),
        compiler_params=_cparams(("parallel", "parallel")),
    )(lax.axis_index("c").reshape(1).astype(jnp.int32), part, got)


def _final_sum(part, got, recv, tr):
    _, _, r, w = part.shape

    def body(idx_ref, p_ref, g_ref, r0_ref, r1_ref, r2_ref, o_ref):
        acc = p_ref[...] + g_ref[...]
        for ref in (r0_ref, r1_ref, r2_ref):
            acc = acc + ref[...].astype(F32)
        o_ref[...] = acc

    x, y, c = _position()
    idx = jnp.stack([c, 2 * x + y, 2 * (1 - x) + y, 2 * x + (1 - y),
                     2 * (1 - x) + (1 - y)]).astype(jnp.int32)
    other = lambda j: pl.BlockSpec((None, tr, w), lambda i, s: (s[2 + j], i, 0))
    return pl.pallas_call(
        body, name="grad_final_sum",
        out_shape=jax.ShapeDtypeStruct((r, w), F32),
        grid_spec=pltpu.PrefetchScalarGridSpec(
            num_scalar_prefetch=1, grid=(r // tr,),
            in_specs=[pl.BlockSpec((None, None, tr, w), lambda i, s: (s[1], s[0], i, 0)),
                      pl.BlockSpec((None, tr, w), lambda i, s: (s[1], i, 0)),
                      other(0), other(1), other(2)],
            out_specs=pl.BlockSpec((tr, w), lambda i, s: (i, 0))),
        compiler_params=_cparams(("parallel",)),
    )(idx, part, got, recv, recv, recv)


def _adamw_math(w, g, m, v):
    m = ADAM_B1 * m + (1.0 - ADAM_B1) * g
    v = ADAM_B2 * v + (1.0 - ADAM_B2) * (g * g)
    m_hat = m / (1.0 - ADAM_B1 ** ADAM_STEP)
    v_hat = v / (1.0 - ADAM_B2 ** ADAM_STEP)
    delta = -ADAM_LR * (m_hat / (jnp.sqrt(v_hat) + ADAM_EPS) + ADAM_WD * w)
    return delta, m, v


def _adamw(w, g, m, v, tr):
    depth, r, width = w.shape

    def body(w_ref, g_ref, m_ref, v_ref, d_ref, nm_ref, nv_ref):
        d_ref[...], nm_ref[...], nv_ref[...] = _adamw_math(
            w_ref[...], g_ref[...], m_ref[...], v_ref[...])

    spec = pl.BlockSpec((None, tr, width), lambda l, i: (l, i, 0))
    shape = jax.ShapeDtypeStruct(w.shape, F32)
    return pl.pallas_call(
        body, name="adamw", out_shape=(shape, shape, shape), grid=(depth, r // tr),
        in_specs=[spec] * 4, out_specs=(spec, spec, spec),
        compiler_params=_cparams(("parallel", "parallel")),
    )(w, g, m, v)


def _gather_direct(block):
    def copies(ins, outs, send, recv):
        x, y, c = _position()
        out = []
        for k in range(1, N_DEV):
            peer = (1 - x if k & 4 else x, 1 - y if k & 2 else y, 1 - c if k & 1 else c)
            out.append((pltpu.make_async_remote_copy(
                src_ref=ins[0], dst_ref=outs[0].at[4 * x + 2 * y + c],
                send_sem=send.at[0, k - 1], recv_sem=recv.at[0, k - 1],
                device_id=peer, device_id_type=MESH), k - 1, peer))
        local = pltpu.make_async_copy(ins[0], outs[0].at[4 * x + 2 * y + c], send.at[0, N_DEV - 1])
        return out, local

    def start(ins, outs, send, recv):
        remote, local = copies(ins, outs, send, recv)
        local.start()
        for cp, _, _ in remote:
            cp.start()

    def finish(ins, outs, send, recv):
        remote, local = copies(ins, outs, send, recv)
        x, y, c = _position()
        for _, k, (px, py, pc) in remote:
            pltpu.make_async_remote_copy(
                src_ref=ins[0], dst_ref=outs[0].at[4 * px + 2 * py + pc],
                send_sem=send.at[0, k], recv_sem=recv.at[0, k],
                device_id=(x, y, c), device_id_type=MESH).wait_recv()
        for cp, _, _ in remote:
            cp.wait_send()
        local.wait()

    landing = [jax.ShapeDtypeStruct((N_DEV,) + block.shape, block.dtype)]
    return _Copies([block], landing, {}, (1, N_DEV), start, finish)


def _small_sum_adamw(gathered, w, m, v):
    prows = w.shape[0]

    def body(g_ref, w_ref, m_ref, v_ref, tot_ref, d_ref, nm_ref, nv_ref):
        tot = g_ref[0]
        for d in range(1, N_DEV):
            tot = tot + g_ref[d]
        tot_ref[...] = tot
        d_ref[...], nm_ref[...], nv_ref[...] = _adamw_math(
            w_ref[...], tot[0:prows, :], m_ref[...], v_ref[...])

    vm = pl.BlockSpec(memory_space=pltpu.VMEM)
    pshape = jax.ShapeDtypeStruct(w.shape, F32)
    return pl.pallas_call(
        body, name="small_sum_adamw",
        out_shape=(jax.ShapeDtypeStruct(gathered.shape[1:], F32), pshape, pshape, pshape),
        in_specs=[vm] * 4, out_specs=(vm, vm, vm, vm),
        compiler_params=pltpu.CompilerParams(vmem_limit_bytes=VMEM_LIMIT),
    )(gathered, w, m, v)


def _pack_rows(vec):
    depth, n = vec.shape
    rows = -(-n // LANES)
    rows = -(-rows // 8) * 8
    return jnp.pad(vec, ((0, 0), (0, rows * LANES - n))).reshape(depth, rows, LANES)


def _pack_small(named):
    blocks = [_pack_rows(a) for a in named]
    extents = [(b.shape[1], a.shape[1]) for b, a in zip(blocks, named)]
    depth = named[0].shape[0]
    packed = jnp.concatenate(blocks, axis=1).reshape(depth * sum(r for r, _ in extents), LANES)
    return packed, extents


def _unpack_small(packed, extents, depth):
    per_layer = sum(r for r, _ in extents)
    packed = packed.reshape(depth, per_layer, LANES)
    out, r0 = [], 0
    for rows, n in extents:
        out.append(packed[:, r0:r0 + rows, :].reshape(depth, rows * LANES)[:, :n])
        r0 += rows
    return out


def kernel(x, w_in, b_in, conv_w, conv_b, conv_ln_g, conv_ln_b, sinks, w_out, b_out, ln_g, ln_b, loss_target, m_w_in, m_b_in, m_conv_w, m_conv_b, m_conv_ln_g, m_conv_ln_b, m_sinks, m_w_out, m_b_out, m_ln_g, m_ln_b, v_w_in, v_b_in, v_conv_w, v_conv_b, v_conv_ln_g, v_conv_ln_b, v_sinks, v_w_out, v_b_out, v_ln_g, v_ln_b):
    depth, d, din_shard = w_in.shape
    s = x.shape[1]
    c_shard = conv_w.shape[2]
    dm = _Dims(s, d, N_DEV * c_shard, depth)
    assert dm.din == N_DEV * din_shard and x.shape[0] == 1 and sinks.shape[1] == dm.nh
    d_shard = w_out.shape[1]
    c, din = dm.c, dm.din

    t_mix = _tile(s, 256, WINDOW)
    tm_row = _tile(s, 256, 8)
    tm_big = _tile(s, 1024, 8)

    w_in_t, m_w_in_t, v_w_in_t = (a.transpose(0, 2, 1) for a in (w_in, m_w_in, v_w_in))
    w_in_b, w_out_b = w_in_t.astype(BF16), w_out.astype(BF16)
    conv_w_pad = jnp.pad(conv_w, ((0, 0), (0, CONV_ROWS - CONV_WIDTH), (0, 0)))
    first = _run_copies([_gather_own([w_in_b[0], conv_w_pad])], "weights_gather_own")
    g_in, g_conv = _run_copies([_gather_forward(first)], "weights_gather_forward")
    g_out = None
    conv_w_full = g_conv.transpose(1, 2, 0, 3).reshape(depth, CONV_ROWS, c)
    by_cols = lambda g: g.reshape(din, d)
    by_rows = lambda g: g.reshape(d, d)

    xs = x[0]
    xb = xs.astype(BF16)
    saved = []
    loss_part = dout = None
    for l in range(depth):
        w_in_l = by_cols(g_in)
        riders = [_gather_own([w_out_b[0]])] if l == 0 else [_gather_forward([g_out])]
        proj, (g_out,) = _matmul(xb, w_in_l, tb=True, tm=tm_big, tn=_tile(din, 1344), tk=d,
                                 out_dtype=F32, name="in_proj", bias=b_in[l][None, :],
                                 copies=riders)
        riders = [_gather_forward([g_out])] if l == 0 else []
        if l + 1 < depth:
            riders.append(_gather_own([w_in_b[l + 1]]))
        (ymix, conv), landed = _mixer_fwd(dm, proj, conv_w_full[l], conv_b[l][None, :],
                                          conv_ln_g[l][None, :], conv_ln_b[l][None, :], sinks[l],
                                          t_mix, copies=riders)
        if l == 0:
            g_out = landed[0]
        g_in_next = landed[-1] if l + 1 < depth else None
        w_out_l = by_rows(g_out)
        target = loss_target[0] if l == depth - 1 else None
        riders = []
        if l + 1 < depth:
            riders = [_gather_forward([g_in_next]), _gather_own([w_out_b[l + 1]])]
        res, landed = _outproj_ln(dm, ymix, w_out_l, b_out[l][None, :], xs, ln_g[l][None, :],
                                  ln_b[l][None, :], target, tm_row, copies=riders)
        saved.append((xb, proj, ymix, conv, res[0], w_in_l, w_out_l))
        if l + 1 < depth:
            g_in, g_out = landed
            xs, xb = res[1], res[2]
        else:
            dout, loss_part = res[1], res[2]

    g_w_in, g_w_out = [None] * depth, [None] * depth
    small_parts = [None] * depth
    dconv_w = [None] * depth
    tr_in, tr_out = _tile(din_shard, 512, 8), _tile(d_shard, 256, 8)
    parts_in = None
    for l in reversed(range(depth)):
        xb, proj, ymix, conv, z, w_in_l, w_out_l = saved[l]
        riders = [_scatter_sibling([parts_in])] if parts_in is not None else []
        (dz, dzb, ln_small, dymix), landed = _ln_bwd(dm, dout, z, ln_g[l][None, :], w_out_l, tm_row,
                                                     copies=riders)
        if parts_in is not None:
            got_in = landed[0]
            sums_in = _pair_sum(parts_in, got_in, tr_in)
        dw_out = _matmul(ymix, dzb, ta=True, tm=_tile(d, 1024), tn=_tile(d, 1024), tk=s,
                         out_dtype=F32, name="dw_out")
        riders = [_scatter_chips([sums_in])] if parts_in is not None else []
        (dproj, dcw, conv_small, dbin, dsk), landed = _mixer_bwd(
            dm, proj, dymix, conv, conv_w_full[l], conv_ln_g[l][None, :], conv_ln_b[l][None, :],
            sinks[l], t_mix, copies=riders)
        if parts_in is not None:
            g_w_in[l + 1] = _final_sum(parts_in, got_in, landed[0], tr_in)
        small_parts[l] = [dbin[0], conv_small[0], conv_small[1], conv_small[2], dsk[0, :dm.nh],
                          ln_small[2], ln_small[0], ln_small[1]]
        dconv_w[l] = dcw
        parts_out = dw_out.reshape(N_CHIPS, 2, d_shard, d)
        dx = dict(tm=_tile(s, 512), tn=_tile(d, 1024), tk=din, out_dtype=F32, name="dx", resid=dz,
                  resid_scale=dm.alpha)
        dw = dict(ta=True, tm=_tile(din, 768), tn=_tile(d, 1024), tk=s, out_dtype=F32, name="dw_in")
        if l > 0:
            dout, (got_out,) = _matmul(dproj, w_in_l, copies=[_scatter_sibling([parts_out])], **dx)
            sums_out = _pair_sum(parts_out, got_out, tr_out)
            dw_in_t, (recv_out,) = _matmul(dproj, xb, copies=[_scatter_chips([sums_out])], **dw)
            parts_in = dw_in_t.reshape(N_CHIPS, 2, din_shard, d)
        else:
            packed_g, extents = _pack_small([jnp.stack([small_parts[k][n] for k in range(depth)])
                                             for n in range(len(small_parts[0]))])
            prows = packed_g.shape[0]
            conv_rows = depth * CONV_ROWS * c // LANES
            partial = jnp.concatenate(
                [packed_g, jnp.stack(dconv_w).reshape(conv_rows, LANES), loss_part], axis=0)
            dw_in_t, (got_out, small_all) = _matmul(
                dproj, xb, copies=[_scatter_sibling([parts_out]), _gather_direct(partial)], **dw)
            parts_in = dw_in_t.reshape(N_CHIPS, 2, din_shard, d)
            sums_out = _pair_sum(parts_out, got_out, tr_out)
            got_in, recv_out = _run_copies(
                [_scatter_sibling([parts_in]), _scatter_chips([sums_out])], "grad_exchange_tail")
            sums_in = _pair_sum(parts_in, got_in, tr_in)
            dout, (recv_in,) = _matmul(dproj, w_in_l, copies=[_scatter_chips([sums_in])], **dx)
            g_w_in[0] = _final_sum(parts_in, got_in, recv_in, tr_in)
        g_w_out[l] = _final_sum(parts_out, got_out, recv_out, tr_out)
    grad_x = dout[None]

    small_w = [b_in, conv_b, conv_ln_g, conv_ln_b, sinks, b_out, ln_g, ln_b]
    small_m = [m_b_in, m_conv_b, m_conv_ln_g, m_conv_ln_b, m_sinks, m_b_out, m_ln_g, m_ln_b]
    small_v = [v_b_in, v_conv_b, v_conv_ln_g, v_conv_ln_b, v_sinks, v_b_out, v_ln_g, v_ln_b]
    packed_w, _ = _pack_small(small_w)
    packed_m, _ = _pack_small(small_m)
    packed_v, _ = _pack_small(small_v)
    total, sm_delta, sm_m, sm_v = _small_sum_adamw(small_all, packed_w, packed_m, packed_v)
    loss = total[prows + conv_rows, 0]
    dconv_w_full = total[prows:prows + conv_rows].reshape(depth, CONV_ROWS, c)
    me = 4 * lax.axis_index("x") + 2 * lax.axis_index("y") + lax.axis_index("c")
    grad_conv_w = lax.dynamic_slice_in_dim(dconv_w_full, me * c_shard, c_shard, axis=2)[:, :CONV_WIDTH]

    grads_small = _unpack_small(total[:prows], extents, depth)
    delta_small = _unpack_small(sm_delta, extents, depth)
    newm_small = _unpack_small(sm_m, extents, depth)
    newv_small = _unpack_small(sm_v, extents, depth)

    grad_w_in_t = jnp.stack(g_w_in)
    grad_w_out = jnp.stack(g_w_out)
    grad_w_in, d_w_in, nm_w_in, nv_w_in = (a.transpose(0, 2, 1) for a in (
        grad_w_in_t, *_adamw(w_in_t, grad_w_in_t, m_w_in_t, v_w_in_t, tr_in)))
    d_w_out, nm_w_out, nv_w_out = _adamw(w_out, grad_w_out, m_w_out, v_w_out, _tile(d_shard, 256, 8))
    d_cw, nm_cw, nv_cw = _adamw(conv_w, grad_conv_w, m_conv_w, v_conv_w, CONV_WIDTH)

    def assemble(w_in_leaf, conv_w_leaf, w_out_leaf, small):
        b_in_, conv_b_, cg_, cb_, sinks_, b_out_, ln_g_, ln_b_ = small
        return [w_in_leaf, b_in_, conv_w_leaf, conv_b_, cg_, cb_, sinks_, w_out_leaf, b_out_,
                ln_g_, ln_b_]

    return (loss, grad_x,
            *assemble(grad_w_in, grad_conv_w, grad_w_out, grads_small),
            *assemble(d_w_in, d_cw, d_w_out, delta_small),
            *assemble(nm_w_in, nm_cw, nm_w_out, newm_small),
            *assemble(nv_w_in, nv_cw, nv_w_out, newv_small))
```
